```python
import jax, jax.numpy as jnp
from jax import lax
import numpy as np

D_MODEL = 2048
BATCH = 4
SEQ = 2048
DEPTH = 1
DEC_BATCH = 128
DEC_SEQ = 1
PAST_LEN = 2048
PAGE_SIZE = 128

N_HEADS = 8
HEAD_DIM = 128
KV_HEADS = 2
HPG = N_HEADS // KV_HEADS
ATTN_DIM = N_HEADS * HEAD_DIM
KV_DIM = KV_HEADS * HEAD_DIM
POOL_DIM = D_MODEL - ATTN_DIM
POOL_WINDOWS = (2, 4, 8, 16)
POOL_GROUPS = len(POOL_WINDOWS)
POOL_GW = POOL_DIM // POOL_GROUPS
POOL_BUF = max(POOL_WINDOWS) - 1
N_BRANCH = 3
PROJ_DIM = ATTN_DIM + 2 * N_BRANCH * KV_DIM + N_BRANCH * N_HEADS + POOL_DIM
ROT_DIM = HEAD_DIM // 4
ROPE_THETA = 500000.0
CMP_LEN = 32
CMP_STRIDE = 16
CMP_HID = HEAD_DIM
SEL_BLOCK = 64
SEL_TOPK = 16
WINDOW = 512
WIN_QBLK = 128
SEL_QBLK = 64
D_FF = 4 * D_MODEL
EPS = 1e-6
SCALE = HEAD_DIM ** -0.5
FORCE_SCORE = 1e4
NEG_INF = -1e30

kernel_name = 'nsa_pool_hybrid_step'


def rms_norm(x, g):
    x32 = x.astype(jnp.float32)
    y = x32 * lax.rsqrt(jnp.mean(x32 * x32, axis=-1, keepdims=True) + EPS)
    return (y * g.astype(jnp.float32)).astype(x.dtype)


def masked_softmax(s, mask):
    s = jnp.where(mask, s.astype(jnp.float32), NEG_INF)
    m = jnp.max(s, axis=-1, keepdims=True)
    e = jnp.where(mask, jnp.exp(s - m), 0.0)
    d = jnp.sum(e, axis=-1, keepdims=True)
    return e / jnp.where(d > 0, d, 1.0)


def partial_rope(x, pos):
    half = ROT_DIM // 2
    inv = jnp.power(ROPE_THETA, -jnp.arange(half, dtype=jnp.float32) * (2.0 / ROT_DIM))
    ang = pos.astype(jnp.float32)[:, None] * inv[None, :]
    cos = jnp.cos(ang)[None, :, None, :]
    sin = jnp.sin(ang)[None, :, None, :]
    x32 = x.astype(jnp.float32)
    x1, x2 = x32[..., :half], x32[..., half:ROT_DIM]
    out = jnp.concatenate([x1 * cos - x2 * sin, x1 * sin + x2 * cos, x32[..., ROT_DIM:]], axis=-1)
    return out.astype(x.dtype)


def project(h, w_in, pos):
    B, T, _ = h.shape
    z = h @ w_in
    sizes = [ATTN_DIM] + [KV_DIM] * (2 * N_BRANCH) + [N_BRANCH * N_HEADS, POOL_DIM]
    offs = np.cumsum(sizes)[:-1].tolist()
    q, kc, vc, ks, vs, kw, vw, gl, u = jnp.split(z, offs, axis=-1)
    kvh = lambda a: a.reshape(B, T, KV_HEADS, HEAD_DIM)
    q = partial_rope(q.reshape(B, T, N_HEADS, HEAD_DIM), pos)
    kc, ks, kw = partial_rope(kvh(kc), pos), partial_rope(kvh(ks), pos), partial_rope(kvh(kw), pos)
    vc, vs, vw = kvh(vc), kvh(vs), kvh(vw)
    gates = jax.nn.sigmoid(gl.astype(jnp.float32)).reshape(B, T, N_HEADS, N_BRANCH)
    return q, kc, vc, ks, vs, kw, vw, gates, u


def compress(rows, pos_emb, w1, w2):
    B, L, G, D = rows.shape
    r = CMP_LEN // CMP_STRIDE
    n_chunk = L // CMP_STRIDE
    ch = rows[:, :n_chunk * CMP_STRIDE].reshape(B, n_chunk, CMP_STRIDE, G, D)
    n_cmp = n_chunk - r + 1
    blocks = jnp.concatenate([ch[:, i:i + n_cmp] for i in range(r)], axis=2)
    blocks = blocks + pos_emb[None, None, :, None, :]
    flat = blocks.transpose(0, 1, 3, 2, 4).reshape(B, n_cmp, G, CMP_LEN * D)
    return jax.nn.gelu(flat @ w1) @ w2


def cmp_branch(q, k_rows, v_rows, q_pos, cmp_w):
    kc = compress(k_rows, *cmp_w[0])
    vc = compress(v_rows, *cmp_w[1])
    B, T = q.shape[:2]
    n_cmp = kc.shape[1]
    blk_end = jnp.arange(n_cmp) * CMP_STRIDE + CMP_LEN - 1
    mask = blk_end[None, :] <= q_pos[:, None]
    qg = q.reshape(B, T, KV_HEADS, HPG, HEAD_DIM)
    s = jnp.einsum('btghd,bngd->bghtn', qg, kc).astype(jnp.float32) * SCALE
    p = masked_softmax(s, mask)
    o = jnp.einsum('bghtn,bngd->btghd', p, vc.astype(jnp.float32))
    return o.reshape(B, T, N_HEADS, HEAD_DIM).astype(q.dtype), p


def select_blocks(p_cmp, q_pos, n_sel):
    n_cmp = p_cmp.shape[-1]
    cs = jnp.arange(n_cmp) * CMP_STRIDE
    ss = jnp.arange(n_sel) * SEL_BLOCK
    cmp_to_sel = ((cs[:, None] < ss[None, :] + SEL_BLOCK) & (cs[:, None] + CMP_LEN > ss[None, :])).astype(jnp.float32)
    imp = jnp.einsum('bghtn,ns->bgts', p_cmp, cmp_to_sel)
    j = jnp.arange(n_sel)[None, :]
    jt = (q_pos // SEL_BLOCK)[:, None]
    force = (j == 0) | (j == jt) | (j == jt - 1)
    valid = j <= jt
    score = jnp.where(force, FORCE_SCORE, jnp.where(valid, imp, -1.0))
    _, idx = lax.top_k(score, min(SEL_TOPK, n_sel))
    return idx


def to_blocks(rows, n_sel):
    B, L, G, D = rows.shape
    rows = jnp.pad(rows, ((0, 0), (0, n_sel * SEL_BLOCK - L), (0, 0), (0, 0)))
    return rows.reshape(B, n_sel, SEL_BLOCK, G, D).transpose(0, 3, 1, 2, 4)


def sel_attend(q, kb, vb, idx, q_pos):
    B, Tq = q.shape[:2]
    take = jax.vmap(jax.vmap(lambda blocks, ix: blocks[ix]))
    kg = take(kb, idx).reshape(B, KV_HEADS, Tq, -1, HEAD_DIM)
    vg = take(vb, idx).reshape(B, KV_HEADS, Tq, -1, HEAD_DIM)
    kpos = (idx[..., None] * SEL_BLOCK + jnp.arange(SEL_BLOCK)).reshape(B, KV_HEADS, Tq, -1)
    mask = (kpos <= q_pos[None, None, :, None])[:, :, None]
    qg = q.reshape(B, Tq, KV_HEADS, HPG, HEAD_DIM)
    s = jnp.einsum('btghd,bgtkd->bghtk', qg, kg).astype(jnp.float32) * SCALE
    p = masked_softmax(s, mask)
    o = jnp.einsum('bghtk,bgtkd->btghd', p, vg.astype(jnp.float32))
    return o.reshape(B, Tq, N_HEADS, HEAD_DIM).astype(q.dtype)


def sel_attend_blocked(q, kb, vb, idx, q_pos):
    B, T = q.shape[:2]
    nc = T // SEL_QBLK
    qs = q.reshape(B, nc, SEL_QBLK, N_HEADS, HEAD_DIM).swapaxes(0, 1)
    ids = idx.reshape(B, KV_HEADS, nc, SEL_QBLK, -1).transpose(2, 0, 1, 3, 4)
    ps = q_pos.reshape(nc, SEL_QBLK)
    out = lax.map(lambda a: sel_attend(a[0], kb, vb, a[1], a[2]), (qs, ids, ps))
    return out.swapaxes(0, 1).reshape(B, T, N_HEADS, HEAD_DIM)


def win_attend_band(q, k, v):
    B, T = q.shape[:2]
    nb = T // WIN_QBLK
    nprev = WINDOW // WIN_QBLK
    pad = ((0, 0), (WINDOW, 0), (0, 0), (0, 0))
    kp = jnp.pad(k, pad).reshape(B, nb + nprev, WIN_QBLK, KV_HEADS, HEAD_DIM)
    vp = jnp.pad(v, pad).reshape(B, nb + nprev, WIN_QBLK, KV_HEADS, HEAD_DIM)
    kband = jnp.concatenate([kp[:, i:i + nb] for i in range(nprev + 1)], axis=2)
    vband = jnp.concatenate([vp[:, i:i + nb] for i in range(nprev + 1)], axis=2)
    q_pos = jnp.arange(T).reshape(nb, WIN_QBLK)
    k_pos = (jnp.arange(nb) * WIN_QBLK)[:, None] - WINDOW + jnp.arange((nprev + 1) * WIN_QBLK)[None, :]
    kq = k_pos[:, None, :]
    qq = q_pos[:, :, None]
    mask = (kq <= qq) & (kq > qq - WINDOW) & (kq >= 0)
    qb = q.reshape(B, nb, WIN_QBLK, KV_HEADS, HPG, HEAD_DIM)
    s = jnp.einsum('bnqghd,bnkgd->bghnqk', qb, kband).astype(jnp.float32) * SCALE
    p = masked_softmax(s, mask)
    o = jnp.einsum('bghnqk,bnkgd->bnqghd', p, vband.astype(jnp.float32))
    return o.reshape(B, T, N_HEADS, HEAD_DIM).astype(q.dtype)


def win_attend_dense(q, k, v, q_pos, k_pos):
    B, T = q.shape[:2]
    kq = k_pos[None, :]
    qq = q_pos[:, None]
    mask = (kq <= qq) & (kq > qq - WINDOW) & (kq >= 0)
    qg = q.reshape(B, T, KV_HEADS, HPG, HEAD_DIM)
    s = jnp.einsum('btghd,bkgd->bghtk', qg, k).astype(jnp.float32) * SCALE
    p = masked_softmax(s, mask)
    o = jnp.einsum('bghtk,bkgd->btghd', p, v.astype(jnp.float32))
    return o.reshape(B, T, N_HEADS, HEAD_DIM).astype(q.dtype)


def pool_mix(u_ext, t_pos, pool_w, pool_scale):
    B, Lx, C = u_ext.shape
    T = Lx - POOL_BUF
    c = jnp.cumsum(u_ext.astype(jnp.float32), axis=1)
    c = jnp.concatenate([jnp.zeros((B, 1, C), jnp.float32), c], axis=1)
    end = c[:, POOL_BUF + 1:]
    u_new = u_ext[:, POOL_BUF:].astype(jnp.float32)
    diffs = []
    for g, w in enumerate(POOL_WINDOWS):
        lo, hi = g * POOL_GW, (g + 1) * POOL_GW
        start = c[:, POOL_BUF + 1 - w:POOL_BUF + 1 - w + T, lo:hi]
        cnt = jnp.minimum(w, t_pos + 1).astype(jnp.float32)[None, :, None]
        diffs.append((end[..., lo:hi] - start) / cnt - u_new[..., lo:hi])
    d = jnp.stack(diffs, axis=2)
    y = jnp.einsum('btgc,gce->btge', d, pool_w.astype(jnp.float32)).reshape(B, T, C)
    return (y * pool_scale.astype(jnp.float32)).astype(u_ext.dtype)


def merge(o_cmp, o_sel, o_win, gates, y_pool, w_o):
    B, T = o_cmp.shape[:2]
    o = gates[..., 0:1] * o_cmp + gates[..., 1:2] * o_sel + gates[..., 2:3] * o_win
    o = o.astype(y_pool.dtype).reshape(B, T, ATTN_DIM)
    return jnp.concatenate([o, y_pool], axis=-1) @ w_o


def mixer_prompt(h, w_in, cmp_w, pool_w, pool_scale, w_o):
    B, T, _ = h.shape
    pos = jnp.arange(T, dtype=jnp.int32)
    q, kc, vc, ks, vs, kw, vw, gates, u = project(h, w_in, pos)
    o_cmp, p_cmp = cmp_branch(q, kc, vc, pos, cmp_w)
    n_sel = -(-T // SEL_BLOCK)
    idx = select_blocks(p_cmp, pos, n_sel)
    o_sel = sel_attend_blocked(q, to_blocks(ks, n_sel), to_blocks(vs, n_sel), idx, pos)
    o_win = win_attend_band(q, kw, vw)
    u_ext = jnp.concatenate([jnp.zeros((B, POOL_BUF, POOL_DIM), u.dtype), u], axis=1)
    y_pool = pool_mix(u_ext, pos, pool_w, pool_scale)
    out = merge(o_cmp, o_sel, o_win, gates, y_pool, w_o)
    wb = min(WINDOW, T)
    kv_rows = jnp.stack([kc, vc, ks, vs], axis=2)
    win_state = jnp.stack([kw[:, T - wb:], vw[:, T - wb:]], axis=2)
    return out, kv_rows, win_state, u_ext[:, -POOL_BUF:]


def mixer_sample(h, cache_kv, win_kv, pool_state, page_table, w_in, cmp_w, pool_w, pool_scale, w_o):
    B, T, _ = h.shape
    n_pages = page_table.shape[1]
    past_len = n_pages * cache_kv.shape[1]
    pos = past_len + jnp.arange(T, dtype=jnp.int32)
    q, kc, vc, ks, vs, kw, vw, gates, u = project(h, w_in, pos)
    past = cache_kv[page_table].reshape(B, past_len, 4, KV_HEADS, HEAD_DIM)
    kc_all = jnp.concatenate([past[:, :, 0], kc], axis=1)
    vc_all = jnp.concatenate([past[:, :, 1], vc], axis=1)
    ks_all = jnp.concatenate([past[:, :, 2], ks], axis=1)
    vs_all = jnp.concatenate([past[:, :, 3], vs], axis=1)
    L = past_len + T
    o_cmp, p_cmp = cmp_branch(q, kc_all, vc_all, pos, cmp_w)
    n_sel = -(-L // SEL_BLOCK)
    idx = select_blocks(p_cmp, pos, n_sel)
    o_sel = sel_attend(q, to_blocks(ks_all, n_sel), to_blocks(vs_all, n_sel), idx, pos)
    wb = win_kv.shape[1]
    kw_all = jnp.concatenate([win_kv[:, :, 0], kw], axis=1)
    vw_all = jnp.concatenate([win_kv[:, :, 1], vw], axis=1)
    k_pos = past_len - wb + jnp.arange(wb + T, dtype=jnp.int32)
    o_win = win_attend_dense(q, kw_all, vw_all, pos, k_pos)
    u_ext = jnp.concatenate([pool_state, u], axis=1)
    y_pool = pool_mix(u_ext, pos, pool_w, pool_scale)
    out = merge(o_cmp, o_sel, o_win, gates, y_pool, w_o)
    kv_rows = jnp.stack([kc, vc, ks, vs], axis=2)
    win_state = jnp.stack([kw_all[:, -wb:], vw_all[:, -wb:]], axis=2)
    return out, kv_rows, win_state, u_ext[:, -POOL_BUF:]


def sq_relu_mlp(h, w_up, w_down):
    a = jax.nn.relu(h @ w_up)
    return (a * a) @ w_down


def setup_inputs(seed: int = 0) -> dict:
    key = jax.random.key(seed)
    ks = jax.random.split(key, 24)
    f32 = jnp.float32
    nrm = lambda k, shape, scale: jax.random.normal(k, shape, f32) * scale
    gain = lambda k: 1.0 + 0.05 * jax.random.normal(k, (DEPTH, D_MODEL), f32)
    n_pages = PAST_LEN // PAGE_SIZE
    n_used = DEC_BATCH * n_pages
    n_phys = n_used + max(1, n_used // 4)
    wb = min(WINDOW, PAST_LEN)
    page_table = jax.random.permutation(ks[5], n_phys)[:n_used].reshape(DEC_BATCH, n_pages).astype(jnp.int32)
    return {
        'x_prompt': nrm(ks[0], (BATCH, SEQ, D_MODEL), 1.0),
        'x_sample': nrm(ks[1], (DEC_BATCH, DEC_SEQ, D_MODEL), 1.0),
        'cache_kv': nrm(ks[2], (DEPTH, n_phys, PAGE_SIZE, 4, KV_HEADS, HEAD_DIM), 1.0),
        'state_win_kv': nrm(ks[3], (DEPTH, DEC_BATCH, wb, 2, KV_HEADS, HEAD_DIM), 1.0),
        'state_pool': nrm(ks[4], (DEPTH, DEC_BATCH, POOL_BUF, POOL_DIM), 1.0),
        'page_table': page_table,
        'norm_mix_pre': gain(ks[6]),
        'w_in': nrm(ks[7], (DEPTH, D_MODEL, PROJ_DIM), D_MODEL ** -0.5),
        'cmp_pos_k': nrm(ks[8], (DEPTH, CMP_LEN, HEAD_DIM), 0.1),
        'cmp_w1_k': nrm(ks[9], (DEPTH, CMP_LEN * HEAD_DIM, CMP_HID), (CMP_LEN * HEAD_DIM) ** -0.5),
        'cmp_w2_k': nrm(ks[10], (DEPTH, CMP_HID, HEAD_DIM), CMP_HID ** -0.5),
        'cmp_pos_v': nrm(ks[11], (DEPTH, CMP_LEN, HEAD_DIM), 0.1),
        'cmp_w1_v': nrm(ks[12], (DEPTH, CMP_LEN * HEAD_DIM, CMP_HID), (CMP_LEN * HEAD_DIM) ** -0.5),
        'cmp_w2_v': nrm(ks[13], (DEPTH, CMP_HID, HEAD_DIM), CMP_HID ** -0.5),
        'pool_w': nrm(ks[14], (DEPTH, POOL_GROUPS, POOL_GW, POOL_GW), POOL_GW ** -0.5),
        'pool_scale': 0.5 + 0.1 * jax.random.normal(ks[15], (DEPTH, POOL_DIM), f32),
        'w_o': nrm(ks[16], (DEPTH, D_MODEL, D_MODEL), D_MODEL ** -0.5),
        'norm_mix_post': gain(ks[17]),
        'norm_mlp_pre': gain(ks[18]),
        'w_up': nrm(ks[19], (DEPTH, D_MODEL, D_FF), D_MODEL ** -0.5),
        'w_down': nrm(ks[20], (DEPTH, D_FF, D_MODEL), D_FF ** -0.5),
        'norm_mlp_post': gain(ks[21]),
    }


def reference(x_prompt, x_sample, cache_kv, state_win_kv, state_pool, page_table,
              norm_mix_pre, w_in, cmp_pos_k, cmp_w1_k, cmp_w2_k, cmp_pos_v, cmp_w1_v, cmp_w2_v,
              pool_w, pool_scale, w_o, norm_mix_post, norm_mlp_pre, w_up, w_down, norm_mlp_post):
    y_p, y_s = x_prompt, x_sample
    kv_p, kv_s, win_p, win_s, pool_p, pool_s = [], [], [], [], [], []
    for l in range(DEPTH):
        cmp_w = ((cmp_pos_k[l], cmp_w1_k[l], cmp_w2_k[l]), (cmp_pos_v[l], cmp_w1_v[l], cmp_w2_v[l]))
        mw = (w_in[l], cmp_w, pool_w[l], pool_scale[l], w_o[l])
        m, kv_r, w_st, p_st = mixer_prompt(rms_norm(y_p, norm_mix_pre[l]), *mw)
        y_p = y_p + rms_norm(m, norm_mix_post[l])
        y_p = y_p + rms_norm(sq_relu_mlp(rms_norm(y_p, norm_mlp_pre[l]), w_up[l], w_down[l]), norm_mlp_post[l])
        kv_p.append(kv_r)
        win_p.append(w_st)
        pool_p.append(p_st)
        m, kv_r, w_st, p_st = mixer_sample(rms_norm(y_s, norm_mix_pre[l]), cache_kv[l], state_win_kv[l],
                                           state_pool[l], page_table, *mw)
        y_s = y_s + rms_norm(m, norm_mix_post[l])
        y_s = y_s + rms_norm(sq_relu_mlp(rms_norm(y_s, norm_mlp_pre[l]), w_up[l], w_down[l]), norm_mlp_post[l])
        kv_s.append(kv_r)
        win_s.append(w_st)
        pool_s.append(p_st)
    kv_prompt = jnp.stack(kv_p)
    kv_sample = jnp.stack(kv_s)
    win_prompt = jnp.stack(win_p)
    win_sample = jnp.stack(win_s)
    pool_prompt = jnp.stack(pool_p)
    pool_sample = jnp.stack(pool_s)
    return (y_p, y_s, kv_prompt, kv_sample, win_prompt, win_sample, pool_prompt, pool_sample)
```

```python
import functools

import numpy as np
import jax
import jax.numpy as jnp
from jax import lax
from jax.experimental import pallas as pl
from jax.experimental.pallas import tpu as pltpu

N_HEADS = 8
HEAD_DIM = 128
KV_HEADS = 2
HPG = N_HEADS // KV_HEADS
ATTN_DIM = N_HEADS * HEAD_DIM
KV_DIM = KV_HEADS * HEAD_DIM
N_BRANCH = 3
POOL_WINDOWS = (2, 4, 8, 16)
POOL_GROUPS = len(POOL_WINDOWS)
POOL_BUF = max(POOL_WINDOWS) - 1
ROT_DIM = HEAD_DIM // 4
ROT_HALF = ROT_DIM // 2
ROPE_THETA = 500000.0
CMP_LEN = 32
CMP_STRIDE = 16
SEL_BLOCK = 64
SEL_TOPK = 16
WINDOW = 512
EPS = 1e-6
SCALE = HEAD_DIM ** -0.5
FORCE_SCORE = 1e4
NEG_INF = -1e30

LANES = 128
CACHE_SLOTS = 4
ROW_W = CACHE_SLOTS * KV_DIM
CHUNK_W = CMP_STRIDE * ROW_W
CMP_IN = CMP_LEN * HEAD_DIM
CMP_HALF = CMP_STRIDE * HEAD_DIM
GATE_PAD = LANES
VMEM_LIMIT = 56 * 1024 * 1024

BF16 = jnp.bfloat16
F32 = jnp.float32


def _dot(a, b):
    return jnp.dot(a, b, preferred_element_type=F32)


def _dot_nt(a, b):
    return lax.dot_general(a, b, (((1,), (1,)), ((), ())), preferred_element_type=F32)


def _rms(x, g):
    return x * lax.rsqrt(jnp.mean(x * x, axis=-1, keepdims=True) + EPS) * g


def _params(n_axes):
    return pltpu.CompilerParams(
        dimension_semantics=("arbitrary",) * n_axes, vmem_limit_bytes=VMEM_LIMIT)


def _split_hi_lo(x):
    hi = x.astype(BF16)
    lo = (x - hi.astype(F32)).astype(BF16)
    return hi, lo


Q_OFF, KV_OFF, WKV_OFF = 0, ATTN_DIM, ATTN_DIM + 4 * KV_DIM


def _proj_kernel(x_ref, g_ref, w_ref, cos_ref, sa_ref, sb_ref, *rest, pool_dim, chunked):
    if chunked:
        q_ref, kv_ref, wkv_ref, kva_ref, gate_ref, u_ref, xc_ref, tmp_ref = rest
    else:
        q_ref, kv_ref, wkv_ref, kva_ref, gate_ref, u_ref = rest
    tm = x_ref.shape[0]
    u_off = WKV_OFF + 2 * KV_DIM
    gate_off = u_off + pool_dim
    h = _rms(x_ref[...], g_ref[...]).astype(BF16)
    cos, sa, sb = cos_ref[...], sa_ref[...], sb_ref[...]

    def rope(z):
        return (z * cos + pltpu.roll(z, LANES - ROT_HALF, 1) * sa
                + pltpu.roll(z, ROT_HALF, 1) * sb)

    zq = _dot(h, w_ref[:, Q_OFF:Q_OFF + ATTN_DIM])
    for hd in range(N_HEADS):
        sl = slice(hd * HEAD_DIM, (hd + 1) * HEAD_DIM)
        q_ref[:, sl] = rope(zq[:, sl]).astype(BF16)

    n_kv = CACHE_SLOTS * KV_HEADS
    zkv = _dot(h, w_ref[:, KV_OFF:KV_OFF + 4 * KV_DIM])
    for blk in range(n_kv):
        z = zkv[:, blk * HEAD_DIM:(blk + 1) * HEAD_DIM]
        if (blk // KV_HEADS) % 2 == 0:
            z = rope(z)
        kv_ref[pl.ds(blk, tm, stride=n_kv), :] = z
        if blk >= 2 * KV_HEADS:
            kva_ref[blk - 2 * KV_HEADS] = z.astype(BF16)
        elif chunked:
            tmp_ref[...] = z
            for r in range(CMP_STRIDE):
                xc_ref[blk, :, r * HEAD_DIM:(r + 1) * HEAD_DIM] = (
                    tmp_ref[pl.ds(r, tm // CMP_STRIDE, stride=CMP_STRIDE), :].astype(BF16))

    n_w = 2 * KV_HEADS
    zw = _dot(h, w_ref[:, WKV_OFF:WKV_OFF + 2 * KV_DIM])
    for blk in range(n_w):
        z = zw[:, blk * HEAD_DIM:(blk + 1) * HEAD_DIM]
        if blk < KV_HEADS:
            z = rope(z)
        wkv_ref[pl.ds(blk, tm, stride=n_w), :] = z
        kva_ref[2 * KV_HEADS + blk] = z.astype(BF16)

    u_ref[...] = _dot(h, w_ref[:, u_off:u_off + pool_dim])
    gl = _dot(h, w_ref[:, gate_off:gate_off + GATE_PAD])
    gate_ref[...] = 1.0 / (1.0 + jnp.exp(-gl))


def _project(x, gain, w_r, tables, table_index, tm, pool_dim, chunked):
    rows, d = x.shape
    n_proj = w_r.shape[1]
    n_kv = CACHE_SLOTS * KV_HEADS
    n_w = 2 * KV_HEADS
    row = lambda i: (i, 0)
    fixed = lambda i: (0, 0)
    tab_spec = pl.BlockSpec((tm, LANES), table_index)
    out_shape = [
        jax.ShapeDtypeStruct((rows, ATTN_DIM), BF16),
        jax.ShapeDtypeStruct((rows * n_kv, HEAD_DIM), F32),
        jax.ShapeDtypeStruct((rows * n_w, HEAD_DIM), F32),
        jax.ShapeDtypeStruct((n_w + n_kv // 2, rows, HEAD_DIM), BF16),
        jax.ShapeDtypeStruct((rows, GATE_PAD), F32),
        jax.ShapeDtypeStruct((rows, pool_dim), F32),
    ]
    out_specs = [
        pl.BlockSpec((tm, ATTN_DIM), row),
        pl.BlockSpec((tm * n_kv, HEAD_DIM), row),
        pl.BlockSpec((tm * n_w, HEAD_DIM), row),
        pl.BlockSpec((n_w + n_kv // 2, tm, HEAD_DIM), lambda i: (0, i, 0)),
        pl.BlockSpec((tm, GATE_PAD), row),
        pl.BlockSpec((tm, pool_dim), row),
    ]
    scratch = []
    if chunked:
        out_shape.append(jax.ShapeDtypeStruct((n_kv // 2, rows // CMP_STRIDE, CMP_HALF), BF16))
        out_specs.append(pl.BlockSpec((n_kv // 2, tm // CMP_STRIDE, CMP_HALF), lambda i: (0, i, 0)))
        scratch.append(pltpu.VMEM((tm, HEAD_DIM), F32))
    return pl.pallas_call(
        functools.partial(_proj_kernel, pool_dim=pool_dim, chunked=chunked),
        grid=(rows // tm,),
        in_specs=[
            pl.BlockSpec((tm, d), row),
            pl.BlockSpec((1, d), fixed),
            pl.BlockSpec((d, n_proj), fixed),
            tab_spec, tab_spec, tab_spec,
        ],
        out_specs=tuple(out_specs),
        out_shape=tuple(out_shape),
        scratch_shapes=scratch,
        compiler_params=_params(1),
        name="project",
    )(x, gain, w_r, *tables)


def _rope_tables(pos):
    inv = jnp.power(ROPE_THETA, -jnp.arange(ROT_HALF, dtype=F32) * (2.0 / ROT_DIM))
    ang = pos.astype(F32)[:, None] * inv[None, :]
    cos, sin = jnp.cos(ang), jnp.sin(ang)
    n = pos.shape[0]
    rest = LANES - ROT_DIM
    c = jnp.concatenate([cos, cos, jnp.ones((n, rest), F32)], axis=1)
    sa = jnp.concatenate([-sin, jnp.zeros((n, LANES - ROT_HALF), F32)], axis=1)
    sb = jnp.concatenate([jnp.zeros((n, ROT_HALF), F32), sin, jnp.zeros((n, rest), F32)], axis=1)
    return c, sa, sb


def _gelu_tanh(x):
    return 0.5 * x * (1.0 + jnp.tanh(0.7978845608028654 * (x + 0.044715 * (x * x * x))))


def _compress_core(x, wcat, pos8, w2):
    rows = x.shape[0]
    ab = _dot(x, wcat)
    a = ab[:, :HEAD_DIM]
    b_next = pltpu.roll(ab[:, HEAD_DIM:], rows - 1, 0)
    pa = _dot(pos8[:, :CMP_HALF], wcat)[0:1, :HEAD_DIM]
    pb = _dot(pos8[:, CMP_HALF:], wcat)[0:1, HEAD_DIM:]
    hid = _gelu_tanh(a + b_next + (pa + pb))
    return _dot(hid.astype(BF16), w2)


def _compress_kernel(x_ref, wcat_ref, pos_ref, w2_ref, o_ref):
    g, _, nck, width = x_ref.shape
    x = x_ref[...].reshape(g * nck, width)
    out = _compress_core(x, wcat_ref[0], pos_ref[0], w2_ref[0])
    o_ref[...] = out.astype(BF16).reshape(o_ref.shape)


def _compress_prompt(xc, wcat, pos8, w2, batch, seq):
    nck = seq // CMP_STRIDE
    x = xc.reshape(xc.shape[0], batch, nck, CMP_HALF)
    return pl.pallas_call(
        _compress_kernel,
        grid=(batch, 2),
        in_specs=[
            pl.BlockSpec((KV_HEADS, 1, nck, CMP_HALF), lambda b, s: (s, b, 0, 0)),
            pl.BlockSpec((1, CMP_HALF, 2 * HEAD_DIM), lambda b, s: (s, 0, 0)),
            pl.BlockSpec((1, 8, CMP_IN), lambda b, s: (s, 0, 0)),
            pl.BlockSpec((1, HEAD_DIM, HEAD_DIM), lambda b, s: (s, 0, 0)),
        ],
        out_specs=pl.BlockSpec((1, 1, KV_HEADS, nck, HEAD_DIM), lambda b, s: (b, s, 0, 0, 0)),
        out_shape=jax.ShapeDtypeStruct((batch, 2, KV_HEADS, nck, HEAD_DIM), BF16),
        compiler_params=_params(2),
        name="compress_prompt",
    )(x, wcat, pos8, w2)


def _topk_rows(score, jidx, topk):
    rank = jnp.zeros_like(score)
    for j in range(score.shape[0]):
        bj = score[j:j + 1, :]
        tie = jnp.where(jidx > j, 1.0, 0.0)
        rank = rank + jnp.where(bj > score, 1.0, jnp.where(bj == score, tie, 0.0))
    return jnp.where(rank < topk, 1.0, 0.0)


def _softmax_parts(s, ok):
    s = jnp.where(ok, s, NEG_INF)
    m = jnp.max(s, axis=-1, keepdims=True)
    e = jnp.where(ok, jnp.exp(s - m), 0.0)
    d = jnp.sum(e, axis=-1, keepdims=True)
    return e, jnp.where(d > 0, d, 1.0)


def _attn_prompt_kernel(q_ref, kc_ref, vc_ref, ks_ref, vs_ref, kw_ref, vw_ref, gate_ref,
                        mt_ref, ex_ref, eye_ref, o_ref, msk_ref, *, tq, tk, seq, topk):
    g = pl.program_id(1)
    i = pl.program_id(2)
    t0 = i * tq
    n_sel = mt_ref.shape[0]
    rows = HPG * tq

    q4 = q_ref[...]
    qs = jnp.concatenate([q4[:, h * HEAD_DIM:(h + 1) * HEAD_DIM] for h in range(HPG)], axis=0)

    def tile_heads(x):
        return jnp.concatenate([x] * HPG, axis=0)

    kc = kc_ref[0, 0, 0]
    vc = vc_ref[0, 0, 0]
    ncp = kc.shape[0]
    s = _dot_nt(qs, kc) * SCALE
    tpos_c = t0 + lax.broadcasted_iota(jnp.int32, (tq, ncp), 0)
    blk_end = lax.broadcasted_iota(jnp.int32, (tq, ncp), 1) * CMP_STRIDE + (CMP_LEN - 1)
    ok = tile_heads(jnp.where(blk_end <= tpos_c, 1.0, 0.0)) > 0.5
    e, d = _softmax_parts(s, ok)
    p = e / d
    o_cmp = _dot(p.astype(BF16), vc)

    p_sum = p[0:tq]
    for h in range(1, HPG):
        p_sum = p_sum + p[h * tq:(h + 1) * tq]
    p_hi, p_lo = _split_hi_lo(p_sum)
    mt = mt_ref[...]
    imp = _dot_nt(mt, p_hi) + _dot_nt(mt, p_lo)
    jidx = lax.broadcasted_iota(jnp.int32, (n_sel, tq), 0)
    jt = (t0 + lax.broadcasted_iota(jnp.int32, (n_sel, tq), 1)) // SEL_BLOCK
    forced = jnp.where(jidx == 0, 1.0, jnp.where(jidx == jt, 1.0, jnp.where(jidx == jt - 1, 1.0, 0.0)))
    score = jnp.where(forced > 0.5, FORCE_SCORE, jnp.where(jidx <= jt, imp, -1.0))
    sel_t = _topk_rows(score, jidx, topk).astype(BF16)
    sel = _dot_nt(eye_ref[...], sel_t)
    msk_ref[...] = _dot(sel.astype(BF16), ex_ref[...])

    tpos_s = t0 + lax.broadcasted_iota(jnp.int32, (tq, tk), 0)
    lane_s = lax.broadcasted_iota(jnp.int32, (tq, tk), 1)

    def sel_step(kb, carry):
        m, l, acc = carry
        k0 = pl.multiple_of(kb * tk, tk)
        k = ks_ref[0, pl.ds(k0, tk), :]
        v = vs_ref[0, pl.ds(k0, tk), :]
        s = _dot_nt(qs, k) * SCALE
        okf = jnp.where(k0 + lane_s <= tpos_s, msk_ref[:, pl.ds(k0, tk)], 0.0)
        ok = tile_heads(okf) > 0.5
        s = jnp.where(ok, s, NEG_INF)
        m_new = jnp.maximum(m, jnp.max(s, axis=-1, keepdims=True))
        e = jnp.where(ok, jnp.exp(s - m_new), 0.0)
        alpha = jnp.exp(m - m_new)
        l = alpha * l + jnp.sum(e, axis=-1, keepdims=True)
        acc = alpha * acc + _dot(e.astype(BF16), v)
        return m_new, l, acc

    n_kb = (t0 + tq + tk - 1) // tk
    init = (jnp.full((rows, 1), NEG_INF, F32), jnp.zeros((rows, 1), F32),
            jnp.zeros((rows, HEAD_DIM), F32))
    _, l, acc = lax.fori_loop(0, n_kb, sel_step, init)
    o_sel = acc / jnp.where(l > 0, l, 1.0)

    wk = min(WINDOW + tq, seq)
    ws = pl.multiple_of(jnp.maximum(t0 + tq - wk, 0), tq)
    kw = kw_ref[0, pl.ds(ws, wk), :]
    vw = vw_ref[0, pl.ds(ws, wk), :]
    s = _dot_nt(qs, kw) * SCALE
    tpos_w = t0 + lax.broadcasted_iota(jnp.int32, (tq, wk), 0)
    kpos_w = ws + lax.broadcasted_iota(jnp.int32, (tq, wk), 1)
    okf = jnp.where(kpos_w <= tpos_w, jnp.where(kpos_w > tpos_w - WINDOW, 1.0, 0.0), 0.0)
    e, d = _softmax_parts(s, tile_heads(okf) > 0.5)
    o_win = _dot(e.astype(BF16), vw) / d

    gates = gate_ref[...]
    for h in range(HPG):
        col = (g * HPG + h) * N_BRANCH
        lane = lax.broadcasted_iota(jnp.int32, gates.shape, 1)

        def gate(br):
            return jnp.sum(jnp.where(lane == col + br, gates, 0.0), axis=-1, keepdims=True)

        r = slice(h * tq, (h + 1) * tq)
        o = gate(0) * o_cmp[r] + gate(1) * o_sel[r] + gate(2) * o_win[r]
        o_ref[:, h * HEAD_DIM:(h + 1) * HEAD_DIM] = o.astype(BF16)


def _cmp_to_sel(n_cmp_pad, n_cmp, n_sel):
    cs = np.arange(n_cmp_pad)[:, None] * CMP_STRIDE
    ss = np.arange(n_sel)[None, :] * SEL_BLOCK
    hit = (cs < ss + SEL_BLOCK) & (cs + CMP_LEN > ss) & (np.arange(n_cmp_pad)[:, None] < n_cmp)
    return hit.astype(np.float32)


def _attend_prompt(q, cmp_kv, kva, gates, batch, seq):
    tq = min(128, seq)
    tk = min(256, seq)
    nq = seq // tq
    ncp = seq // CMP_STRIDE
    n_sel = -(-seq // SEL_BLOCK)
    topk = min(SEL_TOPK, n_sel)
    mt = jnp.asarray(_cmp_to_sel(ncp, ncp - 1, n_sel).T, BF16)
    expand = (np.arange(seq)[None, :] // SEL_BLOCK == np.arange(n_sel)[:, None]).astype(np.float32)
    expand = jnp.asarray(expand, BF16)
    eye = jnp.asarray(np.eye(tq, dtype=np.float32), BF16)
    kv_spec = lambda slot: pl.BlockSpec(
        (1, seq, HEAD_DIM), lambda b, g, i: (slot * KV_HEADS + g, b, 0))
    cmp_spec = lambda s: pl.BlockSpec(
        (1, 1, 1, ncp, HEAD_DIM), lambda b, g, i: (b, s, g, 0, 0))
    full = lambda a: pl.BlockSpec(a.shape, lambda b, g, i: (0, 0))
    return pl.pallas_call(
        functools.partial(_attn_prompt_kernel, tq=tq, tk=tk, seq=seq, topk=topk),
        grid=(batch, KV_HEADS, nq),
        in_specs=[
            pl.BlockSpec((tq, HPG * HEAD_DIM), lambda b, g, i: (b * nq + i, g)),
            cmp_spec(0), cmp_spec(1),
            kv_spec(0), kv_spec(1), kv_spec(2), kv_spec(3),
            pl.BlockSpec((tq, GATE_PAD), lambda b, g, i: (b * nq + i, 0)),
            full(mt), full(expand), full(eye),
        ],
        out_specs=pl.BlockSpec((tq, HPG * HEAD_DIM), lambda b, g, i: (b * nq + i, g)),
        out_shape=jax.ShapeDtypeStruct((batch * seq, ATTN_DIM), BF16),
        scratch_shapes=[pltpu.VMEM((tq, seq), F32)],
        compiler_params=_params(3),
        name="attend_prompt",
    )(q, cmp_kv, cmp_kv, kva, kva, kva, kva, gates, mt, expand, eye)


def _attn_sample_kernel(pt_ref, cache_ref, q_ref, kvn_ref, wn_ref, win_ref, gate_ref,
                        wcat_ref, pos_ref, w2_ref, ms_ref, ek_ref, o_ref, buf, sem,
                        *, n_pages, page, past, topk, n_sel):
    b = pl.program_id(0)
    nb = pl.num_programs(0)
    n_kv = CACHE_SLOTS * KV_HEADS
    n_w = 2 * KV_HEADS
    nch = past // CMP_STRIDE
    wb = win_ref.shape[1] // n_w
    page_rows = page * n_kv

    def page_copy(seq_idx, slot, pi):
        src0 = pl.multiple_of(pt_ref[seq_idx, pi] * page_rows, page_rows)
        return pltpu.make_async_copy(
            cache_ref.at[pl.ds(src0, page_rows), :],
            buf.at[slot, pl.ds(pi * page_rows, page_rows), :], sem.at[slot])

    @pl.when(b == 0)
    def _():
        for pi in range(n_pages):
            page_copy(0, 0, pi).start()

    @pl.when(b + 1 < nb)
    def _():
        for pi in range(n_pages):
            page_copy(b + 1, (b + 1) % 2, pi).start()

    slot = b % 2
    for pi in range(n_pages):
        page_copy(b, slot, pi).wait()

    def token_rows(cache_slot, g, first, count, step):
        return buf[slot, pl.ds(first * n_kv + cache_slot * KV_HEADS + g, count, stride=step * n_kv), :]

    row8 = lax.broadcasted_iota(jnp.int32, (N_HEADS, 1), 0)
    in_g0 = row8 < HPG

    def by_group(x0, x1):
        return jnp.where(in_g0, x0, x1)

    def compressed(cache_slot):
        x = jnp.concatenate(
            [jnp.concatenate([token_rows(cache_slot, g, r, nch, CMP_STRIDE).astype(BF16)
                              for r in range(CMP_STRIDE)], axis=1)
             for g in range(KV_HEADS)], axis=0)
        return _compress_core(x, wcat_ref[cache_slot], pos_ref[cache_slot],
                              w2_ref[cache_slot]).astype(BF16)

    kc = compressed(0)
    vc = compressed(1)

    q8 = q_ref[0]
    qf = q8.astype(F32)

    lane_n = lax.broadcasted_iota(jnp.int32, (N_HEADS, nch), 1)
    s = by_group(_dot_nt(q8, kc[:nch]), _dot_nt(q8, kc[nch:])) * SCALE
    ok = lane_n * CMP_STRIDE + (CMP_LEN - 1) <= past
    e, d = _softmax_parts(s, ok)
    p = e / d
    pb = p.astype(BF16)
    o_cmp = by_group(_dot(pb, vc[:nch]), _dot(pb, vc[nch:]))

    p_g = [jnp.sum(p[g * HPG:(g + 1) * HPG], axis=0, keepdims=True) for g in range(KV_HEADS)]
    p2 = jnp.concatenate(p_g + [jnp.zeros((N_HEADS - KV_HEADS, nch), F32)], axis=0)
    p_hi, p_lo = _split_hi_lo(p2)
    imp = _dot(p_hi, ms_ref[...]) + _dot(p_lo, ms_ref[...])
    jl = lax.broadcasted_iota(jnp.int32, (N_HEADS, LANES), 1)
    jt = past // SEL_BLOCK
    forced = jnp.where(jl == 0, 1.0, jnp.where(jl == jt, 1.0, jnp.where(jl == jt - 1, 1.0, 0.0)))
    score = jnp.where(forced > 0.5, FORCE_SCORE, jnp.where(jl <= jt, imp, -1.0))
    score = jnp.where(jl < n_sel, score, -2.0)
    ii = lax.broadcasted_iota(jnp.int32, (LANES, LANES), 0)
    jj = lax.broadcasted_iota(jnp.int32, (LANES, LANES), 1)
    sel_rows = []
    for g in range(KV_HEADS):
        srow = jnp.broadcast_to(score[g:g + 1, :], (LANES, LANES))
        scol = jnp.sum(jnp.where(ii == jj, srow, 0.0), axis=1, keepdims=True)
        tie = jnp.where(ii < jj, 1.0, 0.0)
        beats = jnp.where(scol > srow, 1.0, jnp.where(scol == srow, tie, 0.0))
        rank = jnp.sum(beats, axis=0, keepdims=True)
        sel_rows.append(jnp.where(rank < topk, 1.0, 0.0))
    sel2 = jnp.concatenate(sel_rows + [jnp.zeros((N_HEADS - KV_HEADS, LANES), F32)], axis=0)
    key_ok2 = _dot(sel2.astype(BF16), ek_ref[...])
    key_ok = by_group(key_ok2[0:1], key_ok2[1:2])
    new_ok = by_group(*[jnp.sum(jnp.where(jl[0:1] == jt, sel2[g:g + 1], 0.0), axis=1, keepdims=True)
                        for g in range(KV_HEADS)])

    def new_row(ref, idx0):
        x = by_group(ref[0, idx0:idx0 + 1, :], ref[0, idx0 + 1:idx0 + 2, :])
        return x.astype(BF16).astype(F32)

    def attend(keys, vals, ok, k_new, v_new, new_ok):
        s = by_group(_dot_nt(q8, keys[0]), _dot_nt(q8, keys[1])) * SCALE
        s_new = jnp.sum(qf * k_new, axis=-1, keepdims=True) * SCALE
        s = jnp.where(ok, s, NEG_INF)
        s_new = jnp.where(new_ok, s_new, NEG_INF)
        m = jnp.maximum(jnp.max(s, axis=-1, keepdims=True), s_new)
        e = jnp.where(ok, jnp.exp(s - m), 0.0)
        e_new = jnp.where(new_ok, jnp.exp(s_new - m), 0.0)
        d = jnp.sum(e, axis=-1, keepdims=True) + e_new
        eb = e.astype(BF16)
        o = by_group(_dot(eb, vals[0]), _dot(eb, vals[1])) + e_new * v_new
        return o / jnp.where(d > 0, d, 1.0)

    def cached(cache_slot, g):
        return token_rows(cache_slot, g, 0, past, 1).astype(BF16)

    o_sel = attend([cached(2, g) for g in range(KV_HEADS)],
                   [cached(3, g) for g in range(KV_HEADS)], key_ok > 0.5,
                   new_row(kvn_ref, 2 * KV_HEADS), new_row(kvn_ref, 3 * KV_HEADS), new_ok > 0.5)

    keys = [win_ref[0, pl.ds(g, wb, stride=n_w), :].astype(BF16) for g in range(KV_HEADS)]
    vals = [win_ref[0, pl.ds(KV_HEADS + g, wb, stride=n_w), :].astype(BF16)
            for g in range(KV_HEADS)]
    kpos = past - wb + lax.broadcasted_iota(jnp.int32, (N_HEADS, wb), 1)
    ok = kpos > past - WINDOW
    o_win = attend(keys, vals, ok, new_row(wn_ref, 0), new_row(wn_ref, KV_HEADS), row8 >= 0)

    gates = jnp.broadcast_to(gate_ref[0], (N_HEADS, GATE_PAD))
    lane = lax.broadcasted_iota(jnp.int32, (N_HEADS, GATE_PAD), 1)

    def gate(br):
        return jnp.sum(jnp.where(lane == row8 * N_BRANCH + br, gates, 0.0), axis=-1, keepdims=True)

    o_ref[0] = (gate(0) * o_cmp + gate(1) * o_sel + gate(2) * o_win).astype(BF16)


def _attend_sample(page_table, cache, q, kv_new, wkv_new, win_state, gates, wcat, pos8, w2):
    nb, n_pages = page_table.shape
    n_phys, page = cache.shape[:2]
    n_kv = CACHE_SLOTS * KV_HEADS
    n_w = 2 * KV_HEADS
    past = n_pages * page
    nch = past // CMP_STRIDE
    wb = win_state.shape[1]
    n_sel = -(-(past + 1) // SEL_BLOCK)
    n_cmp = (past + 1) // CMP_STRIDE - 1
    topk = min(SEL_TOPK, n_sel)
    assert n_sel <= LANES and nch * CMP_STRIDE == past
    ms = np.zeros((nch, LANES), np.float32)
    ms[:, :n_sel] = _cmp_to_sel(nch, n_cmp, n_sel)
    ek = (np.arange(past)[None, :] // SEL_BLOCK == np.arange(LANES)[:, None])
    ms = jnp.asarray(ms, BF16)
    ek = jnp.asarray(ek.astype(np.float32), BF16)
    seq3 = lambda n: pl.BlockSpec((1, n, HEAD_DIM), lambda b, pt: (b, 0, 0))
    full3 = lambda a: pl.BlockSpec(a.shape, lambda b, pt: (0, 0, 0))
    full2 = lambda a: pl.BlockSpec(a.shape, lambda b, pt: (0, 0))
    grid_spec = pltpu.PrefetchScalarGridSpec(
        num_scalar_prefetch=1,
        grid=(nb,),
        in_specs=[
            pl.BlockSpec(memory_space=pl.ANY),
            seq3(N_HEADS), seq3(n_kv), seq3(n_w), seq3(wb * n_w),
            pl.BlockSpec((1, 1, GATE_PAD), lambda b, pt: (b, 0, 0)),
            full3(wcat), full3(pos8), full3(w2), full2(ms), full2(ek),
        ],
        out_specs=seq3(N_HEADS),
        scratch_shapes=[pltpu.VMEM((2, past * n_kv, HEAD_DIM), F32),
                        pltpu.SemaphoreType.DMA((2,))],
    )
    return pl.pallas_call(
        functools.partial(_attn_sample_kernel, n_pages=n_pages, page=page, past=past,
                          topk=topk, n_sel=n_sel),
        grid_spec=grid_spec,
        out_shape=jax.ShapeDtypeStruct((nb, N_HEADS, HEAD_DIM), BF16),
        compiler_params=_params(1),
        name="attend_sample",
    )(page_table, cache.reshape(n_phys * page * n_kv, HEAD_DIM),
      q.reshape(nb, N_HEADS, HEAD_DIM), kv_new.reshape(nb, n_kv, HEAD_DIM),
      wkv_new.reshape(nb, n_w, HEAD_DIM), win_state.reshape(nb, wb * n_w, HEAD_DIM),
      gates.reshape(nb, 1, GATE_PAD), wcat, pos8, w2, ms, ek)


def _mix_tail(o, diffs, x, pw_ref, ps_ref, wo_ref, gpost_ref, gpre_ref, y1_ref, h2_ref):
    gw = diffs[0].shape[1]
    ys = [(_dot(diffs[g].astype(BF16), pw_ref[g]) * ps_ref[:, g * gw:(g + 1) * gw]).astype(BF16)
          for g in range(POOL_GROUPS)]
    cat = jnp.concatenate([o] + ys, axis=1)
    m = _dot(cat, wo_ref[...])
    y1 = x + _rms(m, gpost_ref[...])
    y1_ref[...] = y1
    h2_ref[...] = _rms(y1, gpre_ref[...]).astype(BF16)


def _mix_prompt_kernel(o_ref, u_ref, halo_ref, x_ref, pw_ref, ps_ref, wo_ref, gpost_ref, gpre_ref,
                       y1_ref, h2_ref, *, tm):
    i = pl.program_id(1)
    halo_rows = halo_ref.shape[0]
    halo = jnp.where(i > 0, halo_ref[...], 0.0)
    u = u_ref[...]
    uext = jnp.concatenate([halo, u], axis=0)
    n_ext = uext.shape[0]
    gw = u.shape[1] // POOL_GROUPS
    tpos = i * tm + lax.broadcasted_iota(jnp.int32, (tm, 1), 0)
    diffs = []
    for g, w in enumerate(POOL_WINDOWS):
        s = uext[:, g * gw:(g + 1) * gw]
        k = 1
        while k < w:
            s = s + pltpu.roll(s, k, 0)
            k *= 2
        cnt = jnp.minimum(w, tpos + 1).astype(F32)
        diffs.append(s[halo_rows:n_ext] / cnt - u[:, g * gw:(g + 1) * gw])
    _mix_tail(o_ref[...], diffs, x_ref[...], pw_ref, ps_ref, wo_ref, gpost_ref, gpre_ref,
              y1_ref, h2_ref)


def _mix_sample_kernel(o_ref, u_ref, st_ref, x_ref, pw_ref, ps_ref, wo_ref, gpost_ref, gpre_ref,
                       y1_ref, h2_ref, *, past):
    u = u_ref[...]
    c = u.shape[1]
    gw = c // POOL_GROUPS
    n_hist = st_ref.shape[1] // c
    diffs = []
    for g, w in enumerate(POOL_WINDOWS):
        un = u[:, g * gw:(g + 1) * gw]
        s = un
        for back in range(1, w):
            r = n_hist - back
            s = s + st_ref[:, r * c + g * gw:r * c + (g + 1) * gw]
        diffs.append(s / float(min(w, past + 1)) - un)
    _mix_tail(o_ref[...], diffs, x_ref[...], pw_ref, ps_ref, wo_ref, gpost_ref, gpre_ref,
              y1_ref, h2_ref)


def _mix_specs(tm, d, c, pool_w, idx):
    fixed2 = lambda *a: (0, 0)
    fixed3 = lambda *a: (0, 0, 0)
    weights = [
        pl.BlockSpec(pool_w.shape, fixed3),
        pl.BlockSpec((1, c), fixed2),
        pl.BlockSpec((d, d), fixed2),
        pl.BlockSpec((1, d), fixed2),
        pl.BlockSpec((1, d), fixed2),
    ]
    outs = (pl.BlockSpec((tm, d), idx), pl.BlockSpec((tm, d), idx))
    return weights, outs


def _mix_prompt(o, u, x, pool_w, pool_scale, w_o, g_post, g_pre, batch, seq):
    rows, d = x.shape
    c = u.shape[1]
    tm = min(256, seq)
    nt = seq // tm
    halo = 16
    assert halo >= POOL_BUF and seq % tm == 0 and tm % halo == 0
    idx = lambda b, i: (b * nt + i, 0)
    halo_idx = lambda b, i: (jnp.maximum((b * nt + i) * (tm // halo) - 1, 0), 0)
    weights, outs = _mix_specs(tm, d, c, pool_w, idx)
    return pl.pallas_call(
        functools.partial(_mix_prompt_kernel, tm=tm),
        grid=(batch, nt),
        in_specs=[pl.BlockSpec((tm, ATTN_DIM), idx), pl.BlockSpec((tm, c), idx),
                  pl.BlockSpec((halo, c), halo_idx), pl.BlockSpec((tm, d), idx)] + weights,
        out_specs=outs,
        out_shape=(jax.ShapeDtypeStruct((rows, d), F32), jax.ShapeDtypeStruct((rows, d), BF16)),
        compiler_params=_params(2),
        name="mix_prompt",
    )(o, u, u, x, pool_w, pool_scale, w_o, g_post, g_pre)


def _mix_sample(o, u, pool_state, x, pool_w, pool_scale, w_o, g_post, g_pre, past):
    rows, d = x.shape
    c = u.shape[1]
    tm = rows
    idx = lambda i: (i, 0)
    weights, outs = _mix_specs(tm, d, c, pool_w, idx)
    st = pool_state.reshape(rows, -1)
    return pl.pallas_call(
        functools.partial(_mix_sample_kernel, past=past),
        grid=(rows // tm,),
        in_specs=[pl.BlockSpec((tm, ATTN_DIM), idx), pl.BlockSpec((tm, c), idx),
                  pl.BlockSpec((tm, st.shape[1]), idx), pl.BlockSpec((tm, d), idx)] + weights,
        out_specs=outs,
        out_shape=(jax.ShapeDtypeStruct((rows, d), F32), jax.ShapeDtypeStruct((rows, d), BF16)),
        compiler_params=_params(1),
        name="mix_sample",
    )(o, u, st, x, pool_w, pool_scale, w_o, g_post, g_pre)


def _mlp_kernel(h_ref, wu_ref, wd_ref, y1_ref, g_ref, y_ref, acc_ref):
    j = pl.program_id(1)

    @pl.when(j == 0)
    def _():
        acc_ref[...] = jnp.zeros_like(acc_ref)

    a = jnp.maximum(_dot(h_ref[...], wu_ref[...]), 0.0)
    acc_ref[...] += _dot((a * a).astype(BF16), wd_ref[...])

    @pl.when(j == pl.num_programs(1) - 1)
    def _():
        y_ref[...] = y1_ref[...] + _rms(acc_ref[...], g_ref[...])


def _mlp(h2, y1, w_up, w_down, gain, tm):
    rows, d = y1.shape
    ff = w_up.shape[1]
    tf = min(1024, ff)
    row = lambda i, j: (i, 0)
    return pl.pallas_call(
        _mlp_kernel,
        grid=(rows // tm, ff // tf),
        in_specs=[
            pl.BlockSpec((tm, d), row),
            pl.BlockSpec((d, tf), lambda i, j: (0, j)),
            pl.BlockSpec((tf, d), lambda i, j: (j, 0)),
            pl.BlockSpec((tm, d), row),
            pl.BlockSpec((1, d), lambda i, j: (0, 0)),
        ],
        out_specs=pl.BlockSpec((tm, d), row),
        out_shape=jax.ShapeDtypeStruct((rows, d), F32),
        scratch_shapes=[pltpu.VMEM((tm, d), F32)],
        compiler_params=_params(2),
        name="mlp",
    )(h2, w_up, w_down, y1, gain)


def _layer_weights(w_in, cmp_pos_k, cmp_w1_k, cmp_w2_k, cmp_pos_v, cmp_w1_v, cmp_w2_v,
                   pool_w, w_o, w_up, w_down, pool_dim):
    gate_lo = WKV_OFF + 2 * KV_DIM
    gate_hi = gate_lo + N_BRANCH * N_HEADS
    w_r = jnp.concatenate(
        [w_in[:, :gate_lo], w_in[:, gate_hi:gate_hi + pool_dim],
         jnp.pad(w_in[:, gate_lo:gate_hi], ((0, 0), (0, GATE_PAD - N_BRANCH * N_HEADS)))],
        axis=1).astype(BF16)
    wcat = jnp.stack([jnp.concatenate([w1[:CMP_HALF], w1[CMP_HALF:]], axis=1)
                      for w1 in (cmp_w1_k, cmp_w1_v)]).astype(BF16)
    pos8 = jnp.stack([jnp.pad(p.reshape(1, CMP_IN), ((0, 7), (0, 0)))
                      for p in (cmp_pos_k, cmp_pos_v)]).astype(BF16)
    w2 = jnp.stack([cmp_w2_k, cmp_w2_v]).astype(BF16)
    return (w_r, wcat, pos8, w2, pool_w.astype(BF16), w_o.astype(BF16),
            w_up.astype(BF16), w_down.astype(BF16))


def kernel(x_prompt, x_sample, cache_kv, state_win_kv, state_pool, page_table, norm_mix_pre, w_in,
           cmp_pos_k, cmp_w1_k, cmp_w2_k, cmp_pos_v, cmp_w1_v, cmp_w2_v, pool_w, pool_scale, w_o,
           norm_mix_post, norm_mlp_pre, w_up, w_down, norm_mlp_post):
    batch, seq, d = x_prompt.shape
    nb, dec_seq, _ = x_sample.shape
    depth = w_in.shape[0]
    pool_dim = d - ATTN_DIM
    n_pages = page_table.shape[1]
    page = cache_kv.shape[2]
    past = n_pages * page
    wb = state_win_kv.shape[2]
    assert dec_seq == 1 and seq >= POOL_BUF and seq % CMP_STRIDE == 0
    assert w_in.shape[2] == ATTN_DIM + 6 * KV_DIM + N_BRANCH * N_HEADS + pool_dim

    tm_p = min(256, seq)
    tabs_p = _rope_tables(jnp.arange(seq, dtype=jnp.int32))
    tabs_s = _rope_tables(jnp.full((nb,), past, jnp.int32))
    nt_p = seq // tm_p

    y_p = x_prompt.reshape(batch * seq, d)
    y_s = x_sample.reshape(nb, d)
    kv_p, kv_s, win_p, win_s, pool_p, pool_s = [], [], [], [], [], []
    row_vec = lambda v: v.reshape(1, -1)
    for l in range(depth):
        w_r, wcat, pos8, w2, pw, wo, wu, wd = _layer_weights(
            w_in[l], cmp_pos_k[l], cmp_w1_k[l], cmp_w2_k[l], cmp_pos_v[l], cmp_w1_v[l],
            cmp_w2_v[l], pool_w[l], w_o[l], w_up[l], w_down[l], pool_dim)
        g_pre, g_post = row_vec(norm_mix_pre[l]), row_vec(norm_mix_post[l])
        g_mlp_pre, g_mlp_post = row_vec(norm_mlp_pre[l]), row_vec(norm_mlp_post[l])
        ps = row_vec(pool_scale[l])

        q, kv, wkv, kva, gates, u, xc = _project(
            y_p, g_pre, w_r, tabs_p, lambda i: (i % nt_p, 0), tm_p, pool_dim, True)
        cmp_kv = _compress_prompt(xc, wcat, pos8, w2, batch, seq)
        o = _attend_prompt(q, cmp_kv, kva, gates, batch, seq)
        y1, h2 = _mix_prompt(o, u, y_p, pw, ps, wo, g_post, g_mlp_pre, batch, seq)
        y_p = _mlp(h2, y1, wu, wd, g_mlp_post, min(512, batch * seq))
        kv_p.append(kv.reshape(batch, seq, CACHE_SLOTS, KV_HEADS, HEAD_DIM))
        wp = min(WINDOW, seq)
        win_p.append(wkv.reshape(batch, seq, 2, KV_HEADS, HEAD_DIM)[:, seq - wp:])
        pool_p.append(u.reshape(batch, seq, pool_dim)[:, seq - POOL_BUF:])

        q, kv, wkv, _, gates, u = _project(
            y_s, g_pre, w_r, tabs_s, lambda i: (i, 0), nb, pool_dim, False)
        o = _attend_sample(page_table, cache_kv[l], q, kv, wkv, state_win_kv[l], gates,
                           wcat, pos8, w2)
        y1, h2 = _mix_sample(o.reshape(nb, ATTN_DIM), u, state_pool[l], y_s, pw, ps, wo,
                             g_post, g_mlp_pre, past)
        y_s = _mlp(h2, y1, wu, wd, g_mlp_post, nb)
        kv_s.append(kv.reshape(nb, 1, CACHE_SLOTS, KV_HEADS, HEAD_DIM))
        new_win = wkv.reshape(nb, 1, 2, KV_HEADS, HEAD_DIM)
        win_s.append(jnp.concatenate([state_win_kv[l], new_win], axis=1)[:, 1:])
        pool_s.append(jnp.concatenate([state_pool[l], u[:, None]], axis=1)[:, 1:])

    return (y_p.reshape(batch, seq, d), y_s.reshape(nb, 1, d),
            jnp.stack(kv_p), jnp.stack(kv_s), jnp.stack(win_p), jnp.stack(win_s),
            jnp.stack(pool_p), jnp.stack(pool_s))
```

```python
import functools

import numpy as np
import jax
import jax.numpy as jnp
from jax import lax
from jax.experimental import pallas as pl
from jax.experimental.pallas import tpu as pltpu

N_HEADS = 8
HEAD_DIM = 128
KV_HEADS = 2
HPG = N_HEADS // KV_HEADS
ATTN_DIM = N_HEADS * HEAD_DIM
KV_DIM = KV_HEADS * HEAD_DIM
N_BRANCH = 3
POOL_WINDOWS = (2, 4, 8, 16)
POOL_GROUPS = len(POOL_WINDOWS)
POOL_BUF = max(POOL_WINDOWS) - 1
ROT_DIM = HEAD_DIM // 4
ROT_HALF = ROT_DIM // 2
ROPE_THETA = 500000.0
CMP_LEN = 32
CMP_STRIDE = 16
SEL_BLOCK = 64
SEL_TOPK = 16
WINDOW = 512
EPS = 1e-6
SCALE = HEAD_DIM ** -0.5
FORCE_SCORE = 1e4
NEG_INF = -1e30

LANES = 128
CACHE_SLOTS = 4
ROW_W = CACHE_SLOTS * KV_DIM
CHUNK_W = CMP_STRIDE * ROW_W
CMP_IN = CMP_LEN * HEAD_DIM
CMP_HALF = CMP_STRIDE * HEAD_DIM
GATE_PAD = LANES
VMEM_LIMIT = 56 * 1024 * 1024

BF16 = jnp.bfloat16
F32 = jnp.float32


def _dot(a, b):
    return jnp.dot(a, b, preferred_element_type=F32)


def _dot_nt(a, b):
    return lax.dot_general(a, b, (((1,), (1,)), ((), ())), preferred_element_type=F32)


def _rms(x, g):
    return x * lax.rsqrt(jnp.mean(x * x, axis=-1, keepdims=True) + EPS) * g


def _params(n_axes):
    return pltpu.CompilerParams(
        dimension_semantics=("arbitrary",) * n_axes, vmem_limit_bytes=VMEM_LIMIT)


def _split_hi_lo(x):
    hi = x.astype(BF16)
    lo = (x - hi.astype(F32)).astype(BF16)
    return hi, lo


Q_OFF, KV_OFF, WKV_OFF = 0, ATTN_DIM, ATTN_DIM + 4 * KV_DIM


def _proj_kernel(x_ref, g_ref, w_ref, cos_ref, sa_ref, sb_ref, *rest, pool_dim, chunked):
    if chunked:
        q_ref, kv_ref, wkv_ref, kva_ref, gate_ref, u_ref, xc_ref, tmp_ref = rest
    else:
        q_ref, kv_ref, wkv_ref, kva_ref, gate_ref, u_ref = rest
    tm = x_ref.shape[0]
    u_off = WKV_OFF + 2 * KV_DIM
    gate_off = u_off + pool_dim
    h = _rms(x_ref[...], g_ref[...]).astype(BF16)
    cos, sa, sb = cos_ref[...], sa_ref[...], sb_ref[...]

    def rope(z):
        return (z * cos + pltpu.roll(z, LANES - ROT_HALF, 1) * sa
                + pltpu.roll(z, ROT_HALF, 1) * sb)

    zq = _dot(h, w_ref[:, Q_OFF:Q_OFF + ATTN_DIM])
    for hd in range(N_HEADS):
        sl = slice(hd * HEAD_DIM, (hd + 1) * HEAD_DIM)
        q_ref[:, sl] = rope(zq[:, sl]).astype(BF16)

    n_kv = CACHE_SLOTS * KV_HEADS
    zkv = _dot(h, w_ref[:, KV_OFF:KV_OFF + 4 * KV_DIM])
    for blk in range(n_kv):
        z = zkv[:, blk * HEAD_DIM:(blk + 1) * HEAD_DIM]
        if (blk // KV_HEADS) % 2 == 0:
            z = rope(z)
        kv_ref[pl.ds(blk, tm, stride=n_kv), :] = z
        if blk >= 2 * KV_HEADS:
            kva_ref[blk - 2 * KV_HEADS] = z.astype(BF16)
        elif chunked:
            tmp_ref[...] = z
            for r in range(CMP_STRIDE):
                xc_ref[blk, :, r * HEAD_DIM:(r + 1) * HEAD_DIM] = (
                    tmp_ref[pl.ds(r, tm // CMP_STRIDE, stride=CMP_STRIDE), :].astype(BF16))

    n_w = 2 * KV_HEADS
    zw = _dot(h, w_ref[:, WKV_OFF:WKV_OFF + 2 * KV_DIM])
    for blk in range(n_w):
        z = zw[:, blk * HEAD_DIM:(blk + 1) * HEAD_DIM]
        if blk < KV_HEADS:
            z = rope(z)
        wkv_ref[pl.ds(blk, tm, stride=n_w), :] = z
        kva_ref[2 * KV_HEADS + blk] = z.astype(BF16)

    u_ref[...] = _dot(h, w_ref[:, u_off:u_off + pool_dim])
    gl = _dot(h, w_ref[:, gate_off:gate_off + GATE_PAD])
    gate_ref[...] = 1.0 / (1.0 + jnp.exp(-gl))


def _project(x, gain, w_r, tables, table_index, tm, pool_dim, chunked):
    rows, d = x.shape
    n_proj = w_r.shape[1]
    n_kv = CACHE_SLOTS * KV_HEADS
    n_w = 2 * KV_HEADS
    row = lambda i: (i, 0)
    fixed = lambda i: (0, 0)
    tab_spec = pl.BlockSpec((tm, LANES), table_index)
    out_shape = [
        jax.ShapeDtypeStruct((rows, ATTN_DIM), BF16),
        jax.ShapeDtypeStruct((rows * n_kv, HEAD_DIM), F32),
        jax.ShapeDtypeStruct((rows * n_w, HEAD_DIM), F32),
        jax.ShapeDtypeStruct((n_w + n_kv // 2, rows, HEAD_DIM), BF16),
        jax.ShapeDtypeStruct((rows, GATE_PAD), F32),
        jax.ShapeDtypeStruct((rows, pool_dim), F32),
    ]
    out_specs = [
        pl.BlockSpec((tm, ATTN_DIM), row),
        pl.BlockSpec((tm * n_kv, HEAD_DIM), row),
        pl.BlockSpec((tm * n_w, HEAD_DIM), row),
        pl.BlockSpec((n_w + n_kv // 2, tm, HEAD_DIM), lambda i: (0, i, 0)),
        pl.BlockSpec((tm, GATE_PAD), row),
        pl.BlockSpec((tm, pool_dim), row),
    ]
    scratch = []
    if chunked:
        out_shape.append(jax.ShapeDtypeStruct((n_kv // 2, rows // CMP_STRIDE, CMP_HALF), BF16))
        out_specs.append(pl.BlockSpec((n_kv // 2, tm // CMP_STRIDE, CMP_HALF), lambda i: (0, i, 0)))
        scratch.append(pltpu.VMEM((tm, HEAD_DIM), F32))
    return pl.pallas_call(
        functools.partial(_proj_kernel, pool_dim=pool_dim, chunked=chunked),
        grid=(rows // tm,),
        in_specs=[
            pl.BlockSpec((tm, d), row),
            pl.BlockSpec((1, d), fixed),
            pl.BlockSpec((d, n_proj), fixed),
            tab_spec, tab_spec, tab_spec,
        ],
        out_specs=tuple(out_specs),
        out_shape=tuple(out_shape),
        scratch_shapes=scratch,
        compiler_params=_params(1),
        name="project",
    )(x, gain, w_r, *tables)


def _rope_tables(pos):
    inv = jnp.power(ROPE_THETA, -jnp.arange(ROT_HALF, dtype=F32) * (2.0 / ROT_DIM))
    ang = pos.astype(F32)[:, None] * inv[None, :]
    cos, sin = jnp.cos(ang), jnp.sin(ang)
    n = pos.shape[0]
    rest = LANES - ROT_DIM
    c = jnp.concatenate([cos, cos, jnp.ones((n, rest), F32)], axis=1)
    sa = jnp.concatenate([-sin, jnp.zeros((n, LANES - ROT_HALF), F32)], axis=1)
    sb = jnp.concatenate([jnp.zeros((n, ROT_HALF), F32), sin, jnp.zeros((n, rest), F32)], axis=1)
    return c, sa, sb


def _gelu_tanh(x):
    return 0.5 * x * (1.0 + jnp.tanh(0.7978845608028654 * (x + 0.044715 * (x * x * x))))


def _compress_core(x, wcat, pos8, w2):
    rows = x.shape[0]
    ab = _dot(x, wcat)
    a = ab[:, :HEAD_DIM]
    b_next = pltpu.roll(ab[:, HEAD_DIM:], rows - 1, 0)
    pa = _dot(pos8[:, :CMP_HALF], wcat)[0:1, :HEAD_DIM]
    pb = _dot(pos8[:, CMP_HALF:], wcat)[0:1, HEAD_DIM:]
    hid = _gelu_tanh(a + b_next + (pa + pb))
    return _dot(hid.astype(BF16), w2)


def _compress_kernel(x_ref, wcat_ref, pos_ref, w2_ref, o_ref):
    g, _, nck, width = x_ref.shape
    x = x_ref[...].reshape(g * nck, width)
    out = _compress_core(x, wcat_ref[0], pos_ref[0], w2_ref[0])
    o_ref[...] = out.astype(BF16).reshape(o_ref.shape)


def _compress_prompt(xc, wcat, pos8, w2, batch, seq):
    nck = seq // CMP_STRIDE
    x = xc.reshape(xc.shape[0], batch, nck, CMP_HALF)
    return pl.pallas_call(
        _compress_kernel,
        grid=(batch, 2),
        in_specs=[
            pl.BlockSpec((KV_HEADS, 1, nck, CMP_HALF), lambda b, s: (s, b, 0, 0)),
            pl.BlockSpec((1, CMP_HALF, 2 * HEAD_DIM), lambda b, s: (s, 0, 0)),
            pl.BlockSpec((1, 8, CMP_IN), lambda b, s: (s, 0, 0)),
            pl.BlockSpec((1, HEAD_DIM, HEAD_DIM), lambda b, s: (s, 0, 0)),
        ],
        out_specs=pl.BlockSpec((1, 1, KV_HEADS, nck, HEAD_DIM), lambda b, s: (b, s, 0, 0, 0)),
        out_shape=jax.ShapeDtypeStruct((batch, 2, KV_HEADS, nck, HEAD_DIM), BF16),
        compiler_params=_params(2),
        name="compress_prompt",
    )(x, wcat, pos8, w2)


def _topk_rows(score, jidx, topk):
    rank = jnp.zeros_like(score)
    for j in range(score.shape[0]):
        bj = score[j:j + 1, :]
        tie = jnp.where(jidx > j, 1.0, 0.0)
        rank = rank + jnp.where(bj > score, 1.0, jnp.where(bj == score, tie, 0.0))
    return jnp.where(rank < topk, 1.0, 0.0)


def _softmax_parts(s, ok):
    s = jnp.where(ok, s, NEG_INF)
    m = jnp.max(s, axis=-1, keepdims=True)
    e = jnp.where(ok, jnp.exp(s - m), 0.0)
    d = jnp.sum(e, axis=-1, keepdims=True)
    return e, jnp.where(d > 0, d, 1.0)


EXP2_SCALE = SCALE * 1.4426950408889634
HEAD_PAIRS = HPG // 2


def _attn_prompt_kernel(q_ref, kc_ref, vc_ref, ks_ref, vs_ref, kw_ref, vw_ref, gate_ref,
                        mt_ref, eye_ref, o_ref,
                        vct_ref, vst_ref, vwt_ref, sel_ref, gt_ref, acc_ref,
                        *, tq, tk, seq, topk):
    g = pl.program_id(1)
    i = pl.program_id(2)
    t0 = i * tq
    n_sel = mt_ref.shape[0]
    pair_w = 2 * tq
    eye = eye_ref[...]

    @pl.when(i == 0)
    def _():
        vct_ref[...] = _dot_nt(eye, vc_ref[0, 0, 0]).astype(BF16)
        vst_ref[...] = _dot_nt(eye, vs_ref[0]).astype(BF16)
        vwt_ref[...] = _dot_nt(eye, vw_ref[0]).astype(BF16)

    q4 = q_ref[...]
    q_pairs = [jnp.concatenate([q4[:, (2 * hp) * HEAD_DIM:(2 * hp + 1) * HEAD_DIM],
                                q4[:, (2 * hp + 1) * HEAD_DIM:(2 * hp + 2) * HEAD_DIM]], axis=0)
               for hp in range(HEAD_PAIRS)]

    def both_heads(x):
        return jnp.concatenate([x, x], axis=1)

    def tpos(n_keys):
        return t0 + lax.broadcasted_iota(jnp.int32, (n_keys, tq), 1)

    def kidx(n_keys):
        return lax.broadcasted_iota(jnp.int32, (n_keys, tq), 0)

    pairs = range(HEAD_PAIRS)

    kc = kc_ref[0, 0, 0]
    ncp = kc.shape[0]
    wk = min(WINDOW + tq, seq)
    ws = pl.multiple_of(jnp.maximum(t0 + tq - wk, 0), tq)
    kw = kw_ref[0, pl.ds(ws, wk), :]
    s_cmp = [_dot_nt(kc, q_pairs[hp]) for hp in pairs]
    s_win = [_dot_nt(kw, q_pairs[hp]) for hp in pairs]

    ok = both_heads(jnp.where(kidx(ncp) * CMP_STRIDE + (CMP_LEN - 1) <= tpos(ncp), 1.0, 0.0)) > 0.5
    p_cmp, p_sum = [], None
    for hp in pairs:
        s = jnp.where(ok, s_cmp[hp], NEG_INF)
        m = jnp.max(s, axis=0, keepdims=True)
        e = jnp.where(ok, jnp.exp2((s - m) * EXP2_SCALE), 0.0)
        d = jnp.sum(e, axis=0, keepdims=True)
        p = e / jnp.where(d > 0, d, 1.0)
        p_cmp.append(p.astype(BF16))
        ph = p[:, :tq] + p[:, tq:]
        p_sum = ph if p_sum is None else p_sum + ph

    p_hi, p_lo = _split_hi_lo(p_sum)
    mt = mt_ref[...]
    imp = _dot(mt, p_hi) + _dot(mt, p_lo)
    o_cmp = [_dot(vct_ref[...], p_cmp[hp]) for hp in pairs]

    kpos = ws + kidx(wk)
    bias = both_heads(jnp.where(kpos <= tpos(wk),
                                jnp.where(kpos > tpos(wk) - WINDOW, 0.0, NEG_INF), NEG_INF))
    p_win, l_win = [], []
    for hp in pairs:
        s = s_win[hp] + bias
        p = jnp.exp2((s - jnp.max(s, axis=0, keepdims=True)) * EXP2_SCALE)
        l_win.append(jnp.sum(p, axis=0, keepdims=True))
        p_win.append(p.astype(BF16))
    vwt = vwt_ref[:, pl.ds(ws, wk)]
    o_win = [_dot(vwt, p_win[hp]) for hp in pairs]
    o_win = [o_win[hp] / l_win[hp] for hp in pairs]

    jidx = kidx(n_sel)
    jt = tpos(n_sel) // SEL_BLOCK
    forced = jnp.where(jidx == 0, 1.0, jnp.where(jidx == jt, 1.0, jnp.where(jidx == jt - 1, 1.0, 0.0)))
    score = jnp.where(forced > 0.5, FORCE_SCORE, jnp.where(jidx <= jt, imp, -1.0))
    sel_ref[...] = _topk_rows(score, jidx, topk)

    acc_ref[...] = jnp.zeros_like(acc_ref)

    def sel_step(kb, carry):
        k0 = pl.multiple_of(kb * tk, tk)
        k = ks_ref[0, pl.ds(k0, tk), :]
        vt = vst_ref[:, pl.ds(k0, tk)]
        blk0 = kb * (tk // SEL_BLOCK)
        chosen = jnp.concatenate(
            [jnp.broadcast_to(sel_ref[pl.ds(blk0 + j, 1), :], (SEL_BLOCK, tq))
             for j in range(tk // SEL_BLOCK)], axis=0)
        bias = both_heads(jnp.where(k0 + kidx(tk) <= tpos(tk),
                                    jnp.where(chosen > 0.5, 0.0, NEG_INF), NEG_INF))
        ss = [_dot_nt(k, q_pairs[hp]) + bias for hp in pairs]
        out, ps, alphas = [], [], []
        for hp in pairs:
            m, l = carry[2 * hp], carry[2 * hp + 1]
            m_new = jnp.maximum(m, jnp.max(ss[hp], axis=0, keepdims=True))
            p = jnp.exp2((ss[hp] - m_new) * EXP2_SCALE)
            alpha = jnp.exp2((m - m_new) * EXP2_SCALE)
            out += [m_new, alpha * l + jnp.sum(p, axis=0, keepdims=True)]
            ps.append(p.astype(BF16))
            alphas.append(alpha)
        pvs = [_dot(vt, ps[hp]) for hp in pairs]
        for hp in pairs:
            acc_ref[hp] = alphas[hp] * acc_ref[hp] + pvs[hp]
        return tuple(out)

    n_kb = (t0 + tq + tk - 1) // tk
    init = (jnp.full((1, pair_w), NEG_INF, F32), jnp.zeros((1, pair_w), F32)) * HEAD_PAIRS
    stats = lax.fori_loop(0, n_kb, sel_step, init)
    o_sel = [acc_ref[hp] / stats[2 * hp + 1] for hp in pairs]

    gt_ref[...] = gate_ref[...].T
    o_t = []
    for h in range(HPG):
        hp, lanes = h // 2, slice((h % 2) * tq, (h % 2 + 1) * tq)
        col = (g * HPG + h) * N_BRANCH
        o_t.append((gt_ref[pl.ds(col, 1), :] * o_cmp[hp][:, lanes]
                    + gt_ref[pl.ds(col + 1, 1), :] * o_sel[hp][:, lanes]
                    + gt_ref[pl.ds(col + 2, 1), :] * o_win[hp][:, lanes]).astype(BF16))
    outs = [_dot_nt(eye, o_t[h]) for h in range(HPG)]
    for h in range(HPG):
        o_ref[:, h * HEAD_DIM:(h + 1) * HEAD_DIM] = outs[h].astype(BF16)


def _cmp_to_sel(n_cmp_pad, n_cmp, n_sel):
    cs = np.arange(n_cmp_pad)[:, None] * CMP_STRIDE
    ss = np.arange(n_sel)[None, :] * SEL_BLOCK
    hit = (cs < ss + SEL_BLOCK) & (cs + CMP_LEN > ss) & (np.arange(n_cmp_pad)[:, None] < n_cmp)
    return hit.astype(np.float32)


def _attend_prompt(q, cmp_kv, kva, gates, batch, seq):
    tq = min(128, seq)
    tk = min(512, seq)
    nq = seq // tq
    ncp = seq // CMP_STRIDE
    n_sel = -(-seq // SEL_BLOCK)
    topk = min(SEL_TOPK, n_sel)
    assert tq == HEAD_DIM and seq % tk == 0 and WINDOW % tq == 0
    mt = jnp.asarray(_cmp_to_sel(ncp, ncp - 1, n_sel).T, BF16)
    eye = jnp.asarray(np.eye(tq, dtype=np.float32), BF16)
    kv_spec = lambda slot: pl.BlockSpec(
        (1, seq, HEAD_DIM), lambda b, g, i: (slot * KV_HEADS + g, b, 0))
    cmp_spec = lambda s: pl.BlockSpec(
        (1, 1, 1, ncp, HEAD_DIM), lambda b, g, i: (b, s, g, 0, 0))
    full = lambda a: pl.BlockSpec(a.shape, lambda b, g, i: (0, 0))
    return pl.pallas_call(
        functools.partial(_attn_prompt_kernel, tq=tq, tk=tk, seq=seq, topk=topk),
        grid=(batch, KV_HEADS, nq),
        in_specs=[
            pl.BlockSpec((tq, HPG * HEAD_DIM), lambda b, g, i: (b * nq + i, g)),
            cmp_spec(0), cmp_spec(1),
            kv_spec(0), kv_spec(1), kv_spec(2), kv_spec(3),
            pl.BlockSpec((tq, GATE_PAD), lambda b, g, i: (b * nq + i, 0)),
            full(mt), full(eye),
        ],
        out_specs=pl.BlockSpec((tq, HPG * HEAD_DIM), lambda b, g, i: (b * nq + i, g)),
        out_shape=jax.ShapeDtypeStruct((batch * seq, ATTN_DIM), BF16),
        scratch_shapes=[
            pltpu.VMEM((HEAD_DIM, ncp), BF16),
            pltpu.VMEM((HEAD_DIM, seq), BF16),
            pltpu.VMEM((HEAD_DIM, seq), BF16),
            pltpu.VMEM((n_sel, tq), F32),
            pltpu.VMEM((GATE_PAD, tq), F32),
            pltpu.VMEM((HEAD_PAIRS, HEAD_DIM, 2 * tq), F32),
        ],
        compiler_params=_params(3),
        name="attend_prompt",
    )(q, cmp_kv, cmp_kv, kva, kva, kva, kva, gates, mt, eye)


def _attn_sample_kernel(pt_ref, cache_ref, q_ref, kvn_ref, wn_ref, win_ref, gate_ref,
                        wcat_ref, pos_ref, w2_ref, ms_ref, ek_ref, o_ref, buf, sem, dense,
                        *, n_pages, page, past, topk, n_sel):
    b = pl.program_id(0)
    nb = pl.num_programs(0)
    n_kv = CACHE_SLOTS * KV_HEADS
    n_w = 2 * KV_HEADS
    nch = past // CMP_STRIDE
    wb = win_ref.shape[1] // n_w
    page_rows = page * n_kv

    def page_copy(seq_idx, slot, pi):
        src0 = pl.multiple_of(pt_ref[seq_idx, pi] * page_rows, page_rows)
        return pltpu.make_async_copy(
            cache_ref.at[pl.ds(src0, page_rows), :],
            buf.at[slot, pl.ds(pi * page_rows, page_rows), :], sem.at[slot])

    @pl.when(b == 0)
    def _():
        for pi in range(n_pages):
            page_copy(0, 0, pi).start()

    @pl.when(b + 1 < nb)
    def _():
        for pi in range(n_pages):
            page_copy(b + 1, (b + 1) % 2, pi).start()

    slot = b % 2
    for pi in range(n_pages):
        page_copy(b, slot, pi).wait()

    def token_rows(cache_slot, g, first, count, step):
        return buf[slot, pl.ds(first * n_kv + cache_slot * KV_HEADS + g, count, stride=step * n_kv), :]

    row8 = lax.broadcasted_iota(jnp.int32, (N_HEADS, 1), 0)
    in_g0 = row8 < HPG

    def by_group(x0, x1):
        return jnp.where(in_g0, x0, x1)

    def compressed(cache_slot):
        for g in range(KV_HEADS):
            dense[g] = token_rows(cache_slot, g, 0, past, 1)
        x = jnp.concatenate(
            [jnp.concatenate([dense[g, pl.ds(r, nch, stride=CMP_STRIDE), :].astype(BF16)
                              for r in range(CMP_STRIDE)], axis=1)
             for g in range(KV_HEADS)], axis=0)
        return _compress_core(x, wcat_ref[cache_slot], pos_ref[cache_slot],
                              w2_ref[cache_slot]).astype(BF16)

    kc = compressed(0)
    vc = compressed(1)

    q8 = q_ref[0]
    qf = q8.astype(F32)

    lane_n = lax.broadcasted_iota(jnp.int32, (N_HEADS, nch), 1)
    s = by_group(_dot_nt(q8, kc[:nch]), _dot_nt(q8, kc[nch:])) * SCALE
    ok = lane_n * CMP_STRIDE + (CMP_LEN - 1) <= past
    e, d = _softmax_parts(s, ok)
    p = e / d
    pb = p.astype(BF16)
    o_cmp = by_group(_dot(pb, vc[:nch]), _dot(pb, vc[nch:]))

    p_g = [jnp.sum(p[g * HPG:(g + 1) * HPG], axis=0, keepdims=True) for g in range(KV_HEADS)]
    p2 = jnp.concatenate(p_g + [jnp.zeros((N_HEADS - KV_HEADS, nch), F32)], axis=0)
    p_hi, p_lo = _split_hi_lo(p2)
    imp = _dot(p_hi, ms_ref[...]) + _dot(p_lo, ms_ref[...])
    jl = lax.broadcasted_iota(jnp.int32, (N_HEADS, LANES), 1)
    jt = past // SEL_BLOCK
    forced = jnp.where(jl == 0, 1.0, jnp.where(jl == jt, 1.0, jnp.where(jl == jt - 1, 1.0, 0.0)))
    score = jnp.where(forced > 0.5, FORCE_SCORE, jnp.where(jl <= jt, imp, -1.0))
    score = jnp.where(jl < n_sel, score, -2.0)
    ii = lax.broadcasted_iota(jnp.int32, (LANES, LANES), 0)
    jj = lax.broadcasted_iota(jnp.int32, (LANES, LANES), 1)
    sel_rows = []
    for g in range(KV_HEADS):
        srow = jnp.broadcast_to(score[g:g + 1, :], (LANES, LANES))
        scol = jnp.sum(jnp.where(ii == jj, srow, 0.0), axis=1, keepdims=True)
        tie = jnp.where(ii < jj, 1.0, 0.0)
        beats = jnp.where(scol > srow, 1.0, jnp.where(scol == srow, tie, 0.0))
        rank = jnp.sum(beats, axis=0, keepdims=True)
        sel_rows.append(jnp.where(rank < topk, 1.0, 0.0))
    sel2 = jnp.concatenate(sel_rows + [jnp.zeros((N_HEADS - KV_HEADS, LANES), F32)], axis=0)
    key_ok2 = _dot(sel2.astype(BF16), ek_ref[...])
    key_ok = by_group(key_ok2[0:1], key_ok2[1:2])
    new_ok = by_group(*[jnp.sum(jnp.where(jl[0:1] == jt, sel2[g:g + 1], 0.0), axis=1, keepdims=True)
                        for g in range(KV_HEADS)])

    def new_row(ref, idx0):
        x = by_group(ref[0, idx0:idx0 + 1, :], ref[0, idx0 + 1:idx0 + 2, :])
        return x.astype(BF16).astype(F32)

    def attend(keys, vals, ok, k_new, v_new, new_ok):
        s = by_group(_dot_nt(q8, keys[0]), _dot_nt(q8, keys[1])) * SCALE
        s_new = jnp.sum(qf * k_new, axis=-1, keepdims=True) * SCALE
        s = jnp.where(ok, s, NEG_INF)
        s_new = jnp.where(new_ok, s_new, NEG_INF)
        m = jnp.maximum(jnp.max(s, axis=-1, keepdims=True), s_new)
        e = jnp.where(ok, jnp.exp(s - m), 0.0)
        e_new = jnp.where(new_ok, jnp.exp(s_new - m), 0.0)
        d = jnp.sum(e, axis=-1, keepdims=True) + e_new
        eb = e.astype(BF16)
        o = by_group(_dot(eb, vals[0]), _dot(eb, vals[1])) + e_new * v_new
        return o / jnp.where(d > 0, d, 1.0)

    def cached(cache_slot, g):
        return token_rows(cache_slot, g, 0, past, 1).astype(BF16)

    o_sel = attend([cached(2, g) for g in range(KV_HEADS)],
                   [cached(3, g) for g in range(KV_HEADS)], key_ok > 0.5,
                   new_row(kvn_ref, 2 * KV_HEADS), new_row(kvn_ref, 3 * KV_HEADS), new_ok > 0.5)

    keys = [win_ref[0, pl.ds(g, wb, stride=n_w), :].astype(BF16) for g in range(KV_HEADS)]
    vals = [win_ref[0, pl.ds(KV_HEADS + g, wb, stride=n_w), :].astype(BF16)
            for g in range(KV_HEADS)]
    kpos = past - wb + lax.broadcasted_iota(jnp.int32, (N_HEADS, wb), 1)
    ok = kpos > past - WINDOW
    o_win = attend(keys, vals, ok, new_row(wn_ref, 0), new_row(wn_ref, KV_HEADS), row8 >= 0)

    gates = jnp.broadcast_to(gate_ref[0], (N_HEADS, GATE_PAD))
    lane = lax.broadcasted_iota(jnp.int32, (N_HEADS, GATE_PAD), 1)

    def gate(br):
        return jnp.sum(jnp.where(lane == row8 * N_BRANCH + br, gates, 0.0), axis=-1, keepdims=True)

    o_ref[0] = (gate(0) * o_cmp + gate(1) * o_sel + gate(2) * o_win).astype(BF16)


def _attend_sample(page_table, cache, q, kv_new, wkv_new, win_state, gates, wcat, pos8, w2):
    nb, n_pages = page_table.shape
    n_phys, page = cache.shape[:2]
    n_kv = CACHE_SLOTS * KV_HEADS
    n_w = 2 * KV_HEADS
    past = n_pages * page
    nch = past // CMP_STRIDE
    wb = win_state.shape[1]
    n_sel = -(-(past + 1) // SEL_BLOCK)
    n_cmp = (past + 1) // CMP_STRIDE - 1
    topk = min(SEL_TOPK, n_sel)
    assert n_sel <= LANES and nch * CMP_STRIDE == past
    ms = np.zeros((nch, LANES), np.float32)
    ms[:, :n_sel] = _cmp_to_sel(nch, n_cmp, n_sel)
    ek = (np.arange(past)[None, :] // SEL_BLOCK == np.arange(LANES)[:, None])
    ms = jnp.asarray(ms, BF16)
    ek = jnp.asarray(ek.astype(np.float32), BF16)
    seq3 = lambda n: pl.BlockSpec((1, n, HEAD_DIM), lambda b, pt: (b, 0, 0))
    full3 = lambda a: pl.BlockSpec(a.shape, lambda b, pt: (0, 0, 0))
    full2 = lambda a: pl.BlockSpec(a.shape, lambda b, pt: (0, 0))
    grid_spec = pltpu.PrefetchScalarGridSpec(
        num_scalar_prefetch=1,
        grid=(nb,),
        in_specs=[
            pl.BlockSpec(memory_space=pl.ANY),
            seq3(N_HEADS), seq3(n_kv), seq3(n_w), seq3(wb * n_w),
            pl.BlockSpec((1, 1, GATE_PAD), lambda b, pt: (b, 0, 0)),
            full3(wcat), full3(pos8), full3(w2), full2(ms), full2(ek),
        ],
        out_specs=seq3(N_HEADS),
        scratch_shapes=[pltpu.VMEM((2, past * n_kv, HEAD_DIM), F32),
                        pltpu.SemaphoreType.DMA((2,)),
                        pltpu.VMEM((KV_HEADS, past, HEAD_DIM), F32)],
    )
    return pl.pallas_call(
        functools.partial(_attn_sample_kernel, n_pages=n_pages, page=page, past=past,
                          topk=topk, n_sel=n_sel),
        grid_spec=grid_spec,
        out_shape=jax.ShapeDtypeStruct((nb, N_HEADS, HEAD_DIM), BF16),
        compiler_params=_params(1),
        name="attend_sample",
    )(page_table, cache.reshape(n_phys * page * n_kv, HEAD_DIM),
      q.reshape(nb, N_HEADS, HEAD_DIM), kv_new.reshape(nb, n_kv, HEAD_DIM),
      wkv_new.reshape(nb, n_w, HEAD_DIM), win_state.reshape(nb, wb * n_w, HEAD_DIM),
      gates.reshape(nb, 1, GATE_PAD), wcat, pos8, w2, ms, ek)


def _mix_tail(o, diffs, x, pw_ref, ps_ref, wo_ref, gpost_ref, gpre_ref, y1_ref, h2_ref):
    gw = diffs[0].shape[1]
    ys = [(_dot(diffs[g].astype(BF16), pw_ref[g]) * ps_ref[:, g * gw:(g + 1) * gw]).astype(BF16)
          for g in range(POOL_GROUPS)]
    cat = jnp.concatenate([o] + ys, axis=1)
    m = _dot(cat, wo_ref[...])
    y1 = x + _rms(m, gpost_ref[...])
    y1_ref[...] = y1
    h2_ref[...] = _rms(y1, gpre_ref[...]).astype(BF16)


def _mix_prompt_kernel(o_ref, u_ref, halo_ref, x_ref, pw_ref, ps_ref, wo_ref, gpost_ref, gpre_ref,
                       y1_ref, h2_ref, *, tm):
    i = pl.program_id(1)
    halo_rows = halo_ref.shape[0]
    halo = jnp.where(i > 0, halo_ref[...], 0.0)
    u = u_ref[...]
    uext = jnp.concatenate([halo, u], axis=0)
    n_ext = uext.shape[0]
    gw = u.shape[1] // POOL_GROUPS
    tpos = i * tm + lax.broadcasted_iota(jnp.int32, (tm, 1), 0)
    diffs = []
    for g, w in enumerate(POOL_WINDOWS):
        s = uext[:, g * gw:(g + 1) * gw]
        k = 1
        while k < w:
            s = s + pltpu.roll(s, k, 0)
            k *= 2
        cnt = jnp.minimum(w, tpos + 1).astype(F32)
        diffs.append(s[halo_rows:n_ext] / cnt - u[:, g * gw:(g + 1) * gw])
    _mix_tail(o_ref[...], diffs, x_ref[...], pw_ref, ps_ref, wo_ref, gpost_ref, gpre_ref,
              y1_ref, h2_ref)


def _mix_sample_kernel(o_ref, u_ref, st_ref, x_ref, pw_ref, ps_ref, wo_ref, gpost_ref, gpre_ref,
                       y1_ref, h2_ref, *, past):
    u = u_ref[...]
    c = u.shape[1]
    gw = c // POOL_GROUPS
    n_hist = st_ref.shape[1] // c
    diffs = []
    for g, w in enumerate(POOL_WINDOWS):
        un = u[:, g * gw:(g + 1) * gw]
        s = un
        for back in range(1, w):
            r = n_hist - back
            s = s + st_ref[:, r * c + g * gw:r * c + (g + 1) * gw]
        diffs.append(s / float(min(w, past + 1)) - un)
    _mix_tail(o_ref[...], diffs, x_ref[...], pw_ref, ps_ref, wo_ref, gpost_ref, gpre_ref,
              y1_ref, h2_ref)


def _mix_specs(tm, d, c, pool_w, idx):
    fixed2 = lambda *a: (0, 0)
    fixed3 = lambda *a: (0, 0, 0)
    weights = [
        pl.BlockSpec(pool_w.shape, fixed3),
        pl.BlockSpec((1, c), fixed2),
        pl.BlockSpec((d, d), fixed2),
        pl.BlockSpec((1, d), fixed2),
        pl.BlockSpec((1, d), fixed2),
    ]
    outs = (pl.BlockSpec((tm, d), idx), pl.BlockSpec((tm, d), idx))
    return weights, outs


def _mix_prompt(o, u, x, pool_w, pool_scale, w_o, g_post, g_pre, batch, seq):
    rows, d = x.shape
    c = u.shape[1]
    tm = min(256, seq)
    nt = seq // tm
    halo = 16
    assert halo >= POOL_BUF and seq % tm == 0 and tm % halo == 0
    idx = lambda b, i: (b * nt + i, 0)
    halo_idx = lambda b, i: (jnp.maximum((b * nt + i) * (tm // halo) - 1, 0), 0)
    weights, outs = _mix_specs(tm, d, c, pool_w, idx)
    return pl.pallas_call(
        functools.partial(_mix_prompt_kernel, tm=tm),
        grid=(batch, nt),
        in_specs=[pl.BlockSpec((tm, ATTN_DIM), idx), pl.BlockSpec((tm, c), idx),
                  pl.BlockSpec((halo, c), halo_idx), pl.BlockSpec((tm, d), idx)] + weights,
        out_specs=outs,
        out_shape=(jax.ShapeDtypeStruct((rows, d), F32), jax.ShapeDtypeStruct((rows, d), BF16)),
        compiler_params=_params(2),
        name="mix_prompt",
    )(o, u, u, x, pool_w, pool_scale, w_o, g_post, g_pre)


def _mix_sample(o, u, pool_state, x, pool_w, pool_scale, w_o, g_post, g_pre, past):
    rows, d = x.shape
    c = u.shape[1]
    tm = rows
    idx = lambda i: (i, 0)
    weights, outs = _mix_specs(tm, d, c, pool_w, idx)
    st = pool_state.reshape(rows, -1)
    return pl.pallas_call(
        functools.partial(_mix_sample_kernel, past=past),
        grid=(rows // tm,),
        in_specs=[pl.BlockSpec((tm, ATTN_DIM), idx), pl.BlockSpec((tm, c), idx),
                  pl.BlockSpec((tm, st.shape[1]), idx), pl.BlockSpec((tm, d), idx)] + weights,
        out_specs=outs,
        out_shape=(jax.ShapeDtypeStruct((rows, d), F32), jax.ShapeDtypeStruct((rows, d), BF16)),
        compiler_params=_params(1),
        name="mix_sample",
    )(o, u, st, x, pool_w, pool_scale, w_o, g_post, g_pre)


def _mlp_kernel(h_ref, wu_ref, wd_ref, y1_ref, g_ref, y_ref, acc_ref):
    j = pl.program_id(1)

    @pl.when(j == 0)
    def _():
        acc_ref[...] = jnp.zeros_like(acc_ref)

    a = jnp.maximum(_dot(h_ref[...], wu_ref[...]), 0.0)
    acc_ref[...] += _dot((a * a).astype(BF16), wd_ref[...])

    @pl.when(j == pl.num_programs(1) - 1)
    def _():
        y_ref[...] = y1_ref[...] + _rms(acc_ref[...], g_ref[...])


def _mlp(h2, y1, w_up, w_down, gain, tm):
    rows, d = y1.shape
    ff = w_up.shape[1]
    tf = min(1024, ff)
    row = lambda i, j: (i, 0)
    return pl.pallas_call(
        _mlp_kernel,
        grid=(rows // tm, ff // tf),
        in_specs=[
            pl.BlockSpec((tm, d), row),
            pl.BlockSpec((d, tf), lambda i, j: (0, j)),
            pl.BlockSpec((tf, d), lambda i, j: (j, 0)),
            pl.BlockSpec((tm, d), row),
            pl.BlockSpec((1, d), lambda i, j: (0, 0)),
        ],
        out_specs=pl.BlockSpec((tm, d), row),
        out_shape=jax.ShapeDtypeStruct((rows, d), F32),
        scratch_shapes=[pltpu.VMEM((tm, d), F32)],
        compiler_params=_params(2),
        name="mlp",
    )(h2, w_up, w_down, y1, gain)


def _layer_weights(w_in, cmp_pos_k, cmp_w1_k, cmp_w2_k, cmp_pos_v, cmp_w1_v, cmp_w2_v,
                   pool_w, w_o, w_up, w_down, pool_dim):
    gate_lo = WKV_OFF + 2 * KV_DIM
    gate_hi = gate_lo + N_BRANCH * N_HEADS
    w_r = jnp.concatenate(
        [w_in[:, :gate_lo], w_in[:, gate_hi:gate_hi + pool_dim],
         jnp.pad(w_in[:, gate_lo:gate_hi], ((0, 0), (0, GATE_PAD - N_BRANCH * N_HEADS)))],
        axis=1).astype(BF16)
    wcat = jnp.stack([jnp.concatenate([w1[:CMP_HALF], w1[CMP_HALF:]], axis=1)
                      for w1 in (cmp_w1_k, cmp_w1_v)]).astype(BF16)
    pos8 = jnp.stack([jnp.pad(p.reshape(1, CMP_IN), ((0, 7), (0, 0)))
                      for p in (cmp_pos_k, cmp_pos_v)]).astype(BF16)
    w2 = jnp.stack([cmp_w2_k, cmp_w2_v]).astype(BF16)
    return (w_r, wcat, pos8, w2, pool_w.astype(BF16), w_o.astype(BF16),
            w_up.astype(BF16), w_down.astype(BF16))


def kernel(x_prompt, x_sample, cache_kv, state_win_kv, state_pool, page_table, norm_mix_pre, w_in,
           cmp_pos_k, cmp_w1_k, cmp_w2_k, cmp_pos_v, cmp_w1_v, cmp_w2_v, pool_w, pool_scale, w_o,
           norm_mix_post, norm_mlp_pre, w_up, w_down, norm_mlp_post):
    batch, seq, d = x_prompt.shape
    nb, dec_seq, _ = x_sample.shape
    depth = w_in.shape[0]
    pool_dim = d - ATTN_DIM
    n_pages = page_table.shape[1]
    page = cache_kv.shape[2]
    past = n_pages * page
    wb = state_win_kv.shape[2]
    assert dec_seq == 1 and seq >= POOL_BUF and seq % CMP_STRIDE == 0
    assert w_in.shape[2] == ATTN_DIM + 6 * KV_DIM + N_BRANCH * N_HEADS + pool_dim

    tm_p = min(256, seq)
    tabs_p = _rope_tables(jnp.arange(seq, dtype=jnp.int32))
    tabs_s = _rope_tables(jnp.full((nb,), past, jnp.int32))
    nt_p = seq // tm_p

    y_p = x_prompt.reshape(batch * seq, d)
    y_s = x_sample.reshape(nb, d)
    kv_p, kv_s, win_p, win_s, pool_p, pool_s = [], [], [], [], [], []
    row_vec = lambda v: v.reshape(1, -1)
    for l in range(depth):
        w_r, wcat, pos8, w2, pw, wo, wu, wd = _layer_weights(
            w_in[l], cmp_pos_k[l], cmp_w1_k[l], cmp_w2_k[l], cmp_pos_v[l], cmp_w1_v[l],
            cmp_w2_v[l], pool_w[l], w_o[l], w_up[l], w_down[l], pool_dim)
        g_pre, g_post = row_vec(norm_mix_pre[l]), row_vec(norm_mix_post[l])
        g_mlp_pre, g_mlp_post = row_vec(norm_mlp_pre[l]), row_vec(norm_mlp_post[l])
        ps = row_vec(pool_scale[l])

        q, kv, wkv, kva, gates, u, xc = _project(
            y_p, g_pre, w_r, tabs_p, lambda i: (i % nt_p, 0), tm_p, pool_dim, True)
        cmp_kv = _compress_prompt(xc, wcat, pos8, w2, batch, seq)
        o = _attend_prompt(q, cmp_kv, kva, gates, batch, seq)
        y1, h2 = _mix_prompt(o, u, y_p, pw, ps, wo, g_post, g_mlp_pre, batch, seq)
        y_p = _mlp(h2, y1, wu, wd, g_mlp_post, min(512, batch * seq))
        kv_p.append(kv.reshape(batch, seq, CACHE_SLOTS, KV_HEADS, HEAD_DIM))
        wp = min(WINDOW, seq)
        win_p.append(wkv.reshape(batch, seq, 2, KV_HEADS, HEAD_DIM)[:, seq - wp:])
        pool_p.append(u.reshape(batch, seq, pool_dim)[:, seq - POOL_BUF:])

        q, kv, wkv, _, gates, u = _project(
            y_s, g_pre, w_r, tabs_s, lambda i: (i, 0), nb, pool_dim, False)
        o = _attend_sample(page_table, cache_kv[l], q, kv, wkv, state_win_kv[l], gates,
                           wcat, pos8, w2)
        y1, h2 = _mix_sample(o.reshape(nb, ATTN_DIM), u, state_pool[l], y_s, pw, ps, wo,
                             g_post, g_mlp_pre, past)
        y_s = _mlp(h2, y1, wu, wd, g_mlp_post, nb)
        kv_s.append(kv.reshape(nb, 1, CACHE_SLOTS, KV_HEADS, HEAD_DIM))
        new_win = wkv.reshape(nb, 1, 2, KV_HEADS, HEAD_DIM)
        win_s.append(jnp.concatenate([state_win_kv[l], new_win], axis=1)[:, 1:])
        pool_s.append(jnp.concatenate([state_pool[l], u[:, None]], axis=1)[:, 1:])

    return (y_p.reshape(batch, seq, d), y_s.reshape(nb, 1, d),
            jnp.stack(kv_p), jnp.stack(kv_s), jnp.stack(win_p), jnp.stack(win_s),
            jnp.stack(pool_p), jnp.stack(pool_s))
```

```python
import functools

import numpy as np
import jax
import jax.numpy as jnp
from jax import lax
from jax.experimental import pallas as pl
from jax.experimental.pallas import tpu as pltpu

N_HEADS = 8
HEAD_DIM = 128
KV_HEADS = 2
HPG = N_HEADS // KV_HEADS
ATTN_DIM = N_HEADS * HEAD_DIM
KV_DIM = KV_HEADS * HEAD_DIM
N_BRANCH = 3
POOL_WINDOWS = (2, 4, 8, 16)
POOL_GROUPS = len(POOL_WINDOWS)
POOL_BUF = max(POOL_WINDOWS) - 1
ROT_DIM = HEAD_DIM // 4
ROT_HALF = ROT_DIM // 2
ROPE_THETA = 500000.0
CMP_LEN = 32
CMP_STRIDE = 16
SEL_BLOCK = 64
SEL_TOPK = 16
WINDOW = 512
EPS = 1e-6
SCALE = HEAD_DIM ** -0.5
FORCE_SCORE = 1e4
NEG_INF = -1e30

LANES = 128
CACHE_SLOTS = 4
ROW_W = CACHE_SLOTS * KV_DIM
CHUNK_W = CMP_STRIDE * ROW_W
CMP_IN = CMP_LEN * HEAD_DIM
CMP_HALF = CMP_STRIDE * HEAD_DIM
GATE_PAD = LANES
VMEM_LIMIT = 56 * 1024 * 1024

BF16 = jnp.bfloat16
F32 = jnp.float32


def _dot(a, b):
    return jnp.dot(a, b, preferred_element_type=F32)


def _dot_nt(a, b):
    return lax.dot_general(a, b, (((1,), (1,)), ((), ())), preferred_element_type=F32)


def _rms(x, g):
    return x * lax.rsqrt(jnp.mean(x * x, axis=-1, keepdims=True) + EPS) * g


def _params(n_axes):
    return pltpu.CompilerParams(
        dimension_semantics=("arbitrary",) * n_axes, vmem_limit_bytes=VMEM_LIMIT)


def _split_hi_lo(x):
    hi = x.astype(BF16)
    lo = (x - hi.astype(F32)).astype(BF16)
    return hi, lo


Q_OFF, KV_OFF, WKV_OFF = 0, ATTN_DIM, ATTN_DIM + 4 * KV_DIM


def _proj_kernel(x_ref, g_ref, w_ref, cos_ref, sa_ref, sb_ref, *rest, pool_dim, chunked):
    if chunked:
        q_ref, kv_ref, wkv_ref, kva_ref, gate_ref, u_ref, xc_ref, tmp_ref = rest
    else:
        q_ref, kv_ref, wkv_ref, kva_ref, gate_ref, u_ref = rest
    tm = x_ref.shape[0]
    u_off = WKV_OFF + 2 * KV_DIM
    gate_off = u_off + pool_dim
    h = _rms(x_ref[...], g_ref[...]).astype(BF16)
    cos, sa, sb = cos_ref[...], sa_ref[...], sb_ref[...]

    def rope(z):
        return (z * cos + pltpu.roll(z, LANES - ROT_HALF, 1) * sa
                + pltpu.roll(z, ROT_HALF, 1) * sb)

    zq = _dot(h, w_ref[:, Q_OFF:Q_OFF + ATTN_DIM])
    for hd in range(N_HEADS):
        sl = slice(hd * HEAD_DIM, (hd + 1) * HEAD_DIM)
        q_ref[:, sl] = rope(zq[:, sl]).astype(BF16)

    n_kv = CACHE_SLOTS * KV_HEADS
    zkv = _dot(h, w_ref[:, KV_OFF:KV_OFF + 4 * KV_DIM])
    for blk in range(n_kv):
        z = zkv[:, blk * HEAD_DIM:(blk + 1) * HEAD_DIM]
        if (blk // KV_HEADS) % 2 == 0:
            z = rope(z)
        kv_ref[pl.ds(blk, tm, stride=n_kv), :] = z
        if blk >= 2 * KV_HEADS:
            kva_ref[blk - 2 * KV_HEADS] = z.astype(BF16)
        elif chunked:
            tmp_ref[...] = z
            for r in range(CMP_STRIDE):
                xc_ref[blk, :, r * HEAD_DIM:(r + 1) * HEAD_DIM] = (
                    tmp_ref[pl.ds(r, tm // CMP_STRIDE, stride=CMP_STRIDE), :].astype(BF16))

    n_w = 2 * KV_HEADS
    zw = _dot(h, w_ref[:, WKV_OFF:WKV_OFF + 2 * KV_DIM])
    for blk in range(n_w):
        z = zw[:, blk * HEAD_DIM:(blk + 1) * HEAD_DIM]
        if blk < KV_HEADS:
            z = rope(z)
        wkv_ref[pl.ds(blk, tm, stride=n_w), :] = z
        kva_ref[2 * KV_HEADS + blk] = z.astype(BF16)

    u_ref[...] = _dot(h, w_ref[:, u_off:u_off + pool_dim])
    gl = _dot(h, w_ref[:, gate_off:gate_off + GATE_PAD])
    gate_ref[...] = 1.0 / (1.0 + jnp.exp(-gl))


def _project(x, gain, w_r, tables, table_index, tm, pool_dim, chunked):
    rows, d = x.shape
    n_proj = w_r.shape[1]
    n_kv = CACHE_SLOTS * KV_HEADS
    n_w = 2 * KV_HEADS
    row = lambda i: (i, 0)
    fixed = lambda i: (0, 0)
    tab_spec = pl.BlockSpec((tm, LANES), table_index)
    out_shape = [
        jax.ShapeDtypeStruct((rows, ATTN_DIM), BF16),
        jax.ShapeDtypeStruct((rows * n_kv, HEAD_DIM), F32),
        jax.ShapeDtypeStruct((rows * n_w, HEAD_DIM), F32),
        jax.ShapeDtypeStruct((n_w + n_kv // 2, rows, HEAD_DIM), BF16),
        jax.ShapeDtypeStruct((rows, GATE_PAD), F32),
        jax.ShapeDtypeStruct((rows, pool_dim), F32),
    ]
    out_specs = [
        pl.BlockSpec((tm, ATTN_DIM), row),
        pl.BlockSpec((tm * n_kv, HEAD_DIM), row),
        pl.BlockSpec((tm * n_w, HEAD_DIM), row),
        pl.BlockSpec((n_w + n_kv // 2, tm, HEAD_DIM), lambda i: (0, i, 0)),
        pl.BlockSpec((tm, GATE_PAD), row),
        pl.BlockSpec((tm, pool_dim), row),
    ]
    scratch = []
    if chunked:
        out_shape.append(jax.ShapeDtypeStruct((n_kv // 2, rows // CMP_STRIDE, CMP_HALF), BF16))
        out_specs.append(pl.BlockSpec((n_kv // 2, tm // CMP_STRIDE, CMP_HALF), lambda i: (0, i, 0)))
        scratch.append(pltpu.VMEM((tm, HEAD_DIM), F32))
    return pl.pallas_call(
        functools.partial(_proj_kernel, pool_dim=pool_dim, chunked=chunked),
        grid=(rows // tm,),
        in_specs=[
            pl.BlockSpec((tm, d), row),
            pl.BlockSpec((1, d), fixed),
            pl.BlockSpec((d, n_proj), fixed),
            tab_spec, tab_spec, tab_spec,
        ],
        out_specs=tuple(out_specs),
        out_shape=tuple(out_shape),
        scratch_shapes=scratch,
        compiler_params=_params(1),
        name="project",
    )(x, gain, w_r, *tables)


def _rope_tables(pos):
    inv = jnp.power(ROPE_THETA, -jnp.arange(ROT_HALF, dtype=F32) * (2.0 / ROT_DIM))
    ang = pos.astype(F32)[:, None] * inv[None, :]
    cos, sin = jnp.cos(ang), jnp.sin(ang)
    n = pos.shape[0]
    rest = LANES - ROT_DIM
    c = jnp.concatenate([cos, cos, jnp.ones((n, rest), F32)], axis=1)
    sa = jnp.concatenate([-sin, jnp.zeros((n, LANES - ROT_HALF), F32)], axis=1)
    sb = jnp.concatenate([jnp.zeros((n, ROT_HALF), F32), sin, jnp.zeros((n, rest), F32)], axis=1)
    return c, sa, sb


def _gelu_tanh(x):
    return 0.5 * x * (1.0 + jnp.tanh(0.7978845608028654 * (x + 0.044715 * (x * x * x))))


def _compress_first(x, wcat, pos8):
    ab = _dot(x, wcat)
    pa = _dot(pos8[:, :CMP_HALF], wcat)[0:1, :HEAD_DIM]
    pb = _dot(pos8[:, CMP_HALF:], wcat)[0:1, HEAD_DIM:]
    return ab, pa + pb


def _compress_hidden(ab, pos_term):
    rows = ab.shape[0]
    b_next = pltpu.roll(ab[:, HEAD_DIM:], rows - 1, 0)
    return _gelu_tanh(ab[:, :HEAD_DIM] + b_next + pos_term).astype(BF16)


def _compress_kernel(x_ref, wcat_ref, pos_ref, w2_ref, o_ref):
    g, _, nck, width = x_ref.shape
    x = x_ref[...].reshape(g * nck, width)
    hid = _compress_hidden(*_compress_first(x, wcat_ref[0], pos_ref[0]))
    o_ref[...] = _dot(hid, w2_ref[0]).astype(BF16).reshape(o_ref.shape)


def _compress_prompt(xc, wcat, pos8, w2, batch, seq):
    nck = seq // CMP_STRIDE
    x = xc.reshape(xc.shape[0], batch, nck, CMP_HALF)
    return pl.pallas_call(
        _compress_kernel,
        grid=(batch, 2),
        in_specs=[
            pl.BlockSpec((KV_HEADS, 1, nck, CMP_HALF), lambda b, s: (s, b, 0, 0)),
            pl.BlockSpec((1, CMP_HALF, 2 * HEAD_DIM), lambda b, s: (s, 0, 0)),
            pl.BlockSpec((1, 8, CMP_IN), lambda b, s: (s, 0, 0)),
            pl.BlockSpec((1, HEAD_DIM, HEAD_DIM), lambda b, s: (s, 0, 0)),
        ],
        out_specs=pl.BlockSpec((1, 1, KV_HEADS, nck, HEAD_DIM), lambda b, s: (b, s, 0, 0, 0)),
        out_shape=jax.ShapeDtypeStruct((batch, 2, KV_HEADS, nck, HEAD_DIM), BF16),
        compiler_params=_params(2),
        name="compress_prompt",
    )(x, wcat, pos8, w2)


def _topk_rows(score, jidx, topk):
    rank = jnp.zeros_like(score)
    for j in range(score.shape[0]):
        bj = score[j:j + 1, :]
        tie = jnp.where(jidx > j, 1.0, 0.0)
        rank = rank + jnp.where(bj > score, 1.0, jnp.where(bj == score, tie, 0.0))
    return jnp.where(rank < topk, 1.0, 0.0)


def _softmax_parts(s, ok):
    s = jnp.where(ok, s, NEG_INF)
    m = jnp.max(s, axis=-1, keepdims=True)
    e = jnp.where(ok, jnp.exp(s - m), 0.0)
    d = jnp.sum(e, axis=-1, keepdims=True)
    return e, jnp.where(d > 0, d, 1.0)


EXP2_SCALE = SCALE * 1.4426950408889634
HEAD_PAIRS = HPG // 2


def _attn_prompt_kernel(q_ref, kc_ref, vc_ref, ks_ref, vs_ref, kw_ref, vw_ref, gate_ref,
                        mt_ref, eye_ref, o_ref,
                        vct_ref, vst_ref, vwt_ref, sel_ref, gt_ref, acc_ref,
                        *, tq, tk, seq, topk):
    g = pl.program_id(1)
    i = pl.program_id(2)
    t0 = i * tq
    n_sel = mt_ref.shape[0]
    pair_w = 2 * tq
    eye = eye_ref[...]

    @pl.when(i == 0)
    def _():
        vct_ref[...] = _dot_nt(eye, vc_ref[0, 0, 0]).astype(BF16)
        vst_ref[...] = _dot_nt(eye, vs_ref[0]).astype(BF16)
        vwt_ref[...] = _dot_nt(eye, vw_ref[0]).astype(BF16)

    q4 = q_ref[...]
    q_pairs = [jnp.concatenate([q4[:, (2 * hp) * HEAD_DIM:(2 * hp + 1) * HEAD_DIM],
                                q4[:, (2 * hp + 1) * HEAD_DIM:(2 * hp + 2) * HEAD_DIM]], axis=0)
               for hp in range(HEAD_PAIRS)]

    def both_heads(x):
        return jnp.concatenate([x, x], axis=1)

    def tpos(n_keys):
        return t0 + lax.broadcasted_iota(jnp.int32, (n_keys, tq), 1)

    def kidx(n_keys):
        return lax.broadcasted_iota(jnp.int32, (n_keys, tq), 0)

    pairs = range(HEAD_PAIRS)

    kc = kc_ref[0, 0, 0]
    ncp = kc.shape[0]
    wk = min(WINDOW + tq, seq)
    ws = pl.multiple_of(jnp.maximum(t0 + tq - wk, 0), tq)
    kw = kw_ref[0, pl.ds(ws, wk), :]
    s_cmp = [_dot_nt(kc, q_pairs[hp]) for hp in pairs]
    s_win = [_dot_nt(kw, q_pairs[hp]) for hp in pairs]

    ok = both_heads(jnp.where(kidx(ncp) * CMP_STRIDE + (CMP_LEN - 1) <= tpos(ncp), 1.0, 0.0)) > 0.5
    p_cmp, p_sum = [], None
    for hp in pairs:
        s = jnp.where(ok, s_cmp[hp], NEG_INF)
        m = jnp.max(s, axis=0, keepdims=True)
        e = jnp.where(ok, jnp.exp2((s - m) * EXP2_SCALE), 0.0)
        d = jnp.sum(e, axis=0, keepdims=True)
        p = e / jnp.where(d > 0, d, 1.0)
        p_cmp.append(p.astype(BF16))
        ph = p[:, :tq] + p[:, tq:]
        p_sum = ph if p_sum is None else p_sum + ph

    p_hi, p_lo = _split_hi_lo(p_sum)
    mt = mt_ref[...]
    imp = _dot(mt, p_hi) + _dot(mt, p_lo)
    o_cmp = [_dot(vct_ref[...], p_cmp[hp]) for hp in pairs]

    kpos = ws + kidx(wk)
    bias = both_heads(jnp.where(kpos <= tpos(wk),
                                jnp.where(kpos > tpos(wk) - WINDOW, 0.0, NEG_INF), NEG_INF))
    p_win, l_win = [], []
    for hp in pairs:
        s = s_win[hp] + bias
        p = jnp.exp2((s - jnp.max(s, axis=0, keepdims=True)) * EXP2_SCALE)
        l_win.append(jnp.sum(p, axis=0, keepdims=True))
        p_win.append(p.astype(BF16))
    vwt = vwt_ref[:, pl.ds(ws, wk)]
    o_win = [_dot(vwt, p_win[hp]) for hp in pairs]
    o_win = [o_win[hp] / l_win[hp] for hp in pairs]

    jidx = kidx(n_sel)
    jt = tpos(n_sel) // SEL_BLOCK
    forced = jnp.where(jidx == 0, 1.0, jnp.where(jidx == jt, 1.0, jnp.where(jidx == jt - 1, 1.0, 0.0)))
    score = jnp.where(forced > 0.5, FORCE_SCORE, jnp.where(jidx <= jt, imp, -1.0))
    sel_ref[...] = _topk_rows(score, jidx, topk)

    acc_ref[...] = jnp.zeros_like(acc_ref)

    def sel_step(kb, carry):
        k0 = pl.multiple_of(kb * tk, tk)
        k = ks_ref[0, pl.ds(k0, tk), :]
        vt = vst_ref[:, pl.ds(k0, tk)]
        blk0 = kb * (tk // SEL_BLOCK)
        chosen = jnp.concatenate(
            [jnp.broadcast_to(sel_ref[pl.ds(blk0 + j, 1), :], (SEL_BLOCK, tq))
             for j in range(tk // SEL_BLOCK)], axis=0)
        bias = both_heads(jnp.where(k0 + kidx(tk) <= tpos(tk),
                                    jnp.where(chosen > 0.5, 0.0, NEG_INF), NEG_INF))
        ss = [_dot_nt(k, q_pairs[hp]) + bias for hp in pairs]
        out, ps, alphas = [], [], []
        for hp in pairs:
            m, l = carry[2 * hp], carry[2 * hp + 1]
            m_new = jnp.maximum(m, jnp.max(ss[hp], axis=0, keepdims=True))
            p = jnp.exp2((ss[hp] - m_new) * EXP2_SCALE)
            alpha = jnp.exp2((m - m_new) * EXP2_SCALE)
            out += [m_new, alpha * l + jnp.sum(p, axis=0, keepdims=True)]
            ps.append(p.astype(BF16))
            alphas.append(alpha)
        pvs = [_dot(vt, ps[hp]) for hp in pairs]
        for hp in pairs:
            acc_ref[hp] = alphas[hp] * acc_ref[hp] + pvs[hp]
        return tuple(out)

    n_kb = (t0 + tq + tk - 1) // tk
    init = (jnp.full((1, pair_w), NEG_INF, F32), jnp.zeros((1, pair_w), F32)) * HEAD_PAIRS
    stats = lax.fori_loop(0, n_kb, sel_step, init)
    o_sel = [acc_ref[hp] / stats[2 * hp + 1] for hp in pairs]

    gt_ref[...] = gate_ref[...].T
    o_t = []
    for h in range(HPG):
        hp, lanes = h // 2, slice((h % 2) * tq, (h % 2 + 1) * tq)
        col = (g * HPG + h) * N_BRANCH
        o_t.append((gt_ref[pl.ds(col, 1), :] * o_cmp[hp][:, lanes]
                    + gt_ref[pl.ds(col + 1, 1), :] * o_sel[hp][:, lanes]
                    + gt_ref[pl.ds(col + 2, 1), :] * o_win[hp][:, lanes]).astype(BF16))
    outs = [_dot_nt(eye, o_t[h]) for h in range(HPG)]
    for h in range(HPG):
        o_ref[:, h * HEAD_DIM:(h + 1) * HEAD_DIM] = outs[h].astype(BF16)


def _cmp_to_sel(n_cmp_pad, n_cmp, n_sel):
    cs = np.arange(n_cmp_pad)[:, None] * CMP_STRIDE
    ss = np.arange(n_sel)[None, :] * SEL_BLOCK
    hit = (cs < ss + SEL_BLOCK) & (cs + CMP_LEN > ss) & (np.arange(n_cmp_pad)[:, None] < n_cmp)
    return hit.astype(np.float32)


def _attend_prompt(q, cmp_kv, kva, gates, batch, seq):
    tq = min(128, seq)
    tk = min(512, seq)
    nq = seq // tq
    ncp = seq // CMP_STRIDE
    n_sel = -(-seq // SEL_BLOCK)
    topk = min(SEL_TOPK, n_sel)
    assert tq == HEAD_DIM and seq % tk == 0 and WINDOW % tq == 0
    mt = jnp.asarray(_cmp_to_sel(ncp, ncp - 1, n_sel).T, BF16)
    eye = jnp.asarray(np.eye(tq, dtype=np.float32), BF16)
    kv_spec = lambda slot: pl.BlockSpec(
        (1, seq, HEAD_DIM), lambda b, g, i: (slot * KV_HEADS + g, b, 0))
    cmp_spec = lambda s: pl.BlockSpec(
        (1, 1, 1, ncp, HEAD_DIM), lambda b, g, i: (b, s, g, 0, 0))
    full = lambda a: pl.BlockSpec(a.shape, lambda b, g, i: (0, 0))
    return pl.pallas_call(
        functools.partial(_attn_prompt_kernel, tq=tq, tk=tk, seq=seq, topk=topk),
        grid=(batch, KV_HEADS, nq),
        in_specs=[
            pl.BlockSpec((tq, HPG * HEAD_DIM), lambda b, g, i: (b * nq + i, g)),
            cmp_spec(0), cmp_spec(1),
            kv_spec(0), kv_spec(1), kv_spec(2), kv_spec(3),
            pl.BlockSpec((tq, GATE_PAD), lambda b, g, i: (b * nq + i, 0)),
            full(mt), full(eye),
        ],
        out_specs=pl.BlockSpec((tq, HPG * HEAD_DIM), lambda b, g, i: (b * nq + i, g)),
        out_shape=jax.ShapeDtypeStruct((batch * seq, ATTN_DIM), BF16),
        scratch_shapes=[
            pltpu.VMEM((HEAD_DIM, ncp), BF16),
            pltpu.VMEM((HEAD_DIM, seq), BF16),
            pltpu.VMEM((HEAD_DIM, seq), BF16),
            pltpu.VMEM((n_sel, tq), F32),
            pltpu.VMEM((GATE_PAD, tq), F32),
            pltpu.VMEM((HEAD_PAIRS, HEAD_DIM, 2 * tq), F32),
        ],
        compiler_params=_params(3),
        name="attend_prompt",
    )(q, cmp_kv, cmp_kv, kva, kva, kva, kva, gates, mt, eye)


def _attn_sample_kernel(pt_ref, cache_ref, q_ref, kvn_ref, wn_ref, win_ref, gate_ref,
                        wcat_ref, pos_ref, w2_ref, ms_ref, ek_ref, o_ref, buf, sem,
                        *, n_pages, page, past, topk, n_sel):
    b = pl.program_id(0)
    nb = pl.num_programs(0)
    n_kv = CACHE_SLOTS * KV_HEADS
    n_w = 2 * KV_HEADS
    cpp = page // CMP_STRIDE
    nch = past // CMP_STRIDE
    wb = win_ref.shape[1] // n_w

    def page_copy(seq_idx, slot, pi, r):
        src0 = pl.multiple_of(pt_ref[seq_idx, pi] * cpp, cpp)
        return pltpu.make_async_copy(
            cache_ref.at[pl.ds(src0, cpp), r],
            buf.at[slot, r, pl.ds(pl.multiple_of(pi * cpp, cpp), cpp)], sem.at[slot])

    def for_all_copies(seq_idx, slot, act):
        def per_page(pi, carry):
            for r in range(CMP_STRIDE):
                act(page_copy(seq_idx, slot, pi, r))
            return carry
        lax.fori_loop(0, n_pages, per_page, 0)

    @pl.when(b == 0)
    def _():
        for_all_copies(0, 0, lambda c: c.start())

    @pl.when(b + 1 < nb)
    def _():
        for_all_copies(b + 1, (b + 1) % 2, lambda c: c.start())

    slot = b % 2
    for_all_copies(b, slot, lambda c: c.wait())

    def chunk_rows(cache_slot, g, r):
        rows = buf.reshape(2, CMP_STRIDE, nch * n_kv, HEAD_DIM)
        return rows[slot, r, pl.ds(cache_slot * KV_HEADS + g, nch, stride=n_kv), :]

    row8 = lax.broadcasted_iota(jnp.int32, (N_HEADS, 1), 0)
    in_g0 = row8 < HPG

    def by_group(x0, x1):
        return jnp.where(in_g0, x0, x1)

    q8 = q_ref[0]
    qf = q8.astype(F32)
    groups = range(KV_HEADS)

    def scores(keys):
        return by_group(_dot_nt(q8, keys[0]), _dot_nt(q8, keys[1])) * SCALE

    def cached(cache_slot, g):
        return jnp.concatenate([chunk_rows(cache_slot, g, r).astype(BF16)
                                for r in range(CMP_STRIDE)], axis=0)

    def compress_input(cache_slot):
        return jnp.concatenate(
            [jnp.concatenate([chunk_rows(cache_slot, g, r).astype(BF16)
                              for r in range(CMP_STRIDE)], axis=1)
             for g in groups], axis=0)

    first = [_compress_first(compress_input(cs), wcat_ref[cs], pos_ref[cs]) for cs in range(2)]
    s_sel = scores([cached(2, g) for g in groups])
    win_keys = [win_ref[0, pl.ds(g, wb, stride=n_w), :].astype(BF16) for g in groups]
    s_win = scores(win_keys)
    kc, vc =[_dot(_compress_hidden(*first[cs]), w2_ref[cs]).astype(BF16) for cs in range(2)]

    lane_n = lax.broadcasted_iota(jnp.int32, (N_HEADS, nch), 1)
    s = by_group(_dot_nt(q8, kc[:nch]), _dot_nt(q8, kc[nch:])) * SCALE
    ok = lane_n * CMP_STRIDE + (CMP_LEN - 1) <= past
    e, d = _softmax_parts(s, ok)
    p = e / d
    pb = p.astype(BF16)
    o_cmp = by_group(_dot(pb, vc[:nch]), _dot(pb, vc[nch:]))

    p_g = [jnp.sum(p[g * HPG:(g + 1) * HPG], axis=0, keepdims=True) for g in range(KV_HEADS)]
    p2 = jnp.concatenate(p_g + [jnp.zeros((N_HEADS - KV_HEADS, nch), F32)], axis=0)
    p_hi, p_lo = _split_hi_lo(p2)
    imp = _dot(p_hi, ms_ref[...]) + _dot(p_lo, ms_ref[...])
    jl = lax.broadcasted_iota(jnp.int32, (N_HEADS, LANES), 1)
    jt = past // SEL_BLOCK
    forced = jnp.where(jl == 0, 1.0, jnp.where(jl == jt, 1.0, jnp.where(jl == jt - 1, 1.0, 0.0)))
    score = jnp.where(forced > 0.5, FORCE_SCORE, jnp.where(jl <= jt, imp, -1.0))
    score = jnp.where(jl < n_sel, score, -2.0)
    ii = lax.broadcasted_iota(jnp.int32, (LANES, LANES), 0)
    jj = lax.broadcasted_iota(jnp.int32, (LANES, LANES), 1)
    sel_rows = []
    for g in range(KV_HEADS):
        srow = jnp.broadcast_to(score[g:g + 1, :], (LANES, LANES))
        scol = jnp.sum(jnp.where(ii == jj, srow, 0.0), axis=1, keepdims=True)
        tie = jnp.where(ii < jj, 1.0, 0.0)
        beats = jnp.where(scol > srow, 1.0, jnp.where(scol == srow, tie, 0.0))
        rank = jnp.sum(beats, axis=0, keepdims=True)
        sel_rows.append(jnp.where(rank < topk, 1.0, 0.0))
    sel2 = jnp.concatenate(sel_rows + [jnp.zeros((N_HEADS - KV_HEADS, LANES), F32)], axis=0)
    chunk_ok2 = _dot(sel2.astype(BF16), ek_ref[...])
    chunk_ok = by_group(chunk_ok2[0:1], chunk_ok2[1:2])
    key_ok = jnp.concatenate([chunk_ok] * CMP_STRIDE, axis=1)
    new_ok = by_group(*[jnp.sum(jnp.where(jl[0:1] == jt, sel2[g:g + 1], 0.0), axis=1, keepdims=True)
                        for g in range(KV_HEADS)])

    def new_row(ref, idx0):
        x = by_group(ref[0, idx0:idx0 + 1, :], ref[0, idx0 + 1:idx0 + 2, :])
        return x.astype(BF16).astype(F32)

    def weights(s, ok, k_new, new_ok):
        s_new = jnp.sum(qf * k_new, axis=-1, keepdims=True) * SCALE
        s = jnp.where(ok, s, NEG_INF)
        s_new = jnp.where(new_ok, s_new, NEG_INF)
        m = jnp.maximum(jnp.max(s, axis=-1, keepdims=True), s_new)
        e = jnp.where(ok, jnp.exp(s - m), 0.0)
        e_new = jnp.where(new_ok, jnp.exp(s_new - m), 0.0)
        d = jnp.sum(e, axis=-1, keepdims=True) + e_new
        return e.astype(BF16), e_new, jnp.where(d > 0, d, 1.0)

    def weighted(eb, vals, e_new, v_new, d):
        return (by_group(_dot(eb, vals[0]), _dot(eb, vals[1])) + e_new * v_new) / d

    kpos = past - wb + lax.broadcasted_iota(jnp.int32, (N_HEADS, wb), 1)
    w_sel = weights(s_sel, key_ok > 0.5, new_row(kvn_ref, 2 * KV_HEADS), new_ok > 0.5)
    w_win = weights(s_win, kpos > past - WINDOW, new_row(wn_ref, 0), row8 >= 0)
    win_vals = [win_ref[0, pl.ds(KV_HEADS + g, wb, stride=n_w), :].astype(BF16) for g in groups]
    o_sel = weighted(w_sel[0], [cached(3, g) for g in groups], w_sel[1],
                     new_row(kvn_ref, 3 * KV_HEADS), w_sel[2])
    o_win = weighted(w_win[0], win_vals, w_win[1], new_row(wn_ref, KV_HEADS), w_win[2])

    gates = jnp.broadcast_to(gate_ref[0], (N_HEADS, GATE_PAD))
    lane = lax.broadcasted_iota(jnp.int32, (N_HEADS, GATE_PAD), 1)

    def gate(br):
        return jnp.sum(jnp.where(lane == row8 * N_BRANCH + br, gates, 0.0), axis=-1, keepdims=True)

    o_ref[0] = (gate(0) * o_cmp + gate(1) * o_sel + gate(2) * o_win).astype(BF16)


def _attend_sample(page_table, cache, q, kv_new, wkv_new, win_rows, gates, wcat, pos8, w2):
    nb, n_pages = page_table.shape
    n_phys, page = cache.shape[:2]
    n_kv = CACHE_SLOTS * KV_HEADS
    n_w = 2 * KV_HEADS
    past = n_pages * page
    nch = past // CMP_STRIDE
    wb = win_rows.shape[1] // n_w
    n_sel = -(-(past + 1) // SEL_BLOCK)
    n_cmp = (past + 1) // CMP_STRIDE - 1
    topk = min(SEL_TOPK, n_sel)
    assert n_sel <= LANES and nch * CMP_STRIDE == past
    ms = np.zeros((nch, LANES), np.float32)
    ms[:, :n_sel] = _cmp_to_sel(nch, n_cmp, n_sel)
    ek = (np.arange(nch)[None, :] * CMP_STRIDE // SEL_BLOCK == np.arange(LANES)[:, None])
    ms = jnp.asarray(ms, BF16)
    ek = jnp.asarray(ek.astype(np.float32), BF16)
    seq3 = lambda n: pl.BlockSpec((1, n, HEAD_DIM), lambda b, pt: (b, 0, 0))
    full3 = lambda a: pl.BlockSpec(a.shape, lambda b, pt: (0, 0, 0))
    full2 = lambda a: pl.BlockSpec(a.shape, lambda b, pt: (0, 0))
    grid_spec = pltpu.PrefetchScalarGridSpec(
        num_scalar_prefetch=1,
        grid=(nb,),
        in_specs=[
            pl.BlockSpec(memory_space=pl.ANY),
            seq3(N_HEADS), seq3(n_kv), seq3(n_w), seq3(wb * n_w),
            pl.BlockSpec((1, 1, GATE_PAD), lambda b, pt: (b, 0, 0)),
            full3(wcat), full3(pos8), full3(w2), full2(ms), full2(ek),
        ],
        out_specs=seq3(N_HEADS),
        scratch_shapes=[pltpu.VMEM((2, CMP_STRIDE, nch, n_kv, HEAD_DIM), F32),
                        pltpu.SemaphoreType.DMA((2,))],
    )
    return pl.pallas_call(
        functools.partial(_attn_sample_kernel, n_pages=n_pages, page=page, past=past,
                          topk=topk, n_sel=n_sel),
        grid_spec=grid_spec,
        out_shape=jax.ShapeDtypeStruct((nb, N_HEADS, HEAD_DIM), BF16),
        compiler_params=_params(1),
        name="attend_sample",
    )(page_table, cache.reshape(n_phys * page // CMP_STRIDE, CMP_STRIDE, n_kv, HEAD_DIM),
      q.reshape(nb, N_HEADS, HEAD_DIM), kv_new.reshape(nb, n_kv, HEAD_DIM),
      wkv_new.reshape(nb, n_w, HEAD_DIM), win_rows,
      gates.reshape(nb, 1, GATE_PAD), wcat, pos8, w2, ms, ek)


def _mix_tail(o, diffs, x, pw_ref, ps_ref, wo_ref, gpost_ref, gpre_ref, y1_ref, h2_ref):
    gw = diffs[0].shape[1]
    ys = [(_dot(diffs[g].astype(BF16), pw_ref[g]) * ps_ref[:, g * gw:(g + 1) * gw]).astype(BF16)
          for g in range(POOL_GROUPS)]
    cat = jnp.concatenate([o] + ys, axis=1)
    m = _dot(cat, wo_ref[...])
    y1 = x + _rms(m, gpost_ref[...])
    y1_ref[...] = y1
    h2_ref[...] = _rms(y1, gpre_ref[...]).astype(BF16)


def _mix_prompt_kernel(o_ref, u_ref, halo_ref, x_ref, pw_ref, ps_ref, wo_ref, gpost_ref, gpre_ref,
                       y1_ref, h2_ref, *, tm):
    i = pl.program_id(1)
    halo_rows = halo_ref.shape[0]
    halo = jnp.where(i > 0, halo_ref[...], 0.0)
    u = u_ref[...]
    uext = jnp.concatenate([halo, u], axis=0)
    n_ext = uext.shape[0]
    gw = u.shape[1] // POOL_GROUPS
    tpos = i * tm + lax.broadcasted_iota(jnp.int32, (tm, 1), 0)
    diffs = []
    for g, w in enumerate(POOL_WINDOWS):
        s = uext[:, g * gw:(g + 1) * gw]
        k = 1
        while k < w:
            s = s + pltpu.roll(s, k, 0)
            k *= 2
        cnt = jnp.minimum(w, tpos + 1).astype(F32)
        diffs.append(s[halo_rows:n_ext] / cnt - u[:, g * gw:(g + 1) * gw])
    _mix_tail(o_ref[...], diffs, x_ref[...], pw_ref, ps_ref, wo_ref, gpost_ref, gpre_ref,
              y1_ref, h2_ref)


def _mix_sample_kernel(o_ref, u_ref, st_ref, x_ref, pw_ref, ps_ref, wo_ref, gpost_ref, gpre_ref,
                       y1_ref, h2_ref, *, past):
    u = u_ref[...]
    c = u.shape[1]
    gw = c // POOL_GROUPS
    n_hist = st_ref.shape[1] // c
    diffs = []
    for g, w in enumerate(POOL_WINDOWS):
        un = u[:, g * gw:(g + 1) * gw]
        s = un
        for back in range(1, w):
            r = n_hist - back
            s = s + st_ref[:, r * c + g * gw:r * c + (g + 1) * gw]
        diffs.append(s / float(min(w, past + 1)) - un)
    _mix_tail(o_ref[...], diffs, x_ref[...], pw_ref, ps_ref, wo_ref, gpost_ref, gpre_ref,
              y1_ref, h2_ref)


def _mix_specs(tm, d, c, pool_w, idx):
    fixed2 = lambda *a: (0, 0)
    fixed3 = lambda *a: (0, 0, 0)
    weights = [
        pl.BlockSpec(pool_w.shape, fixed3),
        pl.BlockSpec((1, c), fixed2),
        pl.BlockSpec((d, d), fixed2),
        pl.BlockSpec((1, d), fixed2),
        pl.BlockSpec((1, d), fixed2),
    ]
    outs = (pl.BlockSpec((tm, d), idx), pl.BlockSpec((tm, d), idx))
    return weights, outs


def _mix_prompt(o, u, x, pool_w, pool_scale, w_o, g_post, g_pre, batch, seq):
    rows, d = x.shape
    c = u.shape[1]
    tm = min(256, seq)
    nt = seq // tm
    halo = 16
    assert halo >= POOL_BUF and seq % tm == 0 and tm % halo == 0
    idx = lambda b, i: (b * nt + i, 0)
    halo_idx = lambda b, i: (jnp.maximum((b * nt + i) * (tm // halo) - 1, 0), 0)
    weights, outs = _mix_specs(tm, d, c, pool_w, idx)
    return pl.pallas_call(
        functools.partial(_mix_prompt_kernel, tm=tm),
        grid=(batch, nt),
        in_specs=[pl.BlockSpec((tm, ATTN_DIM), idx), pl.BlockSpec((tm, c), idx),
                  pl.BlockSpec((halo, c), halo_idx), pl.BlockSpec((tm, d), idx)] + weights,
        out_specs=outs,
        out_shape=(jax.ShapeDtypeStruct((rows, d), F32), jax.ShapeDtypeStruct((rows, d), BF16)),
        compiler_params=_params(2),
        name="mix_prompt",
    )(o, u, u, x, pool_w, pool_scale, w_o, g_post, g_pre)


def _mix_sample(o, u, pool_state, x, pool_w, pool_scale, w_o, g_post, g_pre, past):
    rows, d = x.shape
    c = u.shape[1]
    tm = rows
    idx = lambda i: (i, 0)
    weights, outs = _mix_specs(tm, d, c, pool_w, idx)
    st = pool_state.reshape(rows, -1)
    return pl.pallas_call(
        functools.partial(_mix_sample_kernel, past=past),
        grid=(rows // tm,),
        in_specs=[pl.BlockSpec((tm, ATTN_DIM), idx), pl.BlockSpec((tm, c), idx),
                  pl.BlockSpec((tm, st.shape[1]), idx), pl.BlockSpec((tm, d), idx)] + weights,
        out_specs=outs,
        out_shape=(jax.ShapeDtypeStruct((rows, d), F32), jax.ShapeDtypeStruct((rows, d), BF16)),
        compiler_params=_params(1),
        name="mix_sample",
    )(o, u, st, x, pool_w, pool_scale, w_o, g_post, g_pre)


def _mlp_kernel(h_ref, wu_ref, wd_ref, y1_ref, g_ref, *rest, shift):
    if shift:
        state_ref, fresh_ref, y_ref, rolled_ref, acc_ref, sem = rest
    else:
        y_ref, acc_ref = rest
    i, j = pl.program_id(0), pl.program_id(1)
    first = (i == 0) & (j == 0)
    last = (i == pl.num_programs(0) - 1) & (j == pl.num_programs(1) - 1)

    def state_copies():
        n_new = fresh_ref.shape[1]
        n_keep = state_ref.shape[1] - n_new
        return (pltpu.make_async_copy(state_ref.at[:, pl.ds(n_new, n_keep), :],
                                      rolled_ref.at[:, pl.ds(0, n_keep), :], sem.at[0]),
                pltpu.make_async_copy(fresh_ref, rolled_ref.at[:, pl.ds(n_keep, n_new), :],
                                      sem.at[1]))

    if shift:
        @pl.when(first)
        def _():
            for c in state_copies():
                c.start()

    @pl.when(j == 0)
    def _():
        acc_ref[...] = jnp.zeros_like(acc_ref)

    a = jnp.maximum(_dot(h_ref[...], wu_ref[...]), 0.0)
    acc_ref[...] += _dot((a * a).astype(BF16), wd_ref[...])

    @pl.when(j == pl.num_programs(1) - 1)
    def _():
        y_ref[...] = y1_ref[...] + _rms(acc_ref[...], g_ref[...])

    if shift:
        @pl.when(last)
        def _():
            for c in state_copies():
                c.wait()


def _mlp(h2, y1, w_up, w_down, gain, tm, state=None, fresh=None):
    rows, d = y1.shape
    ff = w_up.shape[1]
    tf = min(1024, ff)
    row = lambda i, j: (i, 0)
    shift = state is not None
    any_spec = pl.BlockSpec(memory_space=pl.ANY)
    y_spec = pl.BlockSpec((tm, d), row)
    y_shape = jax.ShapeDtypeStruct((rows, d), F32)
    out = pl.pallas_call(
        functools.partial(_mlp_kernel, shift=shift),
        grid=(rows // tm, ff // tf),
        in_specs=[
            pl.BlockSpec((tm, d), row),
            pl.BlockSpec((d, tf), lambda i, j: (0, j)),
            pl.BlockSpec((tf, d), lambda i, j: (j, 0)),
            pl.BlockSpec((tm, d), row),
            pl.BlockSpec((1, d), lambda i, j: (0, 0)),
        ] + ([any_spec, any_spec] if shift else []),
        out_specs=(y_spec, any_spec) if shift else y_spec,
        out_shape=(y_shape, jax.ShapeDtypeStruct(state.shape, state.dtype)) if shift else y_shape,
        scratch_shapes=[pltpu.VMEM((tm, d), F32)]
        + ([pltpu.SemaphoreType.DMA((2,))] if shift else []),
        compiler_params=_params(2),
        name="mlp",
    )(h2, w_up, w_down, y1, gain, *((state, fresh) if shift else ()))
    return out


def _layer_weights(w_in, cmp_pos_k, cmp_w1_k, cmp_w2_k, cmp_pos_v, cmp_w1_v, cmp_w2_v,
                   pool_w, w_o, w_up, w_down, pool_dim):
    gate_lo = WKV_OFF + 2 * KV_DIM
    gate_hi = gate_lo + N_BRANCH * N_HEADS
    w_r = jnp.concatenate(
        [w_in[:, :gate_lo], w_in[:, gate_hi:gate_hi + pool_dim],
         jnp.pad(w_in[:, gate_lo:gate_hi], ((0, 0), (0, GATE_PAD - N_BRANCH * N_HEADS)))],
        axis=1).astype(BF16)
    wcat = jnp.stack([jnp.concatenate([w1[:CMP_HALF], w1[CMP_HALF:]], axis=1)
                      for w1 in (cmp_w1_k, cmp_w1_v)]).astype(BF16)
    pos8 = jnp.stack([jnp.pad(p.reshape(1, CMP_IN), ((0, 7), (0, 0)))
                      for p in (cmp_pos_k, cmp_pos_v)]).astype(BF16)
    w2 = jnp.stack([cmp_w2_k, cmp_w2_v]).astype(BF16)
    return (w_r, wcat, pos8, w2, pool_w.astype(BF16), w_o.astype(BF16),
            w_up.astype(BF16), w_down.astype(BF16))


def kernel(x_prompt, x_sample, cache_kv, state_win_kv, state_pool, page_table, norm_mix_pre, w_in,
           cmp_pos_k, cmp_w1_k, cmp_w2_k, cmp_pos_v, cmp_w1_v, cmp_w2_v, pool_w, pool_scale, w_o,
           norm_mix_post, norm_mlp_pre, w_up, w_down, norm_mlp_post):
    batch, seq, d = x_prompt.shape
    nb, dec_seq, _ = x_sample.shape
    depth = w_in.shape[0]
    pool_dim = d - ATTN_DIM
    n_pages = page_table.shape[1]
    page = cache_kv.shape[2]
    past = n_pages * page
    wb = state_win_kv.shape[2]
    assert dec_seq == 1 and seq >= POOL_BUF and seq % CMP_STRIDE == 0
    assert w_in.shape[2] == ATTN_DIM + 6 * KV_DIM + N_BRANCH * N_HEADS + pool_dim

    tm_p = min(256, seq)
    tabs_p = _rope_tables(jnp.arange(seq, dtype=jnp.int32))
    tabs_s = _rope_tables(jnp.full((nb,), past, jnp.int32))
    nt_p = seq // tm_p

    y_p = x_prompt.reshape(batch * seq, d)
    y_s = x_sample.reshape(nb, d)
    kv_p, kv_s, win_p, win_s, pool_p, pool_s = [], [], [], [], [], []
    row_vec = lambda v: v.reshape(1, -1)
    for l in range(depth):
        w_r, wcat, pos8, w2, pw, wo, wu, wd = _layer_weights(
            w_in[l], cmp_pos_k[l], cmp_w1_k[l], cmp_w2_k[l], cmp_pos_v[l], cmp_w1_v[l],
            cmp_w2_v[l], pool_w[l], w_o[l], w_up[l], w_down[l], pool_dim)
        g_pre, g_post = row_vec(norm_mix_pre[l]), row_vec(norm_mix_post[l])
        g_mlp_pre, g_mlp_post = row_vec(norm_mlp_pre[l]), row_vec(norm_mlp_post[l])
        ps = row_vec(pool_scale[l])

        n_w = 2 * KV_HEADS
        win_rows = state_win_kv[l].reshape(nb, wb * n_w, HEAD_DIM)

        q_s, kv_s1, wkv_s, _, gates_s, u_s = _project(
            y_s, g_pre, w_r, tabs_s, lambda i: (i, 0), nb, pool_dim, False)

        q, kv, wkv, kva, gates, u, xc = _project(
            y_p, g_pre, w_r, tabs_p, lambda i: (i % nt_p, 0), tm_p, pool_dim, True)
        cmp_kv = _compress_prompt(xc, wcat, pos8, w2, batch, seq)
        o = _attend_prompt(q, cmp_kv, kva, gates, batch, seq)
        y1, h2 = _mix_prompt(o, u, y_p, pw, ps, wo, g_post, g_mlp_pre, batch, seq)
        y_p, win_rolled = _mlp(h2, y1, wu, wd, g_mlp_post, min(512, batch * seq),
                               win_rows, wkv_s.reshape(nb, n_w, HEAD_DIM))
        kv_p.append(kv.reshape(batch, seq, CACHE_SLOTS, KV_HEADS, HEAD_DIM))
        wp = min(WINDOW, seq)
        win_p.append(wkv.reshape(batch, seq, 2, KV_HEADS, HEAD_DIM)[:, seq - wp:])
        pool_p.append(u.reshape(batch, seq, pool_dim)[:, seq - POOL_BUF:])

        o = _attend_sample(page_table, cache_kv[l], q_s, kv_s1, wkv_s, win_rows, gates_s,
                           wcat, pos8, w2)
        y1, h2 = _mix_sample(o.reshape(nb, ATTN_DIM), u_s, state_pool[l], y_s, pw, ps, wo,
                             g_post, g_mlp_pre, past)
        y_s = _mlp(h2, y1, wu, wd, g_mlp_post, nb)
        kv_s.append(kv_s1.reshape(nb, 1, CACHE_SLOTS, KV_HEADS, HEAD_DIM))
        win_s.append(win_rolled.reshape(nb, wb, 2, KV_HEADS, HEAD_DIM))
        pool_s.append(jnp.concatenate([state_pool[l], u_s[:, None]], axis=1)[:, 1:])

    return (y_p.reshape(batch, seq, d), y_s.reshape(nb, 1, d),
            jnp.stack(kv_p), jnp.stack(kv_s), jnp.stack(win_p), jnp.stack(win_s),
            jnp.stack(pool_p), jnp.stack(pool_s))
```

```python
import functools

import numpy as np
import jax
import jax.numpy as jnp
from jax import lax
from jax.experimental import pallas as pl
from jax.experimental.pallas import tpu as pltpu

N_HEADS = 8
HEAD_DIM = 128
KV_HEADS = 2
HPG = N_HEADS // KV_HEADS
ATTN_DIM = N_HEADS * HEAD_DIM
KV_DIM = KV_HEADS * HEAD_DIM
N_BRANCH = 3
POOL_WINDOWS = (2, 4, 8, 16)
POOL_GROUPS = len(POOL_WINDOWS)
POOL_BUF = max(POOL_WINDOWS) - 1
ROT_DIM = HEAD_DIM // 4
ROT_HALF = ROT_DIM // 2
ROPE_THETA = 500000.0
CMP_LEN = 32
CMP_STRIDE = 16
SEL_BLOCK = 64
SEL_TOPK = 16
WINDOW = 512
EPS = 1e-6
SCALE = HEAD_DIM ** -0.5
FORCE_SCORE = 1e4
NEG_INF = -1e30

LANES = 128
CACHE_SLOTS = 4
ROW_W = CACHE_SLOTS * KV_DIM
CHUNK_W = CMP_STRIDE * ROW_W
CMP_IN = CMP_LEN * HEAD_DIM
CMP_HALF = CMP_STRIDE * HEAD_DIM
GATE_PAD = LANES
VMEM_LIMIT = 56 * 1024 * 1024

BF16 = jnp.bfloat16
F32 = jnp.float32


def _dot(a, b):
    return jnp.dot(a, b, preferred_element_type=F32)


def _dot_nt(a, b):
    return lax.dot_general(a, b, (((1,), (1,)), ((), ())), preferred_element_type=F32)


def _rms(x, g):
    return x * lax.rsqrt(jnp.mean(x * x, axis=-1, keepdims=True) + EPS) * g


def _params(n_axes):
    return pltpu.CompilerParams(
        dimension_semantics=("arbitrary",) * n_axes, vmem_limit_bytes=VMEM_LIMIT)


def _split_hi_lo(x):
    hi = x.astype(BF16)
    lo = (x - hi.astype(F32)).astype(BF16)
    return hi, lo


Q_OFF, KV_OFF, WKV_OFF = 0, ATTN_DIM, ATTN_DIM + 4 * KV_DIM


def _proj_kernel(x_ref, g_ref, w_ref, cos_ref, sa_ref, sb_ref, *rest, pool_dim, chunked):
    if chunked:
        q_ref, kv_ref, wkv_ref, kva_ref, gate_ref, u_ref, xc_ref, tmp_ref = rest
    else:
        q_ref, kv_ref, wkv_ref, kva_ref, gate_ref, u_ref = rest
    tm = x_ref.shape[0]
    u_off = WKV_OFF + 2 * KV_DIM
    gate_off = u_off + pool_dim
    h = _rms(x_ref[...], g_ref[...]).astype(BF16)
    cos, sa, sb = cos_ref[...], sa_ref[...], sb_ref[...]

    def rope(z):
        return (z * cos + pltpu.roll(z, LANES - ROT_HALF, 1) * sa
                + pltpu.roll(z, ROT_HALF, 1) * sb)

    zq = _dot(h, w_ref[:, Q_OFF:Q_OFF + ATTN_DIM])
    for hd in range(N_HEADS):
        sl = slice(hd * HEAD_DIM, (hd + 1) * HEAD_DIM)
        q_ref[:, sl] = rope(zq[:, sl]).astype(BF16)

    n_kv = CACHE_SLOTS * KV_HEADS
    zkv = _dot(h, w_ref[:, KV_OFF:KV_OFF + 4 * KV_DIM])
    for blk in range(n_kv):
        z = zkv[:, blk * HEAD_DIM:(blk + 1) * HEAD_DIM]
        if (blk // KV_HEADS) % 2 == 0:
            z = rope(z)
        kv_ref[pl.ds(blk, tm, stride=n_kv), :] = z
        if blk >= 2 * KV_HEADS:
            kva_ref[blk - 2 * KV_HEADS] = z.astype(BF16)
        elif chunked:
            tmp_ref[...] = z
            for r in range(CMP_STRIDE):
                xc_ref[blk, :, r * HEAD_DIM:(r + 1) * HEAD_DIM] = (
                    tmp_ref[pl.ds(r, tm // CMP_STRIDE, stride=CMP_STRIDE), :].astype(BF16))

    n_w = 2 * KV_HEADS
    zw = _dot(h, w_ref[:, WKV_OFF:WKV_OFF + 2 * KV_DIM])
    for blk in range(n_w):
        z = zw[:, blk * HEAD_DIM:(blk + 1) * HEAD_DIM]
        if blk < KV_HEADS:
            z = rope(z)
        wkv_ref[pl.ds(blk, tm, stride=n_w), :] = z
        kva_ref[2 * KV_HEADS + blk] = z.astype(BF16)

    u_ref[...] = _dot(h, w_ref[:, u_off:u_off + pool_dim])
    gl = _dot(h, w_ref[:, gate_off:gate_off + GATE_PAD])
    gate_ref[...] = 1.0 / (1.0 + jnp.exp(-gl))


def _project(x, gain, w_r, tables, table_index, tm, pool_dim, chunked):
    rows, d = x.shape
    n_proj = w_r.shape[1]
    n_kv = CACHE_SLOTS * KV_HEADS
    n_w = 2 * KV_HEADS
    row = lambda i: (i, 0)
    fixed = lambda i: (0, 0)
    tab_spec = pl.BlockSpec((tm, LANES), table_index)
    out_shape = [
        jax.ShapeDtypeStruct((rows, ATTN_DIM), BF16),
        jax.ShapeDtypeStruct((rows * n_kv, HEAD_DIM), F32),
        jax.ShapeDtypeStruct((rows * n_w, HEAD_DIM), F32),
        jax.ShapeDtypeStruct((n_w + n_kv // 2, rows, HEAD_DIM), BF16),
        jax.ShapeDtypeStruct((rows, GATE_PAD), F32),
        jax.ShapeDtypeStruct((rows, pool_dim), F32),
    ]
    out_specs = [
        pl.BlockSpec((tm, ATTN_DIM), row),
        pl.BlockSpec((tm * n_kv, HEAD_DIM), row),
        pl.BlockSpec((tm * n_w, HEAD_DIM), row),
        pl.BlockSpec((n_w + n_kv // 2, tm, HEAD_DIM), lambda i: (0, i, 0)),
        pl.BlockSpec((tm, GATE_PAD), row),
        pl.BlockSpec((tm, pool_dim), row),
    ]
    scratch = []
    if chunked:
        out_shape.append(jax.ShapeDtypeStruct((n_kv // 2, rows // CMP_STRIDE, CMP_HALF), BF16))
        out_specs.append(pl.BlockSpec((n_kv // 2, tm // CMP_STRIDE, CMP_HALF), lambda i: (0, i, 0)))
        scratch.append(pltpu.VMEM((tm, HEAD_DIM), F32))
    return pl.pallas_call(
        functools.partial(_proj_kernel, pool_dim=pool_dim, chunked=chunked),
        grid=(rows // tm,),
        in_specs=[
            pl.BlockSpec((tm, d), row),
            pl.BlockSpec((1, d), fixed),
            pl.BlockSpec((d, n_proj), fixed),
            tab_spec, tab_spec, tab_spec,
        ],
        out_specs=tuple(out_specs),
        out_shape=tuple(out_shape),
        scratch_shapes=scratch,
        compiler_params=_params(1),
        name="project",
    )(x, gain, w_r, *tables)


def _rope_tables(pos):
    inv = jnp.power(ROPE_THETA, -jnp.arange(ROT_HALF, dtype=F32) * (2.0 / ROT_DIM))
    ang = pos.astype(F32)[:, None] * inv[None, :]
    cos, sin = jnp.cos(ang), jnp.sin(ang)
    n = pos.shape[0]
    rest = LANES - ROT_DIM
    c = jnp.concatenate([cos, cos, jnp.ones((n, rest), F32)], axis=1)
    sa = jnp.concatenate([-sin, jnp.zeros((n, LANES - ROT_HALF), F32)], axis=1)
    sb = jnp.concatenate([jnp.zeros((n, ROT_HALF), F32), sin, jnp.zeros((n, rest), F32)], axis=1)
    return c, sa, sb


def _gelu_tanh(x):
    return 0.5 * x * (1.0 + jnp.tanh(0.7978845608028654 * (x + 0.044715 * (x * x * x))))


def _compress_first(x, wcat, pos8):
    ab = _dot(x, wcat)
    pa = _dot(pos8[:, :CMP_HALF], wcat)[0:1, :HEAD_DIM]
    pb = _dot(pos8[:, CMP_HALF:], wcat)[0:1, HEAD_DIM:]
    return ab, pa + pb


def _compress_hidden(ab, pos_term):
    rows = ab.shape[0]
    b_next = pltpu.roll(ab[:, HEAD_DIM:], rows - 1, 0)
    return _gelu_tanh(ab[:, :HEAD_DIM] + b_next + pos_term).astype(BF16)


def _compress_kernel(x_ref, wcat_ref, pos_ref, w2_ref, o_ref):
    g, _, nck, width = x_ref.shape
    x = x_ref[...].reshape(g * nck, width)
    hid = _compress_hidden(*_compress_first(x, wcat_ref[0], pos_ref[0]))
    o_ref[...] = _dot(hid, w2_ref[0]).astype(BF16).reshape(o_ref.shape)


def _compress_prompt(xc, wcat, pos8, w2, batch, seq):
    nck = seq // CMP_STRIDE
    x = xc.reshape(xc.shape[0], batch, nck, CMP_HALF)
    return pl.pallas_call(
        _compress_kernel,
        grid=(batch, 2),
        in_specs=[
            pl.BlockSpec((KV_HEADS, 1, nck, CMP_HALF), lambda b, s: (s, b, 0, 0)),
            pl.BlockSpec((1, CMP_HALF, 2 * HEAD_DIM), lambda b, s: (s, 0, 0)),
            pl.BlockSpec((1, 8, CMP_IN), lambda b, s: (s, 0, 0)),
            pl.BlockSpec((1, HEAD_DIM, HEAD_DIM), lambda b, s: (s, 0, 0)),
        ],
        out_specs=pl.BlockSpec((1, 1, KV_HEADS, nck, HEAD_DIM), lambda b, s: (b, s, 0, 0, 0)),
        out_shape=jax.ShapeDtypeStruct((batch, 2, KV_HEADS, nck, HEAD_DIM), BF16),
        compiler_params=_params(2),
        name="compress_prompt",
    )(x, wcat, pos8, w2)


def _topk_rows(score, jidx, topk):
    rank = jnp.zeros_like(score)
    for j in range(score.shape[0]):
        bj = score[j:j + 1, :]
        tie = jnp.where(jidx > j, 1.0, 0.0)
        rank = rank + jnp.where(bj > score, 1.0, jnp.where(bj == score, tie, 0.0))
    return jnp.where(rank < topk, 1.0, 0.0)


def _softmax_parts(s, ok):
    s = jnp.where(ok, s, NEG_INF)
    m = jnp.max(s, axis=-1, keepdims=True)
    e = jnp.where(ok, jnp.exp(s - m), 0.0)
    d = jnp.sum(e, axis=-1, keepdims=True)
    return e, jnp.where(d > 0, d, 1.0)


EXP2_SCALE = SCALE * 1.4426950408889634
HEAD_PAIRS = HPG // 2


def _attn_prompt_kernel(q_ref, kc_ref, vc_ref, ks_ref, vs_ref, kw_ref, vw_ref, gate_ref,
                        mt_ref, eye_ref, o_ref,
                        vct_ref, vst_ref, vwt_ref, sel_ref, gt_ref, acc_ref,
                        *, tq, tk, seq, topk):
    g = pl.program_id(1)
    i = pl.program_id(2)
    t0 = i * tq
    n_sel = mt_ref.shape[0]
    pair_w = 2 * tq
    eye = eye_ref[...]

    @pl.when(i == 0)
    def _():
        vct_ref[...] = _dot_nt(eye, vc_ref[0, 0, 0]).astype(BF16)
        vst_ref[...] = _dot_nt(eye, vs_ref[0]).astype(BF16)
        vwt_ref[...] = _dot_nt(eye, vw_ref[0]).astype(BF16)

    q4 = q_ref[...]
    q_pairs = [jnp.concatenate([q4[:, (2 * hp) * HEAD_DIM:(2 * hp + 1) * HEAD_DIM],
                                q4[:, (2 * hp + 1) * HEAD_DIM:(2 * hp + 2) * HEAD_DIM]], axis=0)
               for hp in range(HEAD_PAIRS)]

    def both_heads(x):
        return jnp.concatenate([x, x], axis=1)

    def tpos(n_keys):
        return t0 + lax.broadcasted_iota(jnp.int32, (n_keys, tq), 1)

    def kidx(n_keys):
        return lax.broadcasted_iota(jnp.int32, (n_keys, tq), 0)

    pairs = range(HEAD_PAIRS)

    kc = kc_ref[0, 0, 0]
    ncp = kc.shape[0]
    wk = min(WINDOW + tq, seq)
    ws = pl.multiple_of(jnp.maximum(t0 + tq - wk, 0), tq)
    kw = kw_ref[0, pl.ds(ws, wk), :]
    s_cmp = [_dot_nt(kc, q_pairs[hp]) for hp in pairs]
    s_win = [_dot_nt(kw, q_pairs[hp]) for hp in pairs]

    ok = both_heads(jnp.where(kidx(ncp) * CMP_STRIDE + (CMP_LEN - 1) <= tpos(ncp), 1.0, 0.0)) > 0.5
    p_cmp, p_sum = [], None
    for hp in pairs:
        s = jnp.where(ok, s_cmp[hp], NEG_INF)
        m = jnp.max(s, axis=0, keepdims=True)
        e = jnp.where(ok, jnp.exp2((s - m) * EXP2_SCALE), 0.0)
        d = jnp.sum(e, axis=0, keepdims=True)
        p = e / jnp.where(d > 0, d, 1.0)
        p_cmp.append(p.astype(BF16))
        ph = p[:, :tq] + p[:, tq:]
        p_sum = ph if p_sum is None else p_sum + ph

    p_hi, p_lo = _split_hi_lo(p_sum)
    mt = mt_ref[...]
    imp = _dot(mt, p_hi) + _dot(mt, p_lo)
    o_cmp = [_dot(vct_ref[...], p_cmp[hp]) for hp in pairs]

    kpos = ws + kidx(wk)
    bias = both_heads(jnp.where(kpos <= tpos(wk),
                                jnp.where(kpos > tpos(wk) - WINDOW, 0.0, NEG_INF), NEG_INF))
    p_win, l_win = [], []
    for hp in pairs:
        s = s_win[hp] + bias
        p = jnp.exp2((s - jnp.max(s, axis=0, keepdims=True)) * EXP2_SCALE)
        l_win.append(jnp.sum(p, axis=0, keepdims=True))
        p_win.append(p.astype(BF16))
    vwt = vwt_ref[:, pl.ds(ws, wk)]
    o_win = [_dot(vwt, p_win[hp]) for hp in pairs]
    o_win = [o_win[hp] / l_win[hp] for hp in pairs]

    jidx = kidx(n_sel)
    jt = tpos(n_sel) // SEL_BLOCK
    forced = jnp.where(jidx == 0, 1.0, jnp.where(jidx == jt, 1.0, jnp.where(jidx == jt - 1, 1.0, 0.0)))
    score = jnp.where(forced > 0.5, FORCE_SCORE, jnp.where(jidx <= jt, imp, -1.0))
    sel_ref[...] = _topk_rows(score, jidx, topk)

    acc_ref[...] = jnp.zeros_like(acc_ref)

    def sel_step(kb, carry):
        k0 = pl.multiple_of(kb * tk, tk)
        k = ks_ref[0, pl.ds(k0, tk), :]
        vt = vst_ref[:, pl.ds(k0, tk)]
        blk0 = kb * (tk // SEL_BLOCK)
        chosen = jnp.concatenate(
            [jnp.broadcast_to(sel_ref[pl.ds(blk0 + j, 1), :], (SEL_BLOCK, tq))
             for j in range(tk // SEL_BLOCK)], axis=0)
        bias = both_heads(jnp.where(k0 + kidx(tk) <= tpos(tk),
                                    jnp.where(chosen > 0.5, 0.0, NEG_INF), NEG_INF))
        ss = [_dot_nt(k, q_pairs[hp]) + bias for hp in pairs]
        out, ps, alphas = [], [], []
        for hp in pairs:
            m, l = carry[2 * hp], carry[2 * hp + 1]
            m_new = jnp.maximum(m, jnp.max(ss[hp], axis=0, keepdims=True))
            p = jnp.exp2((ss[hp] - m_new) * EXP2_SCALE)
            alpha = jnp.exp2((m - m_new) * EXP2_SCALE)
            out += [m_new, alpha * l + jnp.sum(p, axis=0, keepdims=True)]
            ps.append(p.astype(BF16))
            alphas.append(alpha)
        pvs = [_dot(vt, ps[hp]) for hp in pairs]
        for hp in pairs:
            acc_ref[hp] = alphas[hp] * acc_ref[hp] + pvs[hp]
        return tuple(out)

    n_kb = (t0 + tq + tk - 1) // tk
    init = (jnp.full((1, pair_w), NEG_INF, F32), jnp.zeros((1, pair_w), F32)) * HEAD_PAIRS
    stats = lax.fori_loop(0, n_kb, sel_step, init)
    o_sel = [acc_ref[hp] / stats[2 * hp + 1] for hp in pairs]

    gt_ref[...] = gate_ref[...].T
    o_t = []
    for h in range(HPG):
        hp, lanes = h // 2, slice((h % 2) * tq, (h % 2 + 1) * tq)
        col = (g * HPG + h) * N_BRANCH
        o_t.append((gt_ref[pl.ds(col, 1), :] * o_cmp[hp][:, lanes]
                    + gt_ref[pl.ds(col + 1, 1), :] * o_sel[hp][:, lanes]
                    + gt_ref[pl.ds(col + 2, 1), :] * o_win[hp][:, lanes]).astype(BF16))
    outs = [_dot_nt(eye, o_t[h]) for h in range(HPG)]
    for h in range(HPG):
        o_ref[:, h * HEAD_DIM:(h + 1) * HEAD_DIM] = outs[h].astype(BF16)


def _cmp_to_sel(n_cmp_pad, n_cmp, n_sel):
    cs = np.arange(n_cmp_pad)[:, None] * CMP_STRIDE
    ss = np.arange(n_sel)[None, :] * SEL_BLOCK
    hit = (cs < ss + SEL_BLOCK) & (cs + CMP_LEN > ss) & (np.arange(n_cmp_pad)[:, None] < n_cmp)
    return hit.astype(np.float32)


def _attend_prompt(q, cmp_kv, kva, gates, batch, seq):
    tq = min(128, seq)
    tk = min(512, seq)
    nq = seq // tq
    ncp = seq // CMP_STRIDE
    n_sel = -(-seq // SEL_BLOCK)
    topk = min(SEL_TOPK, n_sel)
    assert tq == HEAD_DIM and seq % tk == 0 and WINDOW % tq == 0
    mt = jnp.asarray(_cmp_to_sel(ncp, ncp - 1, n_sel).T, BF16)
    eye = jnp.asarray(np.eye(tq, dtype=np.float32), BF16)
    kv_spec = lambda slot: pl.BlockSpec(
        (1, seq, HEAD_DIM), lambda b, g, i: (slot * KV_HEADS + g, b, 0))
    cmp_spec = lambda s: pl.BlockSpec(
        (1, 1, 1, ncp, HEAD_DIM), lambda b, g, i: (b, s, g, 0, 0))
    full = lambda a: pl.BlockSpec(a.shape, lambda b, g, i: (0, 0))
    return pl.pallas_call(
        functools.partial(_attn_prompt_kernel, tq=tq, tk=tk, seq=seq, topk=topk),
        grid=(batch, KV_HEADS, nq),
        in_specs=[
            pl.BlockSpec((tq, HPG * HEAD_DIM), lambda b, g, i: (b * nq + i, g)),
            cmp_spec(0), cmp_spec(1),
            kv_spec(0), kv_spec(1), kv_spec(2), kv_spec(3),
            pl.BlockSpec((tq, GATE_PAD), lambda b, g, i: (b * nq + i, 0)),
            full(mt), full(eye),
        ],
        out_specs=pl.BlockSpec((tq, HPG * HEAD_DIM), lambda b, g, i: (b * nq + i, g)),
        out_shape=jax.ShapeDtypeStruct((batch * seq, ATTN_DIM), BF16),
        scratch_shapes=[
            pltpu.VMEM((HEAD_DIM, ncp), BF16),
            pltpu.VMEM((HEAD_DIM, seq), BF16),
            pltpu.VMEM((HEAD_DIM, seq), BF16),
            pltpu.VMEM((n_sel, tq), F32),
            pltpu.VMEM((GATE_PAD, tq), F32),
            pltpu.VMEM((HEAD_PAIRS, HEAD_DIM, 2 * tq), F32),
        ],
        compiler_params=_params(3),
        name="attend_prompt",
    )(q, cmp_kv, cmp_kv, kva, kva, kva, kva, gates, mt, eye)


def _attn_sample_kernel(pt_ref, cache_ref, q_ref, kvn_ref, wn_ref, win_ref, gate_ref,
                        wcat_ref, pos_ref, w2_ref, ms_ref, ek_ref, o_ref, buf, sem,
                        *, n_pages, page, past, topk, n_sel):
    b = pl.program_id(0)
    nb = pl.num_programs(0)
    n_kv = CACHE_SLOTS * KV_HEADS
    n_w = 2 * KV_HEADS
    cpp = page // CMP_STRIDE
    nch = past // CMP_STRIDE
    wb = win_ref.shape[1] // n_w

    def page_copy(seq_idx, slot, pi, r):
        src0 = pl.multiple_of(pt_ref[seq_idx, pi] * cpp, cpp)
        return pltpu.make_async_copy(
            cache_ref.at[pl.ds(src0, cpp), r],
            buf.at[slot, r, pl.ds(pl.multiple_of(pi * cpp, cpp), cpp)], sem.at[slot])

    def for_all_copies(seq_idx, slot, act):
        def per_page(pi, carry):
            for r in range(CMP_STRIDE):
                act(page_copy(seq_idx, slot, pi, r))
            return carry
        lax.fori_loop(0, n_pages, per_page, 0)

    @pl.when(b == 0)
    def _():
        for_all_copies(0, 0, lambda c: c.start())

    @pl.when(b + 1 < nb)
    def _():
        for_all_copies(b + 1, (b + 1) % 2, lambda c: c.start())

    slot = b % 2
    for_all_copies(b, slot, lambda c: c.wait())

    def chunk_rows(cache_slot, g, r):
        rows = buf.reshape(2, CMP_STRIDE, nch * n_kv, HEAD_DIM)
        return rows[slot, r, pl.ds(cache_slot * KV_HEADS + g, nch, stride=n_kv), :]

    row8 = lax.broadcasted_iota(jnp.int32, (N_HEADS, 1), 0)
    in_g0 = row8 < HPG

    def by_group(x0, x1):
        return jnp.where(in_g0, x0, x1)

    q8 = q_ref[0]
    qf = q8.astype(F32)
    groups = range(KV_HEADS)

    def scores(keys):
        return by_group(_dot_nt(q8, keys[0]), _dot_nt(q8, keys[1])) * SCALE

    def cached(cache_slot, g):
        return jnp.concatenate([chunk_rows(cache_slot, g, r).astype(BF16)
                                for r in range(CMP_STRIDE)], axis=0)

    def compress_input(cache_slot):
        return jnp.concatenate(
            [jnp.concatenate([chunk_rows(cache_slot, g, r).astype(BF16)
                              for r in range(CMP_STRIDE)], axis=1)
             for g in groups], axis=0)

    first = [_compress_first(compress_input(cs), wcat_ref[cs], pos_ref[cs]) for cs in range(2)]
    s_sel = scores([cached(2, g) for g in groups])
    win_keys = [win_ref[0, pl.ds(g, wb, stride=n_w), :].astype(BF16) for g in groups]
    s_win = scores(win_keys)
    kc, vc =[_dot(_compress_hidden(*first[cs]), w2_ref[cs]).astype(BF16) for cs in range(2)]

    lane_n = lax.broadcasted_iota(jnp.int32, (N_HEADS, nch), 1)
    s = by_group(_dot_nt(q8, kc[:nch]), _dot_nt(q8, kc[nch:])) * SCALE
    ok = lane_n * CMP_STRIDE + (CMP_LEN - 1) <= past
    e, d = _softmax_parts(s, ok)
    p = e / d
    pb = p.astype(BF16)
    o_cmp = by_group(_dot(pb, vc[:nch]), _dot(pb, vc[nch:]))

    p_g = [jnp.sum(p[g * HPG:(g + 1) * HPG], axis=0, keepdims=True) for g in range(KV_HEADS)]
    p2 = jnp.concatenate(p_g + [jnp.zeros((N_HEADS - KV_HEADS, nch), F32)], axis=0)
    p_hi, p_lo = _split_hi_lo(p2)
    imp = _dot(p_hi, ms_ref[...]) + _dot(p_lo, ms_ref[...])
    jl = lax.broadcasted_iota(jnp.int32, (N_HEADS, LANES), 1)
    jt = past // SEL_BLOCK
    forced = jnp.where(jl == 0, 1.0, jnp.where(jl == jt, 1.0, jnp.where(jl == jt - 1, 1.0, 0.0)))
    score = jnp.where(forced > 0.5, FORCE_SCORE, jnp.where(jl <= jt, imp, -1.0))
    score = jnp.where(jl < n_sel, score, -2.0)
    ii = lax.broadcasted_iota(jnp.int32, (LANES, LANES), 0)
    jj = lax.broadcasted_iota(jnp.int32, (LANES, LANES), 1)
    sel_rows = []
    for g in range(KV_HEADS):
        srow = jnp.broadcast_to(score[g:g + 1, :], (LANES, LANES))
        scol = jnp.sum(jnp.where(ii == jj, srow, 0.0), axis=1, keepdims=True)
        tie = jnp.where(ii < jj, 1.0, 0.0)
        beats = jnp.where(scol > srow, 1.0, jnp.where(scol == srow, tie, 0.0))
        rank = jnp.sum(beats, axis=0, keepdims=True)
        sel_rows.append(jnp.where(rank < topk, 1.0, 0.0))
    sel2 = jnp.concatenate(sel_rows + [jnp.zeros((N_HEADS - KV_HEADS, LANES), F32)], axis=0)
    chunk_ok2 = _dot(sel2.astype(BF16), ek_ref[...])
    chunk_ok = by_group(chunk_ok2[0:1], chunk_ok2[1:2])
    key_ok = jnp.concatenate([chunk_ok] * CMP_STRIDE, axis=1)
    new_ok = by_group(*[jnp.sum(jnp.where(jl[0:1] == jt, sel2[g:g + 1], 0.0), axis=1, keepdims=True)
                        for g in range(KV_HEADS)])

    def new_row(ref, idx0):
        x = by_group(ref[0, idx0:idx0 + 1, :], ref[0, idx0 + 1:idx0 + 2, :])
        return x.astype(BF16).astype(F32)

    def weights(s, ok, k_new, new_ok):
        s_new = jnp.sum(qf * k_new, axis=-1, keepdims=True) * SCALE
        s = jnp.where(ok, s, NEG_INF)
        s_new = jnp.where(new_ok, s_new, NEG_INF)
        m = jnp.maximum(jnp.max(s, axis=-1, keepdims=True), s_new)
        e = jnp.where(ok, jnp.exp(s - m), 0.0)
        e_new = jnp.where(new_ok, jnp.exp(s_new - m), 0.0)
        d = jnp.sum(e, axis=-1, keepdims=True) + e_new
        return e.astype(BF16), e_new, jnp.where(d > 0, d, 1.0)

    def weighted(eb, vals, e_new, v_new, d):
        return (by_group(_dot(eb, vals[0]), _dot(eb, vals[1])) + e_new * v_new) / d

    kpos = past - wb + lax.broadcasted_iota(jnp.int32, (N_HEADS, wb), 1)
    w_sel = weights(s_sel, key_ok > 0.5, new_row(kvn_ref, 2 * KV_HEADS), new_ok > 0.5)
    w_win = weights(s_win, kpos > past - WINDOW, new_row(wn_ref, 0), row8 >= 0)
    win_vals = [win_ref[0, pl.ds(KV_HEADS + g, wb, stride=n_w), :].astype(BF16) for g in groups]
    o_sel = weighted(w_sel[0], [cached(3, g) for g in groups], w_sel[1],
                     new_row(kvn_ref, 3 * KV_HEADS), w_sel[2])
    o_win = weighted(w_win[0], win_vals, w_win[1], new_row(wn_ref, KV_HEADS), w_win[2])

    gates = jnp.broadcast_to(gate_ref[0], (N_HEADS, GATE_PAD))
    lane = lax.broadcasted_iota(jnp.int32, (N_HEADS, GATE_PAD), 1)

    def gate(br):
        return jnp.sum(jnp.where(lane == row8 * N_BRANCH + br, gates, 0.0), axis=-1, keepdims=True)

    o_ref[0] = (gate(0) * o_cmp + gate(1) * o_sel + gate(2) * o_win).astype(BF16)


def _attend_sample(page_table, cache, q, kv_new, wkv_new, win_rows, gates, wcat, pos8, w2):
    nb, n_pages = page_table.shape
    n_phys, page = cache.shape[:2]
    n_kv = CACHE_SLOTS * KV_HEADS
    n_w = 2 * KV_HEADS
    past = n_pages * page
    nch = past // CMP_STRIDE
    wb = win_rows.shape[1] // n_w
    n_sel = -(-(past + 1) // SEL_BLOCK)
    n_cmp = (past + 1) // CMP_STRIDE - 1
    topk = min(SEL_TOPK, n_sel)
    assert n_sel <= LANES and nch * CMP_STRIDE == past
    ms = np.zeros((nch, LANES), np.float32)
    ms[:, :n_sel] = _cmp_to_sel(nch, n_cmp, n_sel)
    ek = (np.arange(nch)[None, :] * CMP_STRIDE // SEL_BLOCK == np.arange(LANES)[:, None])
    ms = jnp.asarray(ms, BF16)
    ek = jnp.asarray(ek.astype(np.float32), BF16)
    seq3 = lambda n: pl.BlockSpec((1, n, HEAD_DIM), lambda b, pt: (b, 0, 0))
    full3 = lambda a: pl.BlockSpec(a.shape, lambda b, pt: (0, 0, 0))
    full2 = lambda a: pl.BlockSpec(a.shape, lambda b, pt: (0, 0))
    grid_spec = pltpu.PrefetchScalarGridSpec(
        num_scalar_prefetch=1,
        grid=(nb,),
        in_specs=[
            pl.BlockSpec(memory_space=pl.ANY),
            seq3(N_HEADS), seq3(n_kv), seq3(n_w), seq3(wb * n_w),
            pl.BlockSpec((1, 1, GATE_PAD), lambda b, pt: (b, 0, 0)),
            full3(wcat), full3(pos8), full3(w2), full2(ms), full2(ek),
        ],
        out_specs=seq3(N_HEADS),
        scratch_shapes=[pltpu.VMEM((2, CMP_STRIDE, nch, n_kv, HEAD_DIM), F32),
                        pltpu.SemaphoreType.DMA((2,))],
    )
    return pl.pallas_call(
        functools.partial(_attn_sample_kernel, n_pages=n_pages, page=page, past=past,
                          topk=topk, n_sel=n_sel),
        grid_spec=grid_spec,
        out_shape=jax.ShapeDtypeStruct((nb, N_HEADS, HEAD_DIM), BF16),
        compiler_params=_params(1),
        name="attend_sample",
    )(page_table, cache.reshape(n_phys * page // CMP_STRIDE, CMP_STRIDE, n_kv, HEAD_DIM),
      q.reshape(nb, N_HEADS, HEAD_DIM), kv_new.reshape(nb, n_kv, HEAD_DIM),
      wkv_new.reshape(nb, n_w, HEAD_DIM), win_rows,
      gates.reshape(nb, 1, GATE_PAD), wcat, pos8, w2, ms, ek)


def _mix_tail(o, diffs, x, pw_ref, ps_ref, wo_ref, gpost_ref, gpre_ref, y1_ref, h2_ref):
    gw = diffs[0].shape[1]
    ys = [(_dot(diffs[g].astype(BF16), pw_ref[g]) * ps_ref[:, g * gw:(g + 1) * gw]).astype(BF16)
          for g in range(POOL_GROUPS)]
    cat = jnp.concatenate([o] + ys, axis=1)
    m = _dot(cat, wo_ref[...])
    y1 = x + _rms(m, gpost_ref[...])
    y1_ref[...] = y1
    h2_ref[...] = _rms(y1, gpre_ref[...]).astype(BF16)


def _mix_prompt_kernel(o_ref, u_ref, halo_ref, x_ref, pw_ref, ps_ref, wo_ref, gpost_ref, gpre_ref,
                       y1_ref, h2_ref, *, tm):
    i = pl.program_id(1)
    halo_rows = halo_ref.shape[0]
    halo = jnp.where(i > 0, halo_ref[...], 0.0)
    u = u_ref[...]
    uext = jnp.concatenate([halo, u], axis=0)
    n_ext = uext.shape[0]
    gw = u.shape[1] // POOL_GROUPS
    tpos = i * tm + lax.broadcasted_iota(jnp.int32, (tm, 1), 0)
    diffs = []
    for g, w in enumerate(POOL_WINDOWS):
        s = uext[:, g * gw:(g + 1) * gw]
        k = 1
        while k < w:
            s = s + pltpu.roll(s, k, 0)
            k *= 2
        cnt = jnp.minimum(w, tpos + 1).astype(F32)
        diffs.append(s[halo_rows:n_ext] / cnt - u[:, g * gw:(g + 1) * gw])
    _mix_tail(o_ref[...], diffs, x_ref[...], pw_ref, ps_ref, wo_ref, gpost_ref, gpre_ref,
              y1_ref, h2_ref)


def _mix_sample_kernel(o_ref, u_ref, st_ref, x_ref, pw_ref, ps_ref, wo_ref, gpost_ref, gpre_ref,
                       y1_ref, h2_ref, *, past):
    u = u_ref[...]
    c = u.shape[1]
    gw = c // POOL_GROUPS
    n_hist = st_ref.shape[1] // c
    diffs = []
    for g, w in enumerate(POOL_WINDOWS):
        un = u[:, g * gw:(g + 1) * gw]
        s = un
        for back in range(1, w):
            r = n_hist - back
            s = s + st_ref[:, r * c + g * gw:r * c + (g + 1) * gw]
        diffs.append(s / float(min(w, past + 1)) - un)
    _mix_tail(o_ref[...], diffs, x_ref[...], pw_ref, ps_ref, wo_ref, gpost_ref, gpre_ref,
              y1_ref, h2_ref)


def _mix_specs(tm, d, c, pool_w, idx):
    fixed2 = lambda *a: (0, 0)
    fixed3 = lambda *a: (0, 0, 0)
    weights = [
        pl.BlockSpec(pool_w.shape, fixed3),
        pl.BlockSpec((1, c), fixed2),
        pl.BlockSpec((d, d), fixed2),
        pl.BlockSpec((1, d), fixed2),
        pl.BlockSpec((1, d), fixed2),
    ]
    outs = (pl.BlockSpec((tm, d), idx), pl.BlockSpec((tm, d), idx))
    return weights, outs


def _mix_prompt(o, u, x, pool_w, pool_scale, w_o, g_post, g_pre, batch, seq):
    rows, d = x.shape
    c = u.shape[1]
    tm = min(256, seq)
    nt = seq // tm
    halo = 16
    assert halo >= POOL_BUF and seq % tm == 0 and tm % halo == 0
    idx = lambda b, i: (b * nt + i, 0)
    halo_idx = lambda b, i: (jnp.maximum((b * nt + i) * (tm // halo) - 1, 0), 0)
    weights, outs = _mix_specs(tm, d, c, pool_w, idx)
    return pl.pallas_call(
        functools.partial(_mix_prompt_kernel, tm=tm),
        grid=(batch, nt),
        in_specs=[pl.BlockSpec((tm, ATTN_DIM), idx), pl.BlockSpec((tm, c), idx),
                  pl.BlockSpec((halo, c), halo_idx), pl.BlockSpec((tm, d), idx)] + weights,
        out_specs=outs,
        out_shape=(jax.ShapeDtypeStruct((rows, d), F32), jax.ShapeDtypeStruct((rows, d), BF16)),
        compiler_params=_params(2),
        name="mix_prompt",
    )(o, u, u, x, pool_w, pool_scale, w_o, g_post, g_pre)


def _mix_sample(o, u, pool_state, x, pool_w, pool_scale, w_o, g_post, g_pre, past):
    rows, d = x.shape
    c = u.shape[1]
    tm = rows
    idx = lambda i: (i, 0)
    weights, outs = _mix_specs(tm, d, c, pool_w, idx)
    st = pool_state.reshape(rows, -1)
    return pl.pallas_call(
        functools.partial(_mix_sample_kernel, past=past),
        grid=(rows // tm,),
        in_specs=[pl.BlockSpec((tm, ATTN_DIM), idx), pl.BlockSpec((tm, c), idx),
                  pl.BlockSpec((tm, st.shape[1]), idx), pl.BlockSpec((tm, d), idx)] + weights,
        out_specs=outs,
        out_shape=(jax.ShapeDtypeStruct((rows, d), F32), jax.ShapeDtypeStruct((rows, d), BF16)),
        compiler_params=_params(1),
        name="mix_sample",
    )(o, u, st, x, pool_w, pool_scale, w_o, g_post, g_pre)


def _mlp_kernel(h_ref, wu_ref, wd_ref, y1_ref, g_ref, *rest, shift):
    if shift:
        state_ref, fresh_ref, y_ref, rolled_ref, acc_ref, sem = rest
    else:
        y_ref, acc_ref = rest
    i, j = pl.program_id(0), pl.program_id(1)
    first = (i == 0) & (j == 0)
    last = (i == pl.num_programs(0) - 1) & (j == pl.num_programs(1) - 1)

    def state_copies():
        n_new = fresh_ref.shape[1]
        n_keep = state_ref.shape[1] - n_new
        return (pltpu.make_async_copy(state_ref.at[:, pl.ds(n_new, n_keep)],
                                      rolled_ref.at[:, pl.ds(0, n_keep)], sem.at[0]),
                pltpu.make_async_copy(fresh_ref, rolled_ref.at[:, pl.ds(n_keep, n_new)],
                                      sem.at[1]))

    if shift:
        @pl.when(first)
        def _():
            for c in state_copies():
                c.start()

    @pl.when(j == 0)
    def _():
        acc_ref[...] = jnp.zeros_like(acc_ref)

    a = jnp.maximum(_dot(h_ref[...], wu_ref[...]), 0.0)
    acc_ref[...] += _dot((a * a).astype(BF16), wd_ref[...])

    @pl.when(j == pl.num_programs(1) - 1)
    def _():
        y_ref[...] = y1_ref[...] + _rms(acc_ref[...], g_ref[...])

    if shift:
        @pl.when(last)
        def _():
            for c in state_copies():
                c.wait()


def _mlp(h2, y1, w_up, w_down, gain, tm, state=None, fresh=None):
    rows, d = y1.shape
    ff = w_up.shape[1]
    tf = min(1024, ff)
    row = lambda i, j: (i, 0)
    shift = state is not None
    any_spec = pl.BlockSpec(memory_space=pl.ANY)
    y_spec = pl.BlockSpec((tm, d), row)
    y_shape = jax.ShapeDtypeStruct((rows, d), F32)
    out = pl.pallas_call(
        functools.partial(_mlp_kernel, shift=shift),
        grid=(rows // tm, ff // tf),
        in_specs=[
            pl.BlockSpec((tm, d), row),
            pl.BlockSpec((d, tf), lambda i, j: (0, j)),
            pl.BlockSpec((tf, d), lambda i, j: (j, 0)),
            pl.BlockSpec((tm, d), row),
            pl.BlockSpec((1, d), lambda i, j: (0, 0)),
        ] + ([any_spec, any_spec] if shift else []),
        out_specs=(y_spec, any_spec) if shift else y_spec,
        out_shape=(y_shape, jax.ShapeDtypeStruct(state.shape, state.dtype)) if shift else y_shape,
        scratch_shapes=[pltpu.VMEM((tm, d), F32)]
        + ([pltpu.SemaphoreType.DMA((2,))] if shift else []),
        compiler_params=_params(2),
        name="mlp",
    )(h2, w_up, w_down, y1, gain, *((state, fresh) if shift else ()))
    return out


def _layer_weights(w_in, cmp_pos_k, cmp_w1_k, cmp_w2_k, cmp_pos_v, cmp_w1_v, cmp_w2_v,
                   pool_w, w_o, w_up, w_down, pool_dim):
    gate_lo = WKV_OFF + 2 * KV_DIM
    gate_hi = gate_lo + N_BRANCH * N_HEADS
    w_r = jnp.concatenate(
        [w_in[:, :gate_lo], w_in[:, gate_hi:gate_hi + pool_dim],
         jnp.pad(w_in[:, gate_lo:gate_hi], ((0, 0), (0, GATE_PAD - N_BRANCH * N_HEADS)))],
        axis=1).astype(BF16)
    wcat = jnp.stack([jnp.concatenate([w1[:CMP_HALF], w1[CMP_HALF:]], axis=1)
                      for w1 in (cmp_w1_k, cmp_w1_v)]).astype(BF16)
    pos8 = jnp.stack([jnp.pad(p.reshape(1, CMP_IN), ((0, 7), (0, 0)))
                      for p in (cmp_pos_k, cmp_pos_v)]).astype(BF16)
    w2 = jnp.stack([cmp_w2_k, cmp_w2_v]).astype(BF16)
    return (w_r, wcat, pos8, w2, pool_w.astype(BF16), w_o.astype(BF16),
            w_up.astype(BF16), w_down.astype(BF16))


def kernel(x_prompt, x_sample, cache_kv, state_win_kv, state_pool, page_table, norm_mix_pre, w_in,
           cmp_pos_k, cmp_w1_k, cmp_w2_k, cmp_pos_v, cmp_w1_v, cmp_w2_v, pool_w, pool_scale, w_o,
           norm_mix_post, norm_mlp_pre, w_up, w_down, norm_mlp_post):
    batch, seq, d = x_prompt.shape
    nb, dec_seq, _ = x_sample.shape
    depth = w_in.shape[0]
    pool_dim = d - ATTN_DIM
    n_pages = page_table.shape[1]
    page = cache_kv.shape[2]
    past = n_pages * page
    wb = state_win_kv.shape[2]
    assert dec_seq == 1 and seq >= POOL_BUF and seq % CMP_STRIDE == 0
    assert w_in.shape[2] == ATTN_DIM + 6 * KV_DIM + N_BRANCH * N_HEADS + pool_dim

    tm_p = min(256, seq)
    tabs_p = _rope_tables(jnp.arange(seq, dtype=jnp.int32))
    tabs_s = _rope_tables(jnp.full((nb,), past, jnp.int32))
    nt_p = seq // tm_p

    y_p = x_prompt.reshape(batch * seq, d)
    y_s = x_sample.reshape(nb, d)
    kv_p, kv_s, win_p, win_s, pool_p, pool_s = [], [], [], [], [], []
    row_vec = lambda v: v.reshape(1, -1)
    for l in range(depth):
        w_r, wcat, pos8, w2, pw, wo, wu, wd = _layer_weights(
            w_in[l], cmp_pos_k[l], cmp_w1_k[l], cmp_w2_k[l], cmp_pos_v[l], cmp_w1_v[l],
            cmp_w2_v[l], pool_w[l], w_o[l], w_up[l], w_down[l], pool_dim)
        g_pre, g_post = row_vec(norm_mix_pre[l]), row_vec(norm_mix_post[l])
        g_mlp_pre, g_mlp_post = row_vec(norm_mlp_pre[l]), row_vec(norm_mlp_post[l])
        ps = row_vec(pool_scale[l])

        n_w = 2 * KV_HEADS
        win_rows = state_win_kv[l].reshape(nb, wb * n_w, HEAD_DIM)

        q_s, kv_s1, wkv_s, _, gates_s, u_s = _project(
            y_s, g_pre, w_r, tabs_s, lambda i: (i, 0), nb, pool_dim, False)

        q, kv, wkv, kva, gates, u, xc = _project(
            y_p, g_pre, w_r, tabs_p, lambda i: (i % nt_p, 0), tm_p, pool_dim, True)
        cmp_kv = _compress_prompt(xc, wcat, pos8, w2, batch, seq)
        o = _attend_prompt(q, cmp_kv, kva, gates, batch, seq)
        y1, h2 = _mix_prompt(o, u, y_p, pw, ps, wo, g_post, g_mlp_pre, batch, seq)
        y_p, win_rolled = _mlp(h2, y1, wu, wd, g_mlp_post, min(512, batch * seq),
                               state_win_kv[l].reshape(nb, wb, n_w, HEAD_DIM),
                               wkv_s.reshape(nb, 1, n_w, HEAD_DIM))
        kv_p.append(kv.reshape(batch, seq, CACHE_SLOTS, KV_HEADS, HEAD_DIM))
        wp = min(WINDOW, seq)
        win_p.append(wkv.reshape(batch, seq, 2, KV_HEADS, HEAD_DIM)[:, seq - wp:])
        pool_p.append(u.reshape(batch, seq, pool_dim)[:, seq - POOL_BUF:])

        o = _attend_sample(page_table, cache_kv[l], q_s, kv_s1, wkv_s, win_rows, gates_s,
                           wcat, pos8, w2)
        y1, h2 = _mix_sample(o.reshape(nb, ATTN_DIM), u_s, state_pool[l], y_s, pw, ps, wo,
                             g_post, g_mlp_pre, past)
        y_s = _mlp(h2, y1, wu, wd, g_mlp_post, nb)
        kv_s.append(kv_s1.reshape(nb, 1, CACHE_SLOTS, KV_HEADS, HEAD_DIM))
        win_s.append(win_rolled.reshape(nb, wb, 2, KV_HEADS, HEAD_DIM))
        pool_s.append(jnp.concatenate([state_pool[l], u_s[:, None]], axis=1)[:, 1:])

    return (y_p.reshape(batch, seq, d), y_s.reshape(nb, 1, d),
            jnp.stack(kv_p), jnp.stack(kv_s), jnp.stack(win_p), jnp.stack(win_s),
            jnp.stack(pool_p), jnp.stack(pool_s))
```

```python
import functools

import numpy as np
import jax
import jax.numpy as jnp
from jax import lax
from jax.experimental import pallas as pl
from jax.experimental.pallas import tpu as pltpu

N_HEADS = 8
HEAD_DIM = 128
KV_HEADS = 2
HPG = N_HEADS // KV_HEADS
ATTN_DIM = N_HEADS * HEAD_DIM
KV_DIM = KV_HEADS * HEAD_DIM
N_BRANCH = 3
POOL_WINDOWS = (2, 4, 8, 16)
POOL_GROUPS = len(POOL_WINDOWS)
POOL_BUF = max(POOL_WINDOWS) - 1
ROT_DIM = HEAD_DIM // 4
ROT_HALF = ROT_DIM // 2
ROPE_THETA = 500000.0
CMP_LEN = 32
CMP_STRIDE = 16
SEL_BLOCK = 64
SEL_TOPK = 16
WINDOW = 512
EPS = 1e-6
SCALE = HEAD_DIM ** -0.5
FORCE_SCORE = 1e4
NEG_INF = -1e30

LANES = 128
CACHE_SLOTS = 4
ROW_W = CACHE_SLOTS * KV_DIM
CHUNK_W = CMP_STRIDE * ROW_W
CMP_IN = CMP_LEN * HEAD_DIM
CMP_HALF = CMP_STRIDE * HEAD_DIM
GATE_PAD = LANES
VMEM_LIMIT = 56 * 1024 * 1024

BF16 = jnp.bfloat16
F32 = jnp.float32


def _dot(a, b):
    return jnp.dot(a, b, preferred_element_type=F32)


def _dot_nt(a, b):
    return lax.dot_general(a, b, (((1,), (1,)), ((), ())), preferred_element_type=F32)


def _rms(x, g):
    return x * lax.rsqrt(jnp.mean(x * x, axis=-1, keepdims=True) + EPS) * g


def _params(n_axes):
    return pltpu.CompilerParams(
        dimension_semantics=("arbitrary",) * n_axes, vmem_limit_bytes=VMEM_LIMIT)


def _split_hi_lo(x):
    hi = x.astype(BF16)
    lo = (x - hi.astype(F32)).astype(BF16)
    return hi, lo


Q_OFF, KV_OFF, WKV_OFF = 0, ATTN_DIM, ATTN_DIM + 4 * KV_DIM


def _proj_kernel(x_ref, g_ref, w_ref, cos_ref, sa_ref, sb_ref, *rest, pool_dim, chunked):
    if chunked:
        q_ref, kv_ref, wkv_ref, kva_ref, gate_ref, u_ref, xc_ref, tmp_ref = rest
    else:
        q_ref, kv_ref, wkv_ref, kva_ref, gate_ref, u_ref = rest
    tm = x_ref.shape[0]
    u_off = WKV_OFF + 2 * KV_DIM
    gate_off = u_off + pool_dim
    h = _rms(x_ref[...], g_ref[...]).astype(BF16)
    cos, sa, sb = cos_ref[...], sa_ref[...], sb_ref[...]

    def rope(z):
        return (z * cos + pltpu.roll(z, LANES - ROT_HALF, 1) * sa
                + pltpu.roll(z, ROT_HALF, 1) * sb)

    zq = _dot(h, w_ref[:, Q_OFF:Q_OFF + ATTN_DIM])
    for hd in range(N_HEADS):
        sl = slice(hd * HEAD_DIM, (hd + 1) * HEAD_DIM)
        q_ref[:, sl] = rope(zq[:, sl]).astype(BF16)

    n_kv = CACHE_SLOTS * KV_HEADS
    zkv = _dot(h, w_ref[:, KV_OFF:KV_OFF + 4 * KV_DIM])
    for blk in range(n_kv):
        z = zkv[:, blk * HEAD_DIM:(blk + 1) * HEAD_DIM]
        if (blk // KV_HEADS) % 2 == 0:
            z = rope(z)
        kv_ref[pl.ds(blk, tm, stride=n_kv), :] = z
        if blk >= 2 * KV_HEADS:
            kva_ref[blk - 2 * KV_HEADS] = z.astype(BF16)
        elif chunked:
            tmp_ref[...] = z
            for r in range(CMP_STRIDE):
                xc_ref[blk, :, r * HEAD_DIM:(r + 1) * HEAD_DIM] = (
                    tmp_ref[pl.ds(r, tm // CMP_STRIDE, stride=CMP_STRIDE), :].astype(BF16))

    n_w = 2 * KV_HEADS
    zw = _dot(h, w_ref[:, WKV_OFF:WKV_OFF + 2 * KV_DIM])
    for blk in range(n_w):
        z = zw[:, blk * HEAD_DIM:(blk + 1) * HEAD_DIM]
        if blk < KV_HEADS:
            z = rope(z)
        wkv_ref[pl.ds(blk, tm, stride=n_w), :] = z
        kva_ref[2 * KV_HEADS + blk] = z.astype(BF16)

    u_ref[...] = _dot(h, w_ref[:, u_off:u_off + pool_dim])
    gl = _dot(h, w_ref[:, gate_off:gate_off + GATE_PAD])
    gate_ref[...] = 1.0 / (1.0 + jnp.exp(-gl))


def _project(x, gain, w_r, tables, table_index, tm, pool_dim, chunked):
    rows, d = x.shape
    n_proj = w_r.shape[1]
    n_kv = CACHE_SLOTS * KV_HEADS
    n_w = 2 * KV_HEADS
    row = lambda i: (i, 0)
    fixed = lambda i: (0, 0)
    tab_spec = pl.BlockSpec((tm, LANES), table_index)
    out_shape = [
        jax.ShapeDtypeStruct((rows, ATTN_DIM), BF16),
        jax.ShapeDtypeStruct((rows * n_kv, HEAD_DIM), F32),
        jax.ShapeDtypeStruct((rows * n_w, HEAD_DIM), F32),
        jax.ShapeDtypeStruct((n_w + n_kv // 2, rows, HEAD_DIM), BF16),
        jax.ShapeDtypeStruct((rows, GATE_PAD), F32),
        jax.ShapeDtypeStruct((rows, pool_dim), F32),
    ]
    out_specs = [
        pl.BlockSpec((tm, ATTN_DIM), row),
        pl.BlockSpec((tm * n_kv, HEAD_DIM), row),
        pl.BlockSpec((tm * n_w, HEAD_DIM), row),
        pl.BlockSpec((n_w + n_kv // 2, tm, HEAD_DIM), lambda i: (0, i, 0)),
        pl.BlockSpec((tm, GATE_PAD), row),
        pl.BlockSpec((tm, pool_dim), row),
    ]
    scratch = []
    if chunked:
        out_shape.append(jax.ShapeDtypeStruct((n_kv // 2, rows // CMP_STRIDE, CMP_HALF), BF16))
        out_specs.append(pl.BlockSpec((n_kv // 2, tm // CMP_STRIDE, CMP_HALF), lambda i: (0, i, 0)))
        scratch.append(pltpu.VMEM((tm, HEAD_DIM), F32))
    return pl.pallas_call(
        functools.partial(_proj_kernel, pool_dim=pool_dim, chunked=chunked),
        grid=(rows // tm,),
        in_specs=[
            pl.BlockSpec((tm, d), row),
            pl.BlockSpec((1, d), fixed),
            pl.BlockSpec((d, n_proj), fixed),
            tab_spec, tab_spec, tab_spec,
        ],
        out_specs=tuple(out_specs),
        out_shape=tuple(out_shape),
        scratch_shapes=scratch,
        compiler_params=_params(1),
        name="project",
    )(x, gain, w_r, *tables)


def _rope_tables(pos):
    inv = jnp.power(ROPE_THETA, -jnp.arange(ROT_HALF, dtype=F32) * (2.0 / ROT_DIM))
    ang = pos.astype(F32)[:, None] * inv[None, :]
    cos, sin = jnp.cos(ang), jnp.sin(ang)
    n = pos.shape[0]
    rest = LANES - ROT_DIM
    c = jnp.concatenate([cos, cos, jnp.ones((n, rest), F32)], axis=1)
    sa = jnp.concatenate([-sin, jnp.zeros((n, LANES - ROT_HALF), F32)], axis=1)
    sb = jnp.concatenate([jnp.zeros((n, ROT_HALF), F32), sin, jnp.zeros((n, rest), F32)], axis=1)
    return c, sa, sb


def _gelu_tanh(x):
    return 0.5 * x * (1.0 + jnp.tanh(0.7978845608028654 * (x + 0.044715 * (x * x * x))))


def _compress_first(x, wcat, pos8):
    ab = _dot(x, wcat)
    pa = _dot(pos8[:, :CMP_HALF], wcat)[0:1, :HEAD_DIM]
    pb = _dot(pos8[:, CMP_HALF:], wcat)[0:1, HEAD_DIM:]
    return ab, pa + pb


def _compress_hidden(ab, pos_term):
    rows = ab.shape[0]
    b_next = pltpu.roll(ab[:, HEAD_DIM:], rows - 1, 0)
    return _gelu_tanh(ab[:, :HEAD_DIM] + b_next + pos_term).astype(BF16)


def _compress_kernel(x_ref, wcat_ref, pos_ref, w2_ref, o_ref):
    g, _, nck, width = x_ref.shape
    x = x_ref[...].reshape(g * nck, width)
    hid = _compress_hidden(*_compress_first(x, wcat_ref[0], pos_ref[0]))
    o_ref[...] = _dot(hid, w2_ref[0]).astype(BF16).reshape(o_ref.shape)


def _compress_prompt(xc, wcat, pos8, w2, batch, seq):
    nck = seq // CMP_STRIDE
    x = xc.reshape(xc.shape[0], batch, nck, CMP_HALF)
    return pl.pallas_call(
        _compress_kernel,
        grid=(batch, 2),
        in_specs=[
            pl.BlockSpec((KV_HEADS, 1, nck, CMP_HALF), lambda b, s: (s, b, 0, 0)),
            pl.BlockSpec((1, CMP_HALF, 2 * HEAD_DIM), lambda b, s: (s, 0, 0)),
            pl.BlockSpec((1, 8, CMP_IN), lambda b, s: (s, 0, 0)),
            pl.BlockSpec((1, HEAD_DIM, HEAD_DIM), lambda b, s: (s, 0, 0)),
        ],
        out_specs=pl.BlockSpec((1, 1, KV_HEADS, nck, HEAD_DIM), lambda b, s: (b, s, 0, 0, 0)),
        out_shape=jax.ShapeDtypeStruct((batch, 2, KV_HEADS, nck, HEAD_DIM), BF16),
        compiler_params=_params(2),
        name="compress_prompt",
    )(x, wcat, pos8, w2)


def _topk_rows(score, jidx, topk):
    rank = jnp.zeros_like(score)
    for j in range(score.shape[0]):
        bj = score[j:j + 1, :]
        tie = jnp.where(jidx > j, 1.0, 0.0)
        rank = rank + jnp.where(bj > score, 1.0, jnp.where(bj == score, tie, 0.0))
    return jnp.where(rank < topk, 1.0, 0.0)


def _softmax_parts(s, ok):
    s = jnp.where(ok, s, NEG_INF)
    m = jnp.max(s, axis=-1, keepdims=True)
    e = jnp.where(ok, jnp.exp(s - m), 0.0)
    d = jnp.sum(e, axis=-1, keepdims=True)
    return e, jnp.where(d > 0, d, 1.0)


EXP2_SCALE = SCALE * 1.4426950408889634
HEAD_PAIRS = HPG // 2


def _attn_prompt_kernel(q_ref, kc_ref, vc_ref, ks_ref, vs_ref, kw_ref, vw_ref, gate_ref,
                        mt_ref, eye_ref, o_ref,
                        vct_ref, vst_ref, vwt_ref, sel_ref, gt_ref, acc_ref,
                        *, tq, tk, seq, topk):
    g = pl.program_id(1)
    i = pl.program_id(2)
    t0 = i * tq
    n_sel = mt_ref.shape[0]
    pair_w = 2 * tq
    eye = eye_ref[...]

    @pl.when(i == 0)
    def _():
        vct_ref[...] = _dot_nt(eye, vc_ref[0, 0, 0]).astype(BF16)
        vst_ref[...] = _dot_nt(eye, vs_ref[0]).astype(BF16)
        vwt_ref[...] = _dot_nt(eye, vw_ref[0]).astype(BF16)

    q4 = q_ref[...]
    q_pairs = [jnp.concatenate([q4[:, (2 * hp) * HEAD_DIM:(2 * hp + 1) * HEAD_DIM],
                                q4[:, (2 * hp + 1) * HEAD_DIM:(2 * hp + 2) * HEAD_DIM]], axis=0)
               for hp in range(HEAD_PAIRS)]

    def both_heads(x):
        return jnp.concatenate([x, x], axis=1)

    def tpos(n_keys):
        return t0 + lax.broadcasted_iota(jnp.int32, (n_keys, tq), 1)

    def kidx(n_keys):
        return lax.broadcasted_iota(jnp.int32, (n_keys, tq), 0)

    pairs = range(HEAD_PAIRS)

    kc = kc_ref[0, 0, 0]
    ncp = kc.shape[0]
    wk = min(WINDOW + tq, seq)
    ws = pl.multiple_of(jnp.maximum(t0 + tq - wk, 0), tq)
    kw = kw_ref[0, pl.ds(ws, wk), :]
    s_cmp = [_dot_nt(kc, q_pairs[hp]) for hp in pairs]
    s_win = [_dot_nt(kw, q_pairs[hp]) for hp in pairs]

    ok = both_heads(jnp.where(kidx(ncp) * CMP_STRIDE + (CMP_LEN - 1) <= tpos(ncp), 1.0, 0.0)) > 0.5
    p_cmp, p_sum = [], None
    for hp in pairs:
        s = jnp.where(ok, s_cmp[hp], NEG_INF)
        m = jnp.max(s, axis=0, keepdims=True)
        e = jnp.where(ok, jnp.exp2((s - m) * EXP2_SCALE), 0.0)
        d = jnp.sum(e, axis=0, keepdims=True)
        p = e / jnp.where(d > 0, d, 1.0)
        p_cmp.append(p.astype(BF16))
        ph = p[:, :tq] + p[:, tq:]
        p_sum = ph if p_sum is None else p_sum + ph

    p_hi, p_lo = _split_hi_lo(p_sum)
    mt = mt_ref[...]
    imp = _dot(mt, p_hi) + _dot(mt, p_lo)
    o_cmp = [_dot(vct_ref[...], p_cmp[hp]) for hp in pairs]

    kpos = ws + kidx(wk)
    bias = both_heads(jnp.where(kpos <= tpos(wk),
                                jnp.where(kpos > tpos(wk) - WINDOW, 0.0, NEG_INF), NEG_INF))
    p_win, l_win = [], []
    for hp in pairs:
        s = s_win[hp] + bias
        p = jnp.exp2((s - jnp.max(s, axis=0, keepdims=True)) * EXP2_SCALE)
        l_win.append(jnp.sum(p, axis=0, keepdims=True))
        p_win.append(p.astype(BF16))
    vwt = vwt_ref[:, pl.ds(ws, wk)]
    o_win = [_dot(vwt, p_win[hp]) for hp in pairs]
    o_win = [o_win[hp] / l_win[hp] for hp in pairs]

    jidx = kidx(n_sel)
    jt = tpos(n_sel) // SEL_BLOCK
    forced = jnp.where(jidx == 0, 1.0, jnp.where(jidx == jt, 1.0, jnp.where(jidx == jt - 1, 1.0, 0.0)))
    score = jnp.where(forced > 0.5, FORCE_SCORE, jnp.where(jidx <= jt, imp, -1.0))
    sel_ref[...] = _topk_rows(score, jidx, topk)

    acc_ref[...] = jnp.zeros_like(acc_ref)

    def sel_step(kb, carry):
        k0 = pl.multiple_of(kb * tk, tk)
        k = ks_ref[0, pl.ds(k0, tk), :]
        vt = vst_ref[:, pl.ds(k0, tk)]
        blk0 = kb * (tk // SEL_BLOCK)
        chosen = jnp.concatenate(
            [jnp.broadcast_to(sel_ref[pl.ds(blk0 + j, 1), :], (SEL_BLOCK, tq))
             for j in range(tk // SEL_BLOCK)], axis=0)
        bias = both_heads(jnp.where(k0 + kidx(tk) <= tpos(tk),
                                    jnp.where(chosen > 0.5, 0.0, NEG_INF), NEG_INF))
        ss = [_dot_nt(k, q_pairs[hp]) + bias for hp in pairs]
        out, ps, alphas = [], [], []
        for hp in pairs:
            m, l = carry[2 * hp], carry[2 * hp + 1]
            m_new = jnp.maximum(m, jnp.max(ss[hp], axis=0, keepdims=True))
            p = jnp.exp2((ss[hp] - m_new) * EXP2_SCALE)
            alpha = jnp.exp2((m - m_new) * EXP2_SCALE)
            out += [m_new, alpha * l + jnp.sum(p, axis=0, keepdims=True)]
            ps.append(p.astype(BF16))
            alphas.append(alpha)
        pvs = [_dot(vt, ps[hp]) for hp in pairs]
        for hp in pairs:
            acc_ref[hp] = alphas[hp] * acc_ref[hp] + pvs[hp]
        return tuple(out)

    n_kb = (t0 + tq + tk - 1) // tk
    init = (jnp.full((1, pair_w), NEG_INF, F32), jnp.zeros((1, pair_w), F32)) * HEAD_PAIRS
    stats = lax.fori_loop(0, n_kb, sel_step, init)
    o_sel = [acc_ref[hp] / stats[2 * hp + 1] for hp in pairs]

    gt_ref[...] = gate_ref[...].T
    o_t = []
    for h in range(HPG):
        hp, lanes = h // 2, slice((h % 2) * tq, (h % 2 + 1) * tq)
        col = (g * HPG + h) * N_BRANCH
        o_t.append((gt_ref[pl.ds(col, 1), :] * o_cmp[hp][:, lanes]
                    + gt_ref[pl.ds(col + 1, 1), :] * o_sel[hp][:, lanes]
                    + gt_ref[pl.ds(col + 2, 1), :] * o_win[hp][:, lanes]).astype(BF16))
    outs = [_dot_nt(eye, o_t[h]) for h in range(HPG)]
    for h in range(HPG):
        o_ref[:, h * HEAD_DIM:(h + 1) * HEAD_DIM] = outs[h].astype(BF16)


def _cmp_to_sel(n_cmp_pad, n_cmp, n_sel):
    cs = np.arange(n_cmp_pad)[:, None] * CMP_STRIDE
    ss = np.arange(n_sel)[None, :] * SEL_BLOCK
    hit = (cs < ss + SEL_BLOCK) & (cs + CMP_LEN > ss) & (np.arange(n_cmp_pad)[:, None] < n_cmp)
    return hit.astype(np.float32)


def _attend_prompt(q, cmp_kv, kva, gates, batch, seq):
    tq = min(128, seq)
    tk = min(512, seq)
    nq = seq // tq
    ncp = seq // CMP_STRIDE
    n_sel = -(-seq // SEL_BLOCK)
    topk = min(SEL_TOPK, n_sel)
    assert tq == HEAD_DIM and seq % tk == 0 and WINDOW % tq == 0
    mt = jnp.asarray(_cmp_to_sel(ncp, ncp - 1, n_sel).T, BF16)
    eye = jnp.asarray(np.eye(tq, dtype=np.float32), BF16)
    kv_spec = lambda slot: pl.BlockSpec(
        (1, seq, HEAD_DIM), lambda b, g, i: (slot * KV_HEADS + g, b, 0))
    cmp_spec = lambda s: pl.BlockSpec(
        (1, 1, 1, ncp, HEAD_DIM), lambda b, g, i: (b, s, g, 0, 0))
    full = lambda a: pl.BlockSpec(a.shape, lambda b, g, i: (0, 0))
    return pl.pallas_call(
        functools.partial(_attn_prompt_kernel, tq=tq, tk=tk, seq=seq, topk=topk),
        grid=(batch, KV_HEADS, nq),
        in_specs=[
            pl.BlockSpec((tq, HPG * HEAD_DIM), lambda b, g, i: (b * nq + i, g)),
            cmp_spec(0), cmp_spec(1),
            kv_spec(0), kv_spec(1), kv_spec(2), kv_spec(3),
            pl.BlockSpec((tq, GATE_PAD), lambda b, g, i: (b * nq + i, 0)),
            full(mt), full(eye),
        ],
        out_specs=pl.BlockSpec((tq, HPG * HEAD_DIM), lambda b, g, i: (b * nq + i, g)),
        out_shape=jax.ShapeDtypeStruct((batch * seq, ATTN_DIM), BF16),
        scratch_shapes=[
            pltpu.VMEM((HEAD_DIM, ncp), BF16),
            pltpu.VMEM((HEAD_DIM, seq), BF16),
            pltpu.VMEM((HEAD_DIM, seq), BF16),
            pltpu.VMEM((n_sel, tq), F32),
            pltpu.VMEM((GATE_PAD, tq), F32),
            pltpu.VMEM((HEAD_PAIRS, HEAD_DIM, 2 * tq), F32),
        ],
        compiler_params=_params(3),
        name="attend_prompt",
    )(q, cmp_kv, cmp_kv, kva, kva, kva, kva, gates, mt, eye)


def _attn_sample_kernel(pt_ref, cache_ref, q_ref, kvn_ref, wn_ref, win_ref, gate_ref,
                        wcat_ref, pos_ref, w2_ref, ms_ref, ek_ref, o_ref, buf, sem,
                        *, n_pages, page, past, topk, n_sel):
    b = pl.program_id(0)
    nb = pl.num_programs(0)
    n_kv = CACHE_SLOTS * KV_HEADS
    n_w = 2 * KV_HEADS
    cpp = page // CMP_STRIDE
    nch = past // CMP_STRIDE
    wb = win_ref.shape[1] // n_w

    def page_copy(seq_idx, slot, pi, r):
        src0 = pl.multiple_of(pt_ref[seq_idx, pi] * cpp, cpp)
        return pltpu.make_async_copy(
            cache_ref.at[pl.ds(src0, cpp), r],
            buf.at[slot, r, pl.ds(pl.multiple_of(pi * cpp, cpp), cpp)], sem.at[slot])

    def for_all_copies(seq_idx, slot, act):
        def per_page(pi, carry):
            for r in range(CMP_STRIDE):
                act(page_copy(seq_idx, slot, pi, r))
            return carry
        lax.fori_loop(0, n_pages, per_page, 0)

    @pl.when(b == 0)
    def _():
        for_all_copies(0, 0, lambda c: c.start())

    @pl.when(b + 1 < nb)
    def _():
        for_all_copies(b + 1, (b + 1) % 2, lambda c: c.start())

    slot = b % 2
    for_all_copies(b, slot, lambda c: c.wait())

    def chunk_rows(cache_slot, g, r):
        rows = buf.reshape(2, CMP_STRIDE, nch * n_kv, HEAD_DIM)
        return rows[slot, r, pl.ds(cache_slot * KV_HEADS + g, nch, stride=n_kv), :]

    row8 = lax.broadcasted_iota(jnp.int32, (N_HEADS, 1), 0)
    in_g0 = row8 < HPG

    def by_group(x0, x1):
        return jnp.where(in_g0, x0, x1)

    q8 = q_ref[0]
    qf = q8.astype(F32)
    groups = range(KV_HEADS)

    def scores(keys):
        return by_group(_dot_nt(q8, keys[0]), _dot_nt(q8, keys[1])) * SCALE

    def cached(cache_slot, g):
        return jnp.concatenate([chunk_rows(cache_slot, g, r).astype(BF16)
                                for r in range(CMP_STRIDE)], axis=0)

    def compress_input(cache_slot):
        return jnp.concatenate(
            [jnp.concatenate([chunk_rows(cache_slot, g, r).astype(BF16)
                              for r in range(CMP_STRIDE)], axis=1)
             for g in groups], axis=0)

    first = [_compress_first(compress_input(cs), wcat_ref[cs], pos_ref[cs]) for cs in range(2)]
    s_sel = scores([cached(2, g) for g in groups])
    win_keys = [win_ref[0, pl.ds(g, wb, stride=n_w), :].astype(BF16) for g in groups]
    s_win = scores(win_keys)
    kc, vc =[_dot(_compress_hidden(*first[cs]), w2_ref[cs]).astype(BF16) for cs in range(2)]

    lane_n = lax.broadcasted_iota(jnp.int32, (N_HEADS, nch), 1)
    s = by_group(_dot_nt(q8, kc[:nch]), _dot_nt(q8, kc[nch:])) * SCALE
    ok = lane_n * CMP_STRIDE + (CMP_LEN - 1) <= past
    e, d = _softmax_parts(s, ok)
    p = e / d
    pb = p.astype(BF16)
    o_cmp = by_group(_dot(pb, vc[:nch]), _dot(pb, vc[nch:]))

    p_g = [jnp.sum(p[g * HPG:(g + 1) * HPG], axis=0, keepdims=True) for g in range(KV_HEADS)]
    p2 = jnp.concatenate(p_g + [jnp.zeros((N_HEADS - KV_HEADS, nch), F32)], axis=0)
    p_hi, p_lo = _split_hi_lo(p2)
    imp = _dot(p_hi, ms_ref[...]) + _dot(p_lo, ms_ref[...])
    jl = lax.broadcasted_iota(jnp.int32, (N_HEADS, LANES), 1)
    jt = past // SEL_BLOCK
    forced = jnp.where(jl == 0, 1.0, jnp.where(jl == jt, 1.0, jnp.where(jl == jt - 1, 1.0, 0.0)))
    score = jnp.where(forced > 0.5, FORCE_SCORE, jnp.where(jl <= jt, imp, -1.0))
    score = jnp.where(jl < n_sel, score, -2.0)
    ii = lax.broadcasted_iota(jnp.int32, (LANES, LANES), 0)
    jj = lax.broadcasted_iota(jnp.int32, (LANES, LANES), 1)
    sel_rows = []
    for g in range(KV_HEADS):
        srow = jnp.broadcast_to(score[g:g + 1, :], (LANES, LANES))
        scol = jnp.sum(jnp.where(ii == jj, srow, 0.0), axis=1, keepdims=True)
        tie = jnp.where(ii < jj, 1.0, 0.0)
        beats = jnp.where(scol > srow, 1.0, jnp.where(scol == srow, tie, 0.0))
        rank = jnp.sum(beats, axis=0, keepdims=True)
        sel_rows.append(jnp.where(rank < topk, 1.0, 0.0))
    sel2 = jnp.concatenate(sel_rows + [jnp.zeros((N_HEADS - KV_HEADS, LANES), F32)], axis=0)
    chunk_ok2 = _dot(sel2.astype(BF16), ek_ref[...])
    chunk_ok = by_group(chunk_ok2[0:1], chunk_ok2[1:2])
    key_ok = jnp.concatenate([chunk_ok] * CMP_STRIDE, axis=1)
    new_ok = by_group(*[jnp.sum(jnp.where(jl[0:1] == jt, sel2[g:g + 1], 0.0), axis=1, keepdims=True)
                        for g in range(KV_HEADS)])

    def new_row(ref, idx0):
        x = by_group(ref[0, idx0:idx0 + 1, :], ref[0, idx0 + 1:idx0 + 2, :])
        return x.astype(BF16).astype(F32)

    def weights(s, ok, k_new, new_ok):
        s_new = jnp.sum(qf * k_new, axis=-1, keepdims=True) * SCALE
        s = jnp.where(ok, s, NEG_INF)
        s_new = jnp.where(new_ok, s_new, NEG_INF)
        m = jnp.maximum(jnp.max(s, axis=-1, keepdims=True), s_new)
        e = jnp.where(ok, jnp.exp(s - m), 0.0)
        e_new = jnp.where(new_ok, jnp.exp(s_new - m), 0.0)
        d = jnp.sum(e, axis=-1, keepdims=True) + e_new
        return e.astype(BF16), e_new, jnp.where(d > 0, d, 1.0)

    def weighted(eb, vals, e_new, v_new, d):
        return (by_group(_dot(eb, vals[0]), _dot(eb, vals[1])) + e_new * v_new) / d

    kpos = past - wb + lax.broadcasted_iota(jnp.int32, (N_HEADS, wb), 1)
    w_sel = weights(s_sel, key_ok > 0.5, new_row(kvn_ref, 2 * KV_HEADS), new_ok > 0.5)
    w_win = weights(s_win, kpos > past - WINDOW, new_row(wn_ref, 0), row8 >= 0)
    win_vals = [win_ref[0, pl.ds(KV_HEADS + g, wb, stride=n_w), :].astype(BF16) for g in groups]
    o_sel = weighted(w_sel[0], [cached(3, g) for g in groups], w_sel[1],
                     new_row(kvn_ref, 3 * KV_HEADS), w_sel[2])
    o_win = weighted(w_win[0], win_vals, w_win[1], new_row(wn_ref, KV_HEADS), w_win[2])

    gates = jnp.broadcast_to(gate_ref[0], (N_HEADS, GATE_PAD))
    lane = lax.broadcasted_iota(jnp.int32, (N_HEADS, GATE_PAD), 1)

    def gate(br):
        return jnp.sum(jnp.where(lane == row8 * N_BRANCH + br, gates, 0.0), axis=-1, keepdims=True)

    o_ref[0] = (gate(0) * o_cmp + gate(1) * o_sel + gate(2) * o_win).astype(BF16)


def _attend_sample(page_table, cache, q, kv_new, wkv_new, win_rows, gates, wcat, pos8, w2):
    nb, n_pages = page_table.shape
    n_phys, page = cache.shape[:2]
    n_kv = CACHE_SLOTS * KV_HEADS
    n_w = 2 * KV_HEADS
    past = n_pages * page
    nch = past // CMP_STRIDE
    wb = win_rows.shape[1] // n_w
    n_sel = -(-(past + 1) // SEL_BLOCK)
    n_cmp = (past + 1) // CMP_STRIDE - 1
    topk = min(SEL_TOPK, n_sel)
    assert n_sel <= LANES and nch * CMP_STRIDE == past
    ms = np.zeros((nch, LANES), np.float32)
    ms[:, :n_sel] = _cmp_to_sel(nch, n_cmp, n_sel)
    ek = (np.arange(nch)[None, :] * CMP_STRIDE // SEL_BLOCK == np.arange(LANES)[:, None])
    ms = jnp.asarray(ms, BF16)
    ek = jnp.asarray(ek.astype(np.float32), BF16)
    seq3 = lambda n: pl.BlockSpec((1, n, HEAD_DIM), lambda b, pt: (b, 0, 0))
    full3 = lambda a: pl.BlockSpec(a.shape, lambda b, pt: (0, 0, 0))
    full2 = lambda a: pl.BlockSpec(a.shape, lambda b, pt: (0, 0))
    grid_spec = pltpu.PrefetchScalarGridSpec(
        num_scalar_prefetch=1,
        grid=(nb,),
        in_specs=[
            pl.BlockSpec(memory_space=pl.ANY),
            seq3(N_HEADS), seq3(n_kv), seq3(n_w), seq3(wb * n_w),
            pl.BlockSpec((1, 1, GATE_PAD), lambda b, pt: (b, 0, 0)),
            full3(wcat), full3(pos8), full3(w2), full2(ms), full2(ek),
        ],
        out_specs=seq3(N_HEADS),
        scratch_shapes=[pltpu.VMEM((2, CMP_STRIDE, nch, n_kv, HEAD_DIM), F32),
                        pltpu.SemaphoreType.DMA((2,))],
    )
    return pl.pallas_call(
        functools.partial(_attn_sample_kernel, n_pages=n_pages, page=page, past=past,
                          topk=topk, n_sel=n_sel),
        grid_spec=grid_spec,
        out_shape=jax.ShapeDtypeStruct((nb, N_HEADS, HEAD_DIM), BF16),
        compiler_params=_params(1),
        name="attend_sample",
    )(page_table, cache.reshape(n_phys * page // CMP_STRIDE, CMP_STRIDE, n_kv, HEAD_DIM),
      q.reshape(nb, N_HEADS, HEAD_DIM), kv_new.reshape(nb, n_kv, HEAD_DIM),
      wkv_new.reshape(nb, n_w, HEAD_DIM), win_rows,
      gates.reshape(nb, 1, GATE_PAD), wcat, pos8, w2, ms, ek)


def _mix_tail(o, diffs, x, pw_ref, ps_ref, wo_ref, gpost_ref, gpre_ref, y1_ref, h2_ref):
    gw = diffs[0].shape[1]
    ys = [(_dot(diffs[g].astype(BF16), pw_ref[g]) * ps_ref[:, g * gw:(g + 1) * gw]).astype(BF16)
          for g in range(POOL_GROUPS)]
    cat = jnp.concatenate([o] + ys, axis=1)
    m = _dot(cat, wo_ref[...])
    y1 = x + _rms(m, gpost_ref[...])
    y1_ref[...] = y1
    h2_ref[...] = _rms(y1, gpre_ref[...]).astype(BF16)


def _mix_prompt_kernel(o_ref, u_ref, halo_ref, x_ref, pw_ref, ps_ref, wo_ref, gpost_ref, gpre_ref,
                       y1_ref, h2_ref, *, tm):
    i = pl.program_id(1)
    halo_rows = halo_ref.shape[0]
    halo = jnp.where(i > 0, halo_ref[...], 0.0)
    u = u_ref[...]
    uext = jnp.concatenate([halo, u], axis=0)
    n_ext = uext.shape[0]
    gw = u.shape[1] // POOL_GROUPS
    tpos = i * tm + lax.broadcasted_iota(jnp.int32, (tm, 1), 0)
    diffs = []
    for g, w in enumerate(POOL_WINDOWS):
        s = uext[:, g * gw:(g + 1) * gw]
        k = 1
        while k < w:
            s = s + pltpu.roll(s, k, 0)
            k *= 2
        cnt = jnp.minimum(w, tpos + 1).astype(F32)
        diffs.append(s[halo_rows:n_ext] / cnt - u[:, g * gw:(g + 1) * gw])
    _mix_tail(o_ref[...], diffs, x_ref[...], pw_ref, ps_ref, wo_ref, gpost_ref, gpre_ref,
              y1_ref, h2_ref)


def _mix_sample_kernel(o_ref, u_ref, st_ref, x_ref, pw_ref, ps_ref, wo_ref, gpost_ref, gpre_ref,
                       y1_ref, h2_ref, *, past):
    u = u_ref[...]
    c = u.shape[1]
    gw = c // POOL_GROUPS
    n_hist = st_ref.shape[1] // c
    diffs = []
    for g, w in enumerate(POOL_WINDOWS):
        un = u[:, g * gw:(g + 1) * gw]
        s = un
        for back in range(1, w):
            r = n_hist - back
            s = s + st_ref[:, r * c + g * gw:r * c + (g + 1) * gw]
        diffs.append(s / float(min(w, past + 1)) - un)
    _mix_tail(o_ref[...], diffs, x_ref[...], pw_ref, ps_ref, wo_ref, gpost_ref, gpre_ref,
              y1_ref, h2_ref)


def _mix_specs(tm, d, c, pool_w, idx):
    fixed2 = lambda *a: (0, 0)
    fixed3 = lambda *a: (0, 0, 0)
    weights = [
        pl.BlockSpec(pool_w.shape, fixed3),
        pl.BlockSpec((1, c), fixed2),
        pl.BlockSpec((d, d), fixed2),
        pl.BlockSpec((1, d), fixed2),
        pl.BlockSpec((1, d), fixed2),
    ]
    outs = (pl.BlockSpec((tm, d), idx), pl.BlockSpec((tm, d), idx))
    return weights, outs


def _mix_prompt(o, u, x, pool_w, pool_scale, w_o, g_post, g_pre, batch, seq):
    rows, d = x.shape
    c = u.shape[1]
    tm = min(256, seq)
    nt = seq // tm
    halo = 16
    assert halo >= POOL_BUF and seq % tm == 0 and tm % halo == 0
    idx = lambda b, i: (b * nt + i, 0)
    halo_idx = lambda b, i: (jnp.maximum((b * nt + i) * (tm // halo) - 1, 0), 0)
    weights, outs = _mix_specs(tm, d, c, pool_w, idx)
    return pl.pallas_call(
        functools.partial(_mix_prompt_kernel, tm=tm),
        grid=(batch, nt),
        in_specs=[pl.BlockSpec((tm, ATTN_DIM), idx), pl.BlockSpec((tm, c), idx),
                  pl.BlockSpec((halo, c), halo_idx), pl.BlockSpec((tm, d), idx)] + weights,
        out_specs=outs,
        out_shape=(jax.ShapeDtypeStruct((rows, d), F32), jax.ShapeDtypeStruct((rows, d), BF16)),
        compiler_params=_params(2),
        name="mix_prompt",
    )(o, u, u, x, pool_w, pool_scale, w_o, g_post, g_pre)


def _mix_sample(o, u, pool_state, x, pool_w, pool_scale, w_o, g_post, g_pre, past):
    rows, d = x.shape
    c = u.shape[1]
    tm = rows
    idx = lambda i: (i, 0)
    weights, outs = _mix_specs(tm, d, c, pool_w, idx)
    st = pool_state.reshape(rows, -1)
    return pl.pallas_call(
        functools.partial(_mix_sample_kernel, past=past),
        grid=(rows // tm,),
        in_specs=[pl.BlockSpec((tm, ATTN_DIM), idx), pl.BlockSpec((tm, c), idx),
                  pl.BlockSpec((tm, st.shape[1]), idx), pl.BlockSpec((tm, d), idx)] + weights,
        out_specs=outs,
        out_shape=(jax.ShapeDtypeStruct((rows, d), F32), jax.ShapeDtypeStruct((rows, d), BF16)),
        compiler_params=_params(1),
        name="mix_sample",
    )(o, u, st, x, pool_w, pool_scale, w_o, g_post, g_pre)


def _mlp_kernel(h_ref, wu_ref, wd_ref, y1_ref, g_ref, *rest, shift):
    if shift:
        state_ref, fresh_ref, y_ref, rolled_ref, acc_ref, sem = rest
    else:
        y_ref, acc_ref = rest
    i, j = pl.program_id(0), pl.program_id(1)
    first = (i == 0) & (j == 0)
    last = (i == pl.num_programs(0) - 1) & (j == pl.num_programs(1) - 1)

    def for_state_copies(act):
        n_new = fresh_ref.shape[1]
        n_keep = state_ref.shape[1] - n_new

        def per_seq(s, carry):
            act(pltpu.make_async_copy(state_ref.at[s, pl.ds(n_new, n_keep)],
                                      rolled_ref.at[s, pl.ds(0, n_keep)], sem.at[0]))
            return carry
        lax.fori_loop(0, state_ref.shape[0], per_seq, 0)
        act(pltpu.make_async_copy(fresh_ref, rolled_ref.at[:, pl.ds(n_keep, n_new)], sem.at[1]))

    if shift:
        @pl.when(first)
        def _():
            for_state_copies(lambda c: c.start())

    @pl.when(j == 0)
    def _():
        acc_ref[...] = jnp.zeros_like(acc_ref)

    a = jnp.maximum(_dot(h_ref[...], wu_ref[...]), 0.0)
    acc_ref[...] += _dot((a * a).astype(BF16), wd_ref[...])

    @pl.when(j == pl.num_programs(1) - 1)
    def _():
        y_ref[...] = y1_ref[...] + _rms(acc_ref[...], g_ref[...])

    if shift:
        @pl.when(last)
        def _():
            for_state_copies(lambda c: c.wait())


def _mlp(h2, y1, w_up, w_down, gain, tm, state=None, fresh=None):
    rows, d = y1.shape
    ff = w_up.shape[1]
    tf = min(1024, ff)
    row = lambda i, j: (i, 0)
    shift = state is not None
    any_spec = pl.BlockSpec(memory_space=pl.ANY)
    y_spec = pl.BlockSpec((tm, d), row)
    y_shape = jax.ShapeDtypeStruct((rows, d), F32)
    out = pl.pallas_call(
        functools.partial(_mlp_kernel, shift=shift),
        grid=(rows // tm, ff // tf),
        in_specs=[
            pl.BlockSpec((tm, d), row),
            pl.BlockSpec((d, tf), lambda i, j: (0, j)),
            pl.BlockSpec((tf, d), lambda i, j: (j, 0)),
            pl.BlockSpec((tm, d), row),
            pl.BlockSpec((1, d), lambda i, j: (0, 0)),
        ] + ([any_spec, any_spec] if shift else []),
        out_specs=(y_spec, any_spec) if shift else y_spec,
        out_shape=(y_shape, jax.ShapeDtypeStruct(state.shape, state.dtype)) if shift else y_shape,
        scratch_shapes=[pltpu.VMEM((tm, d), F32)]
        + ([pltpu.SemaphoreType.DMA((2,))] if shift else []),
        compiler_params=_params(2),
        name="mlp",
    )(h2, w_up, w_down, y1, gain, *((state, fresh) if shift else ()))
    return out


def _layer_weights(w_in, cmp_pos_k, cmp_w1_k, cmp_w2_k, cmp_pos_v, cmp_w1_v, cmp_w2_v,
                   pool_w, w_o, w_up, w_down, pool_dim):
    gate_lo = WKV_OFF + 2 * KV_DIM
    gate_hi = gate_lo + N_BRANCH * N_HEADS
    w_r = jnp.concatenate(
        [w_in[:, :gate_lo], w_in[:, gate_hi:gate_hi + pool_dim],
         jnp.pad(w_in[:, gate_lo:gate_hi], ((0, 0), (0, GATE_PAD - N_BRANCH * N_HEADS)))],
        axis=1).astype(BF16)
    wcat = jnp.stack([jnp.concatenate([w1[:CMP_HALF], w1[CMP_HALF:]], axis=1)
                      for w1 in (cmp_w1_k, cmp_w1_v)]).astype(BF16)
    pos8 = jnp.stack([jnp.pad(p.reshape(1, CMP_IN), ((0, 7), (0, 0)))
                      for p in (cmp_pos_k, cmp_pos_v)]).astype(BF16)
    w2 = jnp.stack([cmp_w2_k, cmp_w2_v]).astype(BF16)
    return (w_r, wcat, pos8, w2, pool_w.astype(BF16), w_o.astype(BF16),
            w_up.astype(BF16), w_down.astype(BF16))


def kernel(x_prompt, x_sample, cache_kv, state_win_kv, state_pool, page_table, norm_mix_pre, w_in,
           cmp_pos_k, cmp_w1_k, cmp_w2_k, cmp_pos_v, cmp_w1_v, cmp_w2_v, pool_w, pool_scale, w_o,
           norm_mix_post, norm_mlp_pre, w_up, w_down, norm_mlp_post):
    batch, seq, d = x_prompt.shape
    nb, dec_seq, _ = x_sample.shape
    depth = w_in.shape[0]
    pool_dim = d - ATTN_DIM
    n_pages = page_table.shape[1]
    page = cache_kv.shape[2]
    past = n_pages * page
    wb = state_win_kv.shape[2]
    assert dec_seq == 1 and seq >= POOL_BUF and seq % CMP_STRIDE == 0
    assert w_in.shape[2] == ATTN_DIM + 6 * KV_DIM + N_BRANCH * N_HEADS + pool_dim

    tm_p = min(256, seq)
    tabs_p = _rope_tables(jnp.arange(seq, dtype=jnp.int32))
    tabs_s = _rope_tables(jnp.full((nb,), past, jnp.int32))
    nt_p = seq // tm_p

    y_p = x_prompt.reshape(batch * seq, d)
    y_s = x_sample.reshape(nb, d)
    kv_p, kv_s, win_p, win_s, pool_p, pool_s = [], [], [], [], [], []
    row_vec = lambda v: v.reshape(1, -1)
    for l in range(depth):
        w_r, wcat, pos8, w2, pw, wo, wu, wd = _layer_weights(
            w_in[l], cmp_pos_k[l], cmp_w1_k[l], cmp_w2_k[l], cmp_pos_v[l], cmp_w1_v[l],
            cmp_w2_v[l], pool_w[l], w_o[l], w_up[l], w_down[l], pool_dim)
        g_pre, g_post = row_vec(norm_mix_pre[l]), row_vec(norm_mix_post[l])
        g_mlp_pre, g_mlp_post = row_vec(norm_mlp_pre[l]), row_vec(norm_mlp_post[l])
        ps = row_vec(pool_scale[l])

        n_w = 2 * KV_HEADS
        win_rows = state_win_kv[l].reshape(nb, wb * n_w, HEAD_DIM)

        q_s, kv_s1, wkv_s, _, gates_s, u_s = _project(
            y_s, g_pre, w_r, tabs_s, lambda i: (i, 0), nb, pool_dim, False)

        q, kv, wkv, kva, gates, u, xc = _project(
            y_p, g_pre, w_r, tabs_p, lambda i: (i % nt_p, 0), tm_p, pool_dim, True)
        cmp_kv = _compress_prompt(xc, wcat, pos8, w2, batch, seq)
        o = _attend_prompt(q, cmp_kv, kva, gates, batch, seq)
        y1, h2 = _mix_prompt(o, u, y_p, pw, ps, wo, g_post, g_mlp_pre, batch, seq)
        y_p, win_rolled = _mlp(h2, y1, wu, wd, g_mlp_post, min(512, batch * seq),
                               state_win_kv[l].reshape(nb, wb, n_w, HEAD_DIM),
                               wkv_s.reshape(nb, 1, n_w, HEAD_DIM))
        kv_p.append(kv.reshape(batch, seq, CACHE_SLOTS, KV_HEADS, HEAD_DIM))
        wp = min(WINDOW, seq)
        win_p.append(wkv.reshape(batch, seq, 2, KV_HEADS, HEAD_DIM)[:, seq - wp:])
        pool_p.append(u.reshape(batch, seq, pool_dim)[:, seq - POOL_BUF:])

        o = _attend_sample(page_table, cache_kv[l], q_s, kv_s1, wkv_s, win_rows, gates_s,
                           wcat, pos8, w2)
        y1, h2 = _mix_sample(o.reshape(nb, ATTN_DIM), u_s, state_pool[l], y_s, pw, ps, wo,
                             g_post, g_mlp_pre, past)
        y_s = _mlp(h2, y1, wu, wd, g_mlp_post, nb)
        kv_s.append(kv_s1.reshape(nb, 1, CACHE_SLOTS, KV_HEADS, HEAD_DIM))
        win_s.append(win_rolled.reshape(nb, wb, 2, KV_HEADS, HEAD_DIM))
        pool_s.append(jnp.concatenate([state_pool[l], u_s[:, None]], axis=1)[:, 1:])

    return (y_p.reshape(batch, seq, d), y_s.reshape(nb, 1, d),
            jnp.stack(kv_p), jnp.stack(kv_s), jnp.stack(win_p), jnp.stack(win_s),
            jnp.stack(pool_p), jnp.stack(pool_s))
```

```python
import functools

import numpy as np
import jax
import jax.numpy as jnp
from jax import lax
from jax.experimental import pallas as pl
from jax.experimental.pallas import tpu as pltpu

N_HEADS = 8
HEAD_DIM = 128
KV_HEADS = 2
HPG = N_HEADS // KV_HEADS
ATTN_DIM = N_HEADS * HEAD_DIM
KV_DIM = KV_HEADS * HEAD_DIM
N_BRANCH = 3
POOL_WINDOWS = (2, 4, 8, 16)
POOL_GROUPS = len(POOL_WINDOWS)
POOL_BUF = max(POOL_WINDOWS) - 1
ROT_DIM = HEAD_DIM // 4
ROT_HALF = ROT_DIM // 2
ROPE_THETA = 500000.0
CMP_LEN = 32
CMP_STRIDE = 16
SEL_BLOCK = 64
SEL_TOPK = 16
WINDOW = 512
EPS = 1e-6
SCALE = HEAD_DIM ** -0.5
FORCE_SCORE = 1e4
NEG_INF = -1e30

LANES = 128
CACHE_SLOTS = 4
ROW_W = CACHE_SLOTS * KV_DIM
CHUNK_W = CMP_STRIDE * ROW_W
CMP_IN = CMP_LEN * HEAD_DIM
CMP_HALF = CMP_STRIDE * HEAD_DIM
GATE_PAD = LANES
VMEM_LIMIT = 56 * 1024 * 1024

BF16 = jnp.bfloat16
F32 = jnp.float32


def _dot(a, b):
    return jnp.dot(a, b, preferred_element_type=F32)


def _dot_nt(a, b):
    return lax.dot_general(a, b, (((1,), (1,)), ((), ())), preferred_element_type=F32)


def _rms(x, g):
    return x * lax.rsqrt(jnp.mean(x * x, axis=-1, keepdims=True) + EPS) * g


def _params(n_axes):
    return pltpu.CompilerParams(
        dimension_semantics=("arbitrary",) * n_axes, vmem_limit_bytes=VMEM_LIMIT)


def _split_hi_lo(x):
    hi = x.astype(BF16)
    lo = (x - hi.astype(F32)).astype(BF16)
    return hi, lo


Q_OFF, KV_OFF, WKV_OFF = 0, ATTN_DIM, ATTN_DIM + 4 * KV_DIM


def _proj_kernel(x_ref, g_ref, w_ref, cos_ref, sa_ref, sb_ref, *rest, pool_dim, chunked):
    if chunked:
        q_ref, kv_ref, wkv_ref, kva_ref, gate_ref, u_ref, xc_ref, tmp_ref = rest
    else:
        q_ref, kv_ref, wkv_ref, kva_ref, gate_ref, u_ref = rest
    tm = x_ref.shape[0]
    u_off = WKV_OFF + 2 * KV_DIM
    gate_off = u_off + pool_dim
    h = _rms(x_ref[...], g_ref[...]).astype(BF16)
    cos, sa, sb = cos_ref[...], sa_ref[...], sb_ref[...]

    def rope(z):
        return (z * cos + pltpu.roll(z, LANES - ROT_HALF, 1) * sa
                + pltpu.roll(z, ROT_HALF, 1) * sb)

    zq = _dot(h, w_ref[:, Q_OFF:Q_OFF + ATTN_DIM])
    for hd in range(N_HEADS):
        sl = slice(hd * HEAD_DIM, (hd + 1) * HEAD_DIM)
        q_ref[:, sl] = rope(zq[:, sl]).astype(BF16)

    n_kv = CACHE_SLOTS * KV_HEADS
    zkv = _dot(h, w_ref[:, KV_OFF:KV_OFF + 4 * KV_DIM])
    for blk in range(n_kv):
        z = zkv[:, blk * HEAD_DIM:(blk + 1) * HEAD_DIM]
        if (blk // KV_HEADS) % 2 == 0:
            z = rope(z)
        kv_ref[pl.ds(blk, tm, stride=n_kv), :] = z
        if blk >= 2 * KV_HEADS:
            kva_ref[blk - 2 * KV_HEADS] = z.astype(BF16)
        elif chunked:
            tmp_ref[...] = z
            for r in range(CMP_STRIDE):
                xc_ref[blk, :, r * HEAD_DIM:(r + 1) * HEAD_DIM] = (
                    tmp_ref[pl.ds(r, tm // CMP_STRIDE, stride=CMP_STRIDE), :].astype(BF16))

    n_w = 2 * KV_HEADS
    zw = _dot(h, w_ref[:, WKV_OFF:WKV_OFF + 2 * KV_DIM])
    for blk in range(n_w):
        z = zw[:, blk * HEAD_DIM:(blk + 1) * HEAD_DIM]
        if blk < KV_HEADS:
            z = rope(z)
        wkv_ref[pl.ds(blk, tm, stride=n_w), :] = z
        kva_ref[2 * KV_HEADS + blk] = z.astype(BF16)

    u_ref[...] = _dot(h, w_ref[:, u_off:u_off + pool_dim])
    gl = _dot(h, w_ref[:, gate_off:gate_off + GATE_PAD])
    gate_ref[...] = 1.0 / (1.0 + jnp.exp(-gl))


def _project(x, gain, w_r, tables, table_index, tm, pool_dim, chunked):
    rows, d = x.shape
    n_proj = w_r.shape[1]
    n_kv = CACHE_SLOTS * KV_HEADS
    n_w = 2 * KV_HEADS
    row = lambda i: (i, 0)
    fixed = lambda i: (0, 0)
    tab_spec = pl.BlockSpec((tm, LANES), table_index)
    out_shape = [
        jax.ShapeDtypeStruct((rows, ATTN_DIM), BF16),
        jax.ShapeDtypeStruct((rows * n_kv, HEAD_DIM), F32),
        jax.ShapeDtypeStruct((rows * n_w, HEAD_DIM), F32),
        jax.ShapeDtypeStruct((n_w + n_kv // 2, rows, HEAD_DIM), BF16),
        jax.ShapeDtypeStruct((rows, GATE_PAD), F32),
        jax.ShapeDtypeStruct((rows, pool_dim), F32),
    ]
    out_specs = [
        pl.BlockSpec((tm, ATTN_DIM), row),
        pl.BlockSpec((tm * n_kv, HEAD_DIM), row),
        pl.BlockSpec((tm * n_w, HEAD_DIM), row),
        pl.BlockSpec((n_w + n_kv // 2, tm, HEAD_DIM), lambda i: (0, i, 0)),
        pl.BlockSpec((tm, GATE_PAD), row),
        pl.BlockSpec((tm, pool_dim), row),
    ]
    scratch = []
    if chunked:
        out_shape.append(jax.ShapeDtypeStruct((n_kv // 2, rows // CMP_STRIDE, CMP_HALF), BF16))
        out_specs.append(pl.BlockSpec((n_kv // 2, tm // CMP_STRIDE, CMP_HALF), lambda i: (0, i, 0)))
        scratch.append(pltpu.VMEM((tm, HEAD_DIM), F32))
    return pl.pallas_call(
        functools.partial(_proj_kernel, pool_dim=pool_dim, chunked=chunked),
        grid=(rows // tm,),
        in_specs=[
            pl.BlockSpec((tm, d), row),
            pl.BlockSpec((1, d), fixed),
            pl.BlockSpec((d, n_proj), fixed),
            tab_spec, tab_spec, tab_spec,
        ],
        out_specs=tuple(out_specs),
        out_shape=tuple(out_shape),
        scratch_shapes=scratch,
        compiler_params=_params(1),
        name="project",
    )(x, gain, w_r, *tables)


def _rope_tables(pos):
    inv = jnp.power(ROPE_THETA, -jnp.arange(ROT_HALF, dtype=F32) * (2.0 / ROT_DIM))
    ang = pos.astype(F32)[:, None] * inv[None, :]
    cos, sin = jnp.cos(ang), jnp.sin(ang)
    n = pos.shape[0]
    rest = LANES - ROT_DIM
    c = jnp.concatenate([cos, cos, jnp.ones((n, rest), F32)], axis=1)
    sa = jnp.concatenate([-sin, jnp.zeros((n, LANES - ROT_HALF), F32)], axis=1)
    sb = jnp.concatenate([jnp.zeros((n, ROT_HALF), F32), sin, jnp.zeros((n, rest), F32)], axis=1)
    return c, sa, sb


def _gelu_tanh(x):
    return 0.5 * x * (1.0 + jnp.tanh(0.7978845608028654 * (x + 0.044715 * (x * x * x))))


def _compress_first(x, wcat, pos8):
    ab = _dot(x, wcat)
    pa = _dot(pos8[:, :CMP_HALF], wcat)[0:1, :HEAD_DIM]
    pb = _dot(pos8[:, CMP_HALF:], wcat)[0:1, HEAD_DIM:]
    return ab, pa + pb


def _compress_hidden(ab, pos_term):
    rows = ab.shape[0]
    b_next = pltpu.roll(ab[:, HEAD_DIM:], rows - 1, 0)
    return _gelu_tanh(ab[:, :HEAD_DIM] + b_next + pos_term).astype(BF16)


def _compress_kernel(x_ref, wcat_ref, pos_ref, w2_ref, o_ref):
    g, _, nck, width = x_ref.shape
    x = x_ref[...].reshape(g * nck, width)
    hid = _compress_hidden(*_compress_first(x, wcat_ref[0], pos_ref[0]))
    o_ref[...] = _dot(hid, w2_ref[0]).astype(BF16).reshape(o_ref.shape)


def _compress_prompt(xc, wcat, pos8, w2, batch, seq):
    nck = seq // CMP_STRIDE
    x = xc.reshape(xc.shape[0], batch, nck, CMP_HALF)
    return pl.pallas_call(
        _compress_kernel,
        grid=(batch, 2),
        in_specs=[
            pl.BlockSpec((KV_HEADS, 1, nck, CMP_HALF), lambda b, s: (s, b, 0, 0)),
            pl.BlockSpec((1, CMP_HALF, 2 * HEAD_DIM), lambda b, s: (s, 0, 0)),
            pl.BlockSpec((1, 8, CMP_IN), lambda b, s: (s, 0, 0)),
            pl.BlockSpec((1, HEAD_DIM, HEAD_DIM), lambda b, s: (s, 0, 0)),
        ],
        out_specs=pl.BlockSpec((1, 1, KV_HEADS, nck, HEAD_DIM), lambda b, s: (b, s, 0, 0, 0)),
        out_shape=jax.ShapeDtypeStruct((batch, 2, KV_HEADS, nck, HEAD_DIM), BF16),
        compiler_params=_params(2),
        name="compress_prompt",
    )(x, wcat, pos8, w2)


def _topk_rows(score, jidx, topk):
    rank = jnp.zeros_like(score)
    for j in range(score.shape[0]):
        bj = score[j:j + 1, :]
        tie = jnp.where(jidx > j, 1.0, 0.0)
        rank = rank + jnp.where(bj > score, 1.0, jnp.where(bj == score, tie, 0.0))
    return jnp.where(rank < topk, 1.0, 0.0)


def _softmax_parts(s, ok):
    s = jnp.where(ok, s, NEG_INF)
    m = jnp.max(s, axis=-1, keepdims=True)
    e = jnp.where(ok, jnp.exp(s - m), 0.0)
    d = jnp.sum(e, axis=-1, keepdims=True)
    return e, jnp.where(d > 0, d, 1.0)


EXP2_SCALE = SCALE * 1.4426950408889634
HEAD_PAIRS = HPG // 2


def _attn_prompt_kernel(q_ref, kc_ref, vc_ref, ks_ref, vs_ref, kw_ref, vw_ref, gate_ref,
                        mt_ref, eye_ref, o_ref,
                        vct_ref, vst_ref, vwt_ref, sel_ref, gt_ref, acc_ref,
                        *, tq, tk, seq, topk):
    g = pl.program_id(1)
    i = pl.program_id(2)
    t0 = i * tq
    n_sel = mt_ref.shape[0]
    pair_w = 2 * tq
    eye = eye_ref[...]

    @pl.when(i == 0)
    def _():
        vct_ref[...] = _dot_nt(eye, vc_ref[0, 0, 0]).astype(BF16)
        vst_ref[...] = _dot_nt(eye, vs_ref[0]).astype(BF16)
        vwt_ref[...] = _dot_nt(eye, vw_ref[0]).astype(BF16)

    q4 = q_ref[...]
    q_pairs = [jnp.concatenate([q4[:, (2 * hp) * HEAD_DIM:(2 * hp + 1) * HEAD_DIM],
                                q4[:, (2 * hp + 1) * HEAD_DIM:(2 * hp + 2) * HEAD_DIM]], axis=0)
               for hp in range(HEAD_PAIRS)]

    def both_heads(x):
        return jnp.concatenate([x, x], axis=1)

    def tpos(n_keys):
        return t0 + lax.broadcasted_iota(jnp.int32, (n_keys, tq), 1)

    def kidx(n_keys):
        return lax.broadcasted_iota(jnp.int32, (n_keys, tq), 0)

    pairs = range(HEAD_PAIRS)

    kc = kc_ref[0, 0, 0]
    ncp = kc.shape[0]
    wk = min(WINDOW + tq, seq)
    ws = pl.multiple_of(jnp.maximum(t0 + tq - wk, 0), tq)
    kw = kw_ref[0, pl.ds(ws, wk), :]
    s_cmp = [_dot_nt(kc, q_pairs[hp]) for hp in pairs]
    s_win = [_dot_nt(kw, q_pairs[hp]) for hp in pairs]

    ok = both_heads(jnp.where(kidx(ncp) * CMP_STRIDE + (CMP_LEN - 1) <= tpos(ncp), 1.0, 0.0)) > 0.5
    p_cmp, p_sum = [], None
    for hp in pairs:
        s = jnp.where(ok, s_cmp[hp], NEG_INF)
        m = jnp.max(s, axis=0, keepdims=True)
        e = jnp.where(ok, jnp.exp2((s - m) * EXP2_SCALE), 0.0)
        d = jnp.sum(e, axis=0, keepdims=True)
        p = e / jnp.where(d > 0, d, 1.0)
        p_cmp.append(p.astype(BF16))
        ph = p[:, :tq] + p[:, tq:]
        p_sum = ph if p_sum is None else p_sum + ph

    p_hi, p_lo = _split_hi_lo(p_sum)
    mt = mt_ref[...]
    imp = _dot(mt, p_hi) + _dot(mt, p_lo)
    o_cmp = [_dot(vct_ref[...], p_cmp[hp]) for hp in pairs]

    kpos = ws + kidx(wk)
    bias = both_heads(jnp.where(kpos <= tpos(wk),
                                jnp.where(kpos > tpos(wk) - WINDOW, 0.0, NEG_INF), NEG_INF))
    p_win, l_win = [], []
    for hp in pairs:
        s = s_win[hp] + bias
        p = jnp.exp2((s - jnp.max(s, axis=0, keepdims=True)) * EXP2_SCALE)
        l_win.append(jnp.sum(p, axis=0, keepdims=True))
        p_win.append(p.astype(BF16))
    vwt = vwt_ref[:, pl.ds(ws, wk)]
    o_win = [_dot(vwt, p_win[hp]) for hp in pairs]
    o_win = [o_win[hp] / l_win[hp] for hp in pairs]

    jidx = kidx(n_sel)
    jt = tpos(n_sel) // SEL_BLOCK
    forced = jnp.where(jidx == 0, 1.0, jnp.where(jidx == jt, 1.0, jnp.where(jidx == jt - 1, 1.0, 0.0)))
    score = jnp.where(forced > 0.5, FORCE_SCORE, jnp.where(jidx <= jt, imp, -1.0))
    sel_ref[...] = _topk_rows(score, jidx, topk)

    acc_ref[...] = jnp.zeros_like(acc_ref)

    def sel_step(kb, carry):
        k0 = pl.multiple_of(kb * tk, tk)
        k = ks_ref[0, pl.ds(k0, tk), :]
        vt = vst_ref[:, pl.ds(k0, tk)]
        blk0 = kb * (tk // SEL_BLOCK)
        chosen = jnp.concatenate(
            [jnp.broadcast_to(sel_ref[pl.ds(blk0 + j, 1), :], (SEL_BLOCK, tq))
             for j in range(tk // SEL_BLOCK)], axis=0)
        bias = both_heads(jnp.where(k0 + kidx(tk) <= tpos(tk),
                                    jnp.where(chosen > 0.5, 0.0, NEG_INF), NEG_INF))
        ss = [_dot_nt(k, q_pairs[hp]) + bias for hp in pairs]
        out, ps, alphas = [], [], []
        for hp in pairs:
            m, l = carry[2 * hp], carry[2 * hp + 1]
            m_new = jnp.maximum(m, jnp.max(ss[hp], axis=0, keepdims=True))
            p = jnp.exp2((ss[hp] - m_new) * EXP2_SCALE)
            alpha = jnp.exp2((m - m_new) * EXP2_SCALE)
            out += [m_new, alpha * l + jnp.sum(p, axis=0, keepdims=True)]
            ps.append(p.astype(BF16))
            alphas.append(alpha)
        pvs = [_dot(vt, ps[hp]) for hp in pairs]
        for hp in pairs:
            acc_ref[hp] = alphas[hp] * acc_ref[hp] + pvs[hp]
        return tuple(out)

    n_kb = (t0 + tq + tk - 1) // tk
    init = (jnp.full((1, pair_w), NEG_INF, F32), jnp.zeros((1, pair_w), F32)) * HEAD_PAIRS
    stats = lax.fori_loop(0, n_kb, sel_step, init)
    o_sel = [acc_ref[hp] / stats[2 * hp + 1] for hp in pairs]

    gt_ref[...] = gate_ref[...].T
    o_t = []
    for h in range(HPG):
        hp, lanes = h // 2, slice((h % 2) * tq, (h % 2 + 1) * tq)
        col = (g * HPG + h) * N_BRANCH
        o_t.append((gt_ref[pl.ds(col, 1), :] * o_cmp[hp][:, lanes]
                    + gt_ref[pl.ds(col + 1, 1), :] * o_sel[hp][:, lanes]
                    + gt_ref[pl.ds(col + 2, 1), :] * o_win[hp][:, lanes]).astype(BF16))
    outs = [_dot_nt(eye, o_t[h]) for h in range(HPG)]
    for h in range(HPG):
        o_ref[:, h * HEAD_DIM:(h + 1) * HEAD_DIM] = outs[h].astype(BF16)


def _cmp_to_sel(n_cmp_pad, n_cmp, n_sel):
    cs = np.arange(n_cmp_pad)[:, None] * CMP_STRIDE
    ss = np.arange(n_sel)[None, :] * SEL_BLOCK
    hit = (cs < ss + SEL_BLOCK) & (cs + CMP_LEN > ss) & (np.arange(n_cmp_pad)[:, None] < n_cmp)
    return hit.astype(np.float32)


def _attend_prompt(q, cmp_kv, kva, gates, batch, seq):
    tq = min(128, seq)
    tk = min(512, seq)
    nq = seq // tq
    ncp = seq // CMP_STRIDE
    n_sel = -(-seq // SEL_BLOCK)
    topk = min(SEL_TOPK, n_sel)
    assert tq == HEAD_DIM and seq % tk == 0 and WINDOW % tq == 0
    mt = jnp.asarray(_cmp_to_sel(ncp, ncp - 1, n_sel).T, BF16)
    eye = jnp.asarray(np.eye(tq, dtype=np.float32), BF16)
    kv_spec = lambda slot: pl.BlockSpec(
        (1, seq, HEAD_DIM), lambda b, g, i: (slot * KV_HEADS + g, b, 0))
    cmp_spec = lambda s: pl.BlockSpec(
        (1, 1, 1, ncp, HEAD_DIM), lambda b, g, i: (b, s, g, 0, 0))
    full = lambda a: pl.BlockSpec(a.shape, lambda b, g, i: (0, 0))
    return pl.pallas_call(
        functools.partial(_attn_prompt_kernel, tq=tq, tk=tk, seq=seq, topk=topk),
        grid=(batch, KV_HEADS, nq),
        in_specs=[
            pl.BlockSpec((tq, HPG * HEAD_DIM), lambda b, g, i: (b * nq + i, g)),
            cmp_spec(0), cmp_spec(1),
            kv_spec(0), kv_spec(1), kv_spec(2), kv_spec(3),
            pl.BlockSpec((tq, GATE_PAD), lambda b, g, i: (b * nq + i, 0)),
            full(mt), full(eye),
        ],
        out_specs=pl.BlockSpec((tq, HPG * HEAD_DIM), lambda b, g, i: (b * nq + i, g)),
        out_shape=jax.ShapeDtypeStruct((batch * seq, ATTN_DIM), BF16),
        scratch_shapes=[
            pltpu.VMEM((HEAD_DIM, ncp), BF16),
            pltpu.VMEM((HEAD_DIM, seq), BF16),
            pltpu.VMEM((HEAD_DIM, seq), BF16),
            pltpu.VMEM((n_sel, tq), F32),
            pltpu.VMEM((GATE_PAD, tq), F32),
            pltpu.VMEM((HEAD_PAIRS, HEAD_DIM, 2 * tq), F32),
        ],
        compiler_params=_params(3),
        name="attend_prompt",
    )(q, cmp_kv, cmp_kv, kva, kva, kva, kva, gates, mt, eye)


def _attn_sample_kernel(pt_ref, cache_ref, q_ref, kvn_ref, wn_ref, win_ref, gate_ref,
                        wcat_ref, pos_ref, w2_ref, ms_ref, ek_ref, o_ref, buf, sem,
                        *, n_pages, page, past, topk, n_sel):
    b = pl.program_id(0)
    nb = pl.num_programs(0)
    n_kv = CACHE_SLOTS * KV_HEADS
    n_w = 2 * KV_HEADS
    cpp = page // CMP_STRIDE
    nch = past // CMP_STRIDE
    wb = win_ref.shape[1] // n_w

    def page_copy(seq_idx, slot, pi, r):
        src0 = pl.multiple_of(pt_ref[seq_idx, pi] * cpp, cpp)
        return pltpu.make_async_copy(
            cache_ref.at[pl.ds(src0, cpp), r],
            buf.at[slot, r, pl.ds(pl.multiple_of(pi * cpp, cpp), cpp)], sem.at[slot])

    def for_all_copies(seq_idx, slot, act):
        def per_page(pi, carry):
            for r in range(CMP_STRIDE):
                act(page_copy(seq_idx, slot, pi, r))
            return carry
        lax.fori_loop(0, n_pages, per_page, 0)

    @pl.when(b == 0)
    def _():
        for_all_copies(0, 0, lambda c: c.start())

    @pl.when(b + 1 < nb)
    def _():
        for_all_copies(b + 1, (b + 1) % 2, lambda c: c.start())

    slot = b % 2
    for_all_copies(b, slot, lambda c: c.wait())

    def chunk_rows(cache_slot, g, r):
        rows = buf.reshape(2, CMP_STRIDE, nch * n_kv, HEAD_DIM)
        return rows[slot, r, pl.ds(cache_slot * KV_HEADS + g, nch, stride=n_kv), :]

    row8 = lax.broadcasted_iota(jnp.int32, (N_HEADS, 1), 0)
    in_g0 = row8 < HPG

    def by_group(x0, x1):
        return jnp.where(in_g0, x0, x1)

    q8 = q_ref[0]
    qf = q8.astype(F32)
    groups = range(KV_HEADS)

    def scores(keys):
        return by_group(_dot_nt(q8, keys[0]), _dot_nt(q8, keys[1])) * SCALE

    def cached(cache_slot, g):
        return jnp.concatenate([chunk_rows(cache_slot, g, r).astype(BF16)
                                for r in range(CMP_STRIDE)], axis=0)

    def compress_input(cache_slot):
        return jnp.concatenate(
            [jnp.concatenate([chunk_rows(cache_slot, g, r).astype(BF16)
                              for r in range(CMP_STRIDE)], axis=1)
             for g in groups], axis=0)

    first = [_compress_first(compress_input(cs), wcat_ref[cs], pos_ref[cs]) for cs in range(2)]
    s_sel = scores([cached(2, g) for g in groups])
    win_keys = [win_ref[0, pl.ds(g, wb, stride=n_w), :].astype(BF16) for g in groups]
    s_win = scores(win_keys)
    kc, vc =[_dot(_compress_hidden(*first[cs]), w2_ref[cs]).astype(BF16) for cs in range(2)]

    lane_n = lax.broadcasted_iota(jnp.int32, (N_HEADS, nch), 1)
    s = by_group(_dot_nt(q8, kc[:nch]), _dot_nt(q8, kc[nch:])) * SCALE
    ok = lane_n * CMP_STRIDE + (CMP_LEN - 1) <= past
    e, d = _softmax_parts(s, ok)
    p = e / d
    pb = p.astype(BF16)
    o_cmp = by_group(_dot(pb, vc[:nch]), _dot(pb, vc[nch:]))

    p_g = [jnp.sum(p[g * HPG:(g + 1) * HPG], axis=0, keepdims=True) for g in range(KV_HEADS)]
    p2 = jnp.concatenate(p_g + [jnp.zeros((N_HEADS - KV_HEADS, nch), F32)], axis=0)
    p_hi, p_lo = _split_hi_lo(p2)
    imp = _dot(p_hi, ms_ref[...]) + _dot(p_lo, ms_ref[...])
    jl = lax.broadcasted_iota(jnp.int32, (N_HEADS, LANES), 1)
    jt = past // SEL_BLOCK
    forced = jnp.where(jl == 0, 1.0, jnp.where(jl == jt, 1.0, jnp.where(jl == jt - 1, 1.0, 0.0)))
    score = jnp.where(forced > 0.5, FORCE_SCORE, jnp.where(jl <= jt, imp, -1.0))
    score = jnp.where(jl < n_sel, score, -2.0)
    ii = lax.broadcasted_iota(jnp.int32, (LANES, LANES), 0)
    jj = lax.broadcasted_iota(jnp.int32, (LANES, LANES), 1)
    sel_rows = []
    for g in range(KV_HEADS):
        srow = jnp.broadcast_to(score[g:g + 1, :], (LANES, LANES))
        scol = jnp.sum(jnp.where(ii == jj, srow, 0.0), axis=1, keepdims=True)
        tie = jnp.where(ii < jj, 1.0, 0.0)
        beats = jnp.where(scol > srow, 1.0, jnp.where(scol == srow, tie, 0.0))
        rank = jnp.sum(beats, axis=0, keepdims=True)
        sel_rows.append(jnp.where(rank < topk, 1.0, 0.0))
    sel2 = jnp.concatenate(sel_rows + [jnp.zeros((N_HEADS - KV_HEADS, LANES), F32)], axis=0)
    chunk_ok2 = _dot(sel2.astype(BF16), ek_ref[...])
    chunk_ok = by_group(chunk_ok2[0:1], chunk_ok2[1:2])
    key_ok = jnp.concatenate([chunk_ok] * CMP_STRIDE, axis=1)
    new_ok = by_group(*[jnp.sum(jnp.where(jl[0:1] == jt, sel2[g:g + 1], 0.0), axis=1, keepdims=True)
                        for g in range(KV_HEADS)])

    def new_row(ref, idx0):
        x = by_group(ref[0, idx0:idx0 + 1, :], ref[0, idx0 + 1:idx0 + 2, :])
        return x.astype(BF16).astype(F32)

    def weights(s, ok, k_new, new_ok):
        s_new = jnp.sum(qf * k_new, axis=-1, keepdims=True) * SCALE
        s = jnp.where(ok, s, NEG_INF)
        s_new = jnp.where(new_ok, s_new, NEG_INF)
        m = jnp.maximum(jnp.max(s, axis=-1, keepdims=True), s_new)
        e = jnp.where(ok, jnp.exp(s - m), 0.0)
        e_new = jnp.where(new_ok, jnp.exp(s_new - m), 0.0)
        d = jnp.sum(e, axis=-1, keepdims=True) + e_new
        return e.astype(BF16), e_new, jnp.where(d > 0, d, 1.0)

    def weighted(eb, vals, e_new, v_new, d):
        return (by_group(_dot(eb, vals[0]), _dot(eb, vals[1])) + e_new * v_new) / d

    kpos = past - wb + lax.broadcasted_iota(jnp.int32, (N_HEADS, wb), 1)
    w_sel = weights(s_sel, key_ok > 0.5, new_row(kvn_ref, 2 * KV_HEADS), new_ok > 0.5)
    w_win = weights(s_win, kpos > past - WINDOW, new_row(wn_ref, 0), row8 >= 0)
    win_vals = [win_ref[0, pl.ds(KV_HEADS + g, wb, stride=n_w), :].astype(BF16) for g in groups]
    o_sel = weighted(w_sel[0], [cached(3, g) for g in groups], w_sel[1],
                     new_row(kvn_ref, 3 * KV_HEADS), w_sel[2])
    o_win = weighted(w_win[0], win_vals, w_win[1], new_row(wn_ref, KV_HEADS), w_win[2])

    gates = jnp.broadcast_to(gate_ref[0], (N_HEADS, GATE_PAD))
    lane = lax.broadcasted_iota(jnp.int32, (N_HEADS, GATE_PAD), 1)

    def gate(br):
        return jnp.sum(jnp.where(lane == row8 * N_BRANCH + br, gates, 0.0), axis=-1, keepdims=True)

    o_ref[0] = (gate(0) * o_cmp + gate(1) * o_sel + gate(2) * o_win).astype(BF16)


def _attend_sample(page_table, cache, q, kv_new, wkv_new, win_rows, gates, wcat, pos8, w2):
    nb, n_pages = page_table.shape
    n_phys, page = cache.shape[:2]
    n_kv = CACHE_SLOTS * KV_HEADS
    n_w = 2 * KV_HEADS
    past = n_pages * page
    nch = past // CMP_STRIDE
    wb = win_rows.shape[1] // n_w
    n_sel = -(-(past + 1) // SEL_BLOCK)
    n_cmp = (past + 1) // CMP_STRIDE - 1
    topk = min(SEL_TOPK, n_sel)
    assert n_sel <= LANES and nch * CMP_STRIDE == past
    ms = np.zeros((nch, LANES), np.float32)
    ms[:, :n_sel] = _cmp_to_sel(nch, n_cmp, n_sel)
    ek = (np.arange(nch)[None, :] * CMP_STRIDE // SEL_BLOCK == np.arange(LANES)[:, None])
    ms = jnp.asarray(ms, BF16)
    ek = jnp.asarray(ek.astype(np.float32), BF16)
    seq3 = lambda n: pl.BlockSpec((1, n, HEAD_DIM), lambda b, pt: (b, 0, 0))
    full3 = lambda a: pl.BlockSpec(a.shape, lambda b, pt: (0, 0, 0))
    full2 = lambda a: pl.BlockSpec(a.shape, lambda b, pt: (0, 0))
    grid_spec = pltpu.PrefetchScalarGridSpec(
        num_scalar_prefetch=1,
        grid=(nb,),
        in_specs=[
            pl.BlockSpec(memory_space=pl.ANY),
            seq3(N_HEADS), seq3(n_kv), seq3(n_w), seq3(wb * n_w),
            pl.BlockSpec((1, 1, GATE_PAD), lambda b, pt: (b, 0, 0)),
            full3(wcat), full3(pos8), full3(w2), full2(ms), full2(ek),
        ],
        out_specs=seq3(N_HEADS),
        scratch_shapes=[pltpu.VMEM((2, CMP_STRIDE, nch, n_kv, HEAD_DIM), F32),
                        pltpu.SemaphoreType.DMA((2,))],
    )
    return pl.pallas_call(
        functools.partial(_attn_sample_kernel, n_pages=n_pages, page=page, past=past,
                          topk=topk, n_sel=n_sel),
        grid_spec=grid_spec,
        out_shape=jax.ShapeDtypeStruct((nb, N_HEADS, HEAD_DIM), BF16),
        compiler_params=_params(1),
        name="attend_sample",
    )(page_table, cache.reshape(n_phys * page // CMP_STRIDE, CMP_STRIDE, n_kv, HEAD_DIM),
      q.reshape(nb, N_HEADS, HEAD_DIM), kv_new.reshape(nb, n_kv, HEAD_DIM),
      wkv_new.reshape(nb, n_w, HEAD_DIM), win_rows,
      gates.reshape(nb, 1, GATE_PAD), wcat, pos8, w2, ms, ek)


def _mix_tail(o, diffs, x, pw_ref, ps_ref, wo_ref, gpost_ref, gpre_ref, y1_ref, h2_ref):
    gw = diffs[0].shape[1]
    ys = [(_dot(diffs[g].astype(BF16), pw_ref[g]) * ps_ref[:, g * gw:(g + 1) * gw]).astype(BF16)
          for g in range(POOL_GROUPS)]
    cat = jnp.concatenate([o] + ys, axis=1)
    m = _dot(cat, wo_ref[...])
    y1 = x + _rms(m, gpost_ref[...])
    y1_ref[...] = y1
    h2_ref[...] = _rms(y1, gpre_ref[...]).astype(BF16)


def _mix_prompt_kernel(o_ref, u_ref, halo_ref, x_ref, pw_ref, ps_ref, wo_ref, gpost_ref, gpre_ref,
                       y1_ref, h2_ref, *, tm):
    i = pl.program_id(1)
    halo_rows = halo_ref.shape[0]
    halo = jnp.where(i > 0, halo_ref[...], 0.0)
    u = u_ref[...]
    uext = jnp.concatenate([halo, u], axis=0)
    n_ext = uext.shape[0]
    gw = u.shape[1] // POOL_GROUPS
    tpos = i * tm + lax.broadcasted_iota(jnp.int32, (tm, 1), 0)
    diffs = []
    for g, w in enumerate(POOL_WINDOWS):
        s = uext[:, g * gw:(g + 1) * gw]
        k = 1
        while k < w:
            s = s + pltpu.roll(s, k, 0)
            k *= 2
        cnt = jnp.minimum(w, tpos + 1).astype(F32)
        diffs.append(s[halo_rows:n_ext] / cnt - u[:, g * gw:(g + 1) * gw])
    _mix_tail(o_ref[...], diffs, x_ref[...], pw_ref, ps_ref, wo_ref, gpost_ref, gpre_ref,
              y1_ref, h2_ref)


def _mix_sample_kernel(o_ref, u_ref, st_ref, x_ref, pw_ref, ps_ref, wo_ref, gpost_ref, gpre_ref,
                       y1_ref, h2_ref, *, past):
    u = u_ref[...]
    c = u.shape[1]
    gw = c // POOL_GROUPS
    n_hist = st_ref.shape[1] // c
    diffs = []
    for g, w in enumerate(POOL_WINDOWS):
        un = u[:, g * gw:(g + 1) * gw]
        s = un
        for back in range(1, w):
            r = n_hist - back
            s = s + st_ref[:, r * c + g * gw:r * c + (g + 1) * gw]
        diffs.append(s / float(min(w, past + 1)) - un)
    _mix_tail(o_ref[...], diffs, x_ref[...], pw_ref, ps_ref, wo_ref, gpost_ref, gpre_ref,
              y1_ref, h2_ref)


def _mix_specs(tm, d, c, pool_w, idx):
    fixed2 = lambda *a: (0, 0)
    fixed3 = lambda *a: (0, 0, 0)
    weights = [
        pl.BlockSpec(pool_w.shape, fixed3),
        pl.BlockSpec((1, c), fixed2),
        pl.BlockSpec((d, d), fixed2),
        pl.BlockSpec((1, d), fixed2),
        pl.BlockSpec((1, d), fixed2),
    ]
    outs = (pl.BlockSpec((tm, d), idx), pl.BlockSpec((tm, d), idx))
    return weights, outs


def _mix_prompt(o, u, x, pool_w, pool_scale, w_o, g_post, g_pre, batch, seq):
    rows, d = x.shape
    c = u.shape[1]
    tm = min(256, seq)
    nt = seq // tm
    halo = 16
    assert halo >= POOL_BUF and seq % tm == 0 and tm % halo == 0
    idx = lambda b, i: (b * nt + i, 0)
    halo_idx = lambda b, i: (jnp.maximum((b * nt + i) * (tm // halo) - 1, 0), 0)
    weights, outs = _mix_specs(tm, d, c, pool_w, idx)
    return pl.pallas_call(
        functools.partial(_mix_prompt_kernel, tm=tm),
        grid=(batch, nt),
        in_specs=[pl.BlockSpec((tm, ATTN_DIM), idx), pl.BlockSpec((tm, c), idx),
                  pl.BlockSpec((halo, c), halo_idx), pl.BlockSpec((tm, d), idx)] + weights,
        out_specs=outs,
        out_shape=(jax.ShapeDtypeStruct((rows, d), F32), jax.ShapeDtypeStruct((rows, d), BF16)),
        compiler_params=_params(2),
        name="mix_prompt",
    )(o, u, u, x, pool_w, pool_scale, w_o, g_post, g_pre)


def _mix_sample(o, u, pool_state, x, pool_w, pool_scale, w_o, g_post, g_pre, past):
    rows, d = x.shape
    c = u.shape[1]
    tm = rows
    idx = lambda i: (i, 0)
    weights, outs = _mix_specs(tm, d, c, pool_w, idx)
    st = pool_state.reshape(rows, -1)
    return pl.pallas_call(
        functools.partial(_mix_sample_kernel, past=past),
        grid=(rows // tm,),
        in_specs=[pl.BlockSpec((tm, ATTN_DIM), idx), pl.BlockSpec((tm, c), idx),
                  pl.BlockSpec((tm, st.shape[1]), idx), pl.BlockSpec((tm, d), idx)] + weights,
        out_specs=outs,
        out_shape=(jax.ShapeDtypeStruct((rows, d), F32), jax.ShapeDtypeStruct((rows, d), BF16)),
        compiler_params=_params(1),
        name="mix_sample",
    )(o, u, st, x, pool_w, pool_scale, w_o, g_post, g_pre)


def _mlp_kernel(h_ref, wu_ref, wd_ref, y1_ref, g_ref, *rest, shift):
    if shift:
        state_ref, fresh_ref, y_ref, rolled_ref, acc_ref = rest
        n_new = fresh_ref.shape[1]
        n_keep = state_ref.shape[1] - n_new
        rolled_ref[:, pl.ds(0, n_keep), :] = state_ref[:, pl.ds(n_new, n_keep), :]
        rolled_ref[:, pl.ds(n_keep, n_new), :] = fresh_ref[...]
    else:
        y_ref, acc_ref = rest
    j = pl.program_id(1)

    @pl.when(j == 0)
    def _():
        acc_ref[...] = jnp.zeros_like(acc_ref)

    a = jnp.maximum(_dot(h_ref[...], wu_ref[...]), 0.0)
    acc_ref[...] += _dot((a * a).astype(BF16), wd_ref[...])

    @pl.when(j == pl.num_programs(1) - 1)
    def _():
        y_ref[...] = y1_ref[...] + _rms(acc_ref[...], g_ref[...])


def _mlp(h2, y1, w_up, w_down, gain, tm, state=None, fresh=None):
    rows, d = y1.shape
    ff = w_up.shape[1]
    tf = min(1024, ff)
    n_i, n_j = rows // tm, ff // tf
    row = lambda i, j: (i, 0)
    shift = state is not None
    y_spec = pl.BlockSpec((tm, d), row)
    y_shape = jax.ShapeDtypeStruct((rows, d), F32)
    extra_in, out_specs, out_shape = [], y_spec, y_shape
    if shift:
        per_step = state.shape[0] // (n_i * n_j)
        assert per_step * n_i * n_j == state.shape[0]
        blk = lambda a: pl.BlockSpec((per_step,) + a.shape[1:], lambda i, j: (i * n_j + j, 0, 0))
        extra_in = [blk(state), blk(fresh)]
        out_specs = (y_spec, blk(state))
        out_shape = (y_shape, jax.ShapeDtypeStruct(state.shape, state.dtype))
    return pl.pallas_call(
        functools.partial(_mlp_kernel, shift=shift),
        grid=(n_i, n_j),
        in_specs=[
            pl.BlockSpec((tm, d), row),
            pl.BlockSpec((d, tf), lambda i, j: (0, j)),
            pl.BlockSpec((tf, d), lambda i, j: (j, 0)),
            pl.BlockSpec((tm, d), row),
            pl.BlockSpec((1, d), lambda i, j: (0, 0)),
        ] + extra_in,
        out_specs=out_specs,
        out_shape=out_shape,
        scratch_shapes=[pltpu.VMEM((tm, d), F32)],
        compiler_params=_params(2),
        name="mlp",
    )(h2, w_up, w_down, y1, gain, *((state, fresh) if shift else ()))


def _layer_weights(w_in, cmp_pos_k, cmp_w1_k, cmp_w2_k, cmp_pos_v, cmp_w1_v, cmp_w2_v,
                   pool_w, w_o, w_up, w_down, pool_dim):
    gate_lo = WKV_OFF + 2 * KV_DIM
    gate_hi = gate_lo + N_BRANCH * N_HEADS
    w_r = jnp.concatenate(
        [w_in[:, :gate_lo], w_in[:, gate_hi:gate_hi + pool_dim],
         jnp.pad(w_in[:, gate_lo:gate_hi], ((0, 0), (0, GATE_PAD - N_BRANCH * N_HEADS)))],
        axis=1).astype(BF16)
    wcat = jnp.stack([jnp.concatenate([w1[:CMP_HALF], w1[CMP_HALF:]], axis=1)
                      for w1 in (cmp_w1_k, cmp_w1_v)]).astype(BF16)
    pos8 = jnp.stack([jnp.pad(p.reshape(1, CMP_IN), ((0, 7), (0, 0)))
                      for p in (cmp_pos_k, cmp_pos_v)]).astype(BF16)
    w2 = jnp.stack([cmp_w2_k, cmp_w2_v]).astype(BF16)
    return (w_r, wcat, pos8, w2, pool_w.astype(BF16), w_o.astype(BF16),
            w_up.astype(BF16), w_down.astype(BF16))


def kernel(x_prompt, x_sample, cache_kv, state_win_kv, state_pool, page_table, norm_mix_pre, w_in,
           cmp_pos_k, cmp_w1_k, cmp_w2_k, cmp_pos_v, cmp_w1_v, cmp_w2_v, pool_w, pool_scale, w_o,
           norm_mix_post, norm_mlp_pre, w_up, w_down, norm_mlp_post):
    batch, seq, d = x_prompt.shape
    nb, dec_seq, _ = x_sample.shape
    depth = w_in.shape[0]
    pool_dim = d - ATTN_DIM
    n_pages = page_table.shape[1]
    page = cache_kv.shape[2]
    past = n_pages * page
    wb = state_win_kv.shape[2]
    assert dec_seq == 1 and seq >= POOL_BUF and seq % CMP_STRIDE == 0
    assert w_in.shape[2] == ATTN_DIM + 6 * KV_DIM + N_BRANCH * N_HEADS + pool_dim

    tm_p = min(256, seq)
    tabs_p = _rope_tables(jnp.arange(seq, dtype=jnp.int32))
    tabs_s = _rope_tables(jnp.full((nb,), past, jnp.int32))
    nt_p = seq // tm_p

    y_p = x_prompt.reshape(batch * seq, d)
    y_s = x_sample.reshape(nb, d)
    kv_p, kv_s, win_p, win_s, pool_p, pool_s = [], [], [], [], [], []
    row_vec = lambda v: v.reshape(1, -1)
    for l in range(depth):
        w_r, wcat, pos8, w2, pw, wo, wu, wd = _layer_weights(
            w_in[l], cmp_pos_k[l], cmp_w1_k[l], cmp_w2_k[l], cmp_pos_v[l], cmp_w1_v[l],
            cmp_w2_v[l], pool_w[l], w_o[l], w_up[l], w_down[l], pool_dim)
        g_pre, g_post = row_vec(norm_mix_pre[l]), row_vec(norm_mix_post[l])
        g_mlp_pre, g_mlp_post = row_vec(norm_mlp_pre[l]), row_vec(norm_mlp_post[l])
        ps = row_vec(pool_scale[l])

        n_w = 2 * KV_HEADS
        win_rows = state_win_kv[l].reshape(nb, wb * n_w, HEAD_DIM)

        q_s, kv_s1, wkv_s, _, gates_s, u_s = _project(
            y_s, g_pre, w_r, tabs_s, lambda i: (i, 0), nb, pool_dim, False)

        q, kv, wkv, kva, gates, u, xc = _project(
            y_p, g_pre, w_r, tabs_p, lambda i: (i % nt_p, 0), tm_p, pool_dim, True)
        cmp_kv = _compress_prompt(xc, wcat, pos8, w2, batch, seq)
        o = _attend_prompt(q, cmp_kv, kva, gates, batch, seq)
        y1, h2 = _mix_prompt(o, u, y_p, pw, ps, wo, g_post, g_mlp_pre, batch, seq)
        y_p, win_rolled = _mlp(h2, y1, wu, wd, g_mlp_post, min(512, batch * seq),
                               win_rows, wkv_s.reshape(nb, n_w, HEAD_DIM))
        kv_p.append(kv.reshape(batch, seq, CACHE_SLOTS, KV_HEADS, HEAD_DIM))
        wp = min(WINDOW, seq)
        win_p.append(wkv.reshape(batch, seq, 2, KV_HEADS, HEAD_DIM)[:, seq - wp:])
        pool_p.append(u.reshape(batch, seq, pool_dim)[:, seq - POOL_BUF:])

        o = _attend_sample(page_table, cache_kv[l], q_s, kv_s1, wkv_s, win_rows, gates_s,
                           wcat, pos8, w2)
        y1, h2 = _mix_sample(o.reshape(nb, ATTN_DIM), u_s, state_pool[l], y_s, pw, ps, wo,
                             g_post, g_mlp_pre, past)
        y_s = _mlp(h2, y1, wu, wd, g_mlp_post, nb)
        kv_s.append(kv_s1.reshape(nb, 1, CACHE_SLOTS, KV_HEADS, HEAD_DIM))
        win_s.append(win_rolled.reshape(nb, wb, 2, KV_HEADS, HEAD_DIM))
        pool_s.append(jnp.concatenate([state_pool[l], u_s[:, None]], axis=1)[:, 1:])

    return (y_p.reshape(batch, seq, d), y_s.reshape(nb, 1, d),
            jnp.stack(kv_p), jnp.stack(kv_s), jnp.stack(win_p), jnp.stack(win_s),
            jnp.stack(pool_p), jnp.stack(pool_s))
```

```python
import functools

import numpy as np
import jax
import jax.numpy as jnp
from jax import lax
from jax.experimental import pallas as pl
from jax.experimental.pallas import tpu as pltpu

N_HEADS = 8
HEAD_DIM = 128
KV_HEADS = 2
HPG = N_HEADS // KV_HEADS
ATTN_DIM = N_HEADS * HEAD_DIM
KV_DIM = KV_HEADS * HEAD_DIM
N_BRANCH = 3
POOL_WINDOWS = (2, 4, 8, 16)
POOL_GROUPS = len(POOL_WINDOWS)
POOL_BUF = max(POOL_WINDOWS) - 1
ROT_DIM = HEAD_DIM // 4
ROT_HALF = ROT_DIM // 2
ROPE_THETA = 500000.0
CMP_LEN = 32
CMP_STRIDE = 16
SEL_BLOCK = 64
SEL_TOPK = 16
WINDOW = 512
EPS = 1e-6
SCALE = HEAD_DIM ** -0.5
FORCE_SCORE = 1e4
NEG_INF = -1e30

LANES = 128
CACHE_SLOTS = 4
ROW_W = CACHE_SLOTS * KV_DIM
CHUNK_W = CMP_STRIDE * ROW_W
CMP_IN = CMP_LEN * HEAD_DIM
CMP_HALF = CMP_STRIDE * HEAD_DIM
GATE_PAD = LANES
VMEM_LIMIT = 56 * 1024 * 1024

BF16 = jnp.bfloat16
F32 = jnp.float32


def _dot(a, b):
    return jnp.dot(a, b, preferred_element_type=F32)


def _dot_nt(a, b):
    return lax.dot_general(a, b, (((1,), (1,)), ((), ())), preferred_element_type=F32)


def _rms(x, g):
    return x * lax.rsqrt(jnp.mean(x * x, axis=-1, keepdims=True) + EPS) * g


def _params(n_axes):
    return pltpu.CompilerParams(
        dimension_semantics=("arbitrary",) * n_axes, vmem_limit_bytes=VMEM_LIMIT)


def _split_hi_lo(x):
    hi = x.astype(BF16)
    lo = (x - hi.astype(F32)).astype(BF16)
    return hi, lo


Q_OFF, KV_OFF, WKV_OFF = 0, ATTN_DIM, ATTN_DIM + 4 * KV_DIM


def _proj_kernel(x_ref, g_ref, w_ref, cos_ref, sa_ref, sb_ref, *rest, pool_dim, chunked):
    if chunked:
        q_ref, kv_ref, wkv_ref, kva_ref, gate_ref, u_ref, xc_ref, tmp_ref = rest
    else:
        q_ref, kv_ref, wkv_ref, kva_ref, gate_ref, u_ref = rest
    tm = x_ref.shape[0]
    u_off = WKV_OFF + 2 * KV_DIM
    gate_off = u_off + pool_dim
    h = _rms(x_ref[...], g_ref[...]).astype(BF16)
    cos, sa, sb = cos_ref[...], sa_ref[...], sb_ref[...]

    def rope(z):
        return (z * cos + pltpu.roll(z, LANES - ROT_HALF, 1) * sa
                + pltpu.roll(z, ROT_HALF, 1) * sb)

    zq = _dot(h, w_ref[:, Q_OFF:Q_OFF + ATTN_DIM])
    for hd in range(N_HEADS):
        sl = slice(hd * HEAD_DIM, (hd + 1) * HEAD_DIM)
        q_ref[:, sl] = rope(zq[:, sl]).astype(BF16)

    n_kv = CACHE_SLOTS * KV_HEADS
    zkv = _dot(h, w_ref[:, KV_OFF:KV_OFF + 4 * KV_DIM])
    for blk in range(n_kv):
        z = zkv[:, blk * HEAD_DIM:(blk + 1) * HEAD_DIM]
        if (blk // KV_HEADS) % 2 == 0:
            z = rope(z)
        kv_ref[pl.ds(blk, tm, stride=n_kv), :] = z
        if blk >= 2 * KV_HEADS:
            kva_ref[blk - 2 * KV_HEADS] = z.astype(BF16)
        elif chunked:
            tmp_ref[...] = z
            for r in range(CMP_STRIDE):
                xc_ref[blk, :, r * HEAD_DIM:(r + 1) * HEAD_DIM] = (
                    tmp_ref[pl.ds(r, tm // CMP_STRIDE, stride=CMP_STRIDE), :].astype(BF16))

    n_w = 2 * KV_HEADS
    zw = _dot(h, w_ref[:, WKV_OFF:WKV_OFF + 2 * KV_DIM])
    for blk in range(n_w):
        z = zw[:, blk * HEAD_DIM:(blk + 1) * HEAD_DIM]
        if blk < KV_HEADS:
            z = rope(z)
        wkv_ref[pl.ds(blk, tm, stride=n_w), :] = z
        kva_ref[2 * KV_HEADS + blk] = z.astype(BF16)

    u_ref[...] = _dot(h, w_ref[:, u_off:u_off + pool_dim])
    gl = _dot(h, w_ref[:, gate_off:gate_off + GATE_PAD])
    gate_ref[...] = 1.0 / (1.0 + jnp.exp(-gl))


def _project(x, gain, w_r, tables, table_index, tm, pool_dim, chunked):
    rows, d = x.shape
    n_proj = w_r.shape[1]
    n_kv = CACHE_SLOTS * KV_HEADS
    n_w = 2 * KV_HEADS
    row = lambda i: (i, 0)
    fixed = lambda i: (0, 0)
    tab_spec = pl.BlockSpec((tm, LANES), table_index)
    out_shape = [
        jax.ShapeDtypeStruct((rows, ATTN_DIM), BF16),
        jax.ShapeDtypeStruct((rows * n_kv, HEAD_DIM), F32),
        jax.ShapeDtypeStruct((rows * n_w, HEAD_DIM), F32),
        jax.ShapeDtypeStruct((n_w + n_kv // 2, rows, HEAD_DIM), BF16),
        jax.ShapeDtypeStruct((rows, GATE_PAD), F32),
        jax.ShapeDtypeStruct((rows, pool_dim), F32),
    ]
    out_specs = [
        pl.BlockSpec((tm, ATTN_DIM), row),
        pl.BlockSpec((tm * n_kv, HEAD_DIM), row),
        pl.BlockSpec((tm * n_w, HEAD_DIM), row),
        pl.BlockSpec((n_w + n_kv // 2, tm, HEAD_DIM), lambda i: (0, i, 0)),
        pl.BlockSpec((tm, GATE_PAD), row),
        pl.BlockSpec((tm, pool_dim), row),
    ]
    scratch = []
    if chunked:
        out_shape.append(jax.ShapeDtypeStruct((n_kv // 2, rows // CMP_STRIDE, CMP_HALF), BF16))
        out_specs.append(pl.BlockSpec((n_kv // 2, tm // CMP_STRIDE, CMP_HALF), lambda i: (0, i, 0)))
        scratch.append(pltpu.VMEM((tm, HEAD_DIM), F32))
    return pl.pallas_call(
        functools.partial(_proj_kernel, pool_dim=pool_dim, chunked=chunked),
        grid=(rows // tm,),
        in_specs=[
            pl.BlockSpec((tm, d), row),
            pl.BlockSpec((1, d), fixed),
            pl.BlockSpec((d, n_proj), fixed),
            tab_spec, tab_spec, tab_spec,
        ],
        out_specs=tuple(out_specs),
        out_shape=tuple(out_shape),
        scratch_shapes=scratch,
        compiler_params=_params(1),
        name="project",
    )(x, gain, w_r, *tables)


def _rope_tables(pos):
    inv = jnp.power(ROPE_THETA, -jnp.arange(ROT_HALF, dtype=F32) * (2.0 / ROT_DIM))
    ang = pos.astype(F32)[:, None] * inv[None, :]
    cos, sin = jnp.cos(ang), jnp.sin(ang)
    n = pos.shape[0]
    rest = LANES - ROT_DIM
    c = jnp.concatenate([cos, cos, jnp.ones((n, rest), F32)], axis=1)
    sa = jnp.concatenate([-sin, jnp.zeros((n, LANES - ROT_HALF), F32)], axis=1)
    sb = jnp.concatenate([jnp.zeros((n, ROT_HALF), F32), sin, jnp.zeros((n, rest), F32)], axis=1)
    return c, sa, sb


def _gelu_tanh(x):
    return 0.5 * x * (1.0 + jnp.tanh(0.7978845608028654 * (x + 0.044715 * (x * x * x))))


def _compress_first(x, wcat, pos8):
    ab = _dot(x, wcat)
    pa = _dot(pos8[:, :CMP_HALF], wcat)[0:1, :HEAD_DIM]
    pb = _dot(pos8[:, CMP_HALF:], wcat)[0:1, HEAD_DIM:]
    return ab, pa + pb


def _compress_hidden(ab, pos_term):
    rows = ab.shape[0]
    b_next = pltpu.roll(ab[:, HEAD_DIM:], rows - 1, 0)
    return _gelu_tanh(ab[:, :HEAD_DIM] + b_next + pos_term).astype(BF16)


def _compress_kernel(x_ref, wcat_ref, pos_ref, w2_ref, o_ref):
    g, _, nck, width = x_ref.shape
    x = x_ref[...].reshape(g * nck, width)
    hid = _compress_hidden(*_compress_first(x, wcat_ref[0], pos_ref[0]))
    o_ref[...] = _dot(hid, w2_ref[0]).astype(BF16).reshape(o_ref.shape)


def _compress_prompt(xc, wcat, pos8, w2, batch, seq):
    nck = seq // CMP_STRIDE
    x = xc.reshape(xc.shape[0], batch, nck, CMP_HALF)
    return pl.pallas_call(
        _compress_kernel,
        grid=(batch, 2),
        in_specs=[
            pl.BlockSpec((KV_HEADS, 1, nck, CMP_HALF), lambda b, s: (s, b, 0, 0)),
            pl.BlockSpec((1, CMP_HALF, 2 * HEAD_DIM), lambda b, s: (s, 0, 0)),
            pl.BlockSpec((1, 8, CMP_IN), lambda b, s: (s, 0, 0)),
            pl.BlockSpec((1, HEAD_DIM, HEAD_DIM), lambda b, s: (s, 0, 0)),
        ],
        out_specs=pl.BlockSpec((1, 1, KV_HEADS, nck, HEAD_DIM), lambda b, s: (b, s, 0, 0, 0)),
        out_shape=jax.ShapeDtypeStruct((batch, 2, KV_HEADS, nck, HEAD_DIM), BF16),
        compiler_params=_params(2),
        name="compress_prompt",
    )(x, wcat, pos8, w2)


def _topk_rows(score, jidx, topk):
    rank = jnp.zeros_like(score)
    for j in range(score.shape[0]):
        bj = score[j:j + 1, :]
        tie = jnp.where(jidx > j, 1.0, 0.0)
        rank = rank + jnp.where(bj > score, 1.0, jnp.where(bj == score, tie, 0.0))
    return jnp.where(rank < topk, 1.0, 0.0)


def _softmax_parts(s, ok):
    s = jnp.where(ok, s, NEG_INF)
    m = jnp.max(s, axis=-1, keepdims=True)
    e = jnp.where(ok, jnp.exp(s - m), 0.0)
    d = jnp.sum(e, axis=-1, keepdims=True)
    return e, jnp.where(d > 0, d, 1.0)


EXP2_SCALE = SCALE * 1.4426950408889634
HEAD_PAIRS = HPG // 2


def _attn_prompt_kernel(q_ref, kc_ref, vc_ref, ks_ref, vs_ref, kw_ref, vw_ref, gate_ref,
                        mt_ref, eye_ref, o_ref,
                        vct_ref, vst_ref, vwt_ref, sel_ref, gt_ref, acc_ref,
                        *, tq, tk, seq, topk):
    i = pl.program_id(1)
    t0 = i * tq
    n_sel = mt_ref.shape[0]
    pair_w = 2 * tq
    eye = eye_ref[...]
    groups = range(KV_HEADS)
    pairs = range(KV_HEADS * HEAD_PAIRS)
    group_of = [hp // HEAD_PAIRS for hp in pairs]

    @pl.when(i == 0)
    def _():
        for g in groups:
            vct_ref[g] = _dot_nt(eye, vc_ref[0, 0, g]).astype(BF16)
            vst_ref[g] = _dot_nt(eye, vs_ref[g]).astype(BF16)
            vwt_ref[g] = _dot_nt(eye, vw_ref[g]).astype(BF16)

    q_all = q_ref[...]
    q_pairs = [jnp.concatenate([q_all[:, (2 * hp) * HEAD_DIM:(2 * hp + 1) * HEAD_DIM],
                                q_all[:, (2 * hp + 1) * HEAD_DIM:(2 * hp + 2) * HEAD_DIM]], axis=0)
               for hp in pairs]

    def both_heads(x):
        return jnp.concatenate([x, x], axis=1)

    def tpos(n_keys):
        return t0 + lax.broadcasted_iota(jnp.int32, (n_keys, tq), 1)

    def kidx(n_keys):
        return lax.broadcasted_iota(jnp.int32, (n_keys, tq), 0)


    ncp = kc_ref.shape[3]
    wk = min(WINDOW + tq, seq)
    ws = pl.multiple_of(jnp.maximum(t0 + tq - wk, 0), tq)
    s_cmp = [_dot_nt(kc_ref[0, 0, group_of[hp]], q_pairs[hp]) for hp in pairs]
    s_win = [_dot_nt(kw_ref[group_of[hp], pl.ds(ws, wk), :], q_pairs[hp]) for hp in pairs]

    ok = both_heads(jnp.where(kidx(ncp) * CMP_STRIDE + (CMP_LEN - 1) <= tpos(ncp), 1.0, 0.0)) > 0.5
    p_cmp, p_sum = [], [None] * KV_HEADS
    for hp in pairs:
        s = jnp.where(ok, s_cmp[hp], NEG_INF)
        m = jnp.max(s, axis=0, keepdims=True)
        e = jnp.where(ok, jnp.exp2((s - m) * EXP2_SCALE), 0.0)
        d = jnp.sum(e, axis=0, keepdims=True)
        p = e / jnp.where(d > 0, d, 1.0)
        p_cmp.append(p.astype(BF16))
        ph = p[:, :tq] + p[:, tq:]
        g = group_of[hp]
        p_sum[g] = ph if p_sum[g] is None else p_sum[g] + ph

    mt = mt_ref[...]
    split = [_split_hi_lo(p_sum[g]) for g in groups]
    imp = [_dot(mt, split[g][0]) + _dot(mt, split[g][1]) for g in groups]
    o_cmp = [_dot(vct_ref[group_of[hp]], p_cmp[hp]) for hp in pairs]

    kpos = ws + kidx(wk)
    bias = both_heads(jnp.where(kpos <= tpos(wk),
                                jnp.where(kpos > tpos(wk) - WINDOW, 0.0, NEG_INF), NEG_INF))
    p_win, l_win = [], []
    for hp in pairs:
        s = s_win[hp] + bias
        p = jnp.exp2((s - jnp.max(s, axis=0, keepdims=True)) * EXP2_SCALE)
        l_win.append(jnp.sum(p, axis=0, keepdims=True))
        p_win.append(p.astype(BF16))
    o_win = [_dot(vwt_ref[group_of[hp], :, pl.ds(ws, wk)], p_win[hp]) for hp in pairs]
    o_win = [o_win[hp] / l_win[hp] for hp in pairs]

    jidx = kidx(n_sel)
    jt = tpos(n_sel) // SEL_BLOCK
    forced = jnp.where(jidx == 0, 1.0, jnp.where(jidx == jt, 1.0, jnp.where(jidx == jt - 1, 1.0, 0.0)))
    for g in groups:
        score = jnp.where(forced > 0.5, FORCE_SCORE, jnp.where(jidx <= jt, imp[g], -1.0))
        sel_ref[g] = _topk_rows(score, jidx, topk)

    acc_ref[...] = jnp.zeros_like(acc_ref)

    def sel_step(kb, carry):
        k0 = pl.multiple_of(kb * tk, tk)
        blk0 = kb * (tk // SEL_BLOCK)
        causal = k0 + kidx(tk) <= tpos(tk)
        bias = []
        for g in groups:
            chosen = jnp.concatenate(
                [jnp.broadcast_to(sel_ref[g, pl.ds(blk0 + j, 1), :], (SEL_BLOCK, tq))
                 for j in range(tk // SEL_BLOCK)], axis=0)
            bias.append(both_heads(
                jnp.where(causal, jnp.where(chosen > 0.5, 0.0, NEG_INF), NEG_INF)))
        ss = [_dot_nt(ks_ref[group_of[hp], pl.ds(k0, tk), :], q_pairs[hp]) + bias[group_of[hp]]
              for hp in pairs]
        out, ps, alphas = [], [], []
        for hp in pairs:
            m, l = carry[2 * hp], carry[2 * hp + 1]
            m_new = jnp.maximum(m, jnp.max(ss[hp], axis=0, keepdims=True))
            p = jnp.exp2((ss[hp] - m_new) * EXP2_SCALE)
            alpha = jnp.exp2((m - m_new) * EXP2_SCALE)
            out += [m_new, alpha * l + jnp.sum(p, axis=0, keepdims=True)]
            ps.append(p.astype(BF16))
            alphas.append(alpha)
        pvs = [_dot(vst_ref[group_of[hp], :, pl.ds(k0, tk)], ps[hp]) for hp in pairs]
        for hp in pairs:
            acc_ref[hp] = alphas[hp] * acc_ref[hp] + pvs[hp]
        return tuple(out)

    n_kb = (t0 + tq + tk - 1) // tk
    init = (jnp.full((1, pair_w), NEG_INF, F32), jnp.zeros((1, pair_w), F32)) * len(pairs)
    stats = lax.fori_loop(0, n_kb, sel_step, init)
    o_sel = [acc_ref[hp] / stats[2 * hp + 1] for hp in pairs]

    gt_ref[...] = gate_ref[...].T
    o_t = []
    for h in range(N_HEADS):
        hp, lanes = h // 2, slice((h % 2) * tq, (h % 2 + 1) * tq)
        col = h * N_BRANCH
        o_t.append((gt_ref[col:col + 1, :] * o_cmp[hp][:, lanes]
                    + gt_ref[col + 1:col + 2, :] * o_sel[hp][:, lanes]
                    + gt_ref[col + 2:col + 3, :] * o_win[hp][:, lanes]).astype(BF16))
    outs = [_dot_nt(eye, o_t[h]) for h in range(N_HEADS)]
    for h in range(N_HEADS):
        o_ref[:, h * HEAD_DIM:(h + 1) * HEAD_DIM] = outs[h].astype(BF16)


def _cmp_to_sel(n_cmp_pad, n_cmp, n_sel):
    cs = np.arange(n_cmp_pad)[:, None] * CMP_STRIDE
    ss = np.arange(n_sel)[None, :] * SEL_BLOCK
    hit = (cs < ss + SEL_BLOCK) & (cs + CMP_LEN > ss) & (np.arange(n_cmp_pad)[:, None] < n_cmp)
    return hit.astype(np.float32)


def _attend_prompt(q, cmp_kv, kva, gates, batch, seq):
    tq = min(128, seq)
    tk = min(512, seq)
    nq = seq // tq
    ncp = seq // CMP_STRIDE
    n_sel = -(-seq // SEL_BLOCK)
    topk = min(SEL_TOPK, n_sel)
    assert tq == HEAD_DIM and seq % tk == 0 and WINDOW % tq == 0
    mt = jnp.asarray(_cmp_to_sel(ncp, ncp - 1, n_sel).T, BF16)
    eye = jnp.asarray(np.eye(tq, dtype=np.float32), BF16)
    kv_spec = lambda slot: pl.BlockSpec((KV_HEADS, seq, HEAD_DIM), lambda b, i: (slot, b, 0))
    cmp_spec = lambda s: pl.BlockSpec(
        (1, 1, KV_HEADS, ncp, HEAD_DIM), lambda b, i: (b, s, 0, 0, 0))
    full = lambda a: pl.BlockSpec(a.shape, lambda b, i: (0, 0))
    return pl.pallas_call(
        functools.partial(_attn_prompt_kernel, tq=tq, tk=tk, seq=seq, topk=topk),
        grid=(batch, nq),
        in_specs=[
            pl.BlockSpec((tq, ATTN_DIM), lambda b, i: (b * nq + i, 0)),
            cmp_spec(0), cmp_spec(1),
            kv_spec(0), kv_spec(1), kv_spec(2), kv_spec(3),
            pl.BlockSpec((tq, GATE_PAD), lambda b, i: (b * nq + i, 0)),
            full(mt), full(eye),
        ],
        out_specs=pl.BlockSpec((tq, ATTN_DIM), lambda b, i: (b * nq + i, 0)),
        out_shape=jax.ShapeDtypeStruct((batch * seq, ATTN_DIM), BF16),
        scratch_shapes=[
            pltpu.VMEM((KV_HEADS, HEAD_DIM, ncp), BF16),
            pltpu.VMEM((KV_HEADS, HEAD_DIM, seq), BF16),
            pltpu.VMEM((KV_HEADS, HEAD_DIM, seq), BF16),
            pltpu.VMEM((KV_HEADS, n_sel, tq), F32),
            pltpu.VMEM((GATE_PAD, tq), F32),
            pltpu.VMEM((N_HEADS // 2, HEAD_DIM, 2 * tq), F32),
        ],
        compiler_params=_params(2),
        name="attend_prompt",
    )(q, cmp_kv, cmp_kv, kva, kva, kva, kva, gates, mt, eye)


def _attn_sample_kernel(pt_ref, cache_ref, q_ref, kvn_ref, wn_ref, win_ref, gate_ref,
                        wcat_ref, pos_ref, w2_ref, ms_ref, ek_ref, o_ref, buf, sem,
                        *, n_pages, page, past, topk, n_sel):
    b = pl.program_id(0)
    nb = pl.num_programs(0)
    n_kv = CACHE_SLOTS * KV_HEADS
    n_w = 2 * KV_HEADS
    cpp = page // CMP_STRIDE
    nch = past // CMP_STRIDE
    wb = win_ref.shape[1] // n_w

    def page_copy(seq_idx, slot, pi, r):
        src0 = pl.multiple_of(pt_ref[seq_idx, pi] * cpp, cpp)
        return pltpu.make_async_copy(
            cache_ref.at[pl.ds(src0, cpp), r],
            buf.at[slot, r, pl.ds(pl.multiple_of(pi * cpp, cpp), cpp)], sem.at[slot])

    def for_all_copies(seq_idx, slot, act):
        def per_page(pi, carry):
            for r in range(CMP_STRIDE):
                act(page_copy(seq_idx, slot, pi, r))
            return carry
        lax.fori_loop(0, n_pages, per_page, 0)

    @pl.when(b == 0)
    def _():
        for_all_copies(0, 0, lambda c: c.start())

    @pl.when(b + 1 < nb)
    def _():
        for_all_copies(b + 1, (b + 1) % 2, lambda c: c.start())

    slot = b % 2
    for_all_copies(b, slot, lambda c: c.wait())

    def chunk_rows(cache_slot, g, r):
        rows = buf.reshape(2, CMP_STRIDE, nch * n_kv, HEAD_DIM)
        return rows[slot, r, pl.ds(cache_slot * KV_HEADS + g, nch, stride=n_kv), :]

    row8 = lax.broadcasted_iota(jnp.int32, (N_HEADS, 1), 0)
    in_g0 = row8 < HPG

    def by_group(x0, x1):
        return jnp.where(in_g0, x0, x1)

    q8 = q_ref[0]
    qf = q8.astype(F32)
    groups = range(KV_HEADS)

    def scores(keys):
        return by_group(_dot_nt(q8, keys[0]), _dot_nt(q8, keys[1])) * SCALE

    def cached(cache_slot, g):
        return jnp.concatenate([chunk_rows(cache_slot, g, r).astype(BF16)
                                for r in range(CMP_STRIDE)], axis=0)

    def compress_input(cache_slot):
        return jnp.concatenate(
            [jnp.concatenate([chunk_rows(cache_slot, g, r).astype(BF16)
                              for r in range(CMP_STRIDE)], axis=1)
             for g in groups], axis=0)

    first = [_compress_first(compress_input(cs), wcat_ref[cs], pos_ref[cs]) for cs in range(2)]
    s_sel = scores([cached(2, g) for g in groups])
    win_keys = [win_ref[0, pl.ds(g, wb, stride=n_w), :].astype(BF16) for g in groups]
    s_win = scores(win_keys)
    kc, vc =[_dot(_compress_hidden(*first[cs]), w2_ref[cs]).astype(BF16) for cs in range(2)]

    lane_n = lax.broadcasted_iota(jnp.int32, (N_HEADS, nch), 1)
    s = by_group(_dot_nt(q8, kc[:nch]), _dot_nt(q8, kc[nch:])) * SCALE
    ok = lane_n * CMP_STRIDE + (CMP_LEN - 1) <= past
    e, d = _softmax_parts(s, ok)
    p = e / d
    pb = p.astype(BF16)
    o_cmp = by_group(_dot(pb, vc[:nch]), _dot(pb, vc[nch:]))

    p_g = [jnp.sum(p[g * HPG:(g + 1) * HPG], axis=0, keepdims=True) for g in range(KV_HEADS)]
    p2 = jnp.concatenate(p_g + [jnp.zeros((N_HEADS - KV_HEADS, nch), F32)], axis=0)
    p_hi, p_lo = _split_hi_lo(p2)
    imp = _dot(p_hi, ms_ref[...]) + _dot(p_lo, ms_ref[...])
    jl = lax.broadcasted_iota(jnp.int32, (N_HEADS, LANES), 1)
    jt = past // SEL_BLOCK
    forced = jnp.where(jl == 0, 1.0, jnp.where(jl == jt, 1.0, jnp.where(jl == jt - 1, 1.0, 0.0)))
    score = jnp.where(forced > 0.5, FORCE_SCORE, jnp.where(jl <= jt, imp, -1.0))
    score = jnp.where(jl < n_sel, score, -2.0)
    ii = lax.broadcasted_iota(jnp.int32, (LANES, LANES), 0)
    jj = lax.broadcasted_iota(jnp.int32, (LANES, LANES), 1)
    sel_rows = []
    for g in range(KV_HEADS):
        srow = jnp.broadcast_to(score[g:g + 1, :], (LANES, LANES))
        scol = jnp.sum(jnp.where(ii == jj, srow, 0.0), axis=1, keepdims=True)
        tie = jnp.where(ii < jj, 1.0, 0.0)
        beats = jnp.where(scol > srow, 1.0, jnp.where(scol == srow, tie, 0.0))
        rank = jnp.sum(beats, axis=0, keepdims=True)
        sel_rows.append(jnp.where(rank < topk, 1.0, 0.0))
    sel2 = jnp.concatenate(sel_rows + [jnp.zeros((N_HEADS - KV_HEADS, LANES), F32)], axis=0)
    chunk_ok2 = _dot(sel2.astype(BF16), ek_ref[...])
    chunk_ok = by_group(chunk_ok2[0:1], chunk_ok2[1:2])
    key_ok = jnp.concatenate([chunk_ok] * CMP_STRIDE, axis=1)
    new_ok = by_group(*[jnp.sum(jnp.where(jl[0:1] == jt, sel2[g:g + 1], 0.0), axis=1, keepdims=True)
                        for g in range(KV_HEADS)])

    def new_row(ref, idx0):
        x = by_group(ref[0, idx0:idx0 + 1, :], ref[0, idx0 + 1:idx0 + 2, :])
        return x.astype(BF16).astype(F32)

    def weights(s, ok, k_new, new_ok):
        s_new = jnp.sum(qf * k_new, axis=-1, keepdims=True) * SCALE
        s = jnp.where(ok, s, NEG_INF)
        s_new = jnp.where(new_ok, s_new, NEG_INF)
        m = jnp.maximum(jnp.max(s, axis=-1, keepdims=True), s_new)
        e = jnp.where(ok, jnp.exp(s - m), 0.0)
        e_new = jnp.where(new_ok, jnp.exp(s_new - m), 0.0)
        d = jnp.sum(e, axis=-1, keepdims=True) + e_new
        return e.astype(BF16), e_new, jnp.where(d > 0, d, 1.0)

    def weighted(eb, vals, e_new, v_new, d):
        return (by_group(_dot(eb, vals[0]), _dot(eb, vals[1])) + e_new * v_new) / d

    kpos = past - wb + lax.broadcasted_iota(jnp.int32, (N_HEADS, wb), 1)
    w_sel = weights(s_sel, key_ok > 0.5, new_row(kvn_ref, 2 * KV_HEADS), new_ok > 0.5)
    w_win = weights(s_win, kpos > past - WINDOW, new_row(wn_ref, 0), row8 >= 0)
    win_vals = [win_ref[0, pl.ds(KV_HEADS + g, wb, stride=n_w), :].astype(BF16) for g in groups]
    o_sel = weighted(w_sel[0], [cached(3, g) for g in groups], w_sel[1],
                     new_row(kvn_ref, 3 * KV_HEADS), w_sel[2])
    o_win = weighted(w_win[0], win_vals, w_win[1], new_row(wn_ref, KV_HEADS), w_win[2])

    gates = jnp.broadcast_to(gate_ref[0], (N_HEADS, GATE_PAD))
    lane = lax.broadcasted_iota(jnp.int32, (N_HEADS, GATE_PAD), 1)

    def gate(br):
        return jnp.sum(jnp.where(lane == row8 * N_BRANCH + br, gates, 0.0), axis=-1, keepdims=True)

    o_ref[0] = (gate(0) * o_cmp + gate(1) * o_sel + gate(2) * o_win).astype(BF16)


def _attend_sample(page_table, cache, q, kv_new, wkv_new, win_rows, gates, wcat, pos8, w2):
    nb, n_pages = page_table.shape
    n_phys, page = cache.shape[:2]
    n_kv = CACHE_SLOTS * KV_HEADS
    n_w = 2 * KV_HEADS
    past = n_pages * page
    nch = past // CMP_STRIDE
    wb = win_rows.shape[1] // n_w
    n_sel = -(-(past + 1) // SEL_BLOCK)
    n_cmp = (past + 1) // CMP_STRIDE - 1
    topk = min(SEL_TOPK, n_sel)
    assert n_sel <= LANES and nch * CMP_STRIDE == past
    ms = np.zeros((nch, LANES), np.float32)
    ms[:, :n_sel] = _cmp_to_sel(nch, n_cmp, n_sel)
    ek = (np.arange(nch)[None, :] * CMP_STRIDE // SEL_BLOCK == np.arange(LANES)[:, None])
    ms = jnp.asarray(ms, BF16)
    ek = jnp.asarray(ek.astype(np.float32), BF16)
    seq3 = lambda n: pl.BlockSpec((1, n, HEAD_DIM), lambda b, pt: (b, 0, 0))
    full3 = lambda a: pl.BlockSpec(a.shape, lambda b, pt: (0, 0, 0))
    full2 = lambda a: pl.BlockSpec(a.shape, lambda b, pt: (0, 0))
    grid_spec = pltpu.PrefetchScalarGridSpec(
        num_scalar_prefetch=1,
        grid=(nb,),
        in_specs=[
            pl.BlockSpec(memory_space=pl.ANY),
            seq3(N_HEADS), seq3(n_kv), seq3(n_w), seq3(wb * n_w),
            pl.BlockSpec((1, 1, GATE_PAD), lambda b, pt: (b, 0, 0)),
            full3(wcat), full3(pos8), full3(w2), full2(ms), full2(ek),
        ],
        out_specs=seq3(N_HEADS),
        scratch_shapes=[pltpu.VMEM((2, CMP_STRIDE, nch, n_kv, HEAD_DIM), F32),
                        pltpu.SemaphoreType.DMA((2,))],
    )
    return pl.pallas_call(
        functools.partial(_attn_sample_kernel, n_pages=n_pages, page=page, past=past,
                          topk=topk, n_sel=n_sel),
        grid_spec=grid_spec,
        out_shape=jax.ShapeDtypeStruct((nb, N_HEADS, HEAD_DIM), BF16),
        compiler_params=_params(1),
        name="attend_sample",
    )(page_table, cache.reshape(n_phys * page // CMP_STRIDE, CMP_STRIDE, n_kv, HEAD_DIM),
      q.reshape(nb, N_HEADS, HEAD_DIM), kv_new.reshape(nb, n_kv, HEAD_DIM),
      wkv_new.reshape(nb, n_w, HEAD_DIM), win_rows,
      gates.reshape(nb, 1, GATE_PAD), wcat, pos8, w2, ms, ek)


def _mix_tail(o, diffs, x, pw_ref, ps_ref, wo_ref, gpost_ref, gpre_ref, y1_ref, h2_ref):
    gw = diffs[0].shape[1]
    ys = [(_dot(diffs[g].astype(BF16), pw_ref[g]) * ps_ref[:, g * gw:(g + 1) * gw]).astype(BF16)
          for g in range(POOL_GROUPS)]
    cat = jnp.concatenate([o] + ys, axis=1)
    m = _dot(cat, wo_ref[...])
    y1 = x + _rms(m, gpost_ref[...])
    y1_ref[...] = y1
    h2_ref[...] = _rms(y1, gpre_ref[...]).astype(BF16)


def _mix_prompt_kernel(o_ref, u_ref, halo_ref, x_ref, pw_ref, ps_ref, wo_ref, gpost_ref, gpre_ref,
                       y1_ref, h2_ref, *, tm):
    i = pl.program_id(1)
    halo_rows = halo_ref.shape[0]
    halo = jnp.where(i > 0, halo_ref[...], 0.0)
    u = u_ref[...]
    uext = jnp.concatenate([halo, u], axis=0)
    n_ext = uext.shape[0]
    gw = u.shape[1] // POOL_GROUPS
    tpos = i * tm + lax.broadcasted_iota(jnp.int32, (tm, 1), 0)
    diffs = []
    for g, w in enumerate(POOL_WINDOWS):
        s = uext[:, g * gw:(g + 1) * gw]
        k = 1
        while k < w:
            s = s + pltpu.roll(s, k, 0)
            k *= 2
        cnt = jnp.minimum(w, tpos + 1).astype(F32)
        diffs.append(s[halo_rows:n_ext] / cnt - u[:, g * gw:(g + 1) * gw])
    _mix_tail(o_ref[...], diffs, x_ref[...], pw_ref, ps_ref, wo_ref, gpost_ref, gpre_ref,
              y1_ref, h2_ref)


def _mix_sample_kernel(o_ref, u_ref, st_ref, x_ref, pw_ref, ps_ref, wo_ref, gpost_ref, gpre_ref,
                       y1_ref, h2_ref, *, past):
    u = u_ref[...]
    c = u.shape[1]
    gw = c // POOL_GROUPS
    n_hist = st_ref.shape[1] // c
    diffs = []
    for g, w in enumerate(POOL_WINDOWS):
        un = u[:, g * gw:(g + 1) * gw]
        s = un
        for back in range(1, w):
            r = n_hist - back
            s = s + st_ref[:, r * c + g * gw:r * c + (g + 1) * gw]
        diffs.append(s / float(min(w, past + 1)) - un)
    _mix_tail(o_ref[...], diffs, x_ref[...], pw_ref, ps_ref, wo_ref, gpost_ref, gpre_ref,
              y1_ref, h2_ref)


def _mix_specs(tm, d, c, pool_w, idx):
    fixed2 = lambda *a: (0, 0)
    fixed3 = lambda *a: (0, 0, 0)
    weights = [
        pl.BlockSpec(pool_w.shape, fixed3),
        pl.BlockSpec((1, c), fixed2),
        pl.BlockSpec((d, d), fixed2),
        pl.BlockSpec((1, d), fixed2),
        pl.BlockSpec((1, d), fixed2),
    ]
    outs = (pl.BlockSpec((tm, d), idx), pl.BlockSpec((tm, d), idx))
    return weights, outs


def _mix_prompt(o, u, x, pool_w, pool_scale, w_o, g_post, g_pre, batch, seq):
    rows, d = x.shape
    c = u.shape[1]
    tm = min(512, seq)
    nt = seq // tm
    halo = 16
    assert halo >= POOL_BUF and seq % tm == 0 and tm % halo == 0
    idx = lambda b, i: (b * nt + i, 0)
    halo_idx = lambda b, i: (jnp.maximum((b * nt + i) * (tm // halo) - 1, 0), 0)
    weights, outs = _mix_specs(tm, d, c, pool_w, idx)
    return pl.pallas_call(
        functools.partial(_mix_prompt_kernel, tm=tm),
        grid=(batch, nt),
        in_specs=[pl.BlockSpec((tm, ATTN_DIM), idx), pl.BlockSpec((tm, c), idx),
                  pl.BlockSpec((halo, c), halo_idx), pl.BlockSpec((tm, d), idx)] + weights,
        out_specs=outs,
        out_shape=(jax.ShapeDtypeStruct((rows, d), F32), jax.ShapeDtypeStruct((rows, d), BF16)),
        compiler_params=_params(2),
        name="mix_prompt",
    )(o, u, u, x, pool_w, pool_scale, w_o, g_post, g_pre)


def _mix_sample(o, u, pool_state, x, pool_w, pool_scale, w_o, g_post, g_pre, past):
    rows, d = x.shape
    c = u.shape[1]
    tm = rows
    idx = lambda i: (i, 0)
    weights, outs = _mix_specs(tm, d, c, pool_w, idx)
    st = pool_state.reshape(rows, -1)
    return pl.pallas_call(
        functools.partial(_mix_sample_kernel, past=past),
        grid=(rows // tm,),
        in_specs=[pl.BlockSpec((tm, ATTN_DIM), idx), pl.BlockSpec((tm, c), idx),
                  pl.BlockSpec((tm, st.shape[1]), idx), pl.BlockSpec((tm, d), idx)] + weights,
        out_specs=outs,
        out_shape=(jax.ShapeDtypeStruct((rows, d), F32), jax.ShapeDtypeStruct((rows, d), BF16)),
        compiler_params=_params(1),
        name="mix_sample",
    )(o, u, st, x, pool_w, pool_scale, w_o, g_post, g_pre)


def _mlp_kernel(h_ref, wu_ref, wd_ref, y1_ref, g_ref, *rest, shift):
    if shift:
        state_ref, fresh_ref, y_ref, rolled_ref, acc_ref = rest
        n_new = fresh_ref.shape[1]
        n_keep = state_ref.shape[1] - n_new
        rolled_ref[:, pl.ds(0, n_keep), :] = state_ref[:, pl.ds(n_new, n_keep), :]
        rolled_ref[:, pl.ds(n_keep, n_new), :] = fresh_ref[...]
    else:
        y_ref, acc_ref = rest
    j = pl.program_id(1)

    @pl.when(j == 0)
    def _():
        acc_ref[...] = jnp.zeros_like(acc_ref)

    a = jnp.maximum(_dot(h_ref[...], wu_ref[...]), 0.0)
    acc_ref[...] += _dot((a * a).astype(BF16), wd_ref[...])

    @pl.when(j == pl.num_programs(1) - 1)
    def _():
        y_ref[...] = y1_ref[...] + _rms(acc_ref[...], g_ref[...])


def _mlp(h2, y1, w_up, w_down, gain, tm, state=None, fresh=None):
    rows, d = y1.shape
    ff = w_up.shape[1]
    tf = min(1024, ff)
    n_i, n_j = rows // tm, ff // tf
    row = lambda i, j: (i, 0)
    shift = state is not None
    y_spec = pl.BlockSpec((tm, d), row)
    y_shape = jax.ShapeDtypeStruct((rows, d), F32)
    extra_in, out_specs, out_shape = [], y_spec, y_shape
    if shift:
        per_step = state.shape[0] // (n_i * n_j)
        assert per_step * n_i * n_j == state.shape[0]
        blk = lambda a: pl.BlockSpec((per_step,) + a.shape[1:], lambda i, j: (i * n_j + j, 0, 0))
        extra_in = [blk(state), blk(fresh)]
        out_specs = (y_spec, blk(state))
        out_shape = (y_shape, jax.ShapeDtypeStruct(state.shape, state.dtype))
    return pl.pallas_call(
        functools.partial(_mlp_kernel, shift=shift),
        grid=(n_i, n_j),
        in_specs=[
            pl.BlockSpec((tm, d), row),
            pl.BlockSpec((d, tf), lambda i, j: (0, j)),
            pl.BlockSpec((tf, d), lambda i, j: (j, 0)),
            pl.BlockSpec((tm, d), row),
            pl.BlockSpec((1, d), lambda i, j: (0, 0)),
        ] + extra_in,
        out_specs=out_specs,
        out_shape=out_shape,
        scratch_shapes=[pltpu.VMEM((tm, d), F32)],
        compiler_params=_params(2),
        name="mlp",
    )(h2, w_up, w_down, y1, gain, *((state, fresh) if shift else ()))


def _layer_weights(w_in, cmp_pos_k, cmp_w1_k, cmp_w2_k, cmp_pos_v, cmp_w1_v, cmp_w2_v,
                   pool_w, w_o, w_up, w_down, pool_dim):
    gate_lo = WKV_OFF + 2 * KV_DIM
    gate_hi = gate_lo + N_BRANCH * N_HEADS
    w_r = jnp.concatenate(
        [w_in[:, :gate_lo], w_in[:, gate_hi:gate_hi + pool_dim],
         jnp.pad(w_in[:, gate_lo:gate_hi], ((0, 0), (0, GATE_PAD - N_BRANCH * N_HEADS)))],
        axis=1).astype(BF16)
    wcat = jnp.stack([jnp.concatenate([w1[:CMP_HALF], w1[CMP_HALF:]], axis=1)
                      for w1 in (cmp_w1_k, cmp_w1_v)]).astype(BF16)
    pos8 = jnp.stack([jnp.pad(p.reshape(1, CMP_IN), ((0, 7), (0, 0)))
                      for p in (cmp_pos_k, cmp_pos_v)]).astype(BF16)
    w2 = jnp.stack([cmp_w2_k, cmp_w2_v]).astype(BF16)
    return (w_r, wcat, pos8, w2, pool_w.astype(BF16), w_o.astype(BF16),
            w_up.astype(BF16), w_down.astype(BF16))


def kernel(x_prompt, x_sample, cache_kv, state_win_kv, state_pool, page_table, norm_mix_pre, w_in,
           cmp_pos_k, cmp_w1_k, cmp_w2_k, cmp_pos_v, cmp_w1_v, cmp_w2_v, pool_w, pool_scale, w_o,
           norm_mix_post, norm_mlp_pre, w_up, w_down, norm_mlp_post):
    batch, seq, d = x_prompt.shape
    nb, dec_seq, _ = x_sample.shape
    depth = w_in.shape[0]
    pool_dim = d - ATTN_DIM
    n_pages = page_table.shape[1]
    page = cache_kv.shape[2]
    past = n_pages * page
    wb = state_win_kv.shape[2]
    assert dec_seq == 1 and seq >= POOL_BUF and seq % CMP_STRIDE == 0
    assert w_in.shape[2] == ATTN_DIM + 6 * KV_DIM + N_BRANCH * N_HEADS + pool_dim

    tm_p = min(512, seq)
    tabs_p = _rope_tables(jnp.arange(seq, dtype=jnp.int32))
    tabs_s = _rope_tables(jnp.full((nb,), past, jnp.int32))
    nt_p = seq // tm_p

    y_p = x_prompt.reshape(batch * seq, d)
    y_s = x_sample.reshape(nb, d)
    kv_p, kv_s, win_p, win_s, pool_p, pool_s = [], [], [], [], [], []
    row_vec = lambda v: v.reshape(1, -1)
    for l in range(depth):
        w_r, wcat, pos8, w2, pw, wo, wu, wd = _layer_weights(
            w_in[l], cmp_pos_k[l], cmp_w1_k[l], cmp_w2_k[l], cmp_pos_v[l], cmp_w1_v[l],
            cmp_w2_v[l], pool_w[l], w_o[l], w_up[l], w_down[l], pool_dim)
        g_pre, g_post = row_vec(norm_mix_pre[l]), row_vec(norm_mix_post[l])
        g_mlp_pre, g_mlp_post = row_vec(norm_mlp_pre[l]), row_vec(norm_mlp_post[l])
        ps = row_vec(pool_scale[l])

        n_w = 2 * KV_HEADS
        win_rows = state_win_kv[l].reshape(nb, wb * n_w, HEAD_DIM)

        q_s, kv_s1, wkv_s, _, gates_s, u_s = _project(
            y_s, g_pre, w_r, tabs_s, lambda i: (i, 0), nb, pool_dim, False)

        q, kv, wkv, kva, gates, u, xc = _project(
            y_p, g_pre, w_r, tabs_p, lambda i: (i % nt_p, 0), tm_p, pool_dim, True)
        cmp_kv = _compress_prompt(xc, wcat, pos8, w2, batch, seq)
        o = _attend_prompt(q, cmp_kv, kva, gates, batch, seq)
        y1, h2 = _mix_prompt(o, u, y_p, pw, ps, wo, g_post, g_mlp_pre, batch, seq)
        y_p, win_rolled = _mlp(h2, y1, wu, wd, g_mlp_post, min(512, batch * seq),
                               win_rows, wkv_s.reshape(nb, n_w, HEAD_DIM))
        kv_p.append(kv.reshape(batch, seq, CACHE_SLOTS, KV_HEADS, HEAD_DIM))
        wp = min(WINDOW, seq)
        win_p.append(wkv.reshape(batch, seq, 2, KV_HEADS, HEAD_DIM)[:, seq - wp:])
        pool_p.append(u.reshape(batch, seq, pool_dim)[:, seq - POOL_BUF:])

        o = _attend_sample(page_table, cache_kv[l], q_s, kv_s1, wkv_s, win_rows, gates_s,
                           wcat, pos8, w2)
        y1, h2 = _mix_sample(o.reshape(nb, ATTN_DIM), u_s, state_pool[l], y_s, pw, ps, wo,
                             g_post, g_mlp_pre, past)
        y_s = _mlp(h2, y1, wu, wd, g_mlp_post, nb)
        kv_s.append(kv_s1.reshape(nb, 1, CACHE_SLOTS, KV_HEADS, HEAD_DIM))
        win_s.append(win_rolled.reshape(nb, wb, 2, KV_HEADS, HEAD_DIM))
        pool_s.append(jnp.concatenate([state_pool[l], u_s[:, None]], axis=1)[:, 1:])

    return (y_p.reshape(batch, seq, d), y_s.reshape(nb, 1, d),
            jnp.stack(kv_p), jnp.stack(kv_s), jnp.stack(win_p), jnp.stack(win_s),
            jnp.stack(pool_p), jnp.stack(pool_s))
```

```python
import functools

import numpy as np
import jax
import jax.numpy as jnp
from jax import lax
from jax.experimental import pallas as pl
from jax.experimental.pallas import tpu as pltpu

N_HEADS = 8
HEAD_DIM = 128
KV_HEADS = 2
HPG = N_HEADS // KV_HEADS
ATTN_DIM = N_HEADS * HEAD_DIM
KV_DIM = KV_HEADS * HEAD_DIM
N_BRANCH = 3
POOL_WINDOWS = (2, 4, 8, 16)
POOL_GROUPS = len(POOL_WINDOWS)
POOL_BUF = max(POOL_WINDOWS) - 1
ROT_DIM = HEAD_DIM // 4
ROT_HALF = ROT_DIM // 2
ROPE_THETA = 500000.0
CMP_LEN = 32
CMP_STRIDE = 16
SEL_BLOCK = 64
SEL_TOPK = 16
WINDOW = 512
EPS = 1e-6
SCALE = HEAD_DIM ** -0.5
FORCE_SCORE = 1e4
NEG_INF = -1e30

LANES = 128
CACHE_SLOTS = 4
ROW_W = CACHE_SLOTS * KV_DIM
CHUNK_W = CMP_STRIDE * ROW_W
CMP_IN = CMP_LEN * HEAD_DIM
CMP_HALF = CMP_STRIDE * HEAD_DIM
GATE_PAD = LANES
VMEM_LIMIT = 56 * 1024 * 1024

BF16 = jnp.bfloat16
F32 = jnp.float32


def _dot(a, b):
    return jnp.dot(a, b, preferred_element_type=F32)


def _dot_nt(a, b):
    return lax.dot_general(a, b, (((1,), (1,)), ((), ())), preferred_element_type=F32)


def _rms(x, g):
    return x * lax.rsqrt(jnp.mean(x * x, axis=-1, keepdims=True) + EPS) * g


def _params(n_axes):
    return pltpu.CompilerParams(
        dimension_semantics=("arbitrary",) * n_axes, vmem_limit_bytes=VMEM_LIMIT)


def _split_hi_lo(x):
    hi = x.astype(BF16)
    lo = (x - hi.astype(F32)).astype(BF16)
    return hi, lo


Q_OFF, KV_OFF, WKV_OFF = 0, ATTN_DIM, ATTN_DIM + 4 * KV_DIM


def _proj_kernel(x_ref, g_ref, w_ref, cos_ref, sa_ref, sb_ref, *rest, pool_dim, chunked):
    if chunked:
        q_ref, kv_ref, wkv_ref, kva_ref, gate_ref, u_ref, xc_ref, tmp_ref = rest
    else:
        q_ref, kv_ref, wkv_ref, kva_ref, gate_ref, u_ref = rest
    tm = x_ref.shape[0]
    u_off = WKV_OFF + 2 * KV_DIM
    gate_off = u_off + pool_dim
    h = _rms(x_ref[...], g_ref[...]).astype(BF16)
    cos, sa, sb = cos_ref[...], sa_ref[...], sb_ref[...]

    def rope(z):
        return (z * cos + pltpu.roll(z, LANES - ROT_HALF, 1) * sa
                + pltpu.roll(z, ROT_HALF, 1) * sb)

    zq = _dot(h, w_ref[:, Q_OFF:Q_OFF + ATTN_DIM])
    for hd in range(N_HEADS):
        sl = slice(hd * HEAD_DIM, (hd + 1) * HEAD_DIM)
        q_ref[:, sl] = rope(zq[:, sl]).astype(BF16)

    n_kv = CACHE_SLOTS * KV_HEADS
    zkv = _dot(h, w_ref[:, KV_OFF:KV_OFF + 4 * KV_DIM])
    for blk in range(n_kv):
        z = zkv[:, blk * HEAD_DIM:(blk + 1) * HEAD_DIM]
        if (blk // KV_HEADS) % 2 == 0:
            z = rope(z)
        kv_ref[pl.ds(blk, tm, stride=n_kv), :] = z
        if blk >= 2 * KV_HEADS:
            kva_ref[blk - 2 * KV_HEADS] = z.astype(BF16)
        elif chunked:
            tmp_ref[...] = z
            for r in range(CMP_STRIDE):
                xc_ref[blk, :, r * HEAD_DIM:(r + 1) * HEAD_DIM] = (
                    tmp_ref[pl.ds(r, tm // CMP_STRIDE, stride=CMP_STRIDE), :].astype(BF16))

    n_w = 2 * KV_HEADS
    zw = _dot(h, w_ref[:, WKV_OFF:WKV_OFF + 2 * KV_DIM])
    for blk in range(n_w):
        z = zw[:, blk * HEAD_DIM:(blk + 1) * HEAD_DIM]
        if blk < KV_HEADS:
            z = rope(z)
        wkv_ref[pl.ds(blk, tm, stride=n_w), :] = z
        kva_ref[2 * KV_HEADS + blk] = z.astype(BF16)

    u_ref[...] = _dot(h, w_ref[:, u_off:u_off + pool_dim])
    gl = _dot(h, w_ref[:, gate_off:gate_off + GATE_PAD])
    gate_ref[...] = 1.0 / (1.0 + jnp.exp(-gl))


def _project(x, gain, w_r, tables, table_index, tm, pool_dim, chunked):
    rows, d = x.shape
    n_proj = w_r.shape[1]
    n_kv = CACHE_SLOTS * KV_HEADS
    n_w = 2 * KV_HEADS
    row = lambda i: (i, 0)
    fixed = lambda i: (0, 0)
    tab_spec = pl.BlockSpec((tm, LANES), table_index)
    out_shape = [
        jax.ShapeDtypeStruct((rows, ATTN_DIM), BF16),
        jax.ShapeDtypeStruct((rows * n_kv, HEAD_DIM), F32),
        jax.ShapeDtypeStruct((rows * n_w, HEAD_DIM), F32),
        jax.ShapeDtypeStruct((n_w + n_kv // 2, rows, HEAD_DIM), BF16),
        jax.ShapeDtypeStruct((rows, GATE_PAD), F32),
        jax.ShapeDtypeStruct((rows, pool_dim), F32),
    ]
    out_specs = [
        pl.BlockSpec((tm, ATTN_DIM), row),
        pl.BlockSpec((tm * n_kv, HEAD_DIM), row),
        pl.BlockSpec((tm * n_w, HEAD_DIM), row),
        pl.BlockSpec((n_w + n_kv // 2, tm, HEAD_DIM), lambda i: (0, i, 0)),
        pl.BlockSpec((tm, GATE_PAD), row),
        pl.BlockSpec((tm, pool_dim), row),
    ]
    scratch = []
    if chunked:
        out_shape.append(jax.ShapeDtypeStruct((n_kv // 2, rows // CMP_STRIDE, CMP_HALF), BF16))
        out_specs.append(pl.BlockSpec((n_kv // 2, tm // CMP_STRIDE, CMP_HALF), lambda i: (0, i, 0)))
        scratch.append(pltpu.VMEM((tm, HEAD_DIM), F32))
    return pl.pallas_call(
        functools.partial(_proj_kernel, pool_dim=pool_dim, chunked=chunked),
        grid=(rows // tm,),
        in_specs=[
            pl.BlockSpec((tm, d), row),
            pl.BlockSpec((1, d), fixed),
            pl.BlockSpec((d, n_proj), fixed),
            tab_spec, tab_spec, tab_spec,
        ],
        out_specs=tuple(out_specs),
        out_shape=tuple(out_shape),
        scratch_shapes=scratch,
        compiler_params=_params(1),
        name="project",
    )(x, gain, w_r, *tables)


def _rope_tables(pos):
    inv = jnp.power(ROPE_THETA, -jnp.arange(ROT_HALF, dtype=F32) * (2.0 / ROT_DIM))
    ang = pos.astype(F32)[:, None] * inv[None, :]
    cos, sin = jnp.cos(ang), jnp.sin(ang)
    n = pos.shape[0]
    rest = LANES - ROT_DIM
    c = jnp.concatenate([cos, cos, jnp.ones((n, rest), F32)], axis=1)
    sa = jnp.concatenate([-sin, jnp.zeros((n, LANES - ROT_HALF), F32)], axis=1)
    sb = jnp.concatenate([jnp.zeros((n, ROT_HALF), F32), sin, jnp.zeros((n, rest), F32)], axis=1)
    return c, sa, sb


def _gelu_tanh(x):
    return 0.5 * x * (1.0 + jnp.tanh(0.7978845608028654 * (x + 0.044715 * (x * x * x))))


def _compress_first(x, wcat, pos8):
    ab = _dot(x, wcat)
    pa = _dot(pos8[:, :CMP_HALF], wcat)[0:1, :HEAD_DIM]
    pb = _dot(pos8[:, CMP_HALF:], wcat)[0:1, HEAD_DIM:]
    return ab, pa + pb


def _compress_hidden(ab, pos_term):
    rows = ab.shape[0]
    b_next = pltpu.roll(ab[:, HEAD_DIM:], rows - 1, 0)
    return _gelu_tanh(ab[:, :HEAD_DIM] + b_next + pos_term).astype(BF16)


def _compress_kernel(x_ref, wcat_ref, pos_ref, w2_ref, o_ref):
    g, _, nck, width = x_ref.shape
    x = x_ref[...].reshape(g * nck, width)
    hid = _compress_hidden(*_compress_first(x, wcat_ref[0], pos_ref[0]))
    o_ref[...] = _dot(hid, w2_ref[0]).astype(BF16).reshape(o_ref.shape)


def _compress_prompt(xc, wcat, pos8, w2, batch, seq):
    nck = seq // CMP_STRIDE
    x = xc.reshape(xc.shape[0], batch, nck, CMP_HALF)
    return pl.pallas_call(
        _compress_kernel,
        grid=(batch, 2),
        in_specs=[
            pl.BlockSpec((KV_HEADS, 1, nck, CMP_HALF), lambda b, s: (s, b, 0, 0)),
            pl.BlockSpec((1, CMP_HALF, 2 * HEAD_DIM), lambda b, s: (s, 0, 0)),
            pl.BlockSpec((1, 8, CMP_IN), lambda b, s: (s, 0, 0)),
            pl.BlockSpec((1, HEAD_DIM, HEAD_DIM), lambda b, s: (s, 0, 0)),
        ],
        out_specs=pl.BlockSpec((1, 1, KV_HEADS, nck, HEAD_DIM), lambda b, s: (b, s, 0, 0, 0)),
        out_shape=jax.ShapeDtypeStruct((batch, 2, KV_HEADS, nck, HEAD_DIM), BF16),
        compiler_params=_params(2),
        name="compress_prompt",
    )(x, wcat, pos8, w2)


def _topk_rows(score, jidx, topk):
    rank = jnp.zeros_like(score)
    for j in range(score.shape[0]):
        bj = score[j:j + 1, :]
        tie = jnp.where(jidx > j, 1.0, 0.0)
        rank = rank + jnp.where(bj > score, 1.0, jnp.where(bj == score, tie, 0.0))
    return jnp.where(rank < topk, 1.0, 0.0)


def _softmax_parts(s, ok):
    s = jnp.where(ok, s, NEG_INF)
    m = jnp.max(s, axis=-1, keepdims=True)
    e = jnp.where(ok, jnp.exp(s - m), 0.0)
    d = jnp.sum(e, axis=-1, keepdims=True)
    return e, jnp.where(d > 0, d, 1.0)


EXP2_SCALE = SCALE * 1.4426950408889634
HEAD_PAIRS = HPG // 2


def _attn_prompt_kernel(q_ref, kc_ref, vc_ref, ks_ref, vs_ref, kw_ref, vw_ref, gate_ref,
                        mt_ref, eye_ref, o_ref,
                        vct_ref, vst_ref, vwt_ref, sel_ref, gt_ref, acc_ref,
                        *, tq, tk, seq, topk):
    i = pl.program_id(1)
    t0 = i * tq
    n_sel = mt_ref.shape[0]
    pair_w = 2 * tq
    eye = eye_ref[...]
    groups = range(KV_HEADS)
    pairs = range(KV_HEADS * HEAD_PAIRS)
    group_of = [hp // HEAD_PAIRS for hp in pairs]

    @pl.when(i == 0)
    def _():
        for g in groups:
            vct_ref[g] = _dot_nt(eye, vc_ref[0, 0, g]).astype(BF16)
            vst_ref[g] = _dot_nt(eye, vs_ref[g]).astype(BF16)
            vwt_ref[g] = _dot_nt(eye, vw_ref[g]).astype(BF16)

    q_all = q_ref[...]
    q_pairs = [jnp.concatenate([q_all[:, (2 * hp) * HEAD_DIM:(2 * hp + 1) * HEAD_DIM],
                                q_all[:, (2 * hp + 1) * HEAD_DIM:(2 * hp + 2) * HEAD_DIM]], axis=0)
               for hp in pairs]

    def both_heads(x):
        return jnp.concatenate([x, x], axis=1)

    def tpos(n_keys):
        return t0 + lax.broadcasted_iota(jnp.int32, (n_keys, tq), 1)

    def kidx(n_keys):
        return lax.broadcasted_iota(jnp.int32, (n_keys, tq), 0)


    ncp = kc_ref.shape[3]
    wk = min(WINDOW + tq, seq)
    ws = pl.multiple_of(jnp.maximum(t0 + tq - wk, 0), tq)
    s_cmp = [_dot_nt(kc_ref[0, 0, group_of[hp]], q_pairs[hp]) for hp in pairs]
    s_win = [_dot_nt(kw_ref[group_of[hp], pl.ds(ws, wk), :], q_pairs[hp]) for hp in pairs]

    ok = both_heads(jnp.where(kidx(ncp) * CMP_STRIDE + (CMP_LEN - 1) <= tpos(ncp), 1.0, 0.0)) > 0.5
    p_cmp, p_sum = [], [None] * KV_HEADS
    for hp in pairs:
        s = jnp.where(ok, s_cmp[hp], NEG_INF)
        m = jnp.max(s, axis=0, keepdims=True)
        e = jnp.where(ok, jnp.exp2((s - m) * EXP2_SCALE), 0.0)
        d = jnp.sum(e, axis=0, keepdims=True)
        p = e / jnp.where(d > 0, d, 1.0)
        p_cmp.append(p.astype(BF16))
        ph = p[:, :tq] + p[:, tq:]
        g = group_of[hp]
        p_sum[g] = ph if p_sum[g] is None else p_sum[g] + ph

    mt = mt_ref[...]
    split = [_split_hi_lo(p_sum[g]) for g in groups]
    imp = [_dot(mt, split[g][0]) + _dot(mt, split[g][1]) for g in groups]
    o_cmp = [_dot(vct_ref[group_of[hp]], p_cmp[hp]) for hp in pairs]

    kpos = ws + kidx(wk)
    bias = both_heads(jnp.where(kpos <= tpos(wk),
                                jnp.where(kpos > tpos(wk) - WINDOW, 0.0, NEG_INF), NEG_INF))
    p_win, l_win = [], []
    for hp in pairs:
        s = s_win[hp] + bias
        p = jnp.exp2((s - jnp.max(s, axis=0, keepdims=True)) * EXP2_SCALE)
        l_win.append(jnp.sum(p, axis=0, keepdims=True))
        p_win.append(p.astype(BF16))
    o_win = [_dot(vwt_ref[group_of[hp], :, pl.ds(ws, wk)], p_win[hp]) for hp in pairs]
    o_win = [o_win[hp] / l_win[hp] for hp in pairs]

    jidx = kidx(n_sel)
    jt = tpos(n_sel) // SEL_BLOCK
    forced = jnp.where(jidx == 0, 1.0, jnp.where(jidx == jt, 1.0, jnp.where(jidx == jt - 1, 1.0, 0.0)))
    for g in groups:
        score = jnp.where(forced > 0.5, FORCE_SCORE, jnp.where(jidx <= jt, imp[g], -1.0))
        sel_ref[g] = _topk_rows(score, jidx, topk)

    acc_ref[...] = jnp.zeros_like(acc_ref)

    def sel_step(kb, carry):
        k0 = pl.multiple_of(kb * tk, tk)
        blk0 = kb * (tk // SEL_BLOCK)
        causal = k0 + kidx(tk) <= tpos(tk)
        bias = []
        for g in groups:
            chosen = jnp.concatenate(
                [jnp.broadcast_to(sel_ref[g, pl.ds(blk0 + j, 1), :], (SEL_BLOCK, tq))
                 for j in range(tk // SEL_BLOCK)], axis=0)
            bias.append(both_heads(
                jnp.where(causal, jnp.where(chosen > 0.5, 0.0, NEG_INF), NEG_INF)))
        ss = [_dot_nt(ks_ref[group_of[hp], pl.ds(k0, tk), :], q_pairs[hp]) + bias[group_of[hp]]
              for hp in pairs]
        out, ps, alphas = [], [], []
        for hp in pairs:
            m, l = carry[2 * hp], carry[2 * hp + 1]
            m_new = jnp.maximum(m, jnp.max(ss[hp], axis=0, keepdims=True))
            p = jnp.exp2((ss[hp] - m_new) * EXP2_SCALE)
            alpha = jnp.exp2((m - m_new) * EXP2_SCALE)
            out += [m_new, alpha * l + jnp.sum(p, axis=0, keepdims=True)]
            ps.append(p.astype(BF16))
            alphas.append(alpha)
        pvs = [_dot(vst_ref[group_of[hp], :, pl.ds(k0, tk)], ps[hp]) for hp in pairs]
        for hp in pairs:
            acc_ref[hp] = alphas[hp] * acc_ref[hp] + pvs[hp]
        return tuple(out)

    n_kb = (t0 + tq + tk - 1) // tk
    init = (jnp.full((1, pair_w), NEG_INF, F32), jnp.zeros((1, pair_w), F32)) * len(pairs)
    stats = lax.fori_loop(0, n_kb, sel_step, init)
    o_sel = [acc_ref[hp] / stats[2 * hp + 1] for hp in pairs]

    gt_ref[...] = gate_ref[...].T
    o_t = []
    for h in range(N_HEADS):
        hp, lanes = h // 2, slice((h % 2) * tq, (h % 2 + 1) * tq)
        col = h * N_BRANCH
        o_t.append((gt_ref[col:col + 1, :] * o_cmp[hp][:, lanes]
                    + gt_ref[col + 1:col + 2, :] * o_sel[hp][:, lanes]
                    + gt_ref[col + 2:col + 3, :] * o_win[hp][:, lanes]).astype(BF16))
    outs = [_dot_nt(eye, o_t[h]) for h in range(N_HEADS)]
    for h in range(N_HEADS):
        o_ref[:, h * HEAD_DIM:(h + 1) * HEAD_DIM] = outs[h].astype(BF16)


def _cmp_to_sel(n_cmp_pad, n_cmp, n_sel):
    cs = np.arange(n_cmp_pad)[:, None] * CMP_STRIDE
    ss = np.arange(n_sel)[None, :] * SEL_BLOCK
    hit = (cs < ss + SEL_BLOCK) & (cs + CMP_LEN > ss) & (np.arange(n_cmp_pad)[:, None] < n_cmp)
    return hit.astype(np.float32)


def _attend_prompt(q, cmp_kv, kva, gates, batch, seq):
    tq = min(128, seq)
    tk = min(512, seq)
    nq = seq // tq
    ncp = seq // CMP_STRIDE
    n_sel = -(-seq // SEL_BLOCK)
    topk = min(SEL_TOPK, n_sel)
    assert tq == HEAD_DIM and seq % tk == 0 and WINDOW % tq == 0
    mt = jnp.asarray(_cmp_to_sel(ncp, ncp - 1, n_sel).T, BF16)
    eye = jnp.asarray(np.eye(tq, dtype=np.float32), BF16)
    kv_spec = lambda slot: pl.BlockSpec((KV_HEADS, seq, HEAD_DIM), lambda b, i: (slot, b, 0))
    cmp_spec = lambda s: pl.BlockSpec(
        (1, 1, KV_HEADS, ncp, HEAD_DIM), lambda b, i: (b, s, 0, 0, 0))
    full = lambda a: pl.BlockSpec(a.shape, lambda b, i: (0, 0))
    return pl.pallas_call(
        functools.partial(_attn_prompt_kernel, tq=tq, tk=tk, seq=seq, topk=topk),
        grid=(batch, nq),
        in_specs=[
            pl.BlockSpec((tq, ATTN_DIM), lambda b, i: (b * nq + i, 0)),
            cmp_spec(0), cmp_spec(1),
            kv_spec(0), kv_spec(1), kv_spec(2), kv_spec(3),
            pl.BlockSpec((tq, GATE_PAD), lambda b, i: (b * nq + i, 0)),
            full(mt), full(eye),
        ],
        out_specs=pl.BlockSpec((tq, ATTN_DIM), lambda b, i: (b * nq + i, 0)),
        out_shape=jax.ShapeDtypeStruct((batch * seq, ATTN_DIM), BF16),
        scratch_shapes=[
            pltpu.VMEM((KV_HEADS, HEAD_DIM, ncp), BF16),
            pltpu.VMEM((KV_HEADS, HEAD_DIM, seq), BF16),
            pltpu.VMEM((KV_HEADS, HEAD_DIM, seq), BF16),
            pltpu.VMEM((KV_HEADS, n_sel, tq), F32),
            pltpu.VMEM((GATE_PAD, tq), F32),
            pltpu.VMEM((N_HEADS // 2, HEAD_DIM, 2 * tq), F32),
        ],
        compiler_params=_params(2),
        name="attend_prompt",
    )(q, cmp_kv, cmp_kv, kva, kva, kva, kva, gates, mt, eye)


def _attn_sample_kernel(pt_ref, cache_ref, q_ref, kvn_ref, wn_ref, win_ref, gate_ref,
                        wcat_ref, pos_ref, w2_ref, ms_ref, ek_ref, o_ref, buf, sem,
                        *, n_pages, page, past, topk, n_sel):
    b = pl.program_id(0)
    nb = pl.num_programs(0)
    n_kv = CACHE_SLOTS * KV_HEADS
    n_w = 2 * KV_HEADS
    cpp = page // CMP_STRIDE
    nch = past // CMP_STRIDE
    wb = win_ref.shape[1] // n_w

    n_seq = q_ref.shape[0]
    seqs = range(n_seq)

    def page_copy(step, slot, s, pi, r):
        src0 = pl.multiple_of(pt_ref[step * n_seq + s, pi] * cpp, cpp)
        return pltpu.make_async_copy(
            cache_ref.at[pl.ds(src0, cpp), r],
            buf.at[slot, s, r, pl.ds(pl.multiple_of(pi * cpp, cpp), cpp)], sem.at[slot])

    def for_all_copies(step, slot, act):
        def per_page(pi, carry):
            for s in seqs:
                for r in range(CMP_STRIDE):
                    act(page_copy(step, slot, s, pi, r))
            return carry
        lax.fori_loop(0, n_pages, per_page, 0)

    @pl.when(b == 0)
    def _():
        for_all_copies(0, 0, lambda c: c.start())

    @pl.when(b + 1 < nb)
    def _():
        for_all_copies(b + 1, (b + 1) % 2, lambda c: c.start())

    slot = b % 2
    for_all_copies(b, slot, lambda c: c.wait())

    def chunk_rows(s, cache_slot, g, r):
        rows = buf.reshape(2, n_seq, CMP_STRIDE, nch * n_kv, HEAD_DIM)
        return rows[slot, s, r, pl.ds(cache_slot * KV_HEADS + g, nch, stride=n_kv), :]

    row8 = lax.broadcasted_iota(jnp.int32, (N_HEADS, 1), 0)
    in_g0 = row8 < HPG

    def by_group(x0, x1):
        return jnp.where(in_g0, x0, x1)

    q8 = [q_ref[s] for s in seqs]
    qf = [q8[s].astype(F32) for s in seqs]
    groups = range(KV_HEADS)

    def scores(s, keys):
        return by_group(_dot_nt(q8[s], keys[0]), _dot_nt(q8[s], keys[1])) * SCALE

    def cached(s, cache_slot, g):
        return jnp.concatenate([chunk_rows(s, cache_slot, g, r).astype(BF16)
                                for r in range(CMP_STRIDE)], axis=0)

    def compress_input(cache_slot):
        return jnp.concatenate(
            [jnp.concatenate([chunk_rows(s, cache_slot, g, r).astype(BF16)
                              for r in range(CMP_STRIDE)], axis=1)
             for s in seqs for g in groups], axis=0)

    first = [_compress_first(compress_input(cs), wcat_ref[cs], pos_ref[cs]) for cs in range(2)]
    s_sel = [scores(s, [cached(s, 2, g) for g in groups]) for s in seqs]
    s_win = [scores(s, [win_ref[s, pl.ds(g, wb, stride=n_w), :].astype(BF16) for g in groups])
             for s in seqs]
    kc, vc = [_dot(_compress_hidden(*first[cs]), w2_ref[cs]).astype(BF16) for cs in range(2)]

    def compressed(x, s, g):
        r0 = (s * KV_HEADS + g) * nch
        return x[r0:r0 + nch]

    lane_n = lax.broadcasted_iota(jnp.int32, (N_HEADS, nch), 1)
    ok = lane_n * CMP_STRIDE + (CMP_LEN - 1) <= past
    s_cmp = [by_group(*[_dot_nt(q8[s], compressed(kc, s, g)) for g in groups]) * SCALE
             for s in seqs]
    p_cmp = []
    for s in seqs:
        e, d = _softmax_parts(s_cmp[s], ok)
        p_cmp.append(e / d)
    o_cmp = [by_group(*[_dot(p_cmp[s].astype(BF16), compressed(vc, s, g)) for g in groups])
             for s in seqs]

    jl = lax.broadcasted_iota(jnp.int32, (N_HEADS, LANES), 1)
    jt = past // SEL_BLOCK
    forced = jnp.where(jl == 0, 1.0, jnp.where(jl == jt, 1.0, jnp.where(jl == jt - 1, 1.0, 0.0)))
    ii = lax.broadcasted_iota(jnp.int32, (LANES, LANES), 0)
    jj = lax.broadcasted_iota(jnp.int32, (LANES, LANES), 1)
    tie = jnp.where(ii < jj, 1.0, 0.0)
    imp = []
    for s in seqs:
        p_g = [jnp.sum(p_cmp[s][g * HPG:(g + 1) * HPG], axis=0, keepdims=True) for g in groups]
        p2 = jnp.concatenate(p_g + [jnp.zeros((N_HEADS - KV_HEADS, nch), F32)], axis=0)
        p_hi, p_lo = _split_hi_lo(p2)
        imp.append(_dot(p_hi, ms_ref[...]) + _dot(p_lo, ms_ref[...]))
    sel2 = []
    for s in seqs:
        score = jnp.where(forced > 0.5, FORCE_SCORE, jnp.where(jl <= jt, imp[s], -1.0))
        score = jnp.where(jl < n_sel, score, -2.0)
        sel_rows = []
        for g in groups:
            srow = jnp.broadcast_to(score[g:g + 1, :], (LANES, LANES))
            scol = jnp.sum(jnp.where(ii == jj, srow, 0.0), axis=1, keepdims=True)
            beats = jnp.where(scol > srow, 1.0, jnp.where(scol == srow, tie, 0.0))
            rank = jnp.sum(beats, axis=0, keepdims=True)
            sel_rows.append(jnp.where(rank < topk, 1.0, 0.0))
        sel2.append(jnp.concatenate(sel_rows + [jnp.zeros((N_HEADS - KV_HEADS, LANES), F32)], axis=0))
    chunk_ok2 = [_dot(sel2[s].astype(BF16), ek_ref[...]) for s in seqs]

    def new_row(ref, s, idx0):
        x = by_group(ref[s, idx0:idx0 + 1, :], ref[s, idx0 + 1:idx0 + 2, :])
        return x.astype(BF16).astype(F32)

    def weights(s, scores_s, ok, k_new, new_ok):
        s_new = jnp.sum(qf[s] * k_new, axis=-1, keepdims=True) * SCALE
        sc = jnp.where(ok, scores_s, NEG_INF)
        s_new = jnp.where(new_ok, s_new, NEG_INF)
        m = jnp.maximum(jnp.max(sc, axis=-1, keepdims=True), s_new)
        e = jnp.where(ok, jnp.exp(sc - m), 0.0)
        e_new = jnp.where(new_ok, jnp.exp(s_new - m), 0.0)
        d = jnp.sum(e, axis=-1, keepdims=True) + e_new
        return e.astype(BF16), e_new, jnp.where(d > 0, d, 1.0)

    def weighted(w, vals, v_new):
        eb, e_new, d = w
        return (by_group(_dot(eb, vals[0]), _dot(eb, vals[1])) + e_new * v_new) / d

    kpos = past - wb + lax.broadcasted_iota(jnp.int32, (N_HEADS, wb), 1)
    w_sel, w_win = [], []
    for s in seqs:
        chunk_ok = by_group(chunk_ok2[s][0:1], chunk_ok2[s][1:2])
        key_ok = jnp.concatenate([chunk_ok] * CMP_STRIDE, axis=1)
        new_ok = by_group(*[jnp.sum(jnp.where(jl[0:1] == jt, sel2[s][g:g + 1], 0.0), axis=1,
                                    keepdims=True) for g in groups])
        w_sel.append(weights(s, s_sel[s], key_ok > 0.5, new_row(kvn_ref, s, 2 * KV_HEADS),
                             new_ok > 0.5))
        w_win.append(weights(s, s_win[s], kpos > past - WINDOW, new_row(wn_ref, s, 0), row8 >= 0))
    o_sel = [weighted(w_sel[s], [cached(s, 3, g) for g in groups],
                      new_row(kvn_ref, s, 3 * KV_HEADS)) for s in seqs]
    o_win = [weighted(w_win[s],
                      [win_ref[s, pl.ds(KV_HEADS + g, wb, stride=n_w), :].astype(BF16)
                       for g in groups], new_row(wn_ref, s, KV_HEADS)) for s in seqs]

    lane = lax.broadcasted_iota(jnp.int32, (N_HEADS, GATE_PAD), 1)
    for s in seqs:
        gates = jnp.broadcast_to(gate_ref[s], (N_HEADS, GATE_PAD))

        def gate(br):
            return jnp.sum(jnp.where(lane == row8 * N_BRANCH + br, gates, 0.0), axis=-1,
                           keepdims=True)

        o_ref[s] = (gate(0) * o_cmp[s] + gate(1) * o_sel[s] + gate(2) * o_win[s]).astype(BF16)


def _attend_sample(page_table, cache, q, kv_new, wkv_new, win_rows, gates, wcat, pos8, w2):
    nb, n_pages = page_table.shape
    n_phys, page = cache.shape[:2]
    n_kv = CACHE_SLOTS * KV_HEADS
    n_w = 2 * KV_HEADS
    past = n_pages * page
    nch = past // CMP_STRIDE
    wb = win_rows.shape[1] // n_w
    n_sel = -(-(past + 1) // SEL_BLOCK)
    n_cmp = (past + 1) // CMP_STRIDE - 1
    topk = min(SEL_TOPK, n_sel)
    assert n_sel <= LANES and nch * CMP_STRIDE == past
    ms = np.zeros((nch, LANES), np.float32)
    ms[:, :n_sel] = _cmp_to_sel(nch, n_cmp, n_sel)
    ek = (np.arange(nch)[None, :] * CMP_STRIDE // SEL_BLOCK == np.arange(LANES)[:, None])
    ms = jnp.asarray(ms, BF16)
    ek = jnp.asarray(ek.astype(np.float32), BF16)
    n_seq = 2 if nb % 2 == 0 else 1
    seq3 = lambda n: pl.BlockSpec((n_seq, n, HEAD_DIM), lambda b, pt: (b, 0, 0))
    full3 = lambda a: pl.BlockSpec(a.shape, lambda b, pt: (0, 0, 0))
    full2 = lambda a: pl.BlockSpec(a.shape, lambda b, pt: (0, 0))
    grid_spec = pltpu.PrefetchScalarGridSpec(
        num_scalar_prefetch=1,
        grid=(nb // n_seq,),
        in_specs=[
            pl.BlockSpec(memory_space=pl.ANY),
            seq3(N_HEADS), seq3(n_kv), seq3(n_w), seq3(wb * n_w),
            pl.BlockSpec((n_seq, 1, GATE_PAD), lambda b, pt: (b, 0, 0)),
            full3(wcat), full3(pos8), full3(w2), full2(ms), full2(ek),
        ],
        out_specs=seq3(N_HEADS),
        scratch_shapes=[pltpu.VMEM((2, n_seq, CMP_STRIDE, nch, n_kv, HEAD_DIM), F32),
                        pltpu.SemaphoreType.DMA((2,))],
    )
    return pl.pallas_call(
        functools.partial(_attn_sample_kernel, n_pages=n_pages, page=page, past=past,
                          topk=topk, n_sel=n_sel),
        grid_spec=grid_spec,
        out_shape=jax.ShapeDtypeStruct((nb, N_HEADS, HEAD_DIM), BF16),
        compiler_params=_params(1),
        name="attend_sample",
    )(page_table, cache.reshape(n_phys * page // CMP_STRIDE, CMP_STRIDE, n_kv, HEAD_DIM),
      q.reshape(nb, N_HEADS, HEAD_DIM), kv_new.reshape(nb, n_kv, HEAD_DIM),
      wkv_new.reshape(nb, n_w, HEAD_DIM), win_rows,
      gates.reshape(nb, 1, GATE_PAD), wcat, pos8, w2, ms, ek)


def _mix_tail(o, diffs, x, pw_ref, ps_ref, wo_ref, gpost_ref, gpre_ref, y1_ref, h2_ref):
    gw = diffs[0].shape[1]
    ys = [(_dot(diffs[g].astype(BF16), pw_ref[g]) * ps_ref[:, g * gw:(g + 1) * gw]).astype(BF16)
          for g in range(POOL_GROUPS)]
    cat = jnp.concatenate([o] + ys, axis=1)
    m = _dot(cat, wo_ref[...])
    y1 = x + _rms(m, gpost_ref[...])
    y1_ref[...] = y1
    h2_ref[...] = _rms(y1, gpre_ref[...]).astype(BF16)


def _mix_prompt_kernel(o_ref, u_ref, halo_ref, x_ref, pw_ref, ps_ref, wo_ref, gpost_ref, gpre_ref,
                       y1_ref, h2_ref, *, tm):
    i = pl.program_id(1)
    halo_rows = halo_ref.shape[0]
    halo = jnp.where(i > 0, halo_ref[...], 0.0)
    u = u_ref[...]
    uext = jnp.concatenate([halo, u], axis=0)
    n_ext = uext.shape[0]
    gw = u.shape[1] // POOL_GROUPS
    tpos = i * tm + lax.broadcasted_iota(jnp.int32, (tm, 1), 0)
    diffs = []
    for g, w in enumerate(POOL_WINDOWS):
        s = uext[:, g * gw:(g + 1) * gw]
        k = 1
        while k < w:
            s = s + pltpu.roll(s, k, 0)
            k *= 2
        cnt = jnp.minimum(w, tpos + 1).astype(F32)
        diffs.append(s[halo_rows:n_ext] / cnt - u[:, g * gw:(g + 1) * gw])
    _mix_tail(o_ref[...], diffs, x_ref[...], pw_ref, ps_ref, wo_ref, gpost_ref, gpre_ref,
              y1_ref, h2_ref)


def _mix_sample_kernel(o_ref, u_ref, st_ref, x_ref, pw_ref, ps_ref, wo_ref, gpost_ref, gpre_ref,
                       y1_ref, h2_ref, *, past):
    u = u_ref[...]
    c = u.shape[1]
    gw = c // POOL_GROUPS
    n_hist = st_ref.shape[1] // c
    diffs = []
    for g, w in enumerate(POOL_WINDOWS):
        un = u[:, g * gw:(g + 1) * gw]
        s = un
        for back in range(1, w):
            r = n_hist - back
            s = s + st_ref[:, r * c + g * gw:r * c + (g + 1) * gw]
        diffs.append(s / float(min(w, past + 1)) - un)
    _mix_tail(o_ref[...], diffs, x_ref[...], pw_ref, ps_ref, wo_ref, gpost_ref, gpre_ref,
              y1_ref, h2_ref)


def _mix_specs(tm, d, c, pool_w, idx):
    fixed2 = lambda *a: (0, 0)
    fixed3 = lambda *a: (0, 0, 0)
    weights = [
        pl.BlockSpec(pool_w.shape, fixed3),
        pl.BlockSpec((1, c), fixed2),
        pl.BlockSpec((d, d), fixed2),
        pl.BlockSpec((1, d), fixed2),
        pl.BlockSpec((1, d), fixed2),
    ]
    outs = (pl.BlockSpec((tm, d), idx), pl.BlockSpec((tm, d), idx))
    return weights, outs


def _mix_prompt(o, u, x, pool_w, pool_scale, w_o, g_post, g_pre, batch, seq):
    rows, d = x.shape
    c = u.shape[1]
    tm = min(512, seq)
    nt = seq // tm
    halo = 16
    assert halo >= POOL_BUF and seq % tm == 0 and tm % halo == 0
    idx = lambda b, i: (b * nt + i, 0)
    halo_idx = lambda b, i: (jnp.maximum((b * nt + i) * (tm // halo) - 1, 0), 0)
    weights, outs = _mix_specs(tm, d, c, pool_w, idx)
    return pl.pallas_call(
        functools.partial(_mix_prompt_kernel, tm=tm),
        grid=(batch, nt),
        in_specs=[pl.BlockSpec((tm, ATTN_DIM), idx), pl.BlockSpec((tm, c), idx),
                  pl.BlockSpec((halo, c), halo_idx), pl.BlockSpec((tm, d), idx)] + weights,
        out_specs=outs,
        out_shape=(jax.ShapeDtypeStruct((rows, d), F32), jax.ShapeDtypeStruct((rows, d), BF16)),
        compiler_params=_params(2),
        name="mix_prompt",
    )(o, u, u, x, pool_w, pool_scale, w_o, g_post, g_pre)


def _mix_sample(o, u, pool_state, x, pool_w, pool_scale, w_o, g_post, g_pre, past):
    rows, d = x.shape
    c = u.shape[1]
    tm = rows
    idx = lambda i: (i, 0)
    weights, outs = _mix_specs(tm, d, c, pool_w, idx)
    st = pool_state.reshape(rows, -1)
    return pl.pallas_call(
        functools.partial(_mix_sample_kernel, past=past),
        grid=(rows // tm,),
        in_specs=[pl.BlockSpec((tm, ATTN_DIM), idx), pl.BlockSpec((tm, c), idx),
                  pl.BlockSpec((tm, st.shape[1]), idx), pl.BlockSpec((tm, d), idx)] + weights,
        out_specs=outs,
        out_shape=(jax.ShapeDtypeStruct((rows, d), F32), jax.ShapeDtypeStruct((rows, d), BF16)),
        compiler_params=_params(1),
        name="mix_sample",
    )(o, u, st, x, pool_w, pool_scale, w_o, g_post, g_pre)


def _mlp_kernel(h_ref, wu_ref, wd_ref, y1_ref, g_ref, *rest, shift):
    if shift:
        state_ref, fresh_ref, y_ref, rolled_ref, acc_ref = rest
        n_new = fresh_ref.shape[1]
        n_keep = state_ref.shape[1] - n_new
        rolled_ref[:, pl.ds(0, n_keep), :] = state_ref[:, pl.ds(n_new, n_keep), :]
        rolled_ref[:, pl.ds(n_keep, n_new), :] = fresh_ref[...]
    else:
        y_ref, acc_ref = rest
    j = pl.program_id(1)

    @pl.when(j == 0)
    def _():
        acc_ref[...] = jnp.zeros_like(acc_ref)

    a = jnp.maximum(_dot(h_ref[...], wu_ref[...]), 0.0)
    acc_ref[...] += _dot((a * a).astype(BF16), wd_ref[...])

    @pl.when(j == pl.num_programs(1) - 1)
    def _():
        y_ref[...] = y1_ref[...] + _rms(acc_ref[...], g_ref[...])


def _mlp(h2, y1, w_up, w_down, gain, tm, state=None, fresh=None):
    rows, d = y1.shape
    ff = w_up.shape[1]
    tf = min(1024, ff)
    n_i, n_j = rows // tm, ff // tf
    row = lambda i, j: (i, 0)
    shift = state is not None
    y_spec = pl.BlockSpec((tm, d), row)
    y_shape = jax.ShapeDtypeStruct((rows, d), F32)
    extra_in, out_specs, out_shape = [], y_spec, y_shape
    if shift:
        per_step = state.shape[0] // (n_i * n_j)
        assert per_step * n_i * n_j == state.shape[0]
        blk = lambda a: pl.BlockSpec((per_step,) + a.shape[1:], lambda i, j: (i * n_j + j, 0, 0))
        extra_in = [blk(state), blk(fresh)]
        out_specs = (y_spec, blk(state))
        out_shape = (y_shape, jax.ShapeDtypeStruct(state.shape, state.dtype))
    return pl.pallas_call(
        functools.partial(_mlp_kernel, shift=shift),
        grid=(n_i, n_j),
        in_specs=[
            pl.BlockSpec((tm, d), row),
            pl.BlockSpec((d, tf), lambda i, j: (0, j)),
            pl.BlockSpec((tf, d), lambda i, j: (j, 0)),
            pl.BlockSpec((tm, d), row),
            pl.BlockSpec((1, d), lambda i, j: (0, 0)),
        ] + extra_in,
        out_specs=out_specs,
        out_shape=out_shape,
        scratch_shapes=[pltpu.VMEM((tm, d), F32)],
        compiler_params=_params(2),
        name="mlp",
    )(h2, w_up, w_down, y1, gain, *((state, fresh) if shift else ()))


def _layer_weights(w_in, cmp_pos_k, cmp_w1_k, cmp_w2_k, cmp_pos_v, cmp_w1_v, cmp_w2_v,
                   pool_w, w_o, w_up, w_down, pool_dim):
    gate_lo = WKV_OFF + 2 * KV_DIM
    gate_hi = gate_lo + N_BRANCH * N_HEADS
    w_r = jnp.concatenate(
        [w_in[:, :gate_lo], w_in[:, gate_hi:gate_hi + pool_dim],
         jnp.pad(w_in[:, gate_lo:gate_hi], ((0, 0), (0, GATE_PAD - N_BRANCH * N_HEADS)))],
        axis=1).astype(BF16)
    wcat = jnp.stack([jnp.concatenate([w1[:CMP_HALF], w1[CMP_HALF:]], axis=1)
                      for w1 in (cmp_w1_k, cmp_w1_v)]).astype(BF16)
    pos8 = jnp.stack([jnp.pad(p.reshape(1, CMP_IN), ((0, 7), (0, 0)))
                      for p in (cmp_pos_k, cmp_pos_v)]).astype(BF16)
    w2 = jnp.stack([cmp_w2_k, cmp_w2_v]).astype(BF16)
    return (w_r, wcat, pos8, w2, pool_w.astype(BF16), w_o.astype(BF16),
            w_up.astype(BF16), w_down.astype(BF16))


def kernel(x_prompt, x_sample, cache_kv, state_win_kv, state_pool, page_table, norm_mix_pre, w_in,
           cmp_pos_k, cmp_w1_k, cmp_w2_k, cmp_pos_v, cmp_w1_v, cmp_w2_v, pool_w, pool_scale, w_o,
           norm_mix_post, norm_mlp_pre, w_up, w_down, norm_mlp_post):
    batch, seq, d = x_prompt.shape
    nb, dec_seq, _ = x_sample.shape
    depth = w_in.shape[0]
    pool_dim = d - ATTN_DIM
    n_pages = page_table.shape[1]
    page = cache_kv.shape[2]
    past = n_pages * page
    wb = state_win_kv.shape[2]
    assert dec_seq == 1 and seq >= POOL_BUF and seq % CMP_STRIDE == 0
    assert w_in.shape[2] == ATTN_DIM + 6 * KV_DIM + N_BRANCH * N_HEADS + pool_dim

    tm_p = min(512, seq)
    tabs_p = _rope_tables(jnp.arange(seq, dtype=jnp.int32))
    tabs_s = _rope_tables(jnp.full((nb,), past, jnp.int32))
    nt_p = seq // tm_p

    y_p = x_prompt.reshape(batch * seq, d)
    y_s = x_sample.reshape(nb, d)
    kv_p, kv_s, win_p, win_s, pool_p, pool_s = [], [], [], [], [], []
    row_vec = lambda v: v.reshape(1, -1)
    for l in range(depth):
        w_r, wcat, pos8, w2, pw, wo, wu, wd = _layer_weights(
            w_in[l], cmp_pos_k[l], cmp_w1_k[l], cmp_w2_k[l], cmp_pos_v[l], cmp_w1_v[l],
            cmp_w2_v[l], pool_w[l], w_o[l], w_up[l], w_down[l], pool_dim)
        g_pre, g_post = row_vec(norm_mix_pre[l]), row_vec(norm_mix_post[l])
        g_mlp_pre, g_mlp_post = row_vec(norm_mlp_pre[l]), row_vec(norm_mlp_post[l])
        ps = row_vec(pool_scale[l])

        n_w = 2 * KV_HEADS
        win_rows = state_win_kv[l].reshape(nb, wb * n_w, HEAD_DIM)

        q_s, kv_s1, wkv_s, _, gates_s, u_s = _project(
            y_s, g_pre, w_r, tabs_s, lambda i: (i, 0), nb, pool_dim, False)

        q, kv, wkv, kva, gates, u, xc = _project(
            y_p, g_pre, w_r, tabs_p, lambda i: (i % nt_p, 0), tm_p, pool_dim, True)
        cmp_kv = _compress_prompt(xc, wcat, pos8, w2, batch, seq)
        o = _attend_prompt(q, cmp_kv, kva, gates, batch, seq)
        y1, h2 = _mix_prompt(o, u, y_p, pw, ps, wo, g_post, g_mlp_pre, batch, seq)
        y_p, win_rolled = _mlp(h2, y1, wu, wd, g_mlp_post, min(512, batch * seq),
                               win_rows, wkv_s.reshape(nb, n_w, HEAD_DIM))
        kv_p.append(kv.reshape(batch, seq, CACHE_SLOTS, KV_HEADS, HEAD_DIM))
        wp = min(WINDOW, seq)
        win_p.append(wkv.reshape(batch, seq, 2, KV_HEADS, HEAD_DIM)[:, seq - wp:])
        pool_p.append(u.reshape(batch, seq, pool_dim)[:, seq - POOL_BUF:])

        o = _attend_sample(page_table, cache_kv[l], q_s, kv_s1, wkv_s, win_rows, gates_s,
                           wcat, pos8, w2)
        y1, h2 = _mix_sample(o.reshape(nb, ATTN_DIM), u_s, state_pool[l], y_s, pw, ps, wo,
                             g_post, g_mlp_pre, past)
        y_s = _mlp(h2, y1, wu, wd, g_mlp_post, nb)
        kv_s.append(kv_s1.reshape(nb, 1, CACHE_SLOTS, KV_HEADS, HEAD_DIM))
        win_s.append(win_rolled.reshape(nb, wb, 2, KV_HEADS, HEAD_DIM))
        pool_s.append(jnp.concatenate([state_pool[l], u_s[:, None]], axis=1)[:, 1:])

    return (y_p.reshape(batch, seq, d), y_s.reshape(nb, 1, d),
            jnp.stack(kv_p), jnp.stack(kv_s), jnp.stack(win_p), jnp.stack(win_s),
            jnp.stack(pool_p), jnp.stack(pool_s))
```

```python
import functools

import numpy as np
import jax
import jax.numpy as jnp
from jax import lax
from jax.experimental import pallas as pl
from jax.experimental.pallas import tpu as pltpu

N_HEADS = 8
HEAD_DIM = 128
KV_HEADS = 2
HPG = N_HEADS // KV_HEADS
ATTN_DIM = N_HEADS * HEAD_DIM
KV_DIM = KV_HEADS * HEAD_DIM
N_BRANCH = 3
POOL_WINDOWS = (2, 4, 8, 16)
POOL_GROUPS = len(POOL_WINDOWS)
POOL_BUF = max(POOL_WINDOWS) - 1
ROT_DIM = HEAD_DIM // 4
ROT_HALF = ROT_DIM // 2
ROPE_THETA = 500000.0
CMP_LEN = 32
CMP_STRIDE = 16
SEL_BLOCK = 64
SEL_TOPK = 16
WINDOW = 512
EPS = 1e-6
SCALE = HEAD_DIM ** -0.5
FORCE_SCORE = 1e4
NEG_INF = -1e30

LANES = 128
CACHE_SLOTS = 4
ROW_W = CACHE_SLOTS * KV_DIM
CHUNK_W = CMP_STRIDE * ROW_W
CMP_IN = CMP_LEN * HEAD_DIM
CMP_HALF = CMP_STRIDE * HEAD_DIM
GATE_PAD = LANES
VMEM_LIMIT = 56 * 1024 * 1024

BF16 = jnp.bfloat16
F32 = jnp.float32


def _dot(a, b):
    return jnp.dot(a, b, preferred_element_type=F32)


def _dot_nt(a, b):
    return lax.dot_general(a, b, (((1,), (1,)), ((), ())), preferred_element_type=F32)


def _rms(x, g):
    return x * lax.rsqrt(jnp.mean(x * x, axis=-1, keepdims=True) + EPS) * g


def _params(n_axes):
    return pltpu.CompilerParams(
        dimension_semantics=("arbitrary",) * n_axes, vmem_limit_bytes=VMEM_LIMIT)


def _split_hi_lo(x):
    hi = x.astype(BF16)
    lo = (x - hi.astype(F32)).astype(BF16)
    return hi, lo


Q_OFF, KV_OFF, WKV_OFF = 0, ATTN_DIM, ATTN_DIM + 4 * KV_DIM


def _proj_kernel(x_ref, g_ref, w_ref, cos_ref, sa_ref, sb_ref, *rest, pool_dim, chunked):
    if chunked:
        q_ref, kv_ref, wkv_ref, kva_ref, gate_ref, u_ref, xc_ref, tmp_ref = rest
    else:
        q_ref, kv_ref, wkv_ref, kva_ref, gate_ref, u_ref = rest
    tm = x_ref.shape[0]
    u_off = WKV_OFF + 2 * KV_DIM
    gate_off = u_off + pool_dim
    h = _rms(x_ref[...], g_ref[...]).astype(BF16)
    cos, sa, sb = cos_ref[...], sa_ref[...], sb_ref[...]

    def rope(z):
        return (z * cos + pltpu.roll(z, LANES - ROT_HALF, 1) * sa
                + pltpu.roll(z, ROT_HALF, 1) * sb)

    zq = _dot(h, w_ref[:, Q_OFF:Q_OFF + ATTN_DIM])
    for hd in range(N_HEADS):
        sl = slice(hd * HEAD_DIM, (hd + 1) * HEAD_DIM)
        q_ref[:, sl] = rope(zq[:, sl]).astype(BF16)

    n_kv = CACHE_SLOTS * KV_HEADS
    zkv = _dot(h, w_ref[:, KV_OFF:KV_OFF + 4 * KV_DIM])
    for blk in range(n_kv):
        z = zkv[:, blk * HEAD_DIM:(blk + 1) * HEAD_DIM]
        if (blk // KV_HEADS) % 2 == 0:
            z = rope(z)
        kv_ref[pl.ds(blk, tm, stride=n_kv), :] = z
        if blk >= 2 * KV_HEADS:
            kva_ref[blk - 2 * KV_HEADS] = z.astype(BF16)
        elif chunked:
            tmp_ref[...] = z
            for r in range(CMP_STRIDE):
                xc_ref[blk, :, r * HEAD_DIM:(r + 1) * HEAD_DIM] = (
                    tmp_ref[pl.ds(r, tm // CMP_STRIDE, stride=CMP_STRIDE), :].astype(BF16))

    n_w = 2 * KV_HEADS
    zw = _dot(h, w_ref[:, WKV_OFF:WKV_OFF + 2 * KV_DIM])
    for blk in range(n_w):
        z = zw[:, blk * HEAD_DIM:(blk + 1) * HEAD_DIM]
        if blk < KV_HEADS:
            z = rope(z)
        wkv_ref[pl.ds(blk, tm, stride=n_w), :] = z
        kva_ref[2 * KV_HEADS + blk] = z.astype(BF16)

    u_ref[...] = _dot(h, w_ref[:, u_off:u_off + pool_dim])
    gl = _dot(h, w_ref[:, gate_off:gate_off + GATE_PAD])
    gate_ref[...] = 1.0 / (1.0 + jnp.exp(-gl))


def _project(x, gain, w_r, tables, table_index, tm, pool_dim, chunked):
    rows, d = x.shape
    n_proj = w_r.shape[1]
    n_kv = CACHE_SLOTS * KV_HEADS
    n_w = 2 * KV_HEADS
    row = lambda i: (i, 0)
    fixed = lambda i: (0, 0)
    tab_spec = pl.BlockSpec((tm, LANES), table_index)
    out_shape = [
        jax.ShapeDtypeStruct((rows, ATTN_DIM), BF16),
        jax.ShapeDtypeStruct((rows * n_kv, HEAD_DIM), F32),
        jax.ShapeDtypeStruct((rows * n_w, HEAD_DIM), F32),
        jax.ShapeDtypeStruct((n_w + n_kv // 2, rows, HEAD_DIM), BF16),
        jax.ShapeDtypeStruct((rows, GATE_PAD), F32),
        jax.ShapeDtypeStruct((rows, pool_dim), F32),
    ]
    out_specs = [
        pl.BlockSpec((tm, ATTN_DIM), row),
        pl.BlockSpec((tm * n_kv, HEAD_DIM), row),
        pl.BlockSpec((tm * n_w, HEAD_DIM), row),
        pl.BlockSpec((n_w + n_kv // 2, tm, HEAD_DIM), lambda i: (0, i, 0)),
        pl.BlockSpec((tm, GATE_PAD), row),
        pl.BlockSpec((tm, pool_dim), row),
    ]
    scratch = []
    if chunked:
        out_shape.append(jax.ShapeDtypeStruct((n_kv // 2, rows // CMP_STRIDE, CMP_HALF), BF16))
        out_specs.append(pl.BlockSpec((n_kv // 2, tm // CMP_STRIDE, CMP_HALF), lambda i: (0, i, 0)))
        scratch.append(pltpu.VMEM((tm, HEAD_DIM), F32))
    return pl.pallas_call(
        functools.partial(_proj_kernel, pool_dim=pool_dim, chunked=chunked),
        grid=(rows // tm,),
        in_specs=[
            pl.BlockSpec((tm, d), row),
            pl.BlockSpec((1, d), fixed),
            pl.BlockSpec((d, n_proj), fixed),
            tab_spec, tab_spec, tab_spec,
        ],
        out_specs=tuple(out_specs),
        out_shape=tuple(out_shape),
        scratch_shapes=scratch,
        compiler_params=_params(1),
        name="project",
    )(x, gain, w_r, *tables)


def _rope_tables(pos):
    inv = jnp.power(ROPE_THETA, -jnp.arange(ROT_HALF, dtype=F32) * (2.0 / ROT_DIM))
    ang = pos.astype(F32)[:, None] * inv[None, :]
    cos, sin = jnp.cos(ang), jnp.sin(ang)
    n = pos.shape[0]
    rest = LANES - ROT_DIM
    c = jnp.concatenate([cos, cos, jnp.ones((n, rest), F32)], axis=1)
    sa = jnp.concatenate([-sin, jnp.zeros((n, LANES - ROT_HALF), F32)], axis=1)
    sb = jnp.concatenate([jnp.zeros((n, ROT_HALF), F32), sin, jnp.zeros((n, rest), F32)], axis=1)
    return c, sa, sb


def _gelu_tanh(x):
    return 0.5 * x * (1.0 + jnp.tanh(0.7978845608028654 * (x + 0.044715 * (x * x * x))))


def _compress_first(x, wcat, pos8):
    ab = _dot(x, wcat)
    pa = _dot(pos8[:, :CMP_HALF], wcat)[0:1, :HEAD_DIM]
    pb = _dot(pos8[:, CMP_HALF:], wcat)[0:1, HEAD_DIM:]
    return ab, pa + pb


def _compress_hidden(ab, pos_term):
    rows = ab.shape[0]
    b_next = pltpu.roll(ab[:, HEAD_DIM:], rows - 1, 0)
    return _gelu_tanh(ab[:, :HEAD_DIM] + b_next + pos_term).astype(BF16)


def _compress_kernel(x_ref, wcat_ref, pos_ref, w2_ref, o_ref):
    g, _, nck, width = x_ref.shape
    x = x_ref[...].reshape(g * nck, width)
    hid = _compress_hidden(*_compress_first(x, wcat_ref[0], pos_ref[0]))
    o_ref[...] = _dot(hid, w2_ref[0]).astype(BF16).reshape(o_ref.shape)


def _compress_prompt(xc, wcat, pos8, w2, batch, seq):
    nck = seq // CMP_STRIDE
    x = xc.reshape(xc.shape[0], batch, nck, CMP_HALF)
    return pl.pallas_call(
        _compress_kernel,
        grid=(batch, 2),
        in_specs=[
            pl.BlockSpec((KV_HEADS, 1, nck, CMP_HALF), lambda b, s: (s, b, 0, 0)),
            pl.BlockSpec((1, CMP_HALF, 2 * HEAD_DIM), lambda b, s: (s, 0, 0)),
            pl.BlockSpec((1, 8, CMP_IN), lambda b, s: (s, 0, 0)),
            pl.BlockSpec((1, HEAD_DIM, HEAD_DIM), lambda b, s: (s, 0, 0)),
        ],
        out_specs=pl.BlockSpec((1, 1, KV_HEADS, nck, HEAD_DIM), lambda b, s: (b, s, 0, 0, 0)),
        out_shape=jax.ShapeDtypeStruct((batch, 2, KV_HEADS, nck, HEAD_DIM), BF16),
        compiler_params=_params(2),
        name="compress_prompt",
    )(x, wcat, pos8, w2)


def _topk_rows(score, jidx, topk):
    rank = jnp.zeros_like(score)
    for j in range(score.shape[0]):
        bj = score[j:j + 1, :]
        tie = jnp.where(jidx > j, 1.0, 0.0)
        rank = rank + jnp.where(bj > score, 1.0, jnp.where(bj == score, tie, 0.0))
    return jnp.where(rank < topk, 1.0, 0.0)


def _softmax_parts(s, ok):
    s = jnp.where(ok, s, NEG_INF)
    m = jnp.max(s, axis=-1, keepdims=True)
    e = jnp.where(ok, jnp.exp(s - m), 0.0)
    d = jnp.sum(e, axis=-1, keepdims=True)
    return e, jnp.where(d > 0, d, 1.0)


EXP2_SCALE = SCALE * 1.4426950408889634
HEAD_PAIRS = HPG // 2


def _attn_prompt_kernel(q_ref, kc_ref, vc_ref, ks_ref, vs_ref, kw_ref, vw_ref, gate_ref,
                        mt_ref, eye_ref, o_ref,
                        vct_ref, vst_ref, vwt_ref, sel_ref, gt_ref, acc_ref,
                        *, tq, tk, seq, topk):
    i = pl.program_id(1)
    t0 = i * tq
    n_sel = mt_ref.shape[0]
    pair_w = 2 * tq
    eye = eye_ref[...]
    groups = range(KV_HEADS)
    pairs = range(KV_HEADS * HEAD_PAIRS)
    group_of = [hp // HEAD_PAIRS for hp in pairs]

    @pl.when(i == 0)
    def _():
        for g in groups:
            vct_ref[g] = _dot_nt(eye, vc_ref[0, 0, g]).astype(BF16)
            vst_ref[g] = _dot_nt(eye, vs_ref[g]).astype(BF16)
            vwt_ref[g] = _dot_nt(eye, vw_ref[g]).astype(BF16)

    q_all = q_ref[...]
    q_pairs = [jnp.concatenate([q_all[:, (2 * hp) * HEAD_DIM:(2 * hp + 1) * HEAD_DIM],
                                q_all[:, (2 * hp + 1) * HEAD_DIM:(2 * hp + 2) * HEAD_DIM]], axis=0)
               for hp in pairs]

    def both_heads(x):
        return jnp.concatenate([x, x], axis=1)

    def tpos(n_keys):
        return t0 + lax.broadcasted_iota(jnp.int32, (n_keys, tq), 1)

    def kidx(n_keys):
        return lax.broadcasted_iota(jnp.int32, (n_keys, tq), 0)


    ncp = kc_ref.shape[3]
    wk = min(WINDOW + tq, seq)
    ws = pl.multiple_of(jnp.maximum(t0 + tq - wk, 0), tq)
    s_cmp = [_dot_nt(kc_ref[0, 0, group_of[hp]], q_pairs[hp]) for hp in pairs]
    s_win = [_dot_nt(kw_ref[group_of[hp], pl.ds(ws, wk), :], q_pairs[hp]) for hp in pairs]

    ok = both_heads(jnp.where(kidx(ncp) * CMP_STRIDE + (CMP_LEN - 1) <= tpos(ncp), 1.0, 0.0)) > 0.5
    p_cmp, p_sum = [], [None] * KV_HEADS
    for hp in pairs:
        s = jnp.where(ok, s_cmp[hp], NEG_INF)
        m = jnp.max(s, axis=0, keepdims=True)
        e = jnp.where(ok, jnp.exp2((s - m) * EXP2_SCALE), 0.0)
        d = jnp.sum(e, axis=0, keepdims=True)
        p = e / jnp.where(d > 0, d, 1.0)
        p_cmp.append(p.astype(BF16))
        ph = p[:, :tq] + p[:, tq:]
        g = group_of[hp]
        p_sum[g] = ph if p_sum[g] is None else p_sum[g] + ph

    mt = mt_ref[...]
    split = [_split_hi_lo(p_sum[g]) for g in groups]
    imp = [_dot(mt, split[g][0]) + _dot(mt, split[g][1]) for g in groups]
    o_cmp = [_dot(vct_ref[group_of[hp]], p_cmp[hp]) for hp in pairs]

    kpos = ws + kidx(wk)
    bias = both_heads(jnp.where(kpos <= tpos(wk),
                                jnp.where(kpos > tpos(wk) - WINDOW, 0.0, NEG_INF), NEG_INF))
    p_win, l_win = [], []
    for hp in pairs:
        s = s_win[hp] + bias
        p = jnp.exp2((s - jnp.max(s, axis=0, keepdims=True)) * EXP2_SCALE)
        l_win.append(jnp.sum(p, axis=0, keepdims=True))
        p_win.append(p.astype(BF16))
    o_win = [_dot(vwt_ref[group_of[hp], :, pl.ds(ws, wk)], p_win[hp]) for hp in pairs]
    o_win = [o_win[hp] / l_win[hp] for hp in pairs]

    jidx = kidx(n_sel)
    jt = tpos(n_sel) // SEL_BLOCK
    forced = jnp.where(jidx == 0, 1.0, jnp.where(jidx == jt, 1.0, jnp.where(jidx == jt - 1, 1.0, 0.0)))
    for g in groups:
        score = jnp.where(forced > 0.5, FORCE_SCORE, jnp.where(jidx <= jt, imp[g], -1.0))
        sel_ref[g] = _topk_rows(score, jidx, topk)

    acc_ref[...] = jnp.zeros_like(acc_ref)

    def sel_step(kb, carry):
        k0 = pl.multiple_of(kb * tk, tk)
        blk0 = kb * (tk // SEL_BLOCK)
        causal = k0 + kidx(tk) <= tpos(tk)
        bias = []
        for g in groups:
            chosen = jnp.concatenate(
                [jnp.broadcast_to(sel_ref[g, pl.ds(blk0 + j, 1), :], (SEL_BLOCK, tq))
                 for j in range(tk // SEL_BLOCK)], axis=0)
            bias.append(both_heads(
                jnp.where(causal, jnp.where(chosen > 0.5, 0.0, NEG_INF), NEG_INF)))
        ss = [_dot_nt(ks_ref[group_of[hp], pl.ds(k0, tk), :], q_pairs[hp]) + bias[group_of[hp]]
              for hp in pairs]
        out, ps, alphas = [], [], []
        for hp in pairs:
            m, l = carry[2 * hp], carry[2 * hp + 1]
            m_new = jnp.maximum(m, jnp.max(ss[hp], axis=0, keepdims=True))
            p = jnp.exp2((ss[hp] - m_new) * EXP2_SCALE)
            alpha = jnp.exp2((m - m_new) * EXP2_SCALE)
            out += [m_new, alpha * l + jnp.sum(p, axis=0, keepdims=True)]
            ps.append(p.astype(BF16))
            alphas.append(alpha)
        pvs = [_dot(vst_ref[group_of[hp], :, pl.ds(k0, tk)], ps[hp]) for hp in pairs]
        for hp in pairs:
            acc_ref[hp] = alphas[hp] * acc_ref[hp] + pvs[hp]
        return tuple(out)

    n_kb = (t0 + tq + tk - 1) // tk
    init = (jnp.full((1, pair_w), NEG_INF, F32), jnp.zeros((1, pair_w), F32)) * len(pairs)
    stats = lax.fori_loop(0, n_kb, sel_step, init)
    o_sel = [acc_ref[hp] / stats[2 * hp + 1] for hp in pairs]

    gt_ref[...] = gate_ref[...].T
    o_t = []
    for h in range(N_HEADS):
        hp, lanes = h // 2, slice((h % 2) * tq, (h % 2 + 1) * tq)
        col = h * N_BRANCH
        o_t.append((gt_ref[col:col + 1, :] * o_cmp[hp][:, lanes]
                    + gt_ref[col + 1:col + 2, :] * o_sel[hp][:, lanes]
                    + gt_ref[col + 2:col + 3, :] * o_win[hp][:, lanes]).astype(BF16))
    outs = [_dot_nt(eye, o_t[h]) for h in range(N_HEADS)]
    for h in range(N_HEADS):
        o_ref[:, h * HEAD_DIM:(h + 1) * HEAD_DIM] = outs[h].astype(BF16)


def _cmp_to_sel(n_cmp_pad, n_cmp, n_sel):
    cs = np.arange(n_cmp_pad)[:, None] * CMP_STRIDE
    ss = np.arange(n_sel)[None, :] * SEL_BLOCK
    hit = (cs < ss + SEL_BLOCK) & (cs + CMP_LEN > ss) & (np.arange(n_cmp_pad)[:, None] < n_cmp)
    return hit.astype(np.float32)


def _attend_prompt(q, cmp_kv, kva, gates, batch, seq):
    tq = min(128, seq)
    tk = min(512, seq)
    nq = seq // tq
    ncp = seq // CMP_STRIDE
    n_sel = -(-seq // SEL_BLOCK)
    topk = min(SEL_TOPK, n_sel)
    assert tq == HEAD_DIM and seq % tk == 0 and WINDOW % tq == 0
    mt = jnp.asarray(_cmp_to_sel(ncp, ncp - 1, n_sel).T, BF16)
    eye = jnp.asarray(np.eye(tq, dtype=np.float32), BF16)
    kv_spec = lambda slot: pl.BlockSpec((KV_HEADS, seq, HEAD_DIM), lambda b, i: (slot, b, 0))
    cmp_spec = lambda s: pl.BlockSpec(
        (1, 1, KV_HEADS, ncp, HEAD_DIM), lambda b, i: (b, s, 0, 0, 0))
    full = lambda a: pl.BlockSpec(a.shape, lambda b, i: (0, 0))
    return pl.pallas_call(
        functools.partial(_attn_prompt_kernel, tq=tq, tk=tk, seq=seq, topk=topk),
        grid=(batch, nq),
        in_specs=[
            pl.BlockSpec((tq, ATTN_DIM), lambda b, i: (b * nq + i, 0)),
            cmp_spec(0), cmp_spec(1),
            kv_spec(0), kv_spec(1), kv_spec(2), kv_spec(3),
            pl.BlockSpec((tq, GATE_PAD), lambda b, i: (b * nq + i, 0)),
            full(mt), full(eye),
        ],
        out_specs=pl.BlockSpec((tq, ATTN_DIM), lambda b, i: (b * nq + i, 0)),
        out_shape=jax.ShapeDtypeStruct((batch * seq, ATTN_DIM), BF16),
        scratch_shapes=[
            pltpu.VMEM((KV_HEADS, HEAD_DIM, ncp), BF16),
            pltpu.VMEM((KV_HEADS, HEAD_DIM, seq), BF16),
            pltpu.VMEM((KV_HEADS, HEAD_DIM, seq), BF16),
            pltpu.VMEM((KV_HEADS, n_sel, tq), F32),
            pltpu.VMEM((GATE_PAD, tq), F32),
            pltpu.VMEM((N_HEADS // 2, HEAD_DIM, 2 * tq), F32),
        ],
        compiler_params=_params(2),
        name="attend_prompt",
    )(q, cmp_kv, cmp_kv, kva, kva, kva, kva, gates, mt, eye)


def _sample_group(q_ref, kvn_ref, wn_ref, win_ref, gate_ref, wcat_ref, pos_ref, w2_ref, ms_ref,
                  ek_ref, o_ref, rows_ref, base, *, n_seq, past, topk, n_sel):
    n_kv = CACHE_SLOTS * KV_HEADS
    n_w = 2 * KV_HEADS
    nch = past // CMP_STRIDE
    wb = win_ref.shape[1] // n_w
    seqs = range(n_seq)

    def chunk_rows(s, cache_slot, g, r):
        rows = rows_ref.reshape(n_seq, CMP_STRIDE, nch * n_kv, HEAD_DIM)
        return rows[s, r, pl.ds(cache_slot * KV_HEADS + g, nch, stride=n_kv), :]

    row8 = lax.broadcasted_iota(jnp.int32, (N_HEADS, 1), 0)
    in_g0 = row8 < HPG

    def by_group(x0, x1):
        return jnp.where(in_g0, x0, x1)

    q8 = [q_ref[base + s] for s in seqs]
    qf = [q8[s].astype(F32) for s in seqs]
    groups = range(KV_HEADS)

    def scores(s, keys):
        return by_group(_dot_nt(q8[s], keys[0]), _dot_nt(q8[s], keys[1])) * SCALE

    def cached(s, cache_slot, g):
        return jnp.concatenate([chunk_rows(s, cache_slot, g, r).astype(BF16)
                                for r in range(CMP_STRIDE)], axis=0)

    def compress_input(cache_slot):
        return jnp.concatenate(
            [jnp.concatenate([chunk_rows(s, cache_slot, g, r).astype(BF16)
                              for r in range(CMP_STRIDE)], axis=1)
             for s in seqs for g in groups], axis=0)

    first = [_compress_first(compress_input(cs), wcat_ref[cs], pos_ref[cs]) for cs in range(2)]
    s_sel = [scores(s, [cached(s, 2, g) for g in groups]) for s in seqs]
    s_win = [scores(s, [win_ref[base + s, pl.ds(g, wb, stride=n_w), :].astype(BF16)
                        for g in groups])
             for s in seqs]
    kc, vc = [_dot(_compress_hidden(*first[cs]), w2_ref[cs]).astype(BF16) for cs in range(2)]

    def compressed(x, s, g):
        r0 = (s * KV_HEADS + g) * nch
        return x[r0:r0 + nch]

    lane_n = lax.broadcasted_iota(jnp.int32, (N_HEADS, nch), 1)
    ok = lane_n * CMP_STRIDE + (CMP_LEN - 1) <= past
    s_cmp = [by_group(*[_dot_nt(q8[s], compressed(kc, s, g)) for g in groups]) * SCALE
             for s in seqs]
    p_cmp = []
    for s in seqs:
        e, d = _softmax_parts(s_cmp[s], ok)
        p_cmp.append(e / d)
    o_cmp = [by_group(*[_dot(p_cmp[s].astype(BF16), compressed(vc, s, g)) for g in groups])
             for s in seqs]

    jl = lax.broadcasted_iota(jnp.int32, (N_HEADS, LANES), 1)
    jt = past // SEL_BLOCK
    forced = jnp.where(jl == 0, 1.0, jnp.where(jl == jt, 1.0, jnp.where(jl == jt - 1, 1.0, 0.0)))
    ii = lax.broadcasted_iota(jnp.int32, (LANES, LANES), 0)
    jj = lax.broadcasted_iota(jnp.int32, (LANES, LANES), 1)
    tie = jnp.where(ii < jj, 1.0, 0.0)
    imp = []
    for s in seqs:
        p_g = [jnp.sum(p_cmp[s][g * HPG:(g + 1) * HPG], axis=0, keepdims=True) for g in groups]
        p2 = jnp.concatenate(p_g + [jnp.zeros((N_HEADS - KV_HEADS, nch), F32)], axis=0)
        p_hi, p_lo = _split_hi_lo(p2)
        imp.append(_dot(p_hi, ms_ref[...]) + _dot(p_lo, ms_ref[...]))
    sel2 = []
    for s in seqs:
        score = jnp.where(forced > 0.5, FORCE_SCORE, jnp.where(jl <= jt, imp[s], -1.0))
        score = jnp.where(jl < n_sel, score, -2.0)
        sel_rows = []
        for g in groups:
            srow = jnp.broadcast_to(score[g:g + 1, :], (LANES, LANES))
            scol = jnp.sum(jnp.where(ii == jj, srow, 0.0), axis=1, keepdims=True)
            beats = jnp.where(scol > srow, 1.0, jnp.where(scol == srow, tie, 0.0))
            rank = jnp.sum(beats, axis=0, keepdims=True)
            sel_rows.append(jnp.where(rank < topk, 1.0, 0.0))
        sel2.append(jnp.concatenate(sel_rows + [jnp.zeros((N_HEADS - KV_HEADS, LANES), F32)], axis=0))
    chunk_ok2 = [_dot(sel2[s].astype(BF16), ek_ref[...]) for s in seqs]

    def new_row(ref, s, idx0):
        x = by_group(ref[base + s, idx0:idx0 + 1, :], ref[base + s, idx0 + 1:idx0 + 2, :])
        return x.astype(BF16).astype(F32)

    def weights(s, scores_s, ok, k_new, new_ok):
        s_new = jnp.sum(qf[s] * k_new, axis=-1, keepdims=True) * SCALE
        sc = jnp.where(ok, scores_s, NEG_INF)
        s_new = jnp.where(new_ok, s_new, NEG_INF)
        m = jnp.maximum(jnp.max(sc, axis=-1, keepdims=True), s_new)
        e = jnp.where(ok, jnp.exp(sc - m), 0.0)
        e_new = jnp.where(new_ok, jnp.exp(s_new - m), 0.0)
        d = jnp.sum(e, axis=-1, keepdims=True) + e_new
        return e.astype(BF16), e_new, jnp.where(d > 0, d, 1.0)

    def weighted(w, vals, v_new):
        eb, e_new, d = w
        return (by_group(_dot(eb, vals[0]), _dot(eb, vals[1])) + e_new * v_new) / d

    kpos = past - wb + lax.broadcasted_iota(jnp.int32, (N_HEADS, wb), 1)
    w_sel, w_win = [], []
    for s in seqs:
        chunk_ok = by_group(chunk_ok2[s][0:1], chunk_ok2[s][1:2])
        key_ok = jnp.concatenate([chunk_ok] * CMP_STRIDE, axis=1)
        new_ok = by_group(*[jnp.sum(jnp.where(jl[0:1] == jt, sel2[s][g:g + 1], 0.0), axis=1,
                                    keepdims=True) for g in groups])
        w_sel.append(weights(s, s_sel[s], key_ok > 0.5, new_row(kvn_ref, s, 2 * KV_HEADS),
                             new_ok > 0.5))
        w_win.append(weights(s, s_win[s], kpos > past - WINDOW, new_row(wn_ref, s, 0), row8 >= 0))
    o_sel = [weighted(w_sel[s], [cached(s, 3, g) for g in groups],
                      new_row(kvn_ref, s, 3 * KV_HEADS)) for s in seqs]
    o_win = [weighted(w_win[s],
                      [win_ref[base + s, pl.ds(KV_HEADS + g, wb, stride=n_w), :].astype(BF16)
                       for g in groups], new_row(wn_ref, s, KV_HEADS)) for s in seqs]

    lane = lax.broadcasted_iota(jnp.int32, (N_HEADS, GATE_PAD), 1)
    for s in seqs:
        gates = jnp.broadcast_to(gate_ref[base + s], (N_HEADS, GATE_PAD))

        def gate(br):
            return jnp.sum(jnp.where(lane == row8 * N_BRANCH + br, gates, 0.0), axis=-1,
                           keepdims=True)

        o_ref[base + s] = (gate(0) * o_cmp[s] + gate(1) * o_sel[s]
                           + gate(2) * o_win[s]).astype(BF16)


def _attn_sample_kernel(pt_ref, cache_ref, q_ref, kvn_ref, wn_ref, win_ref, gate_ref,
                        wcat_ref, pos_ref, w2_ref, ms_ref, ek_ref, o_ref, buf_a, buf_b, sem,
                        *, n_pages, page, n_seq, **consts):
    t = pl.program_id(0)
    n_t = pl.num_programs(0)
    cpp = page // CMP_STRIDE
    refs = (q_ref, kvn_ref, wn_ref, win_ref, gate_ref, wcat_ref, pos_ref, w2_ref, ms_ref,
            ek_ref, o_ref)

    def page_copy(group, buf, slot, s, pi, r):
        src0 = pl.multiple_of(pt_ref[group * n_seq + s, pi] * cpp, cpp)
        return pltpu.make_async_copy(
            cache_ref.at[pl.ds(src0, cpp), r],
            buf.at[s, r, pl.ds(pl.multiple_of(pi * cpp, cpp), cpp)], sem.at[slot])

    def start_group(group, buf, slot):
        for pi in range(n_pages):
            for s in range(n_seq):
                for r in range(CMP_STRIDE):
                    page_copy(group, buf, slot, s, pi, r).start()

    def wait_group(buf, slot):
        def per_page(pi, carry):
            for s in range(n_seq):
                for r in range(CMP_STRIDE):
                    page_copy(0, buf, slot, s, pi, r).wait()
            return carry
        lax.fori_loop(0, n_pages, per_page, 0)

    @pl.when(t == 0)
    def _():
        start_group(0, buf_a, 0)

    wait_group(buf_a, 0)
    start_group(2 * t + 1, buf_b, 1)
    _sample_group(*refs, buf_a, 0, n_seq=n_seq, **consts)

    wait_group(buf_b, 1)
    start_group(jnp.minimum(2 * t + 2, 2 * n_t - 1), buf_a, 0)
    _sample_group(*refs, buf_b, n_seq, n_seq=n_seq, **consts)

    @pl.when(t == n_t - 1)
    def _():
        wait_group(buf_a, 0)


def _attend_sample(page_table, cache, q, kv_new, wkv_new, win_rows, gates, wcat, pos8, w2):
    nb, n_pages = page_table.shape
    n_phys, page = cache.shape[:2]
    n_kv = CACHE_SLOTS * KV_HEADS
    n_w = 2 * KV_HEADS
    past = n_pages * page
    nch = past // CMP_STRIDE
    wb = win_rows.shape[1] // n_w
    n_sel = -(-(past + 1) // SEL_BLOCK)
    n_cmp = (past + 1) // CMP_STRIDE - 1
    topk = min(SEL_TOPK, n_sel)
    assert n_sel <= LANES and nch * CMP_STRIDE == past
    ms = np.zeros((nch, LANES), np.float32)
    ms[:, :n_sel] = _cmp_to_sel(nch, n_cmp, n_sel)
    ek = (np.arange(nch)[None, :] * CMP_STRIDE // SEL_BLOCK == np.arange(LANES)[:, None])
    ms = jnp.asarray(ms, BF16)
    ek = jnp.asarray(ek.astype(np.float32), BF16)
    n_seq = 2 if nb % 4 == 0 else 1
    per_step = 2 * n_seq
    assert nb % per_step == 0
    seq3 = lambda n: pl.BlockSpec((per_step, n, HEAD_DIM), lambda b, pt: (b, 0, 0))
    full3 = lambda a: pl.BlockSpec(a.shape, lambda b, pt: (0, 0, 0))
    full2 = lambda a: pl.BlockSpec(a.shape, lambda b, pt: (0, 0))
    group_rows = pltpu.VMEM((n_seq, CMP_STRIDE, nch, n_kv, HEAD_DIM), F32)
    grid_spec = pltpu.PrefetchScalarGridSpec(
        num_scalar_prefetch=1,
        grid=(nb // per_step,),
        in_specs=[
            pl.BlockSpec(memory_space=pl.ANY),
            seq3(N_HEADS), seq3(n_kv), seq3(n_w), seq3(wb * n_w),
            pl.BlockSpec((per_step, 1, GATE_PAD), lambda b, pt: (b, 0, 0)),
            full3(wcat), full3(pos8), full3(w2), full2(ms), full2(ek),
        ],
        out_specs=seq3(N_HEADS),
        scratch_shapes=[group_rows, group_rows, pltpu.SemaphoreType.DMA((2,))],
    )
    return pl.pallas_call(
        functools.partial(_attn_sample_kernel, n_pages=n_pages, page=page, n_seq=n_seq,
                          past=past, topk=topk, n_sel=n_sel),
        grid_spec=grid_spec,
        out_shape=jax.ShapeDtypeStruct((nb, N_HEADS, HEAD_DIM), BF16),
        compiler_params=_params(1),
        name="attend_sample",
    )(page_table, cache.reshape(n_phys * page // CMP_STRIDE, CMP_STRIDE, n_kv, HEAD_DIM),
      q.reshape(nb, N_HEADS, HEAD_DIM), kv_new.reshape(nb, n_kv, HEAD_DIM),
      wkv_new.reshape(nb, n_w, HEAD_DIM), win_rows,
      gates.reshape(nb, 1, GATE_PAD), wcat, pos8, w2, ms, ek)


def _mix_tail(o, diffs, x, pw_ref, ps_ref, wo_ref, gpost_ref, gpre_ref, y1_ref, h2_ref):
    gw = diffs[0].shape[1]
    ys = [(_dot(diffs[g].astype(BF16), pw_ref[g]) * ps_ref[:, g * gw:(g + 1) * gw]).astype(BF16)
          for g in range(POOL_GROUPS)]
    cat = jnp.concatenate([o] + ys, axis=1)
    m = _dot(cat, wo_ref[...])
    y1 = x + _rms(m, gpost_ref[...])
    y1_ref[...] = y1
    h2_ref[...] = _rms(y1, gpre_ref[...]).astype(BF16)


def _mix_prompt_kernel(o_ref, u_ref, halo_ref, x_ref, pw_ref, ps_ref, wo_ref, gpost_ref, gpre_ref,
                       y1_ref, h2_ref, *, tm):
    i = pl.program_id(1)
    halo_rows = halo_ref.shape[0]
    halo = jnp.where(i > 0, halo_ref[...], 0.0)
    u = u_ref[...]
    uext = jnp.concatenate([halo, u], axis=0)
    n_ext = uext.shape[0]
    gw = u.shape[1] // POOL_GROUPS
    tpos = i * tm + lax.broadcasted_iota(jnp.int32, (tm, 1), 0)
    diffs = []
    for g, w in enumerate(POOL_WINDOWS):
        s = uext[:, g * gw:(g + 1) * gw]
        k = 1
        while k < w:
            s = s + pltpu.roll(s, k, 0)
            k *= 2
        cnt = jnp.minimum(w, tpos + 1).astype(F32)
        diffs.append(s[halo_rows:n_ext] / cnt - u[:, g * gw:(g + 1) * gw])
    _mix_tail(o_ref[...], diffs, x_ref[...], pw_ref, ps_ref, wo_ref, gpost_ref, gpre_ref,
              y1_ref, h2_ref)


def _mix_sample_kernel(o_ref, u_ref, st_ref, x_ref, pw_ref, ps_ref, wo_ref, gpost_ref, gpre_ref,
                       y1_ref, h2_ref, *, past):
    u = u_ref[...]
    c = u.shape[1]
    gw = c // POOL_GROUPS
    n_hist = st_ref.shape[1] // c
    diffs = []
    for g, w in enumerate(POOL_WINDOWS):
        un = u[:, g * gw:(g + 1) * gw]
        s = un
        for back in range(1, w):
            r = n_hist - back
            s = s + st_ref[:, r * c + g * gw:r * c + (g + 1) * gw]
        diffs.append(s / float(min(w, past + 1)) - un)
    _mix_tail(o_ref[...], diffs, x_ref[...], pw_ref, ps_ref, wo_ref, gpost_ref, gpre_ref,
              y1_ref, h2_ref)


def _mix_specs(tm, d, c, pool_w, idx):
    fixed2 = lambda *a: (0, 0)
    fixed3 = lambda *a: (0, 0, 0)
    weights = [
        pl.BlockSpec(pool_w.shape, fixed3),
        pl.BlockSpec((1, c), fixed2),
        pl.BlockSpec((d, d), fixed2),
        pl.BlockSpec((1, d), fixed2),
        pl.BlockSpec((1, d), fixed2),
    ]
    outs = (pl.BlockSpec((tm, d), idx), pl.BlockSpec((tm, d), idx))
    return weights, outs


def _mix_prompt(o, u, x, pool_w, pool_scale, w_o, g_post, g_pre, batch, seq):
    rows, d = x.shape
    c = u.shape[1]
    tm = min(512, seq)
    nt = seq // tm
    halo = 16
    assert halo >= POOL_BUF and seq % tm == 0 and tm % halo == 0
    idx = lambda b, i: (b * nt + i, 0)
    halo_idx = lambda b, i: (jnp.maximum((b * nt + i) * (tm // halo) - 1, 0), 0)
    weights, outs = _mix_specs(tm, d, c, pool_w, idx)
    return pl.pallas_call(
        functools.partial(_mix_prompt_kernel, tm=tm),
        grid=(batch, nt),
        in_specs=[pl.BlockSpec((tm, ATTN_DIM), idx), pl.BlockSpec((tm, c), idx),
                  pl.BlockSpec((halo, c), halo_idx), pl.BlockSpec((tm, d), idx)] + weights,
        out_specs=outs,
        out_shape=(jax.ShapeDtypeStruct((rows, d), F32), jax.ShapeDtypeStruct((rows, d), BF16)),
        compiler_params=_params(2),
        name="mix_prompt",
    )(o, u, u, x, pool_w, pool_scale, w_o, g_post, g_pre)


def _mix_sample(o, u, pool_state, x, pool_w, pool_scale, w_o, g_post, g_pre, past):
    rows, d = x.shape
    c = u.shape[1]
    tm = rows
    idx = lambda i: (i, 0)
    weights, outs = _mix_specs(tm, d, c, pool_w, idx)
    st = pool_state.reshape(rows, -1)
    return pl.pallas_call(
        functools.partial(_mix_sample_kernel, past=past),
        grid=(rows // tm,),
        in_specs=[pl.BlockSpec((tm, ATTN_DIM), idx), pl.BlockSpec((tm, c), idx),
                  pl.BlockSpec((tm, st.shape[1]), idx), pl.BlockSpec((tm, d), idx)] + weights,
        out_specs=outs,
        out_shape=(jax.ShapeDtypeStruct((rows, d), F32), jax.ShapeDtypeStruct((rows, d), BF16)),
        compiler_params=_params(1),
        name="mix_sample",
    )(o, u, st, x, pool_w, pool_scale, w_o, g_post, g_pre)


def _mlp_kernel(h_ref, wu_ref, wd_ref, y1_ref, g_ref, *rest, shift):
    if shift:
        state_ref, fresh_ref, y_ref, rolled_ref, acc_ref = rest
        n_new = fresh_ref.shape[1]
        n_keep = state_ref.shape[1] - n_new
        rolled_ref[:, pl.ds(0, n_keep), :] = state_ref[:, pl.ds(n_new, n_keep), :]
        rolled_ref[:, pl.ds(n_keep, n_new), :] = fresh_ref[...]
    else:
        y_ref, acc_ref = rest
    j = pl.program_id(1)

    @pl.when(j == 0)
    def _():
        acc_ref[...] = jnp.zeros_like(acc_ref)

    a = jnp.maximum(_dot(h_ref[...], wu_ref[...]), 0.0)
    acc_ref[...] += _dot((a * a).astype(BF16), wd_ref[...])

    @pl.when(j == pl.num_programs(1) - 1)
    def _():
        y_ref[...] = y1_ref[...] + _rms(acc_ref[...], g_ref[...])


def _mlp(h2, y1, w_up, w_down, gain, tm, state=None, fresh=None):
    rows, d = y1.shape
    ff = w_up.shape[1]
    tf = min(1024, ff)
    n_i, n_j = rows // tm, ff // tf
    row = lambda i, j: (i, 0)
    shift = state is not None
    y_spec = pl.BlockSpec((tm, d), row)
    y_shape = jax.ShapeDtypeStruct((rows, d), F32)
    extra_in, out_specs, out_shape = [], y_spec, y_shape
    if shift:
        per_step = state.shape[0] // (n_i * n_j)
        assert per_step * n_i * n_j == state.shape[0]
        blk = lambda a: pl.BlockSpec((per_step,) + a.shape[1:], lambda i, j: (i * n_j + j, 0, 0))
        extra_in = [blk(state), blk(fresh)]
        out_specs = (y_spec, blk(state))
        out_shape = (y_shape, jax.ShapeDtypeStruct(state.shape, state.dtype))
    return pl.pallas_call(
        functools.partial(_mlp_kernel, shift=shift),
        grid=(n_i, n_j),
        in_specs=[
            pl.BlockSpec((tm, d), row),
            pl.BlockSpec((d, tf), lambda i, j: (0, j)),
            pl.BlockSpec((tf, d), lambda i, j: (j, 0)),
            pl.BlockSpec((tm, d), row),
            pl.BlockSpec((1, d), lambda i, j: (0, 0)),
        ] + extra_in,
        out_specs=out_specs,
        out_shape=out_shape,
        scratch_shapes=[pltpu.VMEM((tm, d), F32)],
        compiler_params=_params(2),
        name="mlp",
    )(h2, w_up, w_down, y1, gain, *((state, fresh) if shift else ()))


def _layer_weights(w_in, cmp_pos_k, cmp_w1_k, cmp_w2_k, cmp_pos_v, cmp_w1_v, cmp_w2_v,
                   pool_w, w_o, w_up, w_down, pool_dim):
    gate_lo = WKV_OFF + 2 * KV_DIM
    gate_hi = gate_lo + N_BRANCH * N_HEADS
    w_r = jnp.concatenate(
        [w_in[:, :gate_lo], w_in[:, gate_hi:gate_hi + pool_dim],
         jnp.pad(w_in[:, gate_lo:gate_hi], ((0, 0), (0, GATE_PAD - N_BRANCH * N_HEADS)))],
        axis=1).astype(BF16)
    wcat = jnp.stack([jnp.concatenate([w1[:CMP_HALF], w1[CMP_HALF:]], axis=1)
                      for w1 in (cmp_w1_k, cmp_w1_v)]).astype(BF16)
    pos8 = jnp.stack([jnp.pad(p.reshape(1, CMP_IN), ((0, 7), (0, 0)))
                      for p in (cmp_pos_k, cmp_pos_v)]).astype(BF16)
    w2 = jnp.stack([cmp_w2_k, cmp_w2_v]).astype(BF16)
    return (w_r, wcat, pos8, w2, pool_w.astype(BF16), w_o.astype(BF16),
            w_up.astype(BF16), w_down.astype(BF16))


def kernel(x_prompt, x_sample, cache_kv, state_win_kv, state_pool, page_table, norm_mix_pre, w_in,
           cmp_pos_k, cmp_w1_k, cmp_w2_k, cmp_pos_v, cmp_w1_v, cmp_w2_v, pool_w, pool_scale, w_o,
           norm_mix_post, norm_mlp_pre, w_up, w_down, norm_mlp_post):
    batch, seq, d = x_prompt.shape
    nb, dec_seq, _ = x_sample.shape
    depth = w_in.shape[0]
    pool_dim = d - ATTN_DIM
    n_pages = page_table.shape[1]
    page = cache_kv.shape[2]
    past = n_pages * page
    wb = state_win_kv.shape[2]
    assert dec_seq == 1 and seq >= POOL_BUF and seq % CMP_STRIDE == 0
    assert w_in.shape[2] == ATTN_DIM + 6 * KV_DIM + N_BRANCH * N_HEADS + pool_dim

    tm_p = min(512, seq)
    tabs_p = _rope_tables(jnp.arange(seq, dtype=jnp.int32))
    tabs_s = _rope_tables(jnp.full((nb,), past, jnp.int32))
    nt_p = seq // tm_p

    y_p = x_prompt.reshape(batch * seq, d)
    y_s = x_sample.reshape(nb, d)
    kv_p, kv_s, win_p, win_s, pool_p, pool_s = [], [], [], [], [], []
    row_vec = lambda v: v.reshape(1, -1)
    for l in range(depth):
        w_r, wcat, pos8, w2, pw, wo, wu, wd = _layer_weights(
            w_in[l], cmp_pos_k[l], cmp_w1_k[l], cmp_w2_k[l], cmp_pos_v[l], cmp_w1_v[l],
            cmp_w2_v[l], pool_w[l], w_o[l], w_up[l], w_down[l], pool_dim)
        g_pre, g_post = row_vec(norm_mix_pre[l]), row_vec(norm_mix_post[l])
        g_mlp_pre, g_mlp_post = row_vec(norm_mlp_pre[l]), row_vec(norm_mlp_post[l])
        ps = row_vec(pool_scale[l])

        n_w = 2 * KV_HEADS
        win_rows = state_win_kv[l].reshape(nb, wb * n_w, HEAD_DIM)

        q_s, kv_s1, wkv_s, _, gates_s, u_s = _project(
            y_s, g_pre, w_r, tabs_s, lambda i: (i, 0), nb, pool_dim, False)

        q, kv, wkv, kva, gates, u, xc = _project(
            y_p, g_pre, w_r, tabs_p, lambda i: (i % nt_p, 0), tm_p, pool_dim, True)
        cmp_kv = _compress_prompt(xc, wcat, pos8, w2, batch, seq)
        o = _attend_prompt(q, cmp_kv, kva, gates, batch, seq)
        y1, h2 = _mix_prompt(o, u, y_p, pw, ps, wo, g_post, g_mlp_pre, batch, seq)
        y_p, win_rolled = _mlp(h2, y1, wu, wd, g_mlp_post, min(512, batch * seq),
                               win_rows, wkv_s.reshape(nb, n_w, HEAD_DIM))
        kv_p.append(kv.reshape(batch, seq, CACHE_SLOTS, KV_HEADS, HEAD_DIM))
        wp = min(WINDOW, seq)
        win_p.append(wkv.reshape(batch, seq, 2, KV_HEADS, HEAD_DIM)[:, seq - wp:])
        pool_p.append(u.reshape(batch, seq, pool_dim)[:, seq - POOL_BUF:])

        o = _attend_sample(page_table, cache_kv[l], q_s, kv_s1, wkv_s, win_rows, gates_s,
                           wcat, pos8, w2)
        y1, h2 = _mix_sample(o.reshape(nb, ATTN_DIM), u_s, state_pool[l], y_s, pw, ps, wo,
                             g_post, g_mlp_pre, past)
        y_s = _mlp(h2, y1, wu, wd, g_mlp_post, nb)
        kv_s.append(kv_s1.reshape(nb, 1, CACHE_SLOTS, KV_HEADS, HEAD_DIM))
        win_s.append(win_rolled.reshape(nb, wb, 2, KV_HEADS, HEAD_DIM))
        pool_s.append(jnp.concatenate([state_pool[l], u_s[:, None]], axis=1)[:, 1:])

    return (y_p.reshape(batch, seq, d), y_s.reshape(nb, 1, d),
            jnp.stack(kv_p), jnp.stack(kv_s), jnp.stack(win_p), jnp.stack(win_s),
            jnp.stack(pool_p), jnp.stack(pool_s))
```

```python
import functools

import numpy as np
import jax
import jax.numpy as jnp
from jax import lax
from jax.experimental import pallas as pl
from jax.experimental.pallas import tpu as pltpu

N_HEADS = 8
HEAD_DIM = 128
KV_HEADS = 2
HPG = N_HEADS // KV_HEADS
ATTN_DIM = N_HEADS * HEAD_DIM
KV_DIM = KV_HEADS * HEAD_DIM
N_BRANCH = 3
POOL_WINDOWS = (2, 4, 8, 16)
POOL_GROUPS = len(POOL_WINDOWS)
POOL_BUF = max(POOL_WINDOWS) - 1
ROT_DIM = HEAD_DIM // 4
ROT_HALF = ROT_DIM // 2
ROPE_THETA = 500000.0
CMP_LEN = 32
CMP_STRIDE = 16
SEL_BLOCK = 64
SEL_TOPK = 16
WINDOW = 512
EPS = 1e-6
SCALE = HEAD_DIM ** -0.5
FORCE_SCORE = 1e4
NEG_INF = -1e30

LANES = 128
CACHE_SLOTS = 4
ROW_W = CACHE_SLOTS * KV_DIM
CHUNK_W = CMP_STRIDE * ROW_W
CMP_IN = CMP_LEN * HEAD_DIM
CMP_HALF = CMP_STRIDE * HEAD_DIM
GATE_PAD = LANES
VMEM_LIMIT = 56 * 1024 * 1024

BF16 = jnp.bfloat16
F32 = jnp.float32


def _dot(a, b):
    return jnp.dot(a, b, preferred_element_type=F32)


def _dot_nt(a, b):
    return lax.dot_general(a, b, (((1,), (1,)), ((), ())), preferred_element_type=F32)


def _rms(x, g):
    return x * lax.rsqrt(jnp.mean(x * x, axis=-1, keepdims=True) + EPS) * g


def _params(n_axes):
    return pltpu.CompilerParams(
        dimension_semantics=("arbitrary",) * n_axes, vmem_limit_bytes=VMEM_LIMIT)


def _split_hi_lo(x):
    hi = x.astype(BF16)
    lo = (x - hi.astype(F32)).astype(BF16)
    return hi, lo


Q_OFF, KV_OFF, WKV_OFF = 0, ATTN_DIM, ATTN_DIM + 4 * KV_DIM


def _proj_kernel(x_ref, g_ref, w_ref, cos_ref, sa_ref, sb_ref, *rest, pool_dim, chunked):
    if chunked:
        q_ref, kv_ref, wkv_ref, kva_ref, gate_ref, u_ref, xc_ref, tmp_ref = rest
    else:
        q_ref, kv_ref, wkv_ref, kva_ref, gate_ref, u_ref = rest
    tm = x_ref.shape[0]
    u_off = WKV_OFF + 2 * KV_DIM
    gate_off = u_off + pool_dim
    h = _rms(x_ref[...], g_ref[...]).astype(BF16)
    cos, sa, sb = cos_ref[...], sa_ref[...], sb_ref[...]

    def rope(z):
        return (z * cos + pltpu.roll(z, LANES - ROT_HALF, 1) * sa
                + pltpu.roll(z, ROT_HALF, 1) * sb)

    zq = _dot(h, w_ref[:, Q_OFF:Q_OFF + ATTN_DIM])
    for hd in range(N_HEADS):
        sl = slice(hd * HEAD_DIM, (hd + 1) * HEAD_DIM)
        q_ref[:, sl] = rope(zq[:, sl]).astype(BF16)

    n_kv = CACHE_SLOTS * KV_HEADS
    zkv = _dot(h, w_ref[:, KV_OFF:KV_OFF + 4 * KV_DIM])
    for blk in range(n_kv):
        z = zkv[:, blk * HEAD_DIM:(blk + 1) * HEAD_DIM]
        if (blk // KV_HEADS) % 2 == 0:
            z = rope(z)
        kv_ref[pl.ds(blk, tm, stride=n_kv), :] = z
        if blk >= 2 * KV_HEADS:
            kva_ref[blk - 2 * KV_HEADS] = z.astype(BF16)
        elif chunked:
            tmp_ref[...] = z
            for r in range(CMP_STRIDE):
                xc_ref[blk, :, r * HEAD_DIM:(r + 1) * HEAD_DIM] = (
                    tmp_ref[pl.ds(r, tm // CMP_STRIDE, stride=CMP_STRIDE), :].astype(BF16))

    n_w = 2 * KV_HEADS
    zw = _dot(h, w_ref[:, WKV_OFF:WKV_OFF + 2 * KV_DIM])
    for blk in range(n_w):
        z = zw[:, blk * HEAD_DIM:(blk + 1) * HEAD_DIM]
        if blk < KV_HEADS:
            z = rope(z)
        wkv_ref[pl.ds(blk, tm, stride=n_w), :] = z
        kva_ref[2 * KV_HEADS + blk] = z.astype(BF16)

    u_ref[...] = _dot(h, w_ref[:, u_off:u_off + pool_dim])
    gl = _dot(h, w_ref[:, gate_off:gate_off + GATE_PAD])
    gate_ref[...] = 1.0 / (1.0 + jnp.exp(-gl))


def _project(x, gain, w_r, tables, table_index, tm, pool_dim, chunked):
    rows, d = x.shape
    n_proj = w_r.shape[1]
    n_kv = CACHE_SLOTS * KV_HEADS
    n_w = 2 * KV_HEADS
    row = lambda i: (i, 0)
    fixed = lambda i: (0, 0)
    tab_spec = pl.BlockSpec((tm, LANES), table_index)
    out_shape = [
        jax.ShapeDtypeStruct((rows, ATTN_DIM), BF16),
        jax.ShapeDtypeStruct((rows * n_kv, HEAD_DIM), F32),
        jax.ShapeDtypeStruct((rows * n_w, HEAD_DIM), F32),
        jax.ShapeDtypeStruct((n_w + n_kv // 2, rows, HEAD_DIM), BF16),
        jax.ShapeDtypeStruct((rows, GATE_PAD), F32),
        jax.ShapeDtypeStruct((rows, pool_dim), F32),
    ]
    out_specs = [
        pl.BlockSpec((tm, ATTN_DIM), row),
        pl.BlockSpec((tm * n_kv, HEAD_DIM), row),
        pl.BlockSpec((tm * n_w, HEAD_DIM), row),
        pl.BlockSpec((n_w + n_kv // 2, tm, HEAD_DIM), lambda i: (0, i, 0)),
        pl.BlockSpec((tm, GATE_PAD), row),
        pl.BlockSpec((tm, pool_dim), row),
    ]
    scratch = []
    if chunked:
        out_shape.append(jax.ShapeDtypeStruct((n_kv // 2, rows // CMP_STRIDE, CMP_HALF), BF16))
        out_specs.append(pl.BlockSpec((n_kv // 2, tm // CMP_STRIDE, CMP_HALF), lambda i: (0, i, 0)))
        scratch.append(pltpu.VMEM((tm, HEAD_DIM), F32))
    return pl.pallas_call(
        functools.partial(_proj_kernel, pool_dim=pool_dim, chunked=chunked),
        grid=(rows // tm,),
        in_specs=[
            pl.BlockSpec((tm, d), row),
            pl.BlockSpec((1, d), fixed),
            pl.BlockSpec((d, n_proj), fixed),
            tab_spec, tab_spec, tab_spec,
        ],
        out_specs=tuple(out_specs),
        out_shape=tuple(out_shape),
        scratch_shapes=scratch,
        compiler_params=_params(1),
        name="project",
    )(x, gain, w_r, *tables)


def _rope_tables(pos):
    inv = jnp.power(ROPE_THETA, -jnp.arange(ROT_HALF, dtype=F32) * (2.0 / ROT_DIM))
    ang = pos.astype(F32)[:, None] * inv[None, :]
    cos, sin = jnp.cos(ang), jnp.sin(ang)
    n = pos.shape[0]
    rest = LANES - ROT_DIM
    c = jnp.concatenate([cos, cos, jnp.ones((n, rest), F32)], axis=1)
    sa = jnp.concatenate([-sin, jnp.zeros((n, LANES - ROT_HALF), F32)], axis=1)
    sb = jnp.concatenate([jnp.zeros((n, ROT_HALF), F32), sin, jnp.zeros((n, rest), F32)], axis=1)
    return c, sa, sb


def _gelu_tanh(x):
    return 0.5 * x * (1.0 + jnp.tanh(0.7978845608028654 * (x + 0.044715 * (x * x * x))))


def _compress_first(x, wcat, pos8):
    ab = _dot(x, wcat)
    pa = _dot(pos8[:, :CMP_HALF], wcat)[0:1, :HEAD_DIM]
    pb = _dot(pos8[:, CMP_HALF:], wcat)[0:1, HEAD_DIM:]
    return ab, pa + pb


def _compress_hidden(ab, pos_term):
    rows = ab.shape[0]
    b_next = pltpu.roll(ab[:, HEAD_DIM:], rows - 1, 0)
    return _gelu_tanh(ab[:, :HEAD_DIM] + b_next + pos_term).astype(BF16)


def _compress_kernel(x_ref, wcat_ref, pos_ref, w2_ref, o_ref):
    g, _, nck, width = x_ref.shape
    x = x_ref[...].reshape(g * nck, width)
    hid = _compress_hidden(*_compress_first(x, wcat_ref[0], pos_ref[0]))
    o_ref[...] = _dot(hid, w2_ref[0]).astype(BF16).reshape(o_ref.shape)


def _compress_prompt(xc, wcat, pos8, w2, batch, seq):
    nck = seq // CMP_STRIDE
    x = xc.reshape(xc.shape[0], batch, nck, CMP_HALF)
    return pl.pallas_call(
        _compress_kernel,
        grid=(batch, 2),
        in_specs=[
            pl.BlockSpec((KV_HEADS, 1, nck, CMP_HALF), lambda b, s: (s, b, 0, 0)),
            pl.BlockSpec((1, CMP_HALF, 2 * HEAD_DIM), lambda b, s: (s, 0, 0)),
            pl.BlockSpec((1, 8, CMP_IN), lambda b, s: (s, 0, 0)),
            pl.BlockSpec((1, HEAD_DIM, HEAD_DIM), lambda b, s: (s, 0, 0)),
        ],
        out_specs=pl.BlockSpec((1, 1, KV_HEADS, nck, HEAD_DIM), lambda b, s: (b, s, 0, 0, 0)),
        out_shape=jax.ShapeDtypeStruct((batch, 2, KV_HEADS, nck, HEAD_DIM), BF16),
        compiler_params=_params(2),
        name="compress_prompt",
    )(x, wcat, pos8, w2)


def _topk_rows(score, jidx, topk):
    rank = jnp.zeros_like(score)
    for j in range(score.shape[0]):
        bj = score[j:j + 1, :]
        tie = jnp.where(jidx > j, 1.0, 0.0)
        rank = rank + jnp.where(bj > score, 1.0, jnp.where(bj == score, tie, 0.0))
    return jnp.where(rank < topk, 1.0, 0.0)


def _softmax_parts(s, ok):
    s = jnp.where(ok, s, NEG_INF)
    m = jnp.max(s, axis=-1, keepdims=True)
    e = jnp.where(ok, jnp.exp(s - m), 0.0)
    d = jnp.sum(e, axis=-1, keepdims=True)
    return e, jnp.where(d > 0, d, 1.0)


EXP2_SCALE = SCALE * 1.4426950408889634
HEAD_PAIRS = HPG // 2


def _attn_prompt_kernel(q_ref, kc_ref, vc_ref, ks_ref, vs_ref, kw_ref, vw_ref, gate_ref,
                        mt_ref, eye_ref, o_ref,
                        vct_ref, vst_ref, vwt_ref, sel_ref, gt_ref, acc_ref,
                        *, tq, tk, seq, topk):
    i = pl.program_id(1)
    t0 = i * tq
    n_sel = mt_ref.shape[0]
    pair_w = 2 * tq
    eye = eye_ref[...]
    groups = range(KV_HEADS)
    pairs = range(KV_HEADS * HEAD_PAIRS)
    group_of = [hp // HEAD_PAIRS for hp in pairs]

    @pl.when(i == 0)
    def _():
        for g in groups:
            vct_ref[g] = _dot_nt(eye, vc_ref[0, 0, g]).astype(BF16)
            vst_ref[g] = _dot_nt(eye, vs_ref[g]).astype(BF16)
            vwt_ref[g] = _dot_nt(eye, vw_ref[g]).astype(BF16)

    q_all = q_ref[...]
    q_pairs = [jnp.concatenate([q_all[:, (2 * hp) * HEAD_DIM:(2 * hp + 1) * HEAD_DIM],
                                q_all[:, (2 * hp + 1) * HEAD_DIM:(2 * hp + 2) * HEAD_DIM]], axis=0)
               for hp in pairs]

    def both_heads(x):
        return jnp.concatenate([x, x], axis=1)

    def tpos(n_keys):
        return t0 + lax.broadcasted_iota(jnp.int32, (n_keys, tq), 1)

    def kidx(n_keys):
        return lax.broadcasted_iota(jnp.int32, (n_keys, tq), 0)


    ncp = kc_ref.shape[3]
    wk = min(WINDOW + tq, seq)
    ws = pl.multiple_of(jnp.maximum(t0 + tq - wk, 0), tq)
    s_cmp = [_dot_nt(kc_ref[0, 0, group_of[hp]], q_pairs[hp]) for hp in pairs]
    s_win = [_dot_nt(kw_ref[group_of[hp], pl.ds(ws, wk), :], q_pairs[hp]) for hp in pairs]

    ok = both_heads(jnp.where(kidx(ncp) * CMP_STRIDE + (CMP_LEN - 1) <= tpos(ncp), 1.0, 0.0)) > 0.5
    p_cmp, p_sum = [], [None] * KV_HEADS
    for hp in pairs:
        s = jnp.where(ok, s_cmp[hp], NEG_INF)
        m = jnp.max(s, axis=0, keepdims=True)
        e = jnp.where(ok, jnp.exp2((s - m) * EXP2_SCALE), 0.0)
        d = jnp.sum(e, axis=0, keepdims=True)
        p = e / jnp.where(d > 0, d, 1.0)
        p_cmp.append(p.astype(BF16))
        ph = p[:, :tq] + p[:, tq:]
        g = group_of[hp]
        p_sum[g] = ph if p_sum[g] is None else p_sum[g] + ph

    mt = mt_ref[...]
    split = [_split_hi_lo(p_sum[g]) for g in groups]
    imp = [_dot(mt, split[g][0]) + _dot(mt, split[g][1]) for g in groups]
    o_cmp = [_dot(vct_ref[group_of[hp]], p_cmp[hp]) for hp in pairs]

    kpos = ws + kidx(wk)
    bias = both_heads(jnp.where(kpos <= tpos(wk),
                                jnp.where(kpos > tpos(wk) - WINDOW, 0.0, NEG_INF), NEG_INF))
    p_win, l_win = [], []
    for hp in pairs:
        s = s_win[hp] + bias
        p = jnp.exp2((s - jnp.max(s, axis=0, keepdims=True)) * EXP2_SCALE)
        l_win.append(jnp.sum(p, axis=0, keepdims=True))
        p_win.append(p.astype(BF16))
    o_win = [_dot(vwt_ref[group_of[hp], :, pl.ds(ws, wk)], p_win[hp]) for hp in pairs]
    o_win = [o_win[hp] / l_win[hp] for hp in pairs]

    jidx = kidx(n_sel)
    jt = tpos(n_sel) // SEL_BLOCK
    forced = jnp.where(jidx == 0, 1.0, jnp.where(jidx == jt, 1.0, jnp.where(jidx == jt - 1, 1.0, 0.0)))
    for g in groups:
        score = jnp.where(forced > 0.5, FORCE_SCORE, jnp.where(jidx <= jt, imp[g], -1.0))
        sel_ref[g] = _topk_rows(score, jidx, topk)

    acc_ref[...] = jnp.zeros_like(acc_ref)

    def sel_step(kb, carry):
        k0 = pl.multiple_of(kb * tk, tk)
        blk0 = kb * (tk // SEL_BLOCK)
        causal = k0 + kidx(tk) <= tpos(tk)
        bias = []
        for g in groups:
            chosen = jnp.concatenate(
                [jnp.broadcast_to(sel_ref[g, pl.ds(blk0 + j, 1), :], (SEL_BLOCK, tq))
                 for j in range(tk // SEL_BLOCK)], axis=0)
            bias.append(both_heads(
                jnp.where(causal, jnp.where(chosen > 0.5, 0.0, NEG_INF), NEG_INF)))
        ss = [_dot_nt(ks_ref[group_of[hp], pl.ds(k0, tk), :], q_pairs[hp]) + bias[group_of[hp]]
              for hp in pairs]
        out, ps, alphas = [], [], []
        for hp in pairs:
            m, l = carry[2 * hp], carry[2 * hp + 1]
            m_new = jnp.maximum(m, jnp.max(ss[hp], axis=0, keepdims=True))
            p = jnp.exp2((ss[hp] - m_new) * EXP2_SCALE)
            alpha = jnp.exp2((m - m_new) * EXP2_SCALE)
            out += [m_new, alpha * l + jnp.sum(p, axis=0, keepdims=True)]
            ps.append(p.astype(BF16))
            alphas.append(alpha)
        pvs = [_dot(vst_ref[group_of[hp], :, pl.ds(k0, tk)], ps[hp]) for hp in pairs]
        for hp in pairs:
            acc_ref[hp] = alphas[hp] * acc_ref[hp] + pvs[hp]
        return tuple(out)

    n_kb = (t0 + tq + tk - 1) // tk
    init = (jnp.full((1, pair_w), NEG_INF, F32), jnp.zeros((1, pair_w), F32)) * len(pairs)
    stats = lax.fori_loop(0, n_kb, sel_step, init)
    o_sel = [acc_ref[hp] / stats[2 * hp + 1] for hp in pairs]

    gt_ref[...] = gate_ref[...].T
    o_t = []
    for h in range(N_HEADS):
        hp, lanes = h // 2, slice((h % 2) * tq, (h % 2 + 1) * tq)
        col = h * N_BRANCH
        o_t.append((gt_ref[col:col + 1, :] * o_cmp[hp][:, lanes]
                    + gt_ref[col + 1:col + 2, :] * o_sel[hp][:, lanes]
                    + gt_ref[col + 2:col + 3, :] * o_win[hp][:, lanes]).astype(BF16))
    outs = [_dot_nt(eye, o_t[h]) for h in range(N_HEADS)]
    for h in range(N_HEADS):
        o_ref[:, h * HEAD_DIM:(h + 1) * HEAD_DIM] = outs[h].astype(BF16)


def _cmp_to_sel(n_cmp_pad, n_cmp, n_sel):
    cs = np.arange(n_cmp_pad)[:, None] * CMP_STRIDE
    ss = np.arange(n_sel)[None, :] * SEL_BLOCK
    hit = (cs < ss + SEL_BLOCK) & (cs + CMP_LEN > ss) & (np.arange(n_cmp_pad)[:, None] < n_cmp)
    return hit.astype(np.float32)


def _attend_prompt(q, cmp_kv, kva, gates, batch, seq):
    tq = min(128, seq)
    tk = min(512, seq)
    nq = seq // tq
    ncp = seq // CMP_STRIDE
    n_sel = -(-seq // SEL_BLOCK)
    topk = min(SEL_TOPK, n_sel)
    assert tq == HEAD_DIM and seq % tk == 0 and WINDOW % tq == 0
    mt = jnp.asarray(_cmp_to_sel(ncp, ncp - 1, n_sel).T, BF16)
    eye = jnp.asarray(np.eye(tq, dtype=np.float32), BF16)
    kv_spec = lambda slot: pl.BlockSpec((KV_HEADS, seq, HEAD_DIM), lambda b, i: (slot, b, 0))
    cmp_spec = lambda s: pl.BlockSpec(
        (1, 1, KV_HEADS, ncp, HEAD_DIM), lambda b, i: (b, s, 0, 0, 0))
    full = lambda a: pl.BlockSpec(a.shape, lambda b, i: (0, 0))
    return pl.pallas_call(
        functools.partial(_attn_prompt_kernel, tq=tq, tk=tk, seq=seq, topk=topk),
        grid=(batch, nq),
        in_specs=[
            pl.BlockSpec((tq, ATTN_DIM), lambda b, i: (b * nq + i, 0)),
            cmp_spec(0), cmp_spec(1),
            kv_spec(0), kv_spec(1), kv_spec(2), kv_spec(3),
            pl.BlockSpec((tq, GATE_PAD), lambda b, i: (b * nq + i, 0)),
            full(mt), full(eye),
        ],
        out_specs=pl.BlockSpec((tq, ATTN_DIM), lambda b, i: (b * nq + i, 0)),
        out_shape=jax.ShapeDtypeStruct((batch * seq, ATTN_DIM), BF16),
        scratch_shapes=[
            pltpu.VMEM((KV_HEADS, HEAD_DIM, ncp), BF16),
            pltpu.VMEM((KV_HEADS, HEAD_DIM, seq), BF16),
            pltpu.VMEM((KV_HEADS, HEAD_DIM, seq), BF16),
            pltpu.VMEM((KV_HEADS, n_sel, tq), F32),
            pltpu.VMEM((GATE_PAD, tq), F32),
            pltpu.VMEM((N_HEADS // 2, HEAD_DIM, 2 * tq), F32),
        ],
        compiler_params=_params(2),
        name="attend_prompt",
    )(q, cmp_kv, cmp_kv, kva, kva, kva, kva, gates, mt, eye)


def _sample_group(q_ref, kvn_ref, wn_ref, win_ref, gate_ref, wcat_ref, pos_ref, w2_ref, ms_ref,
                  ek_ref, o_ref, rows_ref, base, *, n_seq, past, topk, n_sel):
    n_kv = CACHE_SLOTS * KV_HEADS
    n_w = 2 * KV_HEADS
    nch = past // CMP_STRIDE
    wb = win_ref.shape[1] // n_w
    seqs = range(n_seq)

    def chunk_rows(s, cache_slot, g, r):
        rows = rows_ref.reshape(n_seq, CMP_STRIDE, nch * n_kv, HEAD_DIM)
        return rows[s, r, pl.ds(cache_slot * KV_HEADS + g, nch, stride=n_kv), :]

    row8 = lax.broadcasted_iota(jnp.int32, (N_HEADS, 1), 0)
    in_g0 = row8 < HPG

    def by_group(x0, x1):
        return jnp.where(in_g0, x0, x1)

    q8 = [q_ref[base + s] for s in seqs]
    qf = [q8[s].astype(F32) for s in seqs]
    groups = range(KV_HEADS)

    def scores(s, keys):
        return by_group(_dot_nt(q8[s], keys[0]), _dot_nt(q8[s], keys[1])) * SCALE

    def cached(s, cache_slot, g):
        return jnp.concatenate([chunk_rows(s, cache_slot, g, r).astype(BF16)
                                for r in range(CMP_STRIDE)], axis=0)

    def compress_input(cache_slot):
        return jnp.concatenate(
            [jnp.concatenate([chunk_rows(s, cache_slot, g, r).astype(BF16)
                              for r in range(CMP_STRIDE)], axis=1)
             for s in seqs for g in groups], axis=0)

    first = [_compress_first(compress_input(cs), wcat_ref[cs], pos_ref[cs]) for cs in range(2)]
    s_sel = [scores(s, [cached(s, 2, g) for g in groups]) for s in seqs]
    s_win = [scores(s, [win_ref[base + s, pl.ds(g, wb, stride=n_w), :].astype(BF16)
                        for g in groups])
             for s in seqs]
    kc, vc = [_dot(_compress_hidden(*first[cs]), w2_ref[cs]).astype(BF16) for cs in range(2)]

    def compressed(x, s, g):
        r0 = (s * KV_HEADS + g) * nch
        return x[r0:r0 + nch]

    lane_n = lax.broadcasted_iota(jnp.int32, (N_HEADS, nch), 1)
    ok = lane_n * CMP_STRIDE + (CMP_LEN - 1) <= past
    s_cmp = [by_group(*[_dot_nt(q8[s], compressed(kc, s, g)) for g in groups]) * SCALE
             for s in seqs]
    p_cmp = []
    for s in seqs:
        e, d = _softmax_parts(s_cmp[s], ok)
        p_cmp.append(e / d)
    o_cmp = [by_group(*[_dot(p_cmp[s].astype(BF16), compressed(vc, s, g)) for g in groups])
             for s in seqs]

    jl = lax.broadcasted_iota(jnp.int32, (N_HEADS, LANES), 1)
    jt = past // SEL_BLOCK
    forced = jnp.where(jl == 0, 1.0, jnp.where(jl == jt, 1.0, jnp.where(jl == jt - 1, 1.0, 0.0)))
    ii = lax.broadcasted_iota(jnp.int32, (LANES, LANES), 0)
    jj = lax.broadcasted_iota(jnp.int32, (LANES, LANES), 1)
    tie = jnp.where(ii < jj, 1.0, 0.0)
    imp = []
    for s in seqs:
        p_g = [jnp.sum(p_cmp[s][g * HPG:(g + 1) * HPG], axis=0, keepdims=True) for g in groups]
        p2 = jnp.concatenate(p_g + [jnp.zeros((N_HEADS - KV_HEADS, nch), F32)], axis=0)
        p_hi, p_lo = _split_hi_lo(p2)
        imp.append(_dot(p_hi, ms_ref[...]) + _dot(p_lo, ms_ref[...]))
    sel2 = []
    for s in seqs:
        score = jnp.where(forced > 0.5, FORCE_SCORE, jnp.where(jl <= jt, imp[s], -1.0))
        score = jnp.where(jl < n_sel, score, -2.0)
        sel_rows = []
        for g in groups:
            srow = jnp.broadcast_to(score[g:g + 1, :], (LANES, LANES))
            scol = jnp.sum(jnp.where(ii == jj, srow, 0.0), axis=1, keepdims=True)
            beats = jnp.where(scol > srow, 1.0, jnp.where(scol == srow, tie, 0.0))
            rank = jnp.sum(beats, axis=0, keepdims=True)
            sel_rows.append(jnp.where(rank < topk, 1.0, 0.0))
        sel2.append(jnp.concatenate(sel_rows + [jnp.zeros((N_HEADS - KV_HEADS, LANES), F32)], axis=0))
    chunk_ok2 = [_dot(sel2[s].astype(BF16), ek_ref[...]) for s in seqs]

    def new_row(ref, s, idx0):
        x = by_group(ref[base + s, idx0:idx0 + 1, :], ref[base + s, idx0 + 1:idx0 + 2, :])
        return x.astype(BF16).astype(F32)

    def weights(s, scores_s, ok, k_new, new_ok):
        s_new = jnp.sum(qf[s] * k_new, axis=-1, keepdims=True) * SCALE
        sc = jnp.where(ok, scores_s, NEG_INF)
        s_new = jnp.where(new_ok, s_new, NEG_INF)
        m = jnp.maximum(jnp.max(sc, axis=-1, keepdims=True), s_new)
        e = jnp.where(ok, jnp.exp(sc - m), 0.0)
        e_new = jnp.where(new_ok, jnp.exp(s_new - m), 0.0)
        d = jnp.sum(e, axis=-1, keepdims=True) + e_new
        return e.astype(BF16), e_new, jnp.where(d > 0, d, 1.0)

    def weighted(w, vals, v_new):
        eb, e_new, d = w
        return (by_group(_dot(eb, vals[0]), _dot(eb, vals[1])) + e_new * v_new) / d

    kpos = past - wb + lax.broadcasted_iota(jnp.int32, (N_HEADS, wb), 1)
    w_sel, w_win = [], []
    for s in seqs:
        chunk_ok = by_group(chunk_ok2[s][0:1], chunk_ok2[s][1:2])
        key_ok = jnp.concatenate([chunk_ok] * CMP_STRIDE, axis=1)
        new_ok = by_group(*[jnp.sum(jnp.where(jl[0:1] == jt, sel2[s][g:g + 1], 0.0), axis=1,
                                    keepdims=True) for g in groups])
        w_sel.append(weights(s, s_sel[s], key_ok > 0.5, new_row(kvn_ref, s, 2 * KV_HEADS),
                             new_ok > 0.5))
        w_win.append(weights(s, s_win[s], kpos > past - WINDOW, new_row(wn_ref, s, 0), row8 >= 0))
    o_sel = [weighted(w_sel[s], [cached(s, 3, g) for g in groups],
                      new_row(kvn_ref, s, 3 * KV_HEADS)) for s in seqs]
    o_win = [weighted(w_win[s],
                      [win_ref[base + s, pl.ds(KV_HEADS + g, wb, stride=n_w), :].astype(BF16)
                       for g in groups], new_row(wn_ref, s, KV_HEADS)) for s in seqs]

    lane = lax.broadcasted_iota(jnp.int32, (N_HEADS, GATE_PAD), 1)
    for s in seqs:
        gates = jnp.broadcast_to(gate_ref[base + s], (N_HEADS, GATE_PAD))

        def gate(br):
            return jnp.sum(jnp.where(lane == row8 * N_BRANCH + br, gates, 0.0), axis=-1,
                           keepdims=True)

        o_ref[base + s] = (gate(0) * o_cmp[s] + gate(1) * o_sel[s]
                           + gate(2) * o_win[s]).astype(BF16)


def _attn_sample_kernel(pt_ref, cache_ref, q_ref, kvn_ref, wn_ref, win_ref, gate_ref,
                        wcat_ref, pos_ref, w2_ref, ms_ref, ek_ref, o_ref, buf_a, buf_b, sem,
                        *, n_pages, page, n_seq, **consts):
    t = pl.program_id(0)
    n_t = pl.num_programs(0)
    cpp = page // CMP_STRIDE
    refs = (q_ref, kvn_ref, wn_ref, win_ref, gate_ref, wcat_ref, pos_ref, w2_ref, ms_ref,
            ek_ref, o_ref)

    def for_group_copies(group, buf, slot, act):
        def per_page(pi, carry):
            dst0 = pl.multiple_of(pi * cpp, cpp)
            for s in range(n_seq):
                src0 = pl.multiple_of(pt_ref[group * n_seq + s, pi] * cpp, cpp)
                for r in range(CMP_STRIDE):
                    act(pltpu.make_async_copy(cache_ref.at[pl.ds(src0, cpp), r],
                                              buf.at[s, r, pl.ds(dst0, cpp)], sem.at[slot]))
            return carry
        lax.fori_loop(0, n_pages, per_page, 0)

    def start_group(group, buf, slot):
        for_group_copies(group, buf, slot, lambda c: c.start())

    def wait_group(buf, slot):
        for_group_copies(0, buf, slot, lambda c: c.wait())

    @pl.when(t == 0)
    def _():
        start_group(0, buf_a, 0)

    start_group(2 * t + 1, buf_b, 1)
    wait_group(buf_a, 0)
    _sample_group(*refs, buf_a, 0, n_seq=n_seq, **consts)

    start_group(jnp.minimum(2 * t + 2, 2 * n_t - 1), buf_a, 0)
    wait_group(buf_b, 1)
    _sample_group(*refs, buf_b, n_seq, n_seq=n_seq, **consts)

    @pl.when(t == n_t - 1)
    def _():
        wait_group(buf_a, 0)


def _attend_sample(page_table, cache, q, kv_new, wkv_new, win_rows, gates, wcat, pos8, w2):
    nb, n_pages = page_table.shape
    n_phys, page = cache.shape[:2]
    n_kv = CACHE_SLOTS * KV_HEADS
    n_w = 2 * KV_HEADS
    past = n_pages * page
    nch = past // CMP_STRIDE
    wb = win_rows.shape[1] // n_w
    n_sel = -(-(past + 1) // SEL_BLOCK)
    n_cmp = (past + 1) // CMP_STRIDE - 1
    topk = min(SEL_TOPK, n_sel)
    assert n_sel <= LANES and nch * CMP_STRIDE == past
    ms = np.zeros((nch, LANES), np.float32)
    ms[:, :n_sel] = _cmp_to_sel(nch, n_cmp, n_sel)
    ek = (np.arange(nch)[None, :] * CMP_STRIDE // SEL_BLOCK == np.arange(LANES)[:, None])
    ms = jnp.asarray(ms, BF16)
    ek = jnp.asarray(ek.astype(np.float32), BF16)
    n_seq = 2 if nb % 4 == 0 else 1
    per_step = 2 * n_seq
    assert nb % per_step == 0
    seq3 = lambda n: pl.BlockSpec((per_step, n, HEAD_DIM), lambda b, pt: (b, 0, 0))
    full3 = lambda a: pl.BlockSpec(a.shape, lambda b, pt: (0, 0, 0))
    full2 = lambda a: pl.BlockSpec(a.shape, lambda b, pt: (0, 0))
    group_rows = pltpu.VMEM((n_seq, CMP_STRIDE, nch, n_kv, HEAD_DIM), F32)
    grid_spec = pltpu.PrefetchScalarGridSpec(
        num_scalar_prefetch=1,
        grid=(nb // per_step,),
        in_specs=[
            pl.BlockSpec(memory_space=pl.ANY),
            seq3(N_HEADS), seq3(n_kv), seq3(n_w), seq3(wb * n_w),
            pl.BlockSpec((per_step, 1, GATE_PAD), lambda b, pt: (b, 0, 0)),
            full3(wcat), full3(pos8), full3(w2), full2(ms), full2(ek),
        ],
        out_specs=seq3(N_HEADS),
        scratch_shapes=[group_rows, group_rows, pltpu.SemaphoreType.DMA((2,))],
    )
    return pl.pallas_call(
        functools.partial(_attn_sample_kernel, n_pages=n_pages, page=page, n_seq=n_seq,
                          past=past, topk=topk, n_sel=n_sel),
        grid_spec=grid_spec,
        out_shape=jax.ShapeDtypeStruct((nb, N_HEADS, HEAD_DIM), BF16),
        compiler_params=_params(1),
        name="attend_sample",
    )(page_table, cache.reshape(n_phys * page // CMP_STRIDE, CMP_STRIDE, n_kv, HEAD_DIM),
      q.reshape(nb, N_HEADS, HEAD_DIM), kv_new.reshape(nb, n_kv, HEAD_DIM),
      wkv_new.reshape(nb, n_w, HEAD_DIM), win_rows,
      gates.reshape(nb, 1, GATE_PAD), wcat, pos8, w2, ms, ek)


def _mix_tail(o, diffs, x, pw_ref, ps_ref, wo_ref, gpost_ref, gpre_ref, y1_ref, h2_ref):
    gw = diffs[0].shape[1]
    ys = [(_dot(diffs[g].astype(BF16), pw_ref[g]) * ps_ref[:, g * gw:(g + 1) * gw]).astype(BF16)
          for g in range(POOL_GROUPS)]
    cat = jnp.concatenate([o] + ys, axis=1)
    m = _dot(cat, wo_ref[...])
    y1 = x + _rms(m, gpost_ref[...])
    y1_ref[...] = y1
    h2_ref[...] = _rms(y1, gpre_ref[...]).astype(BF16)


def _mix_prompt_kernel(o_ref, u_ref, halo_ref, x_ref, pw_ref, ps_ref, wo_ref, gpost_ref, gpre_ref,
                       y1_ref, h2_ref, *, tm):
    i = pl.program_id(1)
    halo_rows = halo_ref.shape[0]
    halo = jnp.where(i > 0, halo_ref[...], 0.0)
    u = u_ref[...]
    uext = jnp.concatenate([halo, u], axis=0)
    n_ext = uext.shape[0]
    gw = u.shape[1] // POOL_GROUPS
    tpos = i * tm + lax.broadcasted_iota(jnp.int32, (tm, 1), 0)
    diffs = []
    for g, w in enumerate(POOL_WINDOWS):
        s = uext[:, g * gw:(g + 1) * gw]
        k = 1
        while k < w:
            s = s + pltpu.roll(s, k, 0)
            k *= 2
        cnt = jnp.minimum(w, tpos + 1).astype(F32)
        diffs.append(s[halo_rows:n_ext] / cnt - u[:, g * gw:(g + 1) * gw])
    _mix_tail(o_ref[...], diffs, x_ref[...], pw_ref, ps_ref, wo_ref, gpost_ref, gpre_ref,
              y1_ref, h2_ref)


def _mix_sample_kernel(o_ref, u_ref, st_ref, x_ref, pw_ref, ps_ref, wo_ref, gpost_ref, gpre_ref,
                       y1_ref, h2_ref, *, past):
    u = u_ref[...]
    c = u.shape[1]
    gw = c // POOL_GROUPS
    n_hist = st_ref.shape[1] // c
    diffs = []
    for g, w in enumerate(POOL_WINDOWS):
        un = u[:, g * gw:(g + 1) * gw]
        s = un
        for back in range(1, w):
            r = n_hist - back
            s = s + st_ref[:, r * c + g * gw:r * c + (g + 1) * gw]
        diffs.append(s / float(min(w, past + 1)) - un)
    _mix_tail(o_ref[...], diffs, x_ref[...], pw_ref, ps_ref, wo_ref, gpost_ref, gpre_ref,
              y1_ref, h2_ref)


def _mix_specs(tm, d, c, pool_w, idx):
    fixed2 = lambda *a: (0, 0)
    fixed3 = lambda *a: (0, 0, 0)
    weights = [
        pl.BlockSpec(pool_w.shape, fixed3),
        pl.BlockSpec((1, c), fixed2),
        pl.BlockSpec((d, d), fixed2),
        pl.BlockSpec((1, d), fixed2),
        pl.BlockSpec((1, d), fixed2),
    ]
    outs = (pl.BlockSpec((tm, d), idx), pl.BlockSpec((tm, d), idx))
    return weights, outs


def _mix_prompt(o, u, x, pool_w, pool_scale, w_o, g_post, g_pre, batch, seq):
    rows, d = x.shape
    c = u.shape[1]
    tm = min(512, seq)
    nt = seq // tm
    halo = 16
    assert halo >= POOL_BUF and seq % tm == 0 and tm % halo == 0
    idx = lambda b, i: (b * nt + i, 0)
    halo_idx = lambda b, i: (jnp.maximum((b * nt + i) * (tm // halo) - 1, 0), 0)
    weights, outs = _mix_specs(tm, d, c, pool_w, idx)
    return pl.pallas_call(
        functools.partial(_mix_prompt_kernel, tm=tm),
        grid=(batch, nt),
        in_specs=[pl.BlockSpec((tm, ATTN_DIM), idx), pl.BlockSpec((tm, c), idx),
                  pl.BlockSpec((halo, c), halo_idx), pl.BlockSpec((tm, d), idx)] + weights,
        out_specs=outs,
        out_shape=(jax.ShapeDtypeStruct((rows, d), F32), jax.ShapeDtypeStruct((rows, d), BF16)),
        compiler_params=_params(2),
        name="mix_prompt",
    )(o, u, u, x, pool_w, pool_scale, w_o, g_post, g_pre)


def _mix_sample(o, u, pool_state, x, pool_w, pool_scale, w_o, g_post, g_pre, past):
    rows, d = x.shape
    c = u.shape[1]
    tm = rows
    idx = lambda i: (i, 0)
    weights, outs = _mix_specs(tm, d, c, pool_w, idx)
    st = pool_state.reshape(rows, -1)
    return pl.pallas_call(
        functools.partial(_mix_sample_kernel, past=past),
        grid=(rows // tm,),
        in_specs=[pl.BlockSpec((tm, ATTN_DIM), idx), pl.BlockSpec((tm, c), idx),
                  pl.BlockSpec((tm, st.shape[1]), idx), pl.BlockSpec((tm, d), idx)] + weights,
        out_specs=outs,
        out_shape=(jax.ShapeDtypeStruct((rows, d), F32), jax.ShapeDtypeStruct((rows, d), BF16)),
        compiler_params=_params(1),
        name="mix_sample",
    )(o, u, st, x, pool_w, pool_scale, w_o, g_post, g_pre)


def _mlp_kernel(h_ref, wu_ref, wd_ref, y1_ref, g_ref, *rest, shift):
    if shift:
        state_ref, fresh_ref, y_ref, rolled_ref, acc_ref = rest
        n_new = fresh_ref.shape[1]
        n_keep = state_ref.shape[1] - n_new
        rolled_ref[:, pl.ds(0, n_keep), :] = state_ref[:, pl.ds(n_new, n_keep), :]
        rolled_ref[:, pl.ds(n_keep, n_new), :] = fresh_ref[...]
    else:
        y_ref, acc_ref = rest
    j = pl.program_id(1)

    @pl.when(j == 0)
    def _():
        acc_ref[...] = jnp.zeros_like(acc_ref)

    a = jnp.maximum(_dot(h_ref[...], wu_ref[...]), 0.0)
    acc_ref[...] += _dot((a * a).astype(BF16), wd_ref[...])

    @pl.when(j == pl.num_programs(1) - 1)
    def _():
        y_ref[...] = y1_ref[...] + _rms(acc_ref[...], g_ref[...])


def _mlp(h2, y1, w_up, w_down, gain, tm, state=None, fresh=None):
    rows, d = y1.shape
    ff = w_up.shape[1]
    tf = min(1024, ff)
    n_i, n_j = rows // tm, ff // tf
    row = lambda i, j: (i, 0)
    shift = state is not None
    y_spec = pl.BlockSpec((tm, d), row)
    y_shape = jax.ShapeDtypeStruct((rows, d), F32)
    extra_in, out_specs, out_shape = [], y_spec, y_shape
    if shift:
        per_step = state.shape[0] // (n_i * n_j)
        assert per_step * n_i * n_j == state.shape[0]
        blk = lambda a: pl.BlockSpec((per_step,) + a.shape[1:], lambda i, j: (i * n_j + j, 0, 0))
        extra_in = [blk(state), blk(fresh)]
        out_specs = (y_spec, blk(state))
        out_shape = (y_shape, jax.ShapeDtypeStruct(state.shape, state.dtype))
    return pl.pallas_call(
        functools.partial(_mlp_kernel, shift=shift),
        grid=(n_i, n_j),
        in_specs=[
            pl.BlockSpec((tm, d), row),
            pl.BlockSpec((d, tf), lambda i, j: (0, j)),
            pl.BlockSpec((tf, d), lambda i, j: (j, 0)),
            pl.BlockSpec((tm, d), row),
            pl.BlockSpec((1, d), lambda i, j: (0, 0)),
        ] + extra_in,
        out_specs=out_specs,
        out_shape=out_shape,
        scratch_shapes=[pltpu.VMEM((tm, d), F32)],
        compiler_params=_params(2),
        name="mlp",
    )(h2, w_up, w_down, y1, gain, *((state, fresh) if shift else ()))


def _layer_weights(w_in, cmp_pos_k, cmp_w1_k, cmp_w2_k, cmp_pos_v, cmp_w1_v, cmp_w2_v,
                   pool_w, w_o, w_up, w_down, pool_dim):
    gate_lo = WKV_OFF + 2 * KV_DIM
    gate_hi = gate_lo + N_BRANCH * N_HEADS
    w_r = jnp.concatenate(
        [w_in[:, :gate_lo], w_in[:, gate_hi:gate_hi + pool_dim],
         jnp.pad(w_in[:, gate_lo:gate_hi], ((0, 0), (0, GATE_PAD - N_BRANCH * N_HEADS)))],
        axis=1).astype(BF16)
    wcat = jnp.stack([jnp.concatenate([w1[:CMP_HALF], w1[CMP_HALF:]], axis=1)
                      for w1 in (cmp_w1_k, cmp_w1_v)]).astype(BF16)
    pos8 = jnp.stack([jnp.pad(p.reshape(1, CMP_IN), ((0, 7), (0, 0)))
                      for p in (cmp_pos_k, cmp_pos_v)]).astype(BF16)
    w2 = jnp.stack([cmp_w2_k, cmp_w2_v]).astype(BF16)
    return (w_r, wcat, pos8, w2, pool_w.astype(BF16), w_o.astype(BF16),
            w_up.astype(BF16), w_down.astype(BF16))


def kernel(x_prompt, x_sample, cache_kv, state_win_kv, state_pool, page_table, norm_mix_pre, w_in,
           cmp_pos_k, cmp_w1_k, cmp_w2_k, cmp_pos_v, cmp_w1_v, cmp_w2_v, pool_w, pool_scale, w_o,
           norm_mix_post, norm_mlp_pre, w_up, w_down, norm_mlp_post):
    batch, seq, d = x_prompt.shape
    nb, dec_seq, _ = x_sample.shape
    depth = w_in.shape[0]
    pool_dim = d - ATTN_DIM
    n_pages = page_table.shape[1]
    page = cache_kv.shape[2]
    past = n_pages * page
    wb = state_win_kv.shape[2]
    assert dec_seq == 1 and seq >= POOL_BUF and seq % CMP_STRIDE == 0
    assert w_in.shape[2] == ATTN_DIM + 6 * KV_DIM + N_BRANCH * N_HEADS + pool_dim

    tm_p = min(512, seq)
    tabs_p = _rope_tables(jnp.arange(seq, dtype=jnp.int32))
    tabs_s = _rope_tables(jnp.full((nb,), past, jnp.int32))
    nt_p = seq // tm_p

    y_p = x_prompt.reshape(batch * seq, d)
    y_s = x_sample.reshape(nb, d)
    kv_p, kv_s, win_p, win_s, pool_p, pool_s = [], [], [], [], [], []
    row_vec = lambda v: v.reshape(1, -1)
    for l in range(depth):
        w_r, wcat, pos8, w2, pw, wo, wu, wd = _layer_weights(
            w_in[l], cmp_pos_k[l], cmp_w1_k[l], cmp_w2_k[l], cmp_pos_v[l], cmp_w1_v[l],
            cmp_w2_v[l], pool_w[l], w_o[l], w_up[l], w_down[l], pool_dim)
        g_pre, g_post = row_vec(norm_mix_pre[l]), row_vec(norm_mix_post[l])
        g_mlp_pre, g_mlp_post = row_vec(norm_mlp_pre[l]), row_vec(norm_mlp_post[l])
        ps = row_vec(pool_scale[l])

        n_w = 2 * KV_HEADS
        win_rows = state_win_kv[l].reshape(nb, wb * n_w, HEAD_DIM)

        q_s, kv_s1, wkv_s, _, gates_s, u_s = _project(
            y_s, g_pre, w_r, tabs_s, lambda i: (i, 0), nb, pool_dim, False)

        q, kv, wkv, kva, gates, u, xc = _project(
            y_p, g_pre, w_r, tabs_p, lambda i: (i % nt_p, 0), tm_p, pool_dim, True)
        cmp_kv = _compress_prompt(xc, wcat, pos8, w2, batch, seq)
        o = _attend_prompt(q, cmp_kv, kva, gates, batch, seq)
        y1, h2 = _mix_prompt(o, u, y_p, pw, ps, wo, g_post, g_mlp_pre, batch, seq)
        y_p, win_rolled = _mlp(h2, y1, wu, wd, g_mlp_post, min(512, batch * seq),
                               win_rows, wkv_s.reshape(nb, n_w, HEAD_DIM))
        kv_p.append(kv.reshape(batch, seq, CACHE_SLOTS, KV_HEADS, HEAD_DIM))
        wp = min(WINDOW, seq)
        win_p.append(wkv.reshape(batch, seq, 2, KV_HEADS, HEAD_DIM)[:, seq - wp:])
        pool_p.append(u.reshape(batch, seq, pool_dim)[:, seq - POOL_BUF:])

        o = _attend_sample(page_table, cache_kv[l], q_s, kv_s1, wkv_s, win_rows, gates_s,
                           wcat, pos8, w2)
        y1, h2 = _mix_sample(o.reshape(nb, ATTN_DIM), u_s, state_pool[l], y_s, pw, ps, wo,
                             g_post, g_mlp_pre, past)
        y_s = _mlp(h2, y1, wu, wd, g_mlp_post, nb)
        kv_s.append(kv_s1.reshape(nb, 1, CACHE_SLOTS, KV_HEADS, HEAD_DIM))
        win_s.append(win_rolled.reshape(nb, wb, 2, KV_HEADS, HEAD_DIM))
        pool_s.append(jnp.concatenate([state_pool[l], u_s[:, None]], axis=1)[:, 1:])

    return (y_p.reshape(batch, seq, d), y_s.reshape(nb, 1, d),
            jnp.stack(kv_p), jnp.stack(kv_s), jnp.stack(win_p), jnp.stack(win_s),
            jnp.stack(pool_p), jnp.stack(pool_s))
```

```python
import functools

import numpy as np
import jax
import jax.numpy as jnp
from jax import lax
from jax.experimental import pallas as pl
from jax.experimental.pallas import tpu as pltpu

N_HEADS = 8
HEAD_DIM = 128
KV_HEADS = 2
HPG = N_HEADS // KV_HEADS
ATTN_DIM = N_HEADS * HEAD_DIM
KV_DIM = KV_HEADS * HEAD_DIM
N_BRANCH = 3
POOL_WINDOWS = (2, 4, 8, 16)
POOL_GROUPS = len(POOL_WINDOWS)
POOL_BUF = max(POOL_WINDOWS) - 1
ROT_DIM = HEAD_DIM // 4
ROT_HALF = ROT_DIM // 2
ROPE_THETA = 500000.0
CMP_LEN = 32
CMP_STRIDE = 16
SEL_BLOCK = 64
SEL_TOPK = 16
WINDOW = 512
EPS = 1e-6
SCALE = HEAD_DIM ** -0.5
FORCE_SCORE = 1e4
NEG_INF = -1e30

LANES = 128
CACHE_SLOTS = 4
CMP_IN = CMP_LEN * HEAD_DIM
CMP_HALF = CMP_STRIDE * HEAD_DIM
GATE_PAD = LANES
VMEM_LIMIT = 56 * 1024 * 1024

PROJ_ROWS = 512
MLP_ROWS = 512
MLP_FF = 1024
ATTN_Q = 128
ATTN_K = 512
POOL_HALO = 16
SAMPLE_SEQS = 2

BF16 = jnp.bfloat16
F32 = jnp.float32


def _dot(a, b):
    return jnp.dot(a, b, preferred_element_type=F32)


def _dot_nt(a, b):
    return lax.dot_general(a, b, (((1,), (1,)), ((), ())), preferred_element_type=F32)


def _rms(x, g):
    return x * lax.rsqrt(jnp.mean(x * x, axis=-1, keepdims=True) + EPS) * g


def _params(n_axes):
    return pltpu.CompilerParams(
        dimension_semantics=("arbitrary",) * n_axes, vmem_limit_bytes=VMEM_LIMIT)


def _split_hi_lo(x):
    hi = x.astype(BF16)
    lo = (x - hi.astype(F32)).astype(BF16)
    return hi, lo


Q_OFF, KV_OFF, WKV_OFF = 0, ATTN_DIM, ATTN_DIM + 4 * KV_DIM


def _proj_kernel(x_ref, g_ref, w_ref, cos_ref, sa_ref, sb_ref, *rest, pool_dim, chunked):
    if chunked:
        q_ref, kv_ref, wkv_ref, kva_ref, gate_ref, u_ref, xc_ref, tmp_ref = rest
    else:
        q_ref, kv_ref, wkv_ref, kva_ref, gate_ref, u_ref = rest
    tm = x_ref.shape[0]
    u_off = WKV_OFF + 2 * KV_DIM
    gate_off = u_off + pool_dim
    h = _rms(x_ref[...], g_ref[...]).astype(BF16)
    cos, sa, sb = cos_ref[...], sa_ref[...], sb_ref[...]

    def rope(z):
        return (z * cos + pltpu.roll(z, LANES - ROT_HALF, 1) * sa
                + pltpu.roll(z, ROT_HALF, 1) * sb)

    zq = _dot(h, w_ref[:, Q_OFF:Q_OFF + ATTN_DIM])
    for hd in range(N_HEADS):
        sl = slice(hd * HEAD_DIM, (hd + 1) * HEAD_DIM)
        q_ref[:, sl] = rope(zq[:, sl]).astype(BF16)

    n_kv = CACHE_SLOTS * KV_HEADS
    zkv = _dot(h, w_ref[:, KV_OFF:KV_OFF + 4 * KV_DIM])
    for blk in range(n_kv):
        z = zkv[:, blk * HEAD_DIM:(blk + 1) * HEAD_DIM]
        if (blk // KV_HEADS) % 2 == 0:
            z = rope(z)
        kv_ref[pl.ds(blk, tm, stride=n_kv), :] = z
        if blk >= 2 * KV_HEADS:
            kva_ref[blk - 2 * KV_HEADS] = z.astype(BF16)
        elif chunked:
            tmp_ref[...] = z
            for r in range(CMP_STRIDE):
                xc_ref[blk, :, r * HEAD_DIM:(r + 1) * HEAD_DIM] = (
                    tmp_ref[pl.ds(r, tm // CMP_STRIDE, stride=CMP_STRIDE), :].astype(BF16))

    n_w = 2 * KV_HEADS
    zw = _dot(h, w_ref[:, WKV_OFF:WKV_OFF + 2 * KV_DIM])
    for blk in range(n_w):
        z = zw[:, blk * HEAD_DIM:(blk + 1) * HEAD_DIM]
        if blk < KV_HEADS:
            z = rope(z)
        wkv_ref[pl.ds(blk, tm, stride=n_w), :] = z
        kva_ref[2 * KV_HEADS + blk] = z.astype(BF16)

    u_ref[...] = _dot(h, w_ref[:, u_off:u_off + pool_dim])
    gl = _dot(h, w_ref[:, gate_off:gate_off + GATE_PAD])
    gate_ref[...] = 1.0 / (1.0 + jnp.exp(-gl))


def _project(x, gain, w_r, tables, table_index, tm, pool_dim, chunked):
    rows, d = x.shape
    n_proj = w_r.shape[1]
    n_kv = CACHE_SLOTS * KV_HEADS
    n_w = 2 * KV_HEADS
    row = lambda i: (i, 0)
    fixed = lambda i: (0, 0)
    tab_spec = pl.BlockSpec((tm, LANES), table_index)
    out_shape = [
        jax.ShapeDtypeStruct((rows, ATTN_DIM), BF16),
        jax.ShapeDtypeStruct((rows * n_kv, HEAD_DIM), F32),
        jax.ShapeDtypeStruct((rows * n_w, HEAD_DIM), F32),
        jax.ShapeDtypeStruct((n_w + n_kv // 2, rows, HEAD_DIM), BF16),
        jax.ShapeDtypeStruct((rows, GATE_PAD), F32),
        jax.ShapeDtypeStruct((rows, pool_dim), F32),
    ]
    out_specs = [
        pl.BlockSpec((tm, ATTN_DIM), row),
        pl.BlockSpec((tm * n_kv, HEAD_DIM), row),
        pl.BlockSpec((tm * n_w, HEAD_DIM), row),
        pl.BlockSpec((n_w + n_kv // 2, tm, HEAD_DIM), lambda i: (0, i, 0)),
        pl.BlockSpec((tm, GATE_PAD), row),
        pl.BlockSpec((tm, pool_dim), row),
    ]
    scratch = []
    if chunked:
        out_shape.append(jax.ShapeDtypeStruct((n_kv // 2, rows // CMP_STRIDE, CMP_HALF), BF16))
        out_specs.append(pl.BlockSpec((n_kv // 2, tm // CMP_STRIDE, CMP_HALF), lambda i: (0, i, 0)))
        scratch.append(pltpu.VMEM((tm, HEAD_DIM), F32))
    return pl.pallas_call(
        functools.partial(_proj_kernel, pool_dim=pool_dim, chunked=chunked),
        grid=(rows // tm,),
        in_specs=[
            pl.BlockSpec((tm, d), row),
            pl.BlockSpec((1, d), fixed),
            pl.BlockSpec((d, n_proj), fixed),
            tab_spec, tab_spec, tab_spec,
        ],
        out_specs=tuple(out_specs),
        out_shape=tuple(out_shape),
        scratch_shapes=scratch,
        compiler_params=_params(1),
        name="project",
    )(x, gain, w_r, *tables)


def _rope_tables(pos):
    inv = jnp.power(ROPE_THETA, -jnp.arange(ROT_HALF, dtype=F32) * (2.0 / ROT_DIM))
    ang = pos.astype(F32)[:, None] * inv[None, :]
    cos, sin = jnp.cos(ang), jnp.sin(ang)
    n = pos.shape[0]
    rest = LANES - ROT_DIM
    c = jnp.concatenate([cos, cos, jnp.ones((n, rest), F32)], axis=1)
    sa = jnp.concatenate([-sin, jnp.zeros((n, LANES - ROT_HALF), F32)], axis=1)
    sb = jnp.concatenate([jnp.zeros((n, ROT_HALF), F32), sin, jnp.zeros((n, rest), F32)], axis=1)
    return c, sa, sb


def _gelu_tanh(x):
    return 0.5 * x * (1.0 + jnp.tanh(0.7978845608028654 * (x + 0.044715 * (x * x * x))))


def _compress_first(x, wcat, pos8):
    ab = _dot(x, wcat)
    pa = _dot(pos8[:, :CMP_HALF], wcat)[0:1, :HEAD_DIM]
    pb = _dot(pos8[:, CMP_HALF:], wcat)[0:1, HEAD_DIM:]
    return ab, pa + pb


def _compress_hidden(ab, pos_term):
    rows = ab.shape[0]
    b_next = pltpu.roll(ab[:, HEAD_DIM:], rows - 1, 0)
    return _gelu_tanh(ab[:, :HEAD_DIM] + b_next + pos_term).astype(BF16)


def _compress_kernel(x_ref, wcat_ref, pos_ref, w2_ref, o_ref):
    g, _, nck, width = x_ref.shape
    x = x_ref[...].reshape(g * nck, width)
    hid = _compress_hidden(*_compress_first(x, wcat_ref[0], pos_ref[0]))
    o_ref[...] = _dot(hid, w2_ref[0]).astype(BF16).reshape(o_ref.shape)


def _compress_prompt(xc, wcat, pos8, w2, batch, seq):
    nck = seq // CMP_STRIDE
    x = xc.reshape(xc.shape[0], batch, nck, CMP_HALF)
    return pl.pallas_call(
        _compress_kernel,
        grid=(batch, 2),
        in_specs=[
            pl.BlockSpec((KV_HEADS, 1, nck, CMP_HALF), lambda b, s: (s, b, 0, 0)),
            pl.BlockSpec((1, CMP_HALF, 2 * HEAD_DIM), lambda b, s: (s, 0, 0)),
            pl.BlockSpec((1, 8, CMP_IN), lambda b, s: (s, 0, 0)),
            pl.BlockSpec((1, HEAD_DIM, HEAD_DIM), lambda b, s: (s, 0, 0)),
        ],
        out_specs=pl.BlockSpec((1, 1, KV_HEADS, nck, HEAD_DIM), lambda b, s: (b, s, 0, 0, 0)),
        out_shape=jax.ShapeDtypeStruct((batch, 2, KV_HEADS, nck, HEAD_DIM), BF16),
        compiler_params=_params(2),
        name="compress_prompt",
    )(x, wcat, pos8, w2)


def _topk_rows(score, jidx, topk):
    rank = jnp.zeros_like(score)
    for j in range(score.shape[0]):
        bj = score[j:j + 1, :]
        tie = jnp.where(jidx > j, 1.0, 0.0)
        rank = rank + jnp.where(bj > score, 1.0, jnp.where(bj == score, tie, 0.0))
    return jnp.where(rank < topk, 1.0, 0.0)


def _softmax_parts(s, ok):
    s = jnp.where(ok, s, NEG_INF)
    m = jnp.max(s, axis=-1, keepdims=True)
    e = jnp.where(ok, jnp.exp(s - m), 0.0)
    d = jnp.sum(e, axis=-1, keepdims=True)
    return e, jnp.where(d > 0, d, 1.0)


EXP2_SCALE = SCALE * 1.4426950408889634
HEAD_PAIRS = HPG // 2


def _attn_prompt_kernel(q_ref, kc_ref, vc_ref, ks_ref, vs_ref, kw_ref, vw_ref, gate_ref,
                        mt_ref, eye_ref, o_ref,
                        vct_ref, vst_ref, vwt_ref, sel_ref, gt_ref, acc_ref,
                        *, tq, tk, seq, topk):
    i = pl.program_id(1)
    t0 = i * tq
    n_sel = mt_ref.shape[0]
    pair_w = 2 * tq
    eye = eye_ref[...]
    groups = range(KV_HEADS)
    pairs = range(KV_HEADS * HEAD_PAIRS)
    group_of = [hp // HEAD_PAIRS for hp in pairs]

    @pl.when(i == 0)
    def _():
        for g in groups:
            vct_ref[g] = _dot_nt(eye, vc_ref[0, 0, g]).astype(BF16)
            vst_ref[g] = _dot_nt(eye, vs_ref[g]).astype(BF16)
            vwt_ref[g] = _dot_nt(eye, vw_ref[g]).astype(BF16)

    q_all = q_ref[...]
    q_pairs = [jnp.concatenate([q_all[:, (2 * hp) * HEAD_DIM:(2 * hp + 1) * HEAD_DIM],
                                q_all[:, (2 * hp + 1) * HEAD_DIM:(2 * hp + 2) * HEAD_DIM]], axis=0)
               for hp in pairs]

    def both_heads(x):
        return jnp.concatenate([x, x], axis=1)

    def tpos(n_keys):
        return t0 + lax.broadcasted_iota(jnp.int32, (n_keys, tq), 1)

    def kidx(n_keys):
        return lax.broadcasted_iota(jnp.int32, (n_keys, tq), 0)


    ncp = kc_ref.shape[3]
    wk = min(WINDOW + tq, seq)
    ws = pl.multiple_of(jnp.maximum(t0 + tq - wk, 0), tq)
    s_cmp = [_dot_nt(kc_ref[0, 0, group_of[hp]], q_pairs[hp]) for hp in pairs]
    s_win = [_dot_nt(kw_ref[group_of[hp], pl.ds(ws, wk), :], q_pairs[hp]) for hp in pairs]

    ok = both_heads(jnp.where(kidx(ncp) * CMP_STRIDE + (CMP_LEN - 1) <= tpos(ncp), 1.0, 0.0)) > 0.5
    p_cmp, p_sum = [], [None] * KV_HEADS
    for hp in pairs:
        s = jnp.where(ok, s_cmp[hp], NEG_INF)
        m = jnp.max(s, axis=0, keepdims=True)
        e = jnp.where(ok, jnp.exp2((s - m) * EXP2_SCALE), 0.0)
        d = jnp.sum(e, axis=0, keepdims=True)
        p = e / jnp.where(d > 0, d, 1.0)
        p_cmp.append(p.astype(BF16))
        ph = p[:, :tq] + p[:, tq:]
        g = group_of[hp]
        p_sum[g] = ph if p_sum[g] is None else p_sum[g] + ph

    mt = mt_ref[...]
    split = [_split_hi_lo(p_sum[g]) for g in groups]
    imp = [_dot(mt, split[g][0]) + _dot(mt, split[g][1]) for g in groups]
    o_cmp = [_dot(vct_ref[group_of[hp]], p_cmp[hp]) for hp in pairs]

    kpos = ws + kidx(wk)
    bias = both_heads(jnp.where(kpos <= tpos(wk),
                                jnp.where(kpos > tpos(wk) - WINDOW, 0.0, NEG_INF), NEG_INF))
    p_win, l_win = [], []
    for hp in pairs:
        s = s_win[hp] + bias
        p = jnp.exp2((s - jnp.max(s, axis=0, keepdims=True)) * EXP2_SCALE)
        l_win.append(jnp.sum(p, axis=0, keepdims=True))
        p_win.append(p.astype(BF16))
    o_win = [_dot(vwt_ref[group_of[hp], :, pl.ds(ws, wk)], p_win[hp]) for hp in pairs]
    o_win = [o_win[hp] / l_win[hp] for hp in pairs]

    jidx = kidx(n_sel)
    jt = tpos(n_sel) // SEL_BLOCK
    forced = jnp.where(jidx == 0, 1.0, jnp.where(jidx == jt, 1.0, jnp.where(jidx == jt - 1, 1.0, 0.0)))
    for g in groups:
        score = jnp.where(forced > 0.5, FORCE_SCORE, jnp.where(jidx <= jt, imp[g], -1.0))
        sel_ref[g] = _topk_rows(score, jidx, topk)

    acc_ref[...] = jnp.zeros_like(acc_ref)

    def sel_step(kb, carry):
        k0 = pl.multiple_of(kb * tk, tk)
        blk0 = kb * (tk // SEL_BLOCK)
        causal = k0 + kidx(tk) <= tpos(tk)
        bias = []
        for g in groups:
            chosen = jnp.concatenate(
                [jnp.broadcast_to(sel_ref[g, pl.ds(blk0 + j, 1), :], (SEL_BLOCK, tq))
                 for j in range(tk // SEL_BLOCK)], axis=0)
            bias.append(both_heads(
                jnp.where(causal, jnp.where(chosen > 0.5, 0.0, NEG_INF), NEG_INF)))
        ss = [_dot_nt(ks_ref[group_of[hp], pl.ds(k0, tk), :], q_pairs[hp]) + bias[group_of[hp]]
              for hp in pairs]
        out, ps, alphas = [], [], []
        for hp in pairs:
            m, l = carry[2 * hp], carry[2 * hp + 1]
            m_new = jnp.maximum(m, jnp.max(ss[hp], axis=0, keepdims=True))
            p = jnp.exp2((ss[hp] - m_new) * EXP2_SCALE)
            alpha = jnp.exp2((m - m_new) * EXP2_SCALE)
            out += [m_new, alpha * l + jnp.sum(p, axis=0, keepdims=True)]
            ps.append(p.astype(BF16))
            alphas.append(alpha)
        pvs = [_dot(vst_ref[group_of[hp], :, pl.ds(k0, tk)], ps[hp]) for hp in pairs]
        for hp in pairs:
            acc_ref[hp] = alphas[hp] * acc_ref[hp] + pvs[hp]
        return tuple(out)

    n_kb = (t0 + tq + tk - 1) // tk
    init = (jnp.full((1, pair_w), NEG_INF, F32), jnp.zeros((1, pair_w), F32)) * len(pairs)
    stats = lax.fori_loop(0, n_kb, sel_step, init)
    o_sel = [acc_ref[hp] / stats[2 * hp + 1] for hp in pairs]

    gt_ref[...] = gate_ref[...].T
    o_t = []
    for h in range(N_HEADS):
        hp, lanes = h // 2, slice((h % 2) * tq, (h % 2 + 1) * tq)
        col = h * N_BRANCH
        o_t.append((gt_ref[col:col + 1, :] * o_cmp[hp][:, lanes]
                    + gt_ref[col + 1:col + 2, :] * o_sel[hp][:, lanes]
                    + gt_ref[col + 2:col + 3, :] * o_win[hp][:, lanes]).astype(BF16))
    outs = [_dot_nt(eye, o_t[h]) for h in range(N_HEADS)]
    for h in range(N_HEADS):
        o_ref[:, h * HEAD_DIM:(h + 1) * HEAD_DIM] = outs[h].astype(BF16)


def _cmp_to_sel(n_cmp_pad, n_cmp, n_sel):
    cs = np.arange(n_cmp_pad)[:, None] * CMP_STRIDE
    ss = np.arange(n_sel)[None, :] * SEL_BLOCK
    hit = (cs < ss + SEL_BLOCK) & (cs + CMP_LEN > ss) & (np.arange(n_cmp_pad)[:, None] < n_cmp)
    return hit.astype(np.float32)


def _attend_prompt(q, cmp_kv, kva, gates, batch, seq):
    tq = min(ATTN_Q, seq)
    tk = min(ATTN_K, seq)
    nq = seq // tq
    ncp = seq // CMP_STRIDE
    n_sel = -(-seq // SEL_BLOCK)
    topk = min(SEL_TOPK, n_sel)
    assert tq == HEAD_DIM and seq % tk == 0 and WINDOW % tq == 0
    mt = jnp.asarray(_cmp_to_sel(ncp, ncp - 1, n_sel).T, BF16)
    eye = jnp.asarray(np.eye(tq, dtype=np.float32), BF16)
    kv_spec = lambda slot: pl.BlockSpec((KV_HEADS, seq, HEAD_DIM), lambda b, i: (slot, b, 0))
    cmp_spec = lambda s: pl.BlockSpec(
        (1, 1, KV_HEADS, ncp, HEAD_DIM), lambda b, i: (b, s, 0, 0, 0))
    full = lambda a: pl.BlockSpec(a.shape, lambda b, i: (0, 0))
    return pl.pallas_call(
        functools.partial(_attn_prompt_kernel, tq=tq, tk=tk, seq=seq, topk=topk),
        grid=(batch, nq),
        in_specs=[
            pl.BlockSpec((tq, ATTN_DIM), lambda b, i: (b * nq + i, 0)),
            cmp_spec(0), cmp_spec(1),
            kv_spec(0), kv_spec(1), kv_spec(2), kv_spec(3),
            pl.BlockSpec((tq, GATE_PAD), lambda b, i: (b * nq + i, 0)),
            full(mt), full(eye),
        ],
        out_specs=pl.BlockSpec((tq, ATTN_DIM), lambda b, i: (b * nq + i, 0)),
        out_shape=jax.ShapeDtypeStruct((batch * seq, ATTN_DIM), BF16),
        scratch_shapes=[
            pltpu.VMEM((KV_HEADS, HEAD_DIM, ncp), BF16),
            pltpu.VMEM((KV_HEADS, HEAD_DIM, seq), BF16),
            pltpu.VMEM((KV_HEADS, HEAD_DIM, seq), BF16),
            pltpu.VMEM((KV_HEADS, n_sel, tq), F32),
            pltpu.VMEM((GATE_PAD, tq), F32),
            pltpu.VMEM((N_HEADS // 2, HEAD_DIM, 2 * tq), F32),
        ],
        compiler_params=_params(2),
        name="attend_prompt",
    )(q, cmp_kv, cmp_kv, kva, kva, kva, kva, gates, mt, eye)


def _attn_sample_kernel(pt_ref, cache_ref, q_ref, kvn_ref, wn_ref, win_ref, gate_ref,
                        wcat_ref, pos_ref, w2_ref, ms_ref, ek_ref, o_ref, buf, sem,
                        *, n_pages, page, past, topk, n_sel):
    b = pl.program_id(0)
    nb = pl.num_programs(0)
    n_kv = CACHE_SLOTS * KV_HEADS
    n_w = 2 * KV_HEADS
    cpp = page // CMP_STRIDE
    nch = past // CMP_STRIDE
    wb = win_ref.shape[1] // n_w

    n_seq = q_ref.shape[0]
    seqs = range(n_seq)

    def for_all_copies(step, slot, act):
        def per_page(pi, carry):
            dst0 = pl.multiple_of(pi * cpp, cpp)
            for s in seqs:
                src0 = pl.multiple_of(pt_ref[step * n_seq + s, pi] * cpp, cpp)
                for r in range(CMP_STRIDE):
                    act(pltpu.make_async_copy(cache_ref.at[pl.ds(src0, cpp), r],
                                              buf.at[slot, s, r, pl.ds(dst0, cpp)], sem.at[slot]))
            return carry
        lax.fori_loop(0, n_pages, per_page, 0)

    @pl.when(b == 0)
    def _():
        for_all_copies(0, 0, lambda c: c.start())

    @pl.when(b + 1 < nb)
    def _():
        for_all_copies(b + 1, (b + 1) % 2, lambda c: c.start())

    slot = b % 2
    for_all_copies(b, slot, lambda c: c.wait())

    def chunk_rows(s, cache_slot, g, r):
        rows = buf.reshape(2, n_seq, CMP_STRIDE, nch * n_kv, HEAD_DIM)
        return rows[slot, s, r, pl.ds(cache_slot * KV_HEADS + g, nch, stride=n_kv), :]

    row8 = lax.broadcasted_iota(jnp.int32, (N_HEADS, 1), 0)
    in_g0 = row8 < HPG

    def by_group(x0, x1):
        return jnp.where(in_g0, x0, x1)

    q8 = [q_ref[s] for s in seqs]
    qf = [q8[s].astype(F32) for s in seqs]
    groups = range(KV_HEADS)

    def scores(s, keys):
        return by_group(_dot_nt(q8[s], keys[0]), _dot_nt(q8[s], keys[1])) * SCALE

    def cached(s, cache_slot, g):
        return jnp.concatenate([chunk_rows(s, cache_slot, g, r).astype(BF16)
                                for r in range(CMP_STRIDE)], axis=0)

    def compress_input(cache_slot):
        return jnp.concatenate(
            [jnp.concatenate([chunk_rows(s, cache_slot, g, r).astype(BF16)
                              for r in range(CMP_STRIDE)], axis=1)
             for s in seqs for g in groups], axis=0)

    first = [_compress_first(compress_input(cs), wcat_ref[cs], pos_ref[cs]) for cs in range(2)]
    s_sel = [scores(s, [cached(s, 2, g) for g in groups]) for s in seqs]
    s_win = [scores(s, [win_ref[s, pl.ds(g, wb, stride=n_w), :].astype(BF16) for g in groups])
             for s in seqs]
    kc, vc = [_dot(_compress_hidden(*first[cs]), w2_ref[cs]).astype(BF16) for cs in range(2)]

    def compressed(x, s, g):
        r0 = (s * KV_HEADS + g) * nch
        return x[r0:r0 + nch]

    lane_n = lax.broadcasted_iota(jnp.int32, (N_HEADS, nch), 1)
    ok = lane_n * CMP_STRIDE + (CMP_LEN - 1) <= past
    s_cmp = [by_group(*[_dot_nt(q8[s], compressed(kc, s, g)) for g in groups]) * SCALE
             for s in seqs]
    p_cmp = []
    for s in seqs:
        e, d = _softmax_parts(s_cmp[s], ok)
        p_cmp.append(e / d)
    o_cmp = [by_group(*[_dot(p_cmp[s].astype(BF16), compressed(vc, s, g)) for g in groups])
             for s in seqs]

    jl = lax.broadcasted_iota(jnp.int32, (N_HEADS, LANES), 1)
    jt = past // SEL_BLOCK
    forced = jnp.where(jl == 0, 1.0, jnp.where(jl == jt, 1.0, jnp.where(jl == jt - 1, 1.0, 0.0)))
    ii = lax.broadcasted_iota(jnp.int32, (LANES, LANES), 0)
    jj = lax.broadcasted_iota(jnp.int32, (LANES, LANES), 1)
    tie = jnp.where(ii < jj, 1.0, 0.0)
    imp = []
    for s in seqs:
        p_g = [jnp.sum(p_cmp[s][g * HPG:(g + 1) * HPG], axis=0, keepdims=True) for g in groups]
        p2 = jnp.concatenate(p_g + [jnp.zeros((N_HEADS - KV_HEADS, nch), F32)], axis=0)
        p_hi, p_lo = _split_hi_lo(p2)
        imp.append(_dot(p_hi, ms_ref[...]) + _dot(p_lo, ms_ref[...]))
    sel2 = []
    for s in seqs:
        score = jnp.where(forced > 0.5, FORCE_SCORE, jnp.where(jl <= jt, imp[s], -1.0))
        score = jnp.where(jl < n_sel, score, -2.0)
        sel_rows = []
        for g in groups:
            srow = jnp.broadcast_to(score[g:g + 1, :], (LANES, LANES))
            scol = jnp.sum(jnp.where(ii == jj, srow, 0.0), axis=1, keepdims=True)
            beats = jnp.where(scol > srow, 1.0, jnp.where(scol == srow, tie, 0.0))
            rank = jnp.sum(beats, axis=0, keepdims=True)
            sel_rows.append(jnp.where(rank < topk, 1.0, 0.0))
        sel2.append(jnp.concatenate(sel_rows + [jnp.zeros((N_HEADS - KV_HEADS, LANES), F32)], axis=0))
    chunk_ok2 = [_dot(sel2[s].astype(BF16), ek_ref[...]) for s in seqs]

    def new_row(ref, s, idx0):
        x = by_group(ref[s, idx0:idx0 + 1, :], ref[s, idx0 + 1:idx0 + 2, :])
        return x.astype(BF16).astype(F32)

    def weights(s, scores_s, ok, k_new, new_ok):
        s_new = jnp.sum(qf[s] * k_new, axis=-1, keepdims=True) * SCALE
        sc = jnp.where(ok, scores_s, NEG_INF)
        s_new = jnp.where(new_ok, s_new, NEG_INF)
        m = jnp.maximum(jnp.max(sc, axis=-1, keepdims=True), s_new)
        e = jnp.where(ok, jnp.exp(sc - m), 0.0)
        e_new = jnp.where(new_ok, jnp.exp(s_new - m), 0.0)
        d = jnp.sum(e, axis=-1, keepdims=True) + e_new
        return e.astype(BF16), e_new, jnp.where(d > 0, d, 1.0)

    def weighted(w, vals, v_new):
        eb, e_new, d = w
        return (by_group(_dot(eb, vals[0]), _dot(eb, vals[1])) + e_new * v_new) / d

    kpos = past - wb + lax.broadcasted_iota(jnp.int32, (N_HEADS, wb), 1)
    w_sel, w_win = [], []
    for s in seqs:
        chunk_ok = by_group(chunk_ok2[s][0:1], chunk_ok2[s][1:2])
        key_ok = jnp.concatenate([chunk_ok] * CMP_STRIDE, axis=1)
        new_ok = by_group(*[jnp.sum(jnp.where(jl[0:1] == jt, sel2[s][g:g + 1], 0.0), axis=1,
                                    keepdims=True) for g in groups])
        w_sel.append(weights(s, s_sel[s], key_ok > 0.5, new_row(kvn_ref, s, 2 * KV_HEADS),
                             new_ok > 0.5))
        w_win.append(weights(s, s_win[s], kpos > past - WINDOW, new_row(wn_ref, s, 0), row8 >= 0))
    o_sel = [weighted(w_sel[s], [cached(s, 3, g) for g in groups],
                      new_row(kvn_ref, s, 3 * KV_HEADS)) for s in seqs]
    o_win = [weighted(w_win[s],
                      [win_ref[s, pl.ds(KV_HEADS + g, wb, stride=n_w), :].astype(BF16)
                       for g in groups], new_row(wn_ref, s, KV_HEADS)) for s in seqs]

    lane = lax.broadcasted_iota(jnp.int32, (N_HEADS, GATE_PAD), 1)
    for s in seqs:
        gates = jnp.broadcast_to(gate_ref[s], (N_HEADS, GATE_PAD))

        def gate(br):
            return jnp.sum(jnp.where(lane == row8 * N_BRANCH + br, gates, 0.0), axis=-1,
                           keepdims=True)

        o_ref[s] = (gate(0) * o_cmp[s] + gate(1) * o_sel[s] + gate(2) * o_win[s]).astype(BF16)


def _attend_sample(page_table, cache, q, kv_new, wkv_new, win_rows, gates, wcat, pos8, w2):
    nb, n_pages = page_table.shape
    n_phys, page = cache.shape[:2]
    n_kv = CACHE_SLOTS * KV_HEADS
    n_w = 2 * KV_HEADS
    past = n_pages * page
    nch = past // CMP_STRIDE
    wb = win_rows.shape[1] // n_w
    n_sel = -(-(past + 1) // SEL_BLOCK)
    n_cmp = (past + 1) // CMP_STRIDE - 1
    topk = min(SEL_TOPK, n_sel)
    assert n_sel <= LANES and nch * CMP_STRIDE == past
    ms = np.zeros((nch, LANES), np.float32)
    ms[:, :n_sel] = _cmp_to_sel(nch, n_cmp, n_sel)
    ek = (np.arange(nch)[None, :] * CMP_STRIDE // SEL_BLOCK == np.arange(LANES)[:, None])
    ms = jnp.asarray(ms, BF16)
    ek = jnp.asarray(ek.astype(np.float32), BF16)
    n_seq = SAMPLE_SEQS if nb % SAMPLE_SEQS == 0 else 1
    seq3 = lambda n: pl.BlockSpec((n_seq, n, HEAD_DIM), lambda b, pt: (b, 0, 0))
    full3 = lambda a: pl.BlockSpec(a.shape, lambda b, pt: (0, 0, 0))
    full2 = lambda a: pl.BlockSpec(a.shape, lambda b, pt: (0, 0))
    grid_spec = pltpu.PrefetchScalarGridSpec(
        num_scalar_prefetch=1,
        grid=(nb // n_seq,),
        in_specs=[
            pl.BlockSpec(memory_space=pl.ANY),
            seq3(N_HEADS), seq3(n_kv), seq3(n_w), seq3(wb * n_w),
            pl.BlockSpec((n_seq, 1, GATE_PAD), lambda b, pt: (b, 0, 0)),
            full3(wcat), full3(pos8), full3(w2), full2(ms), full2(ek),
        ],
        out_specs=seq3(N_HEADS),
        scratch_shapes=[pltpu.VMEM((2, n_seq, CMP_STRIDE, nch, n_kv, HEAD_DIM), F32),
                        pltpu.SemaphoreType.DMA((2,))],
    )
    return pl.pallas_call(
        functools.partial(_attn_sample_kernel, n_pages=n_pages, page=page, past=past,
                          topk=topk, n_sel=n_sel),
        grid_spec=grid_spec,
        out_shape=jax.ShapeDtypeStruct((nb, N_HEADS, HEAD_DIM), BF16),
        compiler_params=_params(1),
        name="attend_sample",
    )(page_table, cache.reshape(n_phys * page // CMP_STRIDE, CMP_STRIDE, n_kv, HEAD_DIM),
      q.reshape(nb, N_HEADS, HEAD_DIM), kv_new.reshape(nb, n_kv, HEAD_DIM),
      wkv_new.reshape(nb, n_w, HEAD_DIM), win_rows,
      gates.reshape(nb, 1, GATE_PAD), wcat, pos8, w2, ms, ek)


def _mix_tail(o, diffs, x, pw_ref, ps_ref, wo_ref, gpost_ref, gpre_ref, y1_ref, h2_ref):
    gw = diffs[0].shape[1]
    ys = [(_dot(diffs[g].astype(BF16), pw_ref[g]) * ps_ref[:, g * gw:(g + 1) * gw]).astype(BF16)
          for g in range(POOL_GROUPS)]
    cat = jnp.concatenate([o] + ys, axis=1)
    m = _dot(cat, wo_ref[...])
    y1 = x + _rms(m, gpost_ref[...])
    y1_ref[...] = y1
    h2_ref[...] = _rms(y1, gpre_ref[...]).astype(BF16)


def _mix_prompt_kernel(o_ref, u_ref, halo_ref, x_ref, pw_ref, ps_ref, wo_ref, gpost_ref, gpre_ref,
                       y1_ref, h2_ref, *, tm):
    i = pl.program_id(1)
    halo_rows = halo_ref.shape[0]
    halo = jnp.where(i > 0, halo_ref[...], 0.0)
    u = u_ref[...]
    uext = jnp.concatenate([halo, u], axis=0)
    n_ext = uext.shape[0]
    gw = u.shape[1] // POOL_GROUPS
    tpos = i * tm + lax.broadcasted_iota(jnp.int32, (tm, 1), 0)
    diffs = []
    for g, w in enumerate(POOL_WINDOWS):
        s = uext[:, g * gw:(g + 1) * gw]
        k = 1
        while k < w:
            s = s + pltpu.roll(s, k, 0)
            k *= 2
        cnt = jnp.minimum(w, tpos + 1).astype(F32)
        diffs.append(s[halo_rows:n_ext] / cnt - u[:, g * gw:(g + 1) * gw])
    _mix_tail(o_ref[...], diffs, x_ref[...], pw_ref, ps_ref, wo_ref, gpost_ref, gpre_ref,
              y1_ref, h2_ref)


def _mix_sample_kernel(o_ref, u_ref, st_ref, x_ref, pw_ref, ps_ref, wo_ref, gpost_ref, gpre_ref,
                       y1_ref, h2_ref, *, past):
    u = u_ref[...]
    c = u.shape[1]
    gw = c // POOL_GROUPS
    n_hist = st_ref.shape[1] // c
    diffs = []
    for g, w in enumerate(POOL_WINDOWS):
        un = u[:, g * gw:(g + 1) * gw]
        s = un
        for back in range(1, w):
            r = n_hist - back
            s = s + st_ref[:, r * c + g * gw:r * c + (g + 1) * gw]
        diffs.append(s / float(min(w, past + 1)) - un)
    _mix_tail(o_ref[...], diffs, x_ref[...], pw_ref, ps_ref, wo_ref, gpost_ref, gpre_ref,
              y1_ref, h2_ref)


def _mix_specs(tm, d, c, pool_w, idx):
    fixed2 = lambda *a: (0, 0)
    fixed3 = lambda *a: (0, 0, 0)
    weights = [
        pl.BlockSpec(pool_w.shape, fixed3),
        pl.BlockSpec((1, c), fixed2),
        pl.BlockSpec((d, d), fixed2),
        pl.BlockSpec((1, d), fixed2),
        pl.BlockSpec((1, d), fixed2),
    ]
    outs = (pl.BlockSpec((tm, d), idx), pl.BlockSpec((tm, d), idx))
    return weights, outs


def _mix_prompt(o, u, x, pool_w, pool_scale, w_o, g_post, g_pre, batch, seq):
    rows, d = x.shape
    c = u.shape[1]
    tm = min(PROJ_ROWS, seq)
    nt = seq // tm
    halo = POOL_HALO
    assert halo >= POOL_BUF and seq % tm == 0 and tm % halo == 0
    idx = lambda b, i: (b * nt + i, 0)
    halo_idx = lambda b, i: (jnp.maximum((b * nt + i) * (tm // halo) - 1, 0), 0)
    weights, outs = _mix_specs(tm, d, c, pool_w, idx)
    return pl.pallas_call(
        functools.partial(_mix_prompt_kernel, tm=tm),
        grid=(batch, nt),
        in_specs=[pl.BlockSpec((tm, ATTN_DIM), idx), pl.BlockSpec((tm, c), idx),
                  pl.BlockSpec((halo, c), halo_idx), pl.BlockSpec((tm, d), idx)] + weights,
        out_specs=outs,
        out_shape=(jax.ShapeDtypeStruct((rows, d), F32), jax.ShapeDtypeStruct((rows, d), BF16)),
        compiler_params=_params(2),
        name="mix_prompt",
    )(o, u, u, x, pool_w, pool_scale, w_o, g_post, g_pre)


def _mix_sample(o, u, pool_state, x, pool_w, pool_scale, w_o, g_post, g_pre, past):
    rows, d = x.shape
    c = u.shape[1]
    tm = rows
    idx = lambda i: (i, 0)
    weights, outs = _mix_specs(tm, d, c, pool_w, idx)
    st = pool_state.reshape(rows, -1)
    return pl.pallas_call(
        functools.partial(_mix_sample_kernel, past=past),
        grid=(rows // tm,),
        in_specs=[pl.BlockSpec((tm, ATTN_DIM), idx), pl.BlockSpec((tm, c), idx),
                  pl.BlockSpec((tm, st.shape[1]), idx), pl.BlockSpec((tm, d), idx)] + weights,
        out_specs=outs,
        out_shape=(jax.ShapeDtypeStruct((rows, d), F32), jax.ShapeDtypeStruct((rows, d), BF16)),
        compiler_params=_params(1),
        name="mix_sample",
    )(o, u, st, x, pool_w, pool_scale, w_o, g_post, g_pre)


def _mlp_kernel(h_ref, wu_ref, wd_ref, y1_ref, g_ref, *rest, shift):
    if shift:
        state_ref, fresh_ref, y_ref, rolled_ref, acc_ref = rest
        n_new = fresh_ref.shape[1]
        n_keep = state_ref.shape[1] - n_new
        rolled_ref[:, pl.ds(0, n_keep), :] = state_ref[:, pl.ds(n_new, n_keep), :]
        rolled_ref[:, pl.ds(n_keep, n_new), :] = fresh_ref[...]
    else:
        y_ref, acc_ref = rest
    j = pl.program_id(1)

    @pl.when(j == 0)
    def _():
        acc_ref[...] = jnp.zeros_like(acc_ref)

    a = jnp.maximum(_dot(h_ref[...], wu_ref[...]), 0.0)
    acc_ref[...] += _dot((a * a).astype(BF16), wd_ref[...])

    @pl.when(j == pl.num_programs(1) - 1)
    def _():
        y_ref[...] = y1_ref[...] + _rms(acc_ref[...], g_ref[...])


def _mlp(h2, y1, w_up, w_down, gain, tm, state=None, fresh=None):
    rows, d = y1.shape
    ff = w_up.shape[1]
    tf = min(MLP_FF, ff)
    n_i, n_j = rows // tm, ff // tf
    row = lambda i, j: (i, 0)
    shift = state is not None
    y_spec = pl.BlockSpec((tm, d), row)
    y_shape = jax.ShapeDtypeStruct((rows, d), F32)
    extra_in, out_specs, out_shape = [], y_spec, y_shape
    if shift:
        per_step = state.shape[0] // (n_i * n_j)
        assert per_step * n_i * n_j == state.shape[0]
        blk = lambda a: pl.BlockSpec((per_step,) + a.shape[1:], lambda i, j: (i * n_j + j, 0, 0))
        extra_in = [blk(state), blk(fresh)]
        out_specs = (y_spec, blk(state))
        out_shape = (y_shape, jax.ShapeDtypeStruct(state.shape, state.dtype))
    return pl.pallas_call(
        functools.partial(_mlp_kernel, shift=shift),
        grid=(n_i, n_j),
        in_specs=[
            pl.BlockSpec((tm, d), row),
            pl.BlockSpec((d, tf), lambda i, j: (0, j)),
            pl.BlockSpec((tf, d), lambda i, j: (j, 0)),
            pl.BlockSpec((tm, d), row),
            pl.BlockSpec((1, d), lambda i, j: (0, 0)),
        ] + extra_in,
        out_specs=out_specs,
        out_shape=out_shape,
        scratch_shapes=[pltpu.VMEM((tm, d), F32)],
        compiler_params=_params(2),
        name="mlp",
    )(h2, w_up, w_down, y1, gain, *((state, fresh) if shift else ()))


def _layer_weights(w_in, cmp_pos_k, cmp_w1_k, cmp_w2_k, cmp_pos_v, cmp_w1_v, cmp_w2_v,
                   pool_w, w_o, w_up, w_down, pool_dim):
    gate_lo = WKV_OFF + 2 * KV_DIM
    gate_hi = gate_lo + N_BRANCH * N_HEADS
    w_r = jnp.concatenate(
        [w_in[:, :gate_lo].astype(BF16), w_in[:, gate_hi:gate_hi + pool_dim].astype(BF16),
         jnp.pad(w_in[:, gate_lo:gate_hi].astype(BF16),
                 ((0, 0), (0, GATE_PAD - N_BRANCH * N_HEADS)))], axis=1)
    wcat = jnp.stack([jnp.concatenate([w1[:CMP_HALF], w1[CMP_HALF:]], axis=1)
                      for w1 in (cmp_w1_k, cmp_w1_v)]).astype(BF16)
    pos8 = jnp.stack([jnp.pad(p.reshape(1, CMP_IN), ((0, 7), (0, 0)))
                      for p in (cmp_pos_k, cmp_pos_v)]).astype(BF16)
    w2 = jnp.stack([cmp_w2_k, cmp_w2_v]).astype(BF16)
    return (w_r, wcat, pos8, w2, pool_w.astype(BF16), w_o.astype(BF16),
            w_up.astype(BF16), w_down.astype(BF16))


def kernel(x_prompt, x_sample, cache_kv, state_win_kv, state_pool, page_table, norm_mix_pre, w_in,
           cmp_pos_k, cmp_w1_k, cmp_w2_k, cmp_pos_v, cmp_w1_v, cmp_w2_v, pool_w, pool_scale, w_o,
           norm_mix_post, norm_mlp_pre, w_up, w_down, norm_mlp_post):
    batch, seq, d = x_prompt.shape
    nb, dec_seq, _ = x_sample.shape
    depth = w_in.shape[0]
    pool_dim = d - ATTN_DIM
    n_pages = page_table.shape[1]
    page = cache_kv.shape[2]
    past = n_pages * page
    wb = state_win_kv.shape[2]
    assert dec_seq == 1 and seq >= POOL_BUF and seq % CMP_STRIDE == 0
    assert w_in.shape[2] == ATTN_DIM + 6 * KV_DIM + N_BRANCH * N_HEADS + pool_dim

    tm_p = min(PROJ_ROWS, seq)
    tabs_p = _rope_tables(jnp.arange(seq, dtype=jnp.int32))
    tabs_s = _rope_tables(jnp.full((nb,), past, jnp.int32))
    nt_p = seq // tm_p

    y_p = x_prompt.reshape(batch * seq, d)
    y_s = x_sample.reshape(nb, d)
    kv_p, kv_s, win_p, win_s, pool_p, pool_s = [], [], [], [], [], []
    row_vec = lambda v: v.reshape(1, -1)
    for l in range(depth):
        w_r, wcat, pos8, w2, pw, wo, wu, wd = _layer_weights(
            w_in[l], cmp_pos_k[l], cmp_w1_k[l], cmp_w2_k[l], cmp_pos_v[l], cmp_w1_v[l],
            cmp_w2_v[l], pool_w[l], w_o[l], w_up[l], w_down[l], pool_dim)
        g_pre, g_post = row_vec(norm_mix_pre[l]), row_vec(norm_mix_post[l])
        g_mlp_pre, g_mlp_post = row_vec(norm_mlp_pre[l]), row_vec(norm_mlp_post[l])
        ps = row_vec(pool_scale[l])

        n_w = 2 * KV_HEADS
        win_rows = state_win_kv[l].reshape(nb, wb * n_w, HEAD_DIM)

        q_s, kv_s1, wkv_s, _, gates_s, u_s = _project(
            y_s, g_pre, w_r, tabs_s, lambda i: (i, 0), nb, pool_dim, False)

        q, kv, wkv, kva, gates, u, xc = _project(
            y_p, g_pre, w_r, tabs_p, lambda i: (i % nt_p, 0), tm_p, pool_dim, True)
        cmp_kv = _compress_prompt(xc, wcat, pos8, w2, batch, seq)
        o = _attend_prompt(q, cmp_kv, kva, gates, batch, seq)
        y1, h2 = _mix_prompt(o, u, y_p, pw, ps, wo, g_post, g_mlp_pre, batch, seq)
        y_p, win_rolled = _mlp(h2, y1, wu, wd, g_mlp_post, min(MLP_ROWS, batch * seq),
                               win_rows, wkv_s.reshape(nb, n_w, HEAD_DIM))
        kv_p.append(kv.reshape(batch, seq, CACHE_SLOTS, KV_HEADS, HEAD_DIM))
        wp = min(WINDOW, seq)
        win_p.append(wkv.reshape(batch, seq, 2, KV_HEADS, HEAD_DIM)[:, seq - wp:])
        pool_p.append(u.reshape(batch, seq, pool_dim)[:, seq - POOL_BUF:])

        o = _attend_sample(page_table, cache_kv[l], q_s, kv_s1, wkv_s, win_rows, gates_s,
                           wcat, pos8, w2)
        y1, h2 = _mix_sample(o.reshape(nb, ATTN_DIM), u_s, state_pool[l], y_s, pw, ps, wo,
                             g_post, g_mlp_pre, past)
        y_s = _mlp(h2, y1, wu, wd, g_mlp_post, nb)
        kv_s.append(kv_s1.reshape(nb, 1, CACHE_SLOTS, KV_HEADS, HEAD_DIM))
        win_s.append(win_rolled.reshape(nb, wb, 2, KV_HEADS, HEAD_DIM))
        pool_s.append(jnp.concatenate([state_pool[l], u_s[:, None]], axis=1)[:, 1:])

    return (y_p.reshape(batch, seq, d), y_s.reshape(nb, 1, d),
            jnp.stack(kv_p), jnp.stack(kv_s), jnp.stack(win_p), jnp.stack(win_s),
            jnp.stack(pool_p), jnp.stack(pool_s))
```

```python
import functools

import numpy as np
import jax
import jax.numpy as jnp
from jax import lax
from jax.experimental import pallas as pl
from jax.experimental.pallas import tpu as pltpu

N_HEADS = 8
HEAD_DIM = 128
KV_HEADS = 2
HPG = N_HEADS // KV_HEADS
ATTN_DIM = N_HEADS * HEAD_DIM
KV_DIM = KV_HEADS * HEAD_DIM
N_BRANCH = 3
POOL_WINDOWS = (2, 4, 8, 16)
POOL_GROUPS = len(POOL_WINDOWS)
POOL_BUF = max(POOL_WINDOWS) - 1
ROT_DIM = HEAD_DIM // 4
ROT_HALF = ROT_DIM // 2
ROPE_THETA = 500000.0
CMP_LEN = 32
CMP_STRIDE = 16
SEL_BLOCK = 64
SEL_TOPK = 16
WINDOW = 512
EPS = 1e-6
SCALE = HEAD_DIM ** -0.5
FORCE_SCORE = 1e4
NEG_INF = -1e30

LANES = 128
CACHE_SLOTS = 4
CMP_IN = CMP_LEN * HEAD_DIM
CMP_HALF = CMP_STRIDE * HEAD_DIM
GATE_PAD = LANES
VMEM_LIMIT = 56 * 1024 * 1024

PROJ_ROWS = 512
MLP_ROWS = 512
MLP_FF = 1024
ATTN_Q = 128
ATTN_K = 512
POOL_HALO = 16
SAMPLE_SEQS = 2
COPY_ROWS = 2

BF16 = jnp.bfloat16
F32 = jnp.float32


def _dot(a, b):
    return jnp.dot(a, b, preferred_element_type=F32)


def _dot_nt(a, b):
    return lax.dot_general(a, b, (((1,), (1,)), ((), ())), preferred_element_type=F32)


def _rms(x, g):
    return x * lax.rsqrt(jnp.mean(x * x, axis=-1, keepdims=True) + EPS) * g


def _params(n_axes):
    return pltpu.CompilerParams(
        dimension_semantics=("arbitrary",) * n_axes, vmem_limit_bytes=VMEM_LIMIT)


def _split_hi_lo(x):
    hi = x.astype(BF16)
    lo = (x - hi.astype(F32)).astype(BF16)
    return hi, lo


Q_OFF, KV_OFF, WKV_OFF = 0, ATTN_DIM, ATTN_DIM + 4 * KV_DIM


def _proj_kernel(x_ref, g_ref, w_ref, cos_ref, sa_ref, sb_ref, *rest, pool_dim, chunked):
    if chunked:
        q_ref, kv_ref, wkv_ref, kva_ref, gate_ref, u_ref, xc_ref, tmp_ref = rest
    else:
        q_ref, kv_ref, wkv_ref, kva_ref, gate_ref, u_ref = rest
    tm = x_ref.shape[0]
    u_off = WKV_OFF + 2 * KV_DIM
    gate_off = u_off + pool_dim
    h = _rms(x_ref[...], g_ref[...]).astype(BF16)
    cos, sa, sb = cos_ref[...], sa_ref[...], sb_ref[...]

    def rope(z):
        return (z * cos + pltpu.roll(z, LANES - ROT_HALF, 1) * sa
                + pltpu.roll(z, ROT_HALF, 1) * sb)

    zq = _dot(h, w_ref[:, Q_OFF:Q_OFF + ATTN_DIM])
    for hd in range(N_HEADS):
        sl = slice(hd * HEAD_DIM, (hd + 1) * HEAD_DIM)
        q_ref[:, sl] = rope(zq[:, sl]).astype(BF16)

    n_kv = CACHE_SLOTS * KV_HEADS
    zkv = _dot(h, w_ref[:, KV_OFF:KV_OFF + 4 * KV_DIM])
    for blk in range(n_kv):
        z = zkv[:, blk * HEAD_DIM:(blk + 1) * HEAD_DIM]
        if (blk // KV_HEADS) % 2 == 0:
            z = rope(z)
        kv_ref[pl.ds(blk, tm, stride=n_kv), :] = z
        if blk >= 2 * KV_HEADS:
            kva_ref[blk - 2 * KV_HEADS] = z.astype(BF16)
        elif chunked:
            tmp_ref[...] = z
            for r in range(CMP_STRIDE):
                xc_ref[blk, :, r * HEAD_DIM:(r + 1) * HEAD_DIM] = (
                    tmp_ref[pl.ds(r, tm // CMP_STRIDE, stride=CMP_STRIDE), :].astype(BF16))

    n_w = 2 * KV_HEADS
    zw = _dot(h, w_ref[:, WKV_OFF:WKV_OFF + 2 * KV_DIM])
    for blk in range(n_w):
        z = zw[:, blk * HEAD_DIM:(blk + 1) * HEAD_DIM]
        if blk < KV_HEADS:
            z = rope(z)
        wkv_ref[pl.ds(blk, tm, stride=n_w), :] = z
        kva_ref[2 * KV_HEADS + blk] = z.astype(BF16)

    u_ref[...] = _dot(h, w_ref[:, u_off:u_off + pool_dim])
    gl = _dot(h, w_ref[:, gate_off:gate_off + GATE_PAD])
    gate_ref[...] = 1.0 / (1.0 + jnp.exp(-gl))


def _project(x, gain, w_r, tables, table_index, tm, pool_dim, chunked):
    rows, d = x.shape
    n_proj = w_r.shape[1]
    n_kv = CACHE_SLOTS * KV_HEADS
    n_w = 2 * KV_HEADS
    row = lambda i: (i, 0)
    fixed = lambda i: (0, 0)
    tab_spec = pl.BlockSpec((tm, LANES), table_index)
    out_shape = [
        jax.ShapeDtypeStruct((rows, ATTN_DIM), BF16),
        jax.ShapeDtypeStruct((rows * n_kv, HEAD_DIM), F32),
        jax.ShapeDtypeStruct((rows * n_w, HEAD_DIM), F32),
        jax.ShapeDtypeStruct((n_w + n_kv // 2, rows, HEAD_DIM), BF16),
        jax.ShapeDtypeStruct((rows, GATE_PAD), F32),
        jax.ShapeDtypeStruct((rows, pool_dim), F32),
    ]
    out_specs = [
        pl.BlockSpec((tm, ATTN_DIM), row),
        pl.BlockSpec((tm * n_kv, HEAD_DIM), row),
        pl.BlockSpec((tm * n_w, HEAD_DIM), row),
        pl.BlockSpec((n_w + n_kv // 2, tm, HEAD_DIM), lambda i: (0, i, 0)),
        pl.BlockSpec((tm, GATE_PAD), row),
        pl.BlockSpec((tm, pool_dim), row),
    ]
    scratch = []
    if chunked:
        out_shape.append(jax.ShapeDtypeStruct((n_kv // 2, rows // CMP_STRIDE, CMP_HALF), BF16))
        out_specs.append(pl.BlockSpec((n_kv // 2, tm // CMP_STRIDE, CMP_HALF), lambda i: (0, i, 0)))
        scratch.append(pltpu.VMEM((tm, HEAD_DIM), F32))
    return pl.pallas_call(
        functools.partial(_proj_kernel, pool_dim=pool_dim, chunked=chunked),
        grid=(rows // tm,),
        in_specs=[
            pl.BlockSpec((tm, d), row),
            pl.BlockSpec((1, d), fixed),
            pl.BlockSpec((d, n_proj), fixed),
            tab_spec, tab_spec, tab_spec,
        ],
        out_specs=tuple(out_specs),
        out_shape=tuple(out_shape),
        scratch_shapes=scratch,
        compiler_params=_params(1),
        name="project",
    )(x, gain, w_r, *tables)


def _rope_tables(pos):
    inv = jnp.power(ROPE_THETA, -jnp.arange(ROT_HALF, dtype=F32) * (2.0 / ROT_DIM))
    ang = pos.astype(F32)[:, None] * inv[None, :]
    cos, sin = jnp.cos(ang), jnp.sin(ang)
    n = pos.shape[0]
    rest = LANES - ROT_DIM
    c = jnp.concatenate([cos, cos, jnp.ones((n, rest), F32)], axis=1)
    sa = jnp.concatenate([-sin, jnp.zeros((n, LANES - ROT_HALF), F32)], axis=1)
    sb = jnp.concatenate([jnp.zeros((n, ROT_HALF), F32), sin, jnp.zeros((n, rest), F32)], axis=1)
    return c, sa, sb


def _gelu_tanh(x):
    return 0.5 * x * (1.0 + jnp.tanh(0.7978845608028654 * (x + 0.044715 * (x * x * x))))


def _compress_first(x, wcat, pos8):
    ab = _dot(x, wcat)
    pa = _dot(pos8[:, :CMP_HALF], wcat)[0:1, :HEAD_DIM]
    pb = _dot(pos8[:, CMP_HALF:], wcat)[0:1, HEAD_DIM:]
    return ab, pa + pb


def _compress_hidden(ab, pos_term):
    rows = ab.shape[0]
    b_next = pltpu.roll(ab[:, HEAD_DIM:], rows - 1, 0)
    return _gelu_tanh(ab[:, :HEAD_DIM] + b_next + pos_term).astype(BF16)


def _compress_kernel(x_ref, wcat_ref, pos_ref, w2_ref, o_ref):
    g, _, nck, width = x_ref.shape
    x = x_ref[...].reshape(g * nck, width)
    hid = _compress_hidden(*_compress_first(x, wcat_ref[0], pos_ref[0]))
    o_ref[...] = _dot(hid, w2_ref[0]).astype(BF16).reshape(o_ref.shape)


def _compress_prompt(xc, wcat, pos8, w2, batch, seq):
    nck = seq // CMP_STRIDE
    x = xc.reshape(xc.shape[0], batch, nck, CMP_HALF)
    return pl.pallas_call(
        _compress_kernel,
        grid=(batch, 2),
        in_specs=[
            pl.BlockSpec((KV_HEADS, 1, nck, CMP_HALF), lambda b, s: (s, b, 0, 0)),
            pl.BlockSpec((1, CMP_HALF, 2 * HEAD_DIM), lambda b, s: (s, 0, 0)),
            pl.BlockSpec((1, 8, CMP_IN), lambda b, s: (s, 0, 0)),
            pl.BlockSpec((1, HEAD_DIM, HEAD_DIM), lambda b, s: (s, 0, 0)),
        ],
        out_specs=pl.BlockSpec((1, 1, KV_HEADS, nck, HEAD_DIM), lambda b, s: (b, s, 0, 0, 0)),
        out_shape=jax.ShapeDtypeStruct((batch, 2, KV_HEADS, nck, HEAD_DIM), BF16),
        compiler_params=_params(2),
        name="compress_prompt",
    )(x, wcat, pos8, w2)


def _topk_rows(score, jidx, topk):
    rank = jnp.zeros_like(score)
    for j in range(score.shape[0]):
        bj = score[j:j + 1, :]
        tie = jnp.where(jidx > j, 1.0, 0.0)
        rank = rank + jnp.where(bj > score, 1.0, jnp.where(bj == score, tie, 0.0))
    return jnp.where(rank < topk, 1.0, 0.0)


def _softmax_parts(s, ok):
    s = jnp.where(ok, s, NEG_INF)
    m = jnp.max(s, axis=-1, keepdims=True)
    e = jnp.where(ok, jnp.exp(s - m), 0.0)
    d = jnp.sum(e, axis=-1, keepdims=True)
    return e, jnp.where(d > 0, d, 1.0)


EXP2_SCALE = SCALE * 1.4426950408889634
HEAD_PAIRS = HPG // 2


def _attn_prompt_kernel(q_ref, kc_ref, vc_ref, ks_ref, vs_ref, kw_ref, vw_ref, gate_ref,
                        mt_ref, eye_ref, o_ref,
                        vct_ref, vst_ref, vwt_ref, sel_ref, gt_ref, acc_ref,
                        *, tq, tk, seq, topk):
    i = pl.program_id(1)
    t0 = i * tq
    n_sel = mt_ref.shape[0]
    pair_w = 2 * tq
    eye = eye_ref[...]
    groups = range(KV_HEADS)
    pairs = range(KV_HEADS * HEAD_PAIRS)
    group_of = [hp // HEAD_PAIRS for hp in pairs]

    @pl.when(i == 0)
    def _():
        for g in groups:
            vct_ref[g] = _dot_nt(eye, vc_ref[0, 0, g]).astype(BF16)
            vst_ref[g] = _dot_nt(eye, vs_ref[g]).astype(BF16)
            vwt_ref[g] = _dot_nt(eye, vw_ref[g]).astype(BF16)

    q_all = q_ref[...]
    q_pairs = [jnp.concatenate([q_all[:, (2 * hp) * HEAD_DIM:(2 * hp + 1) * HEAD_DIM],
                                q_all[:, (2 * hp + 1) * HEAD_DIM:(2 * hp + 2) * HEAD_DIM]], axis=0)
               for hp in pairs]

    def both_heads(x):
        return jnp.concatenate([x, x], axis=1)

    def tpos(n_keys):
        return t0 + lax.broadcasted_iota(jnp.int32, (n_keys, tq), 1)

    def kidx(n_keys):
        return lax.broadcasted_iota(jnp.int32, (n_keys, tq), 0)


    ncp = kc_ref.shape[3]
    wk = min(WINDOW + tq, seq)
    ws = pl.multiple_of(jnp.maximum(t0 + tq - wk, 0), tq)
    s_cmp = [_dot_nt(kc_ref[0, 0, group_of[hp]], q_pairs[hp]) for hp in pairs]
    s_win = [_dot_nt(kw_ref[group_of[hp], pl.ds(ws, wk), :], q_pairs[hp]) for hp in pairs]

    ok = both_heads(jnp.where(kidx(ncp) * CMP_STRIDE + (CMP_LEN - 1) <= tpos(ncp), 1.0, 0.0)) > 0.5
    p_cmp, p_sum = [], [None] * KV_HEADS
    for hp in pairs:
        s = jnp.where(ok, s_cmp[hp], NEG_INF)
        m = jnp.max(s, axis=0, keepdims=True)
        e = jnp.where(ok, jnp.exp2((s - m) * EXP2_SCALE), 0.0)
        d = jnp.sum(e, axis=0, keepdims=True)
        p = e / jnp.where(d > 0, d, 1.0)
        p_cmp.append(p.astype(BF16))
        ph = p[:, :tq] + p[:, tq:]
        g = group_of[hp]
        p_sum[g] = ph if p_sum[g] is None else p_sum[g] + ph

    mt = mt_ref[...]
    split = [_split_hi_lo(p_sum[g]) for g in groups]
    imp = [_dot(mt, split[g][0]) + _dot(mt, split[g][1]) for g in groups]
    o_cmp = [_dot(vct_ref[group_of[hp]], p_cmp[hp]) for hp in pairs]

    kpos = ws + kidx(wk)
    bias = both_heads(jnp.where(kpos <= tpos(wk),
                                jnp.where(kpos > tpos(wk) - WINDOW, 0.0, NEG_INF), NEG_INF))
    p_win, l_win = [], []
    for hp in pairs:
        s = s_win[hp] + bias
        p = jnp.exp2((s - jnp.max(s, axis=0, keepdims=True)) * EXP2_SCALE)
        l_win.append(jnp.sum(p, axis=0, keepdims=True))
        p_win.append(p.astype(BF16))
    o_win = [_dot(vwt_ref[group_of[hp], :, pl.ds(ws, wk)], p_win[hp]) for hp in pairs]
    o_win = [o_win[hp] / l_win[hp] for hp in pairs]

    jidx = kidx(n_sel)
    jt = tpos(n_sel) // SEL_BLOCK
    forced = jnp.where(jidx == 0, 1.0, jnp.where(jidx == jt, 1.0, jnp.where(jidx == jt - 1, 1.0, 0.0)))
    for g in groups:
        score = jnp.where(forced > 0.5, FORCE_SCORE, jnp.where(jidx <= jt, imp[g], -1.0))
        sel_ref[g] = _topk_rows(score, jidx, topk)

    acc_ref[...] = jnp.zeros_like(acc_ref)

    def sel_step(kb, carry):
        k0 = pl.multiple_of(kb * tk, tk)
        blk0 = kb * (tk // SEL_BLOCK)
        causal = k0 + kidx(tk) <= tpos(tk)
        bias = []
        for g in groups:
            chosen = jnp.concatenate(
                [jnp.broadcast_to(sel_ref[g, pl.ds(blk0 + j, 1), :], (SEL_BLOCK, tq))
                 for j in range(tk // SEL_BLOCK)], axis=0)
            bias.append(both_heads(
                jnp.where(causal, jnp.where(chosen > 0.5, 0.0, NEG_INF), NEG_INF)))
        ss = [_dot_nt(ks_ref[group_of[hp], pl.ds(k0, tk), :], q_pairs[hp]) + bias[group_of[hp]]
              for hp in pairs]
        out, ps, alphas = [], [], []
        for hp in pairs:
            m, l = carry[2 * hp], carry[2 * hp + 1]
            m_new = jnp.maximum(m, jnp.max(ss[hp], axis=0, keepdims=True))
            p = jnp.exp2((ss[hp] - m_new) * EXP2_SCALE)
            alpha = jnp.exp2((m - m_new) * EXP2_SCALE)
            out += [m_new, alpha * l + jnp.sum(p, axis=0, keepdims=True)]
            ps.append(p.astype(BF16))
            alphas.append(alpha)
        pvs = [_dot(vst_ref[group_of[hp], :, pl.ds(k0, tk)], ps[hp]) for hp in pairs]
        for hp in pairs:
            acc_ref[hp] = alphas[hp] * acc_ref[hp] + pvs[hp]
        return tuple(out)

    n_kb = (t0 + tq + tk - 1) // tk
    init = (jnp.full((1, pair_w), NEG_INF, F32), jnp.zeros((1, pair_w), F32)) * len(pairs)
    stats = lax.fori_loop(0, n_kb, sel_step, init)
    o_sel = [acc_ref[hp] / stats[2 * hp + 1] for hp in pairs]

    gt_ref[...] = gate_ref[...].T
    o_t = []
    for h in range(N_HEADS):
        hp, lanes = h // 2, slice((h % 2) * tq, (h % 2 + 1) * tq)
        col = h * N_BRANCH
        o_t.append((gt_ref[col:col + 1, :] * o_cmp[hp][:, lanes]
                    + gt_ref[col + 1:col + 2, :] * o_sel[hp][:, lanes]
                    + gt_ref[col + 2:col + 3, :] * o_win[hp][:, lanes]).astype(BF16))
    outs = [_dot_nt(eye, o_t[h]) for h in range(N_HEADS)]
    for h in range(N_HEADS):
        o_ref[:, h * HEAD_DIM:(h + 1) * HEAD_DIM] = outs[h].astype(BF16)


def _cmp_to_sel(n_cmp_pad, n_cmp, n_sel):
    cs = np.arange(n_cmp_pad)[:, None] * CMP_STRIDE
    ss = np.arange(n_sel)[None, :] * SEL_BLOCK
    hit = (cs < ss + SEL_BLOCK) & (cs + CMP_LEN > ss) & (np.arange(n_cmp_pad)[:, None] < n_cmp)
    return hit.astype(np.float32)


def _attend_prompt(q, cmp_kv, kva, gates, batch, seq):
    tq = min(ATTN_Q, seq)
    tk = min(ATTN_K, seq)
    nq = seq // tq
    ncp = seq // CMP_STRIDE
    n_sel = -(-seq // SEL_BLOCK)
    topk = min(SEL_TOPK, n_sel)
    assert tq == HEAD_DIM and seq % tk == 0 and WINDOW % tq == 0
    mt = jnp.asarray(_cmp_to_sel(ncp, ncp - 1, n_sel).T, BF16)
    eye = jnp.asarray(np.eye(tq, dtype=np.float32), BF16)
    kv_spec = lambda slot: pl.BlockSpec((KV_HEADS, seq, HEAD_DIM), lambda b, i: (slot, b, 0))
    cmp_spec = lambda s: pl.BlockSpec(
        (1, 1, KV_HEADS, ncp, HEAD_DIM), lambda b, i: (b, s, 0, 0, 0))
    full = lambda a: pl.BlockSpec(a.shape, lambda b, i: (0, 0))
    return pl.pallas_call(
        functools.partial(_attn_prompt_kernel, tq=tq, tk=tk, seq=seq, topk=topk),
        grid=(batch, nq),
        in_specs=[
            pl.BlockSpec((tq, ATTN_DIM), lambda b, i: (b * nq + i, 0)),
            cmp_spec(0), cmp_spec(1),
            kv_spec(0), kv_spec(1), kv_spec(2), kv_spec(3),
            pl.BlockSpec((tq, GATE_PAD), lambda b, i: (b * nq + i, 0)),
            full(mt), full(eye),
        ],
        out_specs=pl.BlockSpec((tq, ATTN_DIM), lambda b, i: (b * nq + i, 0)),
        out_shape=jax.ShapeDtypeStruct((batch * seq, ATTN_DIM), BF16),
        scratch_shapes=[
            pltpu.VMEM((KV_HEADS, HEAD_DIM, ncp), BF16),
            pltpu.VMEM((KV_HEADS, HEAD_DIM, seq), BF16),
            pltpu.VMEM((KV_HEADS, HEAD_DIM, seq), BF16),
            pltpu.VMEM((KV_HEADS, n_sel, tq), F32),
            pltpu.VMEM((GATE_PAD, tq), F32),
            pltpu.VMEM((N_HEADS // 2, HEAD_DIM, 2 * tq), F32),
        ],
        compiler_params=_params(2),
        name="attend_prompt",
    )(q, cmp_kv, cmp_kv, kva, kva, kva, kva, gates, mt, eye)


def _attn_sample_kernel(pt_ref, cache_ref, q_ref, kvn_ref, wn_ref, win_ref, gate_ref,
                        wcat_ref, pos_ref, w2_ref, ms_ref, ek_ref, o_ref, buf, sem,
                        *, n_pages, page, past, topk, n_sel):
    b = pl.program_id(0)
    nb = pl.num_programs(0)
    n_kv = CACHE_SLOTS * KV_HEADS
    n_w = 2 * KV_HEADS
    cpp = page // CMP_STRIDE
    nch = past // CMP_STRIDE
    n_rg = CMP_STRIDE // COPY_ROWS
    wb = win_ref.shape[1] // n_w

    n_seq = q_ref.shape[0]
    seqs = range(n_seq)

    def for_all_copies(step, slot, act):
        def per_page(pi, carry):
            dst0 = pl.multiple_of(pi * cpp, cpp)
            for s in seqs:
                src0 = pl.multiple_of(pt_ref[step * n_seq + s, pi] * cpp, cpp)
                for rg in range(n_rg):
                    act(pltpu.make_async_copy(cache_ref.at[pl.ds(src0, cpp), rg],
                                              buf.at[slot, s, rg, pl.ds(dst0, cpp)], sem.at[slot]))
            return carry
        lax.fori_loop(0, n_pages, per_page, 0)

    @pl.when(b == 0)
    def _():
        for_all_copies(0, 0, lambda c: c.start())

    @pl.when(b + 1 < nb)
    def _():
        for_all_copies(b + 1, (b + 1) % 2, lambda c: c.start())

    slot = b % 2
    for_all_copies(b, slot, lambda c: c.wait())

    def buf_rows():
        return buf.reshape(2, n_seq, n_rg, nch * COPY_ROWS * n_kv, HEAD_DIM)

    def chunk_rows(s, cache_slot, g, r):
        first = (r % COPY_ROWS) * n_kv + cache_slot * KV_HEADS + g
        return buf_rows()[slot, s, r // COPY_ROWS, pl.ds(first, nch, stride=COPY_ROWS * n_kv), :]

    def group_rows(s, cache_slot, g, rg):
        return buf_rows()[slot, s, rg, pl.ds(cache_slot * KV_HEADS + g, COPY_ROWS * nch,
                                             stride=n_kv), :]

    row8 = lax.broadcasted_iota(jnp.int32, (N_HEADS, 1), 0)
    in_g0 = row8 < HPG

    def by_group(x0, x1):
        return jnp.where(in_g0, x0, x1)

    q8 = [q_ref[s] for s in seqs]
    qf = [q8[s].astype(F32) for s in seqs]
    groups = range(KV_HEADS)

    def scores(s, keys):
        return by_group(_dot_nt(q8[s], keys[0]), _dot_nt(q8[s], keys[1])) * SCALE

    def cached(s, cache_slot, g):
        return jnp.concatenate([group_rows(s, cache_slot, g, rg).astype(BF16)
                                for rg in range(n_rg)], axis=0)

    def compress_input(cache_slot):
        return jnp.concatenate(
            [jnp.concatenate([chunk_rows(s, cache_slot, g, r).astype(BF16)
                              for r in range(CMP_STRIDE)], axis=1)
             for s in seqs for g in groups], axis=0)

    first = [_compress_first(compress_input(cs), wcat_ref[cs], pos_ref[cs]) for cs in range(2)]
    s_sel = [scores(s, [cached(s, 2, g) for g in groups]) for s in seqs]
    s_win = [scores(s, [win_ref[s, pl.ds(g, wb, stride=n_w), :].astype(BF16) for g in groups])
             for s in seqs]
    kc, vc = [_dot(_compress_hidden(*first[cs]), w2_ref[cs]).astype(BF16) for cs in range(2)]

    def compressed(x, s, g):
        r0 = (s * KV_HEADS + g) * nch
        return x[r0:r0 + nch]

    lane_n = lax.broadcasted_iota(jnp.int32, (N_HEADS, nch), 1)
    ok = lane_n * CMP_STRIDE + (CMP_LEN - 1) <= past
    s_cmp = [by_group(*[_dot_nt(q8[s], compressed(kc, s, g)) for g in groups]) * SCALE
             for s in seqs]
    p_cmp = []
    for s in seqs:
        e, d = _softmax_parts(s_cmp[s], ok)
        p_cmp.append(e / d)
    o_cmp = [by_group(*[_dot(p_cmp[s].astype(BF16), compressed(vc, s, g)) for g in groups])
             for s in seqs]

    jl = lax.broadcasted_iota(jnp.int32, (N_HEADS, LANES), 1)
    jt = past // SEL_BLOCK
    forced = jnp.where(jl == 0, 1.0, jnp.where(jl == jt, 1.0, jnp.where(jl == jt - 1, 1.0, 0.0)))
    ii = lax.broadcasted_iota(jnp.int32, (LANES, LANES), 0)
    jj = lax.broadcasted_iota(jnp.int32, (LANES, LANES), 1)
    tie = jnp.where(ii < jj, 1.0, 0.0)
    imp = []
    for s in seqs:
        p_g = [jnp.sum(p_cmp[s][g * HPG:(g + 1) * HPG], axis=0, keepdims=True) for g in groups]
        p2 = jnp.concatenate(p_g + [jnp.zeros((N_HEADS - KV_HEADS, nch), F32)], axis=0)
        p_hi, p_lo = _split_hi_lo(p2)
        imp.append(_dot(p_hi, ms_ref[...]) + _dot(p_lo, ms_ref[...]))
    sel2 = []
    for s in seqs:
        score = jnp.where(forced > 0.5, FORCE_SCORE, jnp.where(jl <= jt, imp[s], -1.0))
        score = jnp.where(jl < n_sel, score, -2.0)
        sel_rows = []
        for g in groups:
            srow = jnp.broadcast_to(score[g:g + 1, :], (LANES, LANES))
            scol = jnp.sum(jnp.where(ii == jj, srow, 0.0), axis=1, keepdims=True)
            beats = jnp.where(scol > srow, 1.0, jnp.where(scol == srow, tie, 0.0))
            rank = jnp.sum(beats, axis=0, keepdims=True)
            sel_rows.append(jnp.where(rank < topk, 1.0, 0.0))
        sel2.append(jnp.concatenate(sel_rows + [jnp.zeros((N_HEADS - KV_HEADS, LANES), F32)], axis=0))
    chunk_ok2 = [_dot(sel2[s].astype(BF16), ek_ref[...]) for s in seqs]

    def new_row(ref, s, idx0):
        x = by_group(ref[s, idx0:idx0 + 1, :], ref[s, idx0 + 1:idx0 + 2, :])
        return x.astype(BF16).astype(F32)

    def weights(s, scores_s, ok, k_new, new_ok):
        s_new = jnp.sum(qf[s] * k_new, axis=-1, keepdims=True) * SCALE
        sc = jnp.where(ok, scores_s, NEG_INF)
        s_new = jnp.where(new_ok, s_new, NEG_INF)
        m = jnp.maximum(jnp.max(sc, axis=-1, keepdims=True), s_new)
        e = jnp.where(ok, jnp.exp(sc - m), 0.0)
        e_new = jnp.where(new_ok, jnp.exp(s_new - m), 0.0)
        d = jnp.sum(e, axis=-1, keepdims=True) + e_new
        return e.astype(BF16), e_new, jnp.where(d > 0, d, 1.0)

    def weighted(w, vals, v_new):
        eb, e_new, d = w
        return (by_group(_dot(eb, vals[0]), _dot(eb, vals[1])) + e_new * v_new) / d

    kpos = past - wb + lax.broadcasted_iota(jnp.int32, (N_HEADS, wb), 1)
    w_sel, w_win = [], []
    for s in seqs:
        chunk_ok = by_group(chunk_ok2[s][0:1], chunk_ok2[s][1:2])
        key_ok = jnp.concatenate([chunk_ok] * n_rg, axis=1)
        new_ok = by_group(*[jnp.sum(jnp.where(jl[0:1] == jt, sel2[s][g:g + 1], 0.0), axis=1,
                                    keepdims=True) for g in groups])
        w_sel.append(weights(s, s_sel[s], key_ok > 0.5, new_row(kvn_ref, s, 2 * KV_HEADS),
                             new_ok > 0.5))
        w_win.append(weights(s, s_win[s], kpos > past - WINDOW, new_row(wn_ref, s, 0), row8 >= 0))
    o_sel = [weighted(w_sel[s], [cached(s, 3, g) for g in groups],
                      new_row(kvn_ref, s, 3 * KV_HEADS)) for s in seqs]
    o_win = [weighted(w_win[s],
                      [win_ref[s, pl.ds(KV_HEADS + g, wb, stride=n_w), :].astype(BF16)
                       for g in groups], new_row(wn_ref, s, KV_HEADS)) for s in seqs]

    lane = lax.broadcasted_iota(jnp.int32, (N_HEADS, GATE_PAD), 1)
    for s in seqs:
        gates = jnp.broadcast_to(gate_ref[s], (N_HEADS, GATE_PAD))

        def gate(br):
            return jnp.sum(jnp.where(lane == row8 * N_BRANCH + br, gates, 0.0), axis=-1,
                           keepdims=True)

        o_ref[s] = (gate(0) * o_cmp[s] + gate(1) * o_sel[s] + gate(2) * o_win[s]).astype(BF16)


def _attend_sample(page_table, cache, q, kv_new, wkv_new, win_rows, gates, wcat, pos8, w2):
    nb, n_pages = page_table.shape
    n_phys, page = cache.shape[:2]
    n_kv = CACHE_SLOTS * KV_HEADS
    n_w = 2 * KV_HEADS
    past = n_pages * page
    nch = past // CMP_STRIDE
    wb = win_rows.shape[1] // n_w
    n_sel = -(-(past + 1) // SEL_BLOCK)
    n_cmp = (past + 1) // CMP_STRIDE - 1
    topk = min(SEL_TOPK, n_sel)
    assert n_sel <= LANES and nch * CMP_STRIDE == past
    ms = np.zeros((nch, LANES), np.float32)
    ms[:, :n_sel] = _cmp_to_sel(nch, n_cmp, n_sel)
    ek = ((np.arange(COPY_ROWS * nch)[None, :] // COPY_ROWS) * CMP_STRIDE // SEL_BLOCK
          == np.arange(LANES)[:, None])
    ms = jnp.asarray(ms, BF16)
    ek = jnp.asarray(ek.astype(np.float32), BF16)
    n_seq = SAMPLE_SEQS if nb % SAMPLE_SEQS == 0 else 1
    seq3 = lambda n: pl.BlockSpec((n_seq, n, HEAD_DIM), lambda b, pt: (b, 0, 0))
    full3 = lambda a: pl.BlockSpec(a.shape, lambda b, pt: (0, 0, 0))
    full2 = lambda a: pl.BlockSpec(a.shape, lambda b, pt: (0, 0))
    grid_spec = pltpu.PrefetchScalarGridSpec(
        num_scalar_prefetch=1,
        grid=(nb // n_seq,),
        in_specs=[
            pl.BlockSpec(memory_space=pl.ANY),
            seq3(N_HEADS), seq3(n_kv), seq3(n_w), seq3(wb * n_w),
            pl.BlockSpec((n_seq, 1, GATE_PAD), lambda b, pt: (b, 0, 0)),
            full3(wcat), full3(pos8), full3(w2), full2(ms), full2(ek),
        ],
        out_specs=seq3(N_HEADS),
        scratch_shapes=[pltpu.VMEM((2, n_seq, CMP_STRIDE // COPY_ROWS, nch, COPY_ROWS * n_kv,
                                    HEAD_DIM), F32),
                        pltpu.SemaphoreType.DMA((2,))],
    )
    return pl.pallas_call(
        functools.partial(_attn_sample_kernel, n_pages=n_pages, page=page, past=past,
                          topk=topk, n_sel=n_sel),
        grid_spec=grid_spec,
        out_shape=jax.ShapeDtypeStruct((nb, N_HEADS, HEAD_DIM), BF16),
        compiler_params=_params(1),
        name="attend_sample",
    )(page_table, cache.reshape(n_phys * page // CMP_STRIDE, CMP_STRIDE // COPY_ROWS,
                                COPY_ROWS * n_kv, HEAD_DIM),
      q.reshape(nb, N_HEADS, HEAD_DIM), kv_new.reshape(nb, n_kv, HEAD_DIM),
      wkv_new.reshape(nb, n_w, HEAD_DIM), win_rows,
      gates.reshape(nb, 1, GATE_PAD), wcat, pos8, w2, ms, ek)


def _mix_tail(o, diffs, x, pw_ref, ps_ref, wo_ref, gpost_ref, gpre_ref, y1_ref, h2_ref):
    gw = diffs[0].shape[1]
    ys = [(_dot(diffs[g].astype(BF16), pw_ref[g]) * ps_ref[:, g * gw:(g + 1) * gw]).astype(BF16)
          for g in range(POOL_GROUPS)]
    cat = jnp.concatenate([o] + ys, axis=1)
    m = _dot(cat, wo_ref[...])
    y1 = x + _rms(m, gpost_ref[...])
    y1_ref[...] = y1
    h2_ref[...] = _rms(y1, gpre_ref[...]).astype(BF16)


def _mix_prompt_kernel(o_ref, u_ref, halo_ref, x_ref, pw_ref, ps_ref, wo_ref, gpost_ref, gpre_ref,
                       y1_ref, h2_ref, *, tm):
    i = pl.program_id(1)
    halo_rows = halo_ref.shape[0]
    halo = jnp.where(i > 0, halo_ref[...], 0.0)
    u = u_ref[...]
    uext = jnp.concatenate([halo, u], axis=0)
    n_ext = uext.shape[0]
    gw = u.shape[1] // POOL_GROUPS
    tpos = i * tm + lax.broadcasted_iota(jnp.int32, (tm, 1), 0)
    diffs = []
    for g, w in enumerate(POOL_WINDOWS):
        s = uext[:, g * gw:(g + 1) * gw]
        k = 1
        while k < w:
            s = s + pltpu.roll(s, k, 0)
            k *= 2
        cnt = jnp.minimum(w, tpos + 1).astype(F32)
        diffs.append(s[halo_rows:n_ext] / cnt - u[:, g * gw:(g + 1) * gw])
    _mix_tail(o_ref[...], diffs, x_ref[...], pw_ref, ps_ref, wo_ref, gpost_ref, gpre_ref,
              y1_ref, h2_ref)


def _mix_sample_kernel(o_ref, u_ref, st_ref, x_ref, pw_ref, ps_ref, wo_ref, gpost_ref, gpre_ref,
                       y1_ref, h2_ref, *, past):
    u = u_ref[...]
    c = u.shape[1]
    gw = c // POOL_GROUPS
    n_hist = st_ref.shape[1] // c
    diffs = []
    for g, w in enumerate(POOL_WINDOWS):
        un = u[:, g * gw:(g + 1) * gw]
        s = un
        for back in range(1, w):
            r = n_hist - back
            s = s + st_ref[:, r * c + g * gw:r * c + (g + 1) * gw]
        diffs.append(s / float(min(w, past + 1)) - un)
    _mix_tail(o_ref[...], diffs, x_ref[...], pw_ref, ps_ref, wo_ref, gpost_ref, gpre_ref,
              y1_ref, h2_ref)


def _mix_specs(tm, d, c, pool_w, idx):
    fixed2 = lambda *a: (0, 0)
    fixed3 = lambda *a: (0, 0, 0)
    weights = [
        pl.BlockSpec(pool_w.shape, fixed3),
        pl.BlockSpec((1, c), fixed2),
        pl.BlockSpec((d, d), fixed2),
        pl.BlockSpec((1, d), fixed2),
        pl.BlockSpec((1, d), fixed2),
    ]
    outs = (pl.BlockSpec((tm, d), idx), pl.BlockSpec((tm, d), idx))
    return weights, outs


def _mix_prompt(o, u, x, pool_w, pool_scale, w_o, g_post, g_pre, batch, seq):
    rows, d = x.shape
    c = u.shape[1]
    tm = min(PROJ_ROWS, seq)
    nt = seq // tm
    halo = POOL_HALO
    assert halo >= POOL_BUF and seq % tm == 0 and tm % halo == 0
    idx = lambda b, i: (b * nt + i, 0)
    halo_idx = lambda b, i: (jnp.maximum((b * nt + i) * (tm // halo) - 1, 0), 0)
    weights, outs = _mix_specs(tm, d, c, pool_w, idx)
    return pl.pallas_call(
        functools.partial(_mix_prompt_kernel, tm=tm),
        grid=(batch, nt),
        in_specs=[pl.BlockSpec((tm, ATTN_DIM), idx), pl.BlockSpec((tm, c), idx),
                  pl.BlockSpec((halo, c), halo_idx), pl.BlockSpec((tm, d), idx)] + weights,
        out_specs=outs,
        out_shape=(jax.ShapeDtypeStruct((rows, d), F32), jax.ShapeDtypeStruct((rows, d), BF16)),
        compiler_params=_params(2),
        name="mix_prompt",
    )(o, u, u, x, pool_w, pool_scale, w_o, g_post, g_pre)


def _mix_sample(o, u, pool_state, x, pool_w, pool_scale, w_o, g_post, g_pre, past):
    rows, d = x.shape
    c = u.shape[1]
    tm = rows
    idx = lambda i: (i, 0)
    weights, outs = _mix_specs(tm, d, c, pool_w, idx)
    st = pool_state.reshape(rows, -1)
    return pl.pallas_call(
        functools.partial(_mix_sample_kernel, past=past),
        grid=(rows // tm,),
        in_specs=[pl.BlockSpec((tm, ATTN_DIM), idx), pl.BlockSpec((tm, c), idx),
                  pl.BlockSpec((tm, st.shape[1]), idx), pl.BlockSpec((tm, d), idx)] + weights,
        out_specs=outs,
        out_shape=(jax.ShapeDtypeStruct((rows, d), F32), jax.ShapeDtypeStruct((rows, d), BF16)),
        compiler_params=_params(1),
        name="mix_sample",
    )(o, u, st, x, pool_w, pool_scale, w_o, g_post, g_pre)


def _mlp_kernel(h_ref, wu_ref, wd_ref, y1_ref, g_ref, *rest, shift):
    if shift:
        state_ref, fresh_ref, y_ref, rolled_ref, acc_ref = rest
        n_new = fresh_ref.shape[1]
        n_keep = state_ref.shape[1] - n_new
        rolled_ref[:, pl.ds(0, n_keep), :] = state_ref[:, pl.ds(n_new, n_keep), :]
        rolled_ref[:, pl.ds(n_keep, n_new), :] = fresh_ref[...]
    else:
        y_ref, acc_ref = rest
    j = pl.program_id(1)

    @pl.when(j == 0)
    def _():
        acc_ref[...] = jnp.zeros_like(acc_ref)

    a = jnp.maximum(_dot(h_ref[...], wu_ref[...]), 0.0)
    acc_ref[...] += _dot((a * a).astype(BF16), wd_ref[...])

    @pl.when(j == pl.num_programs(1) - 1)
    def _():
        y_ref[...] = y1_ref[...] + _rms(acc_ref[...], g_ref[...])


def _mlp(h2, y1, w_up, w_down, gain, tm, state=None, fresh=None):
    rows, d = y1.shape
    ff = w_up.shape[1]
    tf = min(MLP_FF, ff)
    n_i, n_j = rows // tm, ff // tf
    row = lambda i, j: (i, 0)
    shift = state is not None
    y_spec = pl.BlockSpec((tm, d), row)
    y_shape = jax.ShapeDtypeStruct((rows, d), F32)
    extra_in, out_specs, out_shape = [], y_spec, y_shape
    if shift:
        per_step = state.shape[0] // (n_i * n_j)
        assert per_step * n_i * n_j == state.shape[0]
        blk = lambda a: pl.BlockSpec((per_step,) + a.shape[1:], lambda i, j: (i * n_j + j, 0, 0))
        extra_in = [blk(state), blk(fresh)]
        out_specs = (y_spec, blk(state))
        out_shape = (y_shape, jax.ShapeDtypeStruct(state.shape, state.dtype))
    return pl.pallas_call(
        functools.partial(_mlp_kernel, shift=shift),
        grid=(n_i, n_j),
        in_specs=[
            pl.BlockSpec((tm, d), row),
            pl.BlockSpec((d, tf), lambda i, j: (0, j)),
            pl.BlockSpec((tf, d), lambda i, j: (j, 0)),
            pl.BlockSpec((tm, d), row),
            pl.BlockSpec((1, d), lambda i, j: (0, 0)),
        ] + extra_in,
        out_specs=out_specs,
        out_shape=out_shape,
        scratch_shapes=[pltpu.VMEM((tm, d), F32)],
        compiler_params=_params(2),
        name="mlp",
    )(h2, w_up, w_down, y1, gain, *((state, fresh) if shift else ()))


def _layer_weights(w_in, cmp_pos_k, cmp_w1_k, cmp_w2_k, cmp_pos_v, cmp_w1_v, cmp_w2_v,
                   pool_w, w_o, w_up, w_down, pool_dim):
    gate_lo = WKV_OFF + 2 * KV_DIM
    gate_hi = gate_lo + N_BRANCH * N_HEADS
    w_r = jnp.concatenate(
        [w_in[:, :gate_lo].astype(BF16), w_in[:, gate_hi:gate_hi + pool_dim].astype(BF16),
         jnp.pad(w_in[:, gate_lo:gate_hi].astype(BF16),
                 ((0, 0), (0, GATE_PAD - N_BRANCH * N_HEADS)))], axis=1)
    wcat = jnp.stack([jnp.concatenate([w1[:CMP_HALF], w1[CMP_HALF:]], axis=1)
                      for w1 in (cmp_w1_k, cmp_w1_v)]).astype(BF16)
    pos8 = jnp.stack([jnp.pad(p.reshape(1, CMP_IN), ((0, 7), (0, 0)))
                      for p in (cmp_pos_k, cmp_pos_v)]).astype(BF16)
    w2 = jnp.stack([cmp_w2_k, cmp_w2_v]).astype(BF16)
    return (w_r, wcat, pos8, w2, pool_w.astype(BF16), w_o.astype(BF16),
            w_up.astype(BF16), w_down.astype(BF16))


def kernel(x_prompt, x_sample, cache_kv, state_win_kv, state_pool, page_table, norm_mix_pre, w_in,
           cmp_pos_k, cmp_w1_k, cmp_w2_k, cmp_pos_v, cmp_w1_v, cmp_w2_v, pool_w, pool_scale, w_o,
           norm_mix_post, norm_mlp_pre, w_up, w_down, norm_mlp_post):
    batch, seq, d = x_prompt.shape
    nb, dec_seq, _ = x_sample.shape
    depth = w_in.shape[0]
    pool_dim = d - ATTN_DIM
    n_pages = page_table.shape[1]
    page = cache_kv.shape[2]
    past = n_pages * page
    wb = state_win_kv.shape[2]
    assert dec_seq == 1 and seq >= POOL_BUF and seq % CMP_STRIDE == 0
    assert w_in.shape[2] == ATTN_DIM + 6 * KV_DIM + N_BRANCH * N_HEADS + pool_dim

    tm_p = min(PROJ_ROWS, seq)
    tabs_p = _rope_tables(jnp.arange(seq, dtype=jnp.int32))
    tabs_s = _rope_tables(jnp.full((nb,), past, jnp.int32))
    nt_p = seq // tm_p

    y_p = x_prompt.reshape(batch * seq, d)
    y_s = x_sample.reshape(nb, d)
    kv_p, kv_s, win_p, win_s, pool_p, pool_s = [], [], [], [], [], []
    row_vec = lambda v: v.reshape(1, -1)
    for l in range(depth):
        w_r, wcat, pos8, w2, pw, wo, wu, wd = _layer_weights(
            w_in[l], cmp_pos_k[l], cmp_w1_k[l], cmp_w2_k[l], cmp_pos_v[l], cmp_w1_v[l],
            cmp_w2_v[l], pool_w[l], w_o[l], w_up[l], w_down[l], pool_dim)
        g_pre, g_post = row_vec(norm_mix_pre[l]), row_vec(norm_mix_post[l])
        g_mlp_pre, g_mlp_post = row_vec(norm_mlp_pre[l]), row_vec(norm_mlp_post[l])
        ps = row_vec(pool_scale[l])

        n_w = 2 * KV_HEADS
        win_rows = state_win_kv[l].reshape(nb, wb * n_w, HEAD_DIM)

        q_s, kv_s1, wkv_s, _, gates_s, u_s = _project(
            y_s, g_pre, w_r, tabs_s, lambda i: (i, 0), nb, pool_dim, False)

        q, kv, wkv, kva, gates, u, xc = _project(
            y_p, g_pre, w_r, tabs_p, lambda i: (i % nt_p, 0), tm_p, pool_dim, True)
        cmp_kv = _compress_prompt(xc, wcat, pos8, w2, batch, seq)
        o = _attend_prompt(q, cmp_kv, kva, gates, batch, seq)
        y1, h2 = _mix_prompt(o, u, y_p, pw, ps, wo, g_post, g_mlp_pre, batch, seq)
        y_p, win_rolled = _mlp(h2, y1, wu, wd, g_mlp_post, min(MLP_ROWS, batch * seq),
                               win_rows, wkv_s.reshape(nb, n_w, HEAD_DIM))
        kv_p.append(kv.reshape(batch, seq, CACHE_SLOTS, KV_HEADS, HEAD_DIM))
        wp = min(WINDOW, seq)
        win_p.append(wkv.reshape(batch, seq, 2, KV_HEADS, HEAD_DIM)[:, seq - wp:])
        pool_p.append(u.reshape(batch, seq, pool_dim)[:, seq - POOL_BUF:])

        o = _attend_sample(page_table, cache_kv[l], q_s, kv_s1, wkv_s, win_rows, gates_s,
                           wcat, pos8, w2)
        y1, h2 = _mix_sample(o.reshape(nb, ATTN_DIM), u_s, state_pool[l], y_s, pw, ps, wo,
                             g_post, g_mlp_pre, past)
        y_s = _mlp(h2, y1, wu, wd, g_mlp_post, nb)
        kv_s.append(kv_s1.reshape(nb, 1, CACHE_SLOTS, KV_HEADS, HEAD_DIM))
        win_s.append(win_rolled.reshape(nb, wb, 2, KV_HEADS, HEAD_DIM))
        pool_s.append(jnp.concatenate([state_pool[l], u_s[:, None]], axis=1)[:, 1:])

    return (y_p.reshape(batch, seq, d), y_s.reshape(nb, 1, d),
            jnp.stack(kv_p), jnp.stack(kv_s), jnp.stack(win_p), jnp.stack(win_s),
            jnp.stack(pool_p), jnp.stack(pool_s))
```

```python
import functools

import numpy as np
import jax
import jax.numpy as jnp
from jax import lax
from jax.experimental import pallas as pl
from jax.experimental.pallas import tpu as pltpu

N_HEADS = 8
HEAD_DIM = 128
KV_HEADS = 2
HPG = N_HEADS // KV_HEADS
ATTN_DIM = N_HEADS * HEAD_DIM
KV_DIM = KV_HEADS * HEAD_DIM
N_BRANCH = 3
POOL_WINDOWS = (2, 4, 8, 16)
POOL_GROUPS = len(POOL_WINDOWS)
POOL_BUF = max(POOL_WINDOWS) - 1
ROT_DIM = HEAD_DIM // 4
ROT_HALF = ROT_DIM // 2
ROPE_THETA = 500000.0
CMP_LEN = 32
CMP_STRIDE = 16
SEL_BLOCK = 64
SEL_TOPK = 16
WINDOW = 512
EPS = 1e-6
SCALE = HEAD_DIM ** -0.5
FORCE_SCORE = 1e4
NEG_INF = -1e30

LANES = 128
CACHE_SLOTS = 4
CMP_IN = CMP_LEN * HEAD_DIM
CMP_HALF = CMP_STRIDE * HEAD_DIM
GATE_PAD = LANES
VMEM_LIMIT = 56 * 1024 * 1024

PROJ_ROWS = 512
MLP_ROWS = 512
MLP_FF = 1024
ATTN_Q = 256
ATTN_K = 512
POOL_HALO = 16
SAMPLE_SEQS = 2

BF16 = jnp.bfloat16
F32 = jnp.float32


def _dot(a, b):
    return jnp.dot(a, b, preferred_element_type=F32)


def _dot_nt(a, b):
    return lax.dot_general(a, b, (((1,), (1,)), ((), ())), preferred_element_type=F32)


def _rms(x, g):
    return x * lax.rsqrt(jnp.mean(x * x, axis=-1, keepdims=True) + EPS) * g


def _params(n_axes):
    return pltpu.CompilerParams(
        dimension_semantics=("arbitrary",) * n_axes, vmem_limit_bytes=VMEM_LIMIT)


def _split_hi_lo(x):
    hi = x.astype(BF16)
    lo = (x - hi.astype(F32)).astype(BF16)
    return hi, lo


Q_OFF, KV_OFF, WKV_OFF = 0, ATTN_DIM, ATTN_DIM + 4 * KV_DIM


def _proj_kernel(x_ref, g_ref, w_ref, cos_ref, sa_ref, sb_ref, *rest, pool_dim, chunked):
    if chunked:
        q_ref, kv_ref, wkv_ref, kva_ref, gate_ref, u_ref, xc_ref, tmp_ref = rest
    else:
        q_ref, kv_ref, wkv_ref, kva_ref, gate_ref, u_ref = rest
    tm = x_ref.shape[0]
    u_off = WKV_OFF + 2 * KV_DIM
    gate_off = u_off + pool_dim
    h = _rms(x_ref[...], g_ref[...]).astype(BF16)
    cos, sa, sb = cos_ref[...], sa_ref[...], sb_ref[...]

    def rope(z):
        return (z * cos + pltpu.roll(z, LANES - ROT_HALF, 1) * sa
                + pltpu.roll(z, ROT_HALF, 1) * sb)

    zq = _dot(h, w_ref[:, Q_OFF:Q_OFF + ATTN_DIM])
    for hd in range(N_HEADS):
        sl = slice(hd * HEAD_DIM, (hd + 1) * HEAD_DIM)
        q_ref[:, sl] = rope(zq[:, sl]).astype(BF16)

    n_kv = CACHE_SLOTS * KV_HEADS
    zkv = _dot(h, w_ref[:, KV_OFF:KV_OFF + 4 * KV_DIM])
    for blk in range(n_kv):
        z = zkv[:, blk * HEAD_DIM:(blk + 1) * HEAD_DIM]
        if (blk // KV_HEADS) % 2 == 0:
            z = rope(z)
        kv_ref[pl.ds(blk, tm, stride=n_kv), :] = z
        if blk >= 2 * KV_HEADS:
            kva_ref[blk - 2 * KV_HEADS] = z.astype(BF16)
        elif chunked:
            tmp_ref[...] = z
            for r in range(CMP_STRIDE):
                xc_ref[blk, :, r * HEAD_DIM:(r + 1) * HEAD_DIM] = (
                    tmp_ref[pl.ds(r, tm // CMP_STRIDE, stride=CMP_STRIDE), :].astype(BF16))

    n_w = 2 * KV_HEADS
    zw = _dot(h, w_ref[:, WKV_OFF:WKV_OFF + 2 * KV_DIM])
    for blk in range(n_w):
        z = zw[:, blk * HEAD_DIM:(blk + 1) * HEAD_DIM]
        if blk < KV_HEADS:
            z = rope(z)
        wkv_ref[pl.ds(blk, tm, stride=n_w), :] = z
        kva_ref[2 * KV_HEADS + blk] = z.astype(BF16)

    u_ref[...] = _dot(h, w_ref[:, u_off:u_off + pool_dim])
    gl = _dot(h, w_ref[:, gate_off:gate_off + GATE_PAD])
    gate_ref[...] = 1.0 / (1.0 + jnp.exp(-gl))


def _project(x, gain, w_r, tables, table_index, tm, pool_dim, chunked):
    rows, d = x.shape
    n_proj = w_r.shape[1]
    n_kv = CACHE_SLOTS * KV_HEADS
    n_w = 2 * KV_HEADS
    row = lambda i: (i, 0)
    fixed = lambda i: (0, 0)
    tab_spec = pl.BlockSpec((tm, LANES), table_index)
    out_shape = [
        jax.ShapeDtypeStruct((rows, ATTN_DIM), BF16),
        jax.ShapeDtypeStruct((rows * n_kv, HEAD_DIM), F32),
        jax.ShapeDtypeStruct((rows * n_w, HEAD_DIM), F32),
        jax.ShapeDtypeStruct((n_w + n_kv // 2, rows, HEAD_DIM), BF16),
        jax.ShapeDtypeStruct((rows, GATE_PAD), F32),
        jax.ShapeDtypeStruct((rows, pool_dim), F32),
    ]
    out_specs = [
        pl.BlockSpec((tm, ATTN_DIM), row),
        pl.BlockSpec((tm * n_kv, HEAD_DIM), row),
        pl.BlockSpec((tm * n_w, HEAD_DIM), row),
        pl.BlockSpec((n_w + n_kv // 2, tm, HEAD_DIM), lambda i: (0, i, 0)),
        pl.BlockSpec((tm, GATE_PAD), row),
        pl.BlockSpec((tm, pool_dim), row),
    ]
    scratch = []
    if chunked:
        out_shape.append(jax.ShapeDtypeStruct((n_kv // 2, rows // CMP_STRIDE, CMP_HALF), BF16))
        out_specs.append(pl.BlockSpec((n_kv // 2, tm // CMP_STRIDE, CMP_HALF), lambda i: (0, i, 0)))
        scratch.append(pltpu.VMEM((tm, HEAD_DIM), F32))
    return pl.pallas_call(
        functools.partial(_proj_kernel, pool_dim=pool_dim, chunked=chunked),
        grid=(rows // tm,),
        in_specs=[
            pl.BlockSpec((tm, d), row),
            pl.BlockSpec((1, d), fixed),
            pl.BlockSpec((d, n_proj), fixed),
            tab_spec, tab_spec, tab_spec,
        ],
        out_specs=tuple(out_specs),
        out_shape=tuple(out_shape),
        scratch_shapes=scratch,
        compiler_params=_params(1),
        name="project",
    )(x, gain, w_r, *tables)


def _rope_tables(pos):
    inv = jnp.power(ROPE_THETA, -jnp.arange(ROT_HALF, dtype=F32) * (2.0 / ROT_DIM))
    ang = pos.astype(F32)[:, None] * inv[None, :]
    cos, sin = jnp.cos(ang), jnp.sin(ang)
    n = pos.shape[0]
    rest = LANES - ROT_DIM
    c = jnp.concatenate([cos, cos, jnp.ones((n, rest), F32)], axis=1)
    sa = jnp.concatenate([-sin, jnp.zeros((n, LANES - ROT_HALF), F32)], axis=1)
    sb = jnp.concatenate([jnp.zeros((n, ROT_HALF), F32), sin, jnp.zeros((n, rest), F32)], axis=1)
    return c, sa, sb


def _gelu_tanh(x):
    return 0.5 * x * (1.0 + jnp.tanh(0.7978845608028654 * (x + 0.044715 * (x * x * x))))


def _compress_first(x, wcat, pos8):
    ab = _dot(x, wcat)
    pa = _dot(pos8[:, :CMP_HALF], wcat)[0:1, :HEAD_DIM]
    pb = _dot(pos8[:, CMP_HALF:], wcat)[0:1, HEAD_DIM:]
    return ab, pa + pb


def _compress_hidden(ab, pos_term):
    rows = ab.shape[0]
    b_next = pltpu.roll(ab[:, HEAD_DIM:], rows - 1, 0)
    return _gelu_tanh(ab[:, :HEAD_DIM] + b_next + pos_term).astype(BF16)


def _compress_kernel(x_ref, wcat_ref, pos_ref, w2_ref, o_ref):
    g, _, nck, width = x_ref.shape
    x = x_ref[...].reshape(g * nck, width)
    hid = _compress_hidden(*_compress_first(x, wcat_ref[0], pos_ref[0]))
    o_ref[...] = _dot(hid, w2_ref[0]).astype(BF16).reshape(o_ref.shape)


def _compress_prompt(xc, wcat, pos8, w2, batch, seq):
    nck = seq // CMP_STRIDE
    x = xc.reshape(xc.shape[0], batch, nck, CMP_HALF)
    return pl.pallas_call(
        _compress_kernel,
        grid=(batch, 2),
        in_specs=[
            pl.BlockSpec((KV_HEADS, 1, nck, CMP_HALF), lambda b, s: (s, b, 0, 0)),
            pl.BlockSpec((1, CMP_HALF, 2 * HEAD_DIM), lambda b, s: (s, 0, 0)),
            pl.BlockSpec((1, 8, CMP_IN), lambda b, s: (s, 0, 0)),
            pl.BlockSpec((1, HEAD_DIM, HEAD_DIM), lambda b, s: (s, 0, 0)),
        ],
        out_specs=pl.BlockSpec((1, 1, KV_HEADS, nck, HEAD_DIM), lambda b, s: (b, s, 0, 0, 0)),
        out_shape=jax.ShapeDtypeStruct((batch, 2, KV_HEADS, nck, HEAD_DIM), BF16),
        compiler_params=_params(2),
        name="compress_prompt",
    )(x, wcat, pos8, w2)


def _topk_rows(score, jidx, topk):
    rank = jnp.zeros_like(score)
    for j in range(score.shape[0]):
        bj = score[j:j + 1, :]
        tie = jnp.where(jidx > j, 1.0, 0.0)
        rank = rank + jnp.where(bj > score, 1.0, jnp.where(bj == score, tie, 0.0))
    return jnp.where(rank < topk, 1.0, 0.0)


def _softmax_parts(s, ok):
    s = jnp.where(ok, s, NEG_INF)
    m = jnp.max(s, axis=-1, keepdims=True)
    e = jnp.where(ok, jnp.exp(s - m), 0.0)
    d = jnp.sum(e, axis=-1, keepdims=True)
    return e, jnp.where(d > 0, d, 1.0)


EXP2_SCALE = SCALE * 1.4426950408889634
HEAD_PAIRS = HPG // 2


def _attn_prompt_kernel(q_ref, kc_ref, vc_ref, ks_ref, vs_ref, kw_ref, vw_ref, gate_ref,
                        mt_ref, eye_ref, o_ref,
                        vct_ref, vst_ref, vwt_ref, sel_ref, gt_ref, acc_ref,
                        *, tq, tk, seq, topk):
    i = pl.program_id(1)
    t0 = i * tq
    n_sel = mt_ref.shape[0]
    pair_w = 2 * tq
    eye = eye_ref[:HEAD_DIM, :HEAD_DIM]
    groups = range(KV_HEADS)
    pairs = range(KV_HEADS * HEAD_PAIRS)
    group_of = [hp // HEAD_PAIRS for hp in pairs]

    @pl.when(i == 0)
    def _():
        for g in groups:
            vct_ref[g] = _dot_nt(eye, vc_ref[0, 0, g]).astype(BF16)
            vst_ref[g] = _dot_nt(eye, vs_ref[g]).astype(BF16)
            vwt_ref[g] = _dot_nt(eye, vw_ref[g]).astype(BF16)

    q_all = q_ref[...]
    q_pairs = [jnp.concatenate([q_all[:, (2 * hp) * HEAD_DIM:(2 * hp + 1) * HEAD_DIM],
                                q_all[:, (2 * hp + 1) * HEAD_DIM:(2 * hp + 2) * HEAD_DIM]], axis=0)
               for hp in pairs]

    def both_heads(x):
        return jnp.concatenate([x, x], axis=1)

    def tpos(n_keys):
        return t0 + lax.broadcasted_iota(jnp.int32, (n_keys, tq), 1)

    def kidx(n_keys):
        return lax.broadcasted_iota(jnp.int32, (n_keys, tq), 0)


    ncp = kc_ref.shape[3]
    wk = min(WINDOW + tq, seq)
    ws = pl.multiple_of(jnp.maximum(t0 + tq - wk, 0), tq)
    s_cmp = [_dot_nt(kc_ref[0, 0, group_of[hp]], q_pairs[hp]) for hp in pairs]
    s_win = [_dot_nt(kw_ref[group_of[hp], pl.ds(ws, wk), :], q_pairs[hp]) for hp in pairs]

    ok = both_heads(jnp.where(kidx(ncp) * CMP_STRIDE + (CMP_LEN - 1) <= tpos(ncp), 1.0, 0.0)) > 0.5
    p_cmp, p_sum = [], [None] * KV_HEADS
    for hp in pairs:
        s = jnp.where(ok, s_cmp[hp], NEG_INF)
        m = jnp.max(s, axis=0, keepdims=True)
        e = jnp.where(ok, jnp.exp2((s - m) * EXP2_SCALE), 0.0)
        d = jnp.sum(e, axis=0, keepdims=True)
        p = e / jnp.where(d > 0, d, 1.0)
        p_cmp.append(p.astype(BF16))
        ph = p[:, :tq] + p[:, tq:]
        g = group_of[hp]
        p_sum[g] = ph if p_sum[g] is None else p_sum[g] + ph

    mt = mt_ref[...]
    split = [_split_hi_lo(p_sum[g]) for g in groups]
    imp = [_dot(mt, split[g][0]) + _dot(mt, split[g][1]) for g in groups]
    o_cmp = [_dot(vct_ref[group_of[hp]], p_cmp[hp]) for hp in pairs]

    kpos = ws + kidx(wk)
    bias = both_heads(jnp.where(kpos <= tpos(wk),
                                jnp.where(kpos > tpos(wk) - WINDOW, 0.0, NEG_INF), NEG_INF))
    p_win, l_win = [], []
    for hp in pairs:
        s = s_win[hp] + bias
        p = jnp.exp2((s - jnp.max(s, axis=0, keepdims=True)) * EXP2_SCALE)
        l_win.append(jnp.sum(p, axis=0, keepdims=True))
        p_win.append(p.astype(BF16))
    o_win = [_dot(vwt_ref[group_of[hp], :, pl.ds(ws, wk)], p_win[hp]) for hp in pairs]
    o_win = [o_win[hp] / l_win[hp] for hp in pairs]

    jidx = kidx(n_sel)
    jt = tpos(n_sel) // SEL_BLOCK
    forced = jnp.where(jidx == 0, 1.0, jnp.where(jidx == jt, 1.0, jnp.where(jidx == jt - 1, 1.0, 0.0)))
    for g in groups:
        score = jnp.where(forced > 0.5, FORCE_SCORE, jnp.where(jidx <= jt, imp[g], -1.0))
        sel_ref[g] = _topk_rows(score, jidx, topk)

    acc_ref[...] = jnp.zeros_like(acc_ref)

    def sel_step(kb, carry):
        k0 = pl.multiple_of(kb * tk, tk)
        blk0 = kb * (tk // SEL_BLOCK)
        causal = k0 + kidx(tk) <= tpos(tk)
        bias = []
        for g in groups:
            chosen = jnp.concatenate(
                [jnp.broadcast_to(sel_ref[g, pl.ds(blk0 + j, 1), :], (SEL_BLOCK, tq))
                 for j in range(tk // SEL_BLOCK)], axis=0)
            bias.append(both_heads(
                jnp.where(causal, jnp.where(chosen > 0.5, 0.0, NEG_INF), NEG_INF)))
        ss = [_dot_nt(ks_ref[group_of[hp], pl.ds(k0, tk), :], q_pairs[hp]) + bias[group_of[hp]]
              for hp in pairs]
        out, ps, alphas = [], [], []
        for hp in pairs:
            m, l = carry[2 * hp], carry[2 * hp + 1]
            m_new = jnp.maximum(m, jnp.max(ss[hp], axis=0, keepdims=True))
            p = jnp.exp2((ss[hp] - m_new) * EXP2_SCALE)
            alpha = jnp.exp2((m - m_new) * EXP2_SCALE)
            out += [m_new, alpha * l + jnp.sum(p, axis=0, keepdims=True)]
            ps.append(p.astype(BF16))
            alphas.append(alpha)
        pvs = [_dot(vst_ref[group_of[hp], :, pl.ds(k0, tk)], ps[hp]) for hp in pairs]
        for hp in pairs:
            acc_ref[hp] = alphas[hp] * acc_ref[hp] + pvs[hp]
        return tuple(out)

    n_kb = (t0 + tq + tk - 1) // tk
    init = (jnp.full((1, pair_w), NEG_INF, F32), jnp.zeros((1, pair_w), F32)) * len(pairs)
    stats = lax.fori_loop(0, n_kb, sel_step, init)
    o_sel = [acc_ref[hp] / stats[2 * hp + 1] for hp in pairs]

    gt_ref[...] = gate_ref[...].T
    o_t = []
    for h in range(N_HEADS):
        hp, lanes = h // 2, slice((h % 2) * tq, (h % 2 + 1) * tq)
        col = h * N_BRANCH
        o_t.append((gt_ref[col:col + 1, :] * o_cmp[hp][:, lanes]
                    + gt_ref[col + 1:col + 2, :] * o_sel[hp][:, lanes]
                    + gt_ref[col + 2:col + 3, :] * o_win[hp][:, lanes]).astype(BF16))
    eye_q = eye_ref[:tq, :tq]
    outs = [_dot_nt(eye_q, o_t[h]) for h in range(N_HEADS)]
    for h in range(N_HEADS):
        o_ref[:, h * HEAD_DIM:(h + 1) * HEAD_DIM] = outs[h].astype(BF16)


def _cmp_to_sel(n_cmp_pad, n_cmp, n_sel):
    cs = np.arange(n_cmp_pad)[:, None] * CMP_STRIDE
    ss = np.arange(n_sel)[None, :] * SEL_BLOCK
    hit = (cs < ss + SEL_BLOCK) & (cs + CMP_LEN > ss) & (np.arange(n_cmp_pad)[:, None] < n_cmp)
    return hit.astype(np.float32)


def _attend_prompt(q, cmp_kv, kva, gates, batch, seq):
    tq = min(ATTN_Q, seq)
    tk = min(ATTN_K, seq)
    nq = seq // tq
    ncp = seq // CMP_STRIDE
    n_sel = -(-seq // SEL_BLOCK)
    topk = min(SEL_TOPK, n_sel)
    assert seq % tk == 0 and WINDOW % tq == 0 and tq % LANES == 0
    mt = jnp.asarray(_cmp_to_sel(ncp, ncp - 1, n_sel).T, BF16)
    eye = jnp.asarray(np.eye(max(tq, HEAD_DIM), dtype=np.float32), BF16)
    kv_spec = lambda slot: pl.BlockSpec((KV_HEADS, seq, HEAD_DIM), lambda b, i: (slot, b, 0))
    cmp_spec = lambda s: pl.BlockSpec(
        (1, 1, KV_HEADS, ncp, HEAD_DIM), lambda b, i: (b, s, 0, 0, 0))
    full = lambda a: pl.BlockSpec(a.shape, lambda b, i: (0, 0))
    return pl.pallas_call(
        functools.partial(_attn_prompt_kernel, tq=tq, tk=tk, seq=seq, topk=topk),
        grid=(batch, nq),
        in_specs=[
            pl.BlockSpec((tq, ATTN_DIM), lambda b, i: (b * nq + i, 0)),
            cmp_spec(0), cmp_spec(1),
            kv_spec(0), kv_spec(1), kv_spec(2), kv_spec(3),
            pl.BlockSpec((tq, GATE_PAD), lambda b, i: (b * nq + i, 0)),
            full(mt), full(eye),
        ],
        out_specs=pl.BlockSpec((tq, ATTN_DIM), lambda b, i: (b * nq + i, 0)),
        out_shape=jax.ShapeDtypeStruct((batch * seq, ATTN_DIM), BF16),
        scratch_shapes=[
            pltpu.VMEM((KV_HEADS, HEAD_DIM, ncp), BF16),
            pltpu.VMEM((KV_HEADS, HEAD_DIM, seq), BF16),
            pltpu.VMEM((KV_HEADS, HEAD_DIM, seq), BF16),
            pltpu.VMEM((KV_HEADS, n_sel, tq), F32),
            pltpu.VMEM((GATE_PAD, tq), F32),
            pltpu.VMEM((N_HEADS // 2, HEAD_DIM, 2 * tq), F32),
        ],
        compiler_params=_params(2),
        name="attend_prompt",
    )(q, cmp_kv, cmp_kv, kva, kva, kva, kva, gates, mt, eye)


def _attn_sample_kernel(pt_ref, cache_ref, q_ref, kvn_ref, wn_ref, win_ref, gate_ref,
                        wcat_ref, pos_ref, w2_ref, ms_ref, ek_ref, o_ref, buf, sem,
                        *, n_pages, page, past, topk, n_sel):
    b = pl.program_id(0)
    nb = pl.num_programs(0)
    n_kv = CACHE_SLOTS * KV_HEADS
    n_w = 2 * KV_HEADS
    cpp = page // CMP_STRIDE
    nch = past // CMP_STRIDE
    wb = win_ref.shape[1] // n_w

    n_seq = q_ref.shape[0]
    seqs = range(n_seq)

    def for_all_copies(step, slot, act):
        def per_page(pi, carry):
            dst0 = pl.multiple_of(pi * cpp, cpp)
            for s in seqs:
                src0 = pl.multiple_of(pt_ref[step * n_seq + s, pi] * cpp, cpp)
                for r in range(CMP_STRIDE):
                    act(pltpu.make_async_copy(cache_ref.at[pl.ds(src0, cpp), r],
                                              buf.at[slot, s, r, pl.ds(dst0, cpp)], sem.at[slot]))
            return carry
        lax.fori_loop(0, n_pages, per_page, 0)

    @pl.when(b == 0)
    def _():
        for_all_copies(0, 0, lambda c: c.start())

    @pl.when(b + 1 < nb)
    def _():
        for_all_copies(b + 1, (b + 1) % 2, lambda c: c.start())

    slot = b % 2
    for_all_copies(b, slot, lambda c: c.wait())

    def chunk_rows(s, cache_slot, g, r):
        rows = buf.reshape(2, n_seq, CMP_STRIDE, nch * n_kv, HEAD_DIM)
        return rows[slot, s, r, pl.ds(cache_slot * KV_HEADS + g, nch, stride=n_kv), :]

    row8 = lax.broadcasted_iota(jnp.int32, (N_HEADS, 1), 0)
    in_g0 = row8 < HPG

    def by_group(x0, x1):
        return jnp.where(in_g0, x0, x1)

    q8 = [q_ref[s] for s in seqs]
    qf = [q8[s].astype(F32) for s in seqs]
    groups = range(KV_HEADS)

    def scores(s, keys):
        return by_group(_dot_nt(q8[s], keys[0]), _dot_nt(q8[s], keys[1])) * SCALE

    def cached(s, cache_slot, g):
        return jnp.concatenate([chunk_rows(s, cache_slot, g, r).astype(BF16)
                                for r in range(CMP_STRIDE)], axis=0)

    def compress_input(cache_slot):
        return jnp.concatenate(
            [jnp.concatenate([chunk_rows(s, cache_slot, g, r).astype(BF16)
                              for r in range(CMP_STRIDE)], axis=1)
             for s in seqs for g in groups], axis=0)

    first = [_compress_first(compress_input(cs), wcat_ref[cs], pos_ref[cs]) for cs in range(2)]
    s_sel = [scores(s, [cached(s, 2, g) for g in groups]) for s in seqs]
    s_win = [scores(s, [win_ref[s, pl.ds(g, wb, stride=n_w), :].astype(BF16) for g in groups])
             for s in seqs]
    kc, vc = [_dot(_compress_hidden(*first[cs]), w2_ref[cs]).astype(BF16) for cs in range(2)]

    def compressed(x, s, g):
        r0 = (s * KV_HEADS + g) * nch
        return x[r0:r0 + nch]

    lane_n = lax.broadcasted_iota(jnp.int32, (N_HEADS, nch), 1)
    ok = lane_n * CMP_STRIDE + (CMP_LEN - 1) <= past
    s_cmp = [by_group(*[_dot_nt(q8[s], compressed(kc, s, g)) for g in groups]) * SCALE
             for s in seqs]
    p_cmp = []
    for s in seqs:
        e, d = _softmax_parts(s_cmp[s], ok)
        p_cmp.append(e / d)
    o_cmp = [by_group(*[_dot(p_cmp[s].astype(BF16), compressed(vc, s, g)) for g in groups])
             for s in seqs]

    jl = lax.broadcasted_iota(jnp.int32, (N_HEADS, LANES), 1)
    jt = past // SEL_BLOCK
    forced = jnp.where(jl == 0, 1.0, jnp.where(jl == jt, 1.0, jnp.where(jl == jt - 1, 1.0, 0.0)))
    ii = lax.broadcasted_iota(jnp.int32, (LANES, LANES), 0)
    jj = lax.broadcasted_iota(jnp.int32, (LANES, LANES), 1)
    tie = jnp.where(ii < jj, 1.0, 0.0)
    imp = []
    for s in seqs:
        p_g = [jnp.sum(p_cmp[s][g * HPG:(g + 1) * HPG], axis=0, keepdims=True) for g in groups]
        p2 = jnp.concatenate(p_g + [jnp.zeros((N_HEADS - KV_HEADS, nch), F32)], axis=0)
        p_hi, p_lo = _split_hi_lo(p2)
        imp.append(_dot(p_hi, ms_ref[...]) + _dot(p_lo, ms_ref[...]))
    sel2 = []
    for s in seqs:
        score = jnp.where(forced > 0.5, FORCE_SCORE, jnp.where(jl <= jt, imp[s], -1.0))
        score = jnp.where(jl < n_sel, score, -2.0)
        sel_rows = []
        for g in groups:
            srow = jnp.broadcast_to(score[g:g + 1, :], (LANES, LANES))
            scol = jnp.sum(jnp.where(ii == jj, srow, 0.0), axis=1, keepdims=True)
            beats = jnp.where(scol > srow, 1.0, jnp.where(scol == srow, tie, 0.0))
            rank = jnp.sum(beats, axis=0, keepdims=True)
            sel_rows.append(jnp.where(rank < topk, 1.0, 0.0))
        sel2.append(jnp.concatenate(sel_rows + [jnp.zeros((N_HEADS - KV_HEADS, LANES), F32)], axis=0))
    chunk_ok2 = [_dot(sel2[s].astype(BF16), ek_ref[...]) for s in seqs]

    def new_row(ref, s, idx0):
        x = by_group(ref[s, idx0:idx0 + 1, :], ref[s, idx0 + 1:idx0 + 2, :])
        return x.astype(BF16).astype(F32)

    def weights(s, scores_s, ok, k_new, new_ok):
        s_new = jnp.sum(qf[s] * k_new, axis=-1, keepdims=True) * SCALE
        sc = jnp.where(ok, scores_s, NEG_INF)
        s_new = jnp.where(new_ok, s_new, NEG_INF)
        m = jnp.maximum(jnp.max(sc, axis=-1, keepdims=True), s_new)
        e = jnp.where(ok, jnp.exp(sc - m), 0.0)
        e_new = jnp.where(new_ok, jnp.exp(s_new - m), 0.0)
        d = jnp.sum(e, axis=-1, keepdims=True) + e_new
        return e.astype(BF16), e_new, jnp.where(d > 0, d, 1.0)

    def weighted(w, vals, v_new):
        eb, e_new, d = w
        return (by_group(_dot(eb, vals[0]), _dot(eb, vals[1])) + e_new * v_new) / d

    kpos = past - wb + lax.broadcasted_iota(jnp.int32, (N_HEADS, wb), 1)
    w_sel, w_win = [], []
    for s in seqs:
        chunk_ok = by_group(chunk_ok2[s][0:1], chunk_ok2[s][1:2])
        key_ok = jnp.concatenate([chunk_ok] * CMP_STRIDE, axis=1)
        new_ok = by_group(*[jnp.sum(jnp.where(jl[0:1] == jt, sel2[s][g:g + 1], 0.0), axis=1,
                                    keepdims=True) for g in groups])
        w_sel.append(weights(s, s_sel[s], key_ok > 0.5, new_row(kvn_ref, s, 2 * KV_HEADS),
                             new_ok > 0.5))
        w_win.append(weights(s, s_win[s], kpos > past - WINDOW, new_row(wn_ref, s, 0), row8 >= 0))
    o_sel = [weighted(w_sel[s], [cached(s, 3, g) for g in groups],
                      new_row(kvn_ref, s, 3 * KV_HEADS)) for s in seqs]
    o_win = [weighted(w_win[s],
                      [win_ref[s, pl.ds(KV_HEADS + g, wb, stride=n_w), :].astype(BF16)
                       for g in groups], new_row(wn_ref, s, KV_HEADS)) for s in seqs]

    lane = lax.broadcasted_iota(jnp.int32, (N_HEADS, GATE_PAD), 1)
    for s in seqs:
        gates = jnp.broadcast_to(gate_ref[s], (N_HEADS, GATE_PAD))

        def gate(br):
            return jnp.sum(jnp.where(lane == row8 * N_BRANCH + br, gates, 0.0), axis=-1,
                           keepdims=True)

        o_ref[s] = (gate(0) * o_cmp[s] + gate(1) * o_sel[s] + gate(2) * o_win[s]).astype(BF16)


def _attend_sample(page_table, cache, q, kv_new, wkv_new, win_rows, gates, wcat, pos8, w2):
    nb, n_pages = page_table.shape
    n_phys, page = cache.shape[:2]
    n_kv = CACHE_SLOTS * KV_HEADS
    n_w = 2 * KV_HEADS
    past = n_pages * page
    nch = past // CMP_STRIDE
    wb = win_rows.shape[1] // n_w
    n_sel = -(-(past + 1) // SEL_BLOCK)
    n_cmp = (past + 1) // CMP_STRIDE - 1
    topk = min(SEL_TOPK, n_sel)
    assert n_sel <= LANES and nch * CMP_STRIDE == past
    ms = np.zeros((nch, LANES), np.float32)
    ms[:, :n_sel] = _cmp_to_sel(nch, n_cmp, n_sel)
    ek = (np.arange(nch)[None, :] * CMP_STRIDE // SEL_BLOCK == np.arange(LANES)[:, None])
    ms = jnp.asarray(ms, BF16)
    ek = jnp.asarray(ek.astype(np.float32), BF16)
    n_seq = SAMPLE_SEQS if nb % SAMPLE_SEQS == 0 else 1
    seq3 = lambda n: pl.BlockSpec((n_seq, n, HEAD_DIM), lambda b, pt: (b, 0, 0))
    full3 = lambda a: pl.BlockSpec(a.shape, lambda b, pt: (0, 0, 0))
    full2 = lambda a: pl.BlockSpec(a.shape, lambda b, pt: (0, 0))
    grid_spec = pltpu.PrefetchScalarGridSpec(
        num_scalar_prefetch=1,
        grid=(nb // n_seq,),
        in_specs=[
            pl.BlockSpec(memory_space=pl.ANY),
            seq3(N_HEADS), seq3(n_kv), seq3(n_w), seq3(wb * n_w),
            pl.BlockSpec((n_seq, 1, GATE_PAD), lambda b, pt: (b, 0, 0)),
            full3(wcat), full3(pos8), full3(w2), full2(ms), full2(ek),
        ],
        out_specs=seq3(N_HEADS),
        scratch_shapes=[pltpu.VMEM((2, n_seq, CMP_STRIDE, nch, n_kv, HEAD_DIM), F32),
                        pltpu.SemaphoreType.DMA((2,))],
    )
    return pl.pallas_call(
        functools.partial(_attn_sample_kernel, n_pages=n_pages, page=page, past=past,
                          topk=topk, n_sel=n_sel),
        grid_spec=grid_spec,
        out_shape=jax.ShapeDtypeStruct((nb, N_HEADS, HEAD_DIM), BF16),
        compiler_params=_params(1),
        name="attend_sample",
    )(page_table, cache.reshape(n_phys * page // CMP_STRIDE, CMP_STRIDE, n_kv, HEAD_DIM),
      q.reshape(nb, N_HEADS, HEAD_DIM), kv_new.reshape(nb, n_kv, HEAD_DIM),
      wkv_new.reshape(nb, n_w, HEAD_DIM), win_rows,
      gates.reshape(nb, 1, GATE_PAD), wcat, pos8, w2, ms, ek)


def _mix_tail(o, diffs, x, pw_ref, ps_ref, wo_ref, gpost_ref, gpre_ref, y1_ref, h2_ref):
    gw = diffs[0].shape[1]
    ys = [(_dot(diffs[g].astype(BF16), pw_ref[g]) * ps_ref[:, g * gw:(g + 1) * gw]).astype(BF16)
          for g in range(POOL_GROUPS)]
    cat = jnp.concatenate([o] + ys, axis=1)
    m = _dot(cat, wo_ref[...])
    y1 = x + _rms(m, gpost_ref[...])
    y1_ref[...] = y1
    h2_ref[...] = _rms(y1, gpre_ref[...]).astype(BF16)


def _mix_prompt_kernel(o_ref, u_ref, halo_ref, x_ref, pw_ref, ps_ref, wo_ref, gpost_ref, gpre_ref,
                       y1_ref, h2_ref, *, tm):
    i = pl.program_id(1)
    halo_rows = halo_ref.shape[0]
    halo = jnp.where(i > 0, halo_ref[...], 0.0)
    u = u_ref[...]
    uext = jnp.concatenate([halo, u], axis=0)
    n_ext = uext.shape[0]
    gw = u.shape[1] // POOL_GROUPS
    tpos = i * tm + lax.broadcasted_iota(jnp.int32, (tm, 1), 0)
    diffs = []
    for g, w in enumerate(POOL_WINDOWS):
        s = uext[:, g * gw:(g + 1) * gw]
        k = 1
        while k < w:
            s = s + pltpu.roll(s, k, 0)
            k *= 2
        cnt = jnp.minimum(w, tpos + 1).astype(F32)
        diffs.append(s[halo_rows:n_ext] / cnt - u[:, g * gw:(g + 1) * gw])
    _mix_tail(o_ref[...], diffs, x_ref[...], pw_ref, ps_ref, wo_ref, gpost_ref, gpre_ref,
              y1_ref, h2_ref)


def _mix_sample_kernel(o_ref, u_ref, st_ref, x_ref, pw_ref, ps_ref, wo_ref, gpost_ref, gpre_ref,
                       y1_ref, h2_ref, *, past):
    u = u_ref[...]
    c = u.shape[1]
    gw = c // POOL_GROUPS
    n_hist = st_ref.shape[1] // c
    diffs = []
    for g, w in enumerate(POOL_WINDOWS):
        un = u[:, g * gw:(g + 1) * gw]
        s = un
        for back in range(1, w):
            r = n_hist - back
            s = s + st_ref[:, r * c + g * gw:r * c + (g + 1) * gw]
        diffs.append(s / float(min(w, past + 1)) - un)
    _mix_tail(o_ref[...], diffs, x_ref[...], pw_ref, ps_ref, wo_ref, gpost_ref, gpre_ref,
              y1_ref, h2_ref)


def _mix_specs(tm, d, c, pool_w, idx):
    fixed2 = lambda *a: (0, 0)
    fixed3 = lambda *a: (0, 0, 0)
    weights = [
        pl.BlockSpec(pool_w.shape, fixed3),
        pl.BlockSpec((1, c), fixed2),
        pl.BlockSpec((d, d), fixed2),
        pl.BlockSpec((1, d), fixed2),
        pl.BlockSpec((1, d), fixed2),
    ]
    outs = (pl.BlockSpec((tm, d), idx), pl.BlockSpec((tm, d), idx))
    return weights, outs


def _mix_prompt(o, u, x, pool_w, pool_scale, w_o, g_post, g_pre, batch, seq):
    rows, d = x.shape
    c = u.shape[1]
    tm = min(PROJ_ROWS, seq)
    nt = seq // tm
    halo = POOL_HALO
    assert halo >= POOL_BUF and seq % tm == 0 and tm % halo == 0
    idx = lambda b, i: (b * nt + i, 0)
    halo_idx = lambda b, i: (jnp.maximum((b * nt + i) * (tm // halo) - 1, 0), 0)
    weights, outs = _mix_specs(tm, d, c, pool_w, idx)
    return pl.pallas_call(
        functools.partial(_mix_prompt_kernel, tm=tm),
        grid=(batch, nt),
        in_specs=[pl.BlockSpec((tm, ATTN_DIM), idx), pl.BlockSpec((tm, c), idx),
                  pl.BlockSpec((halo, c), halo_idx), pl.BlockSpec((tm, d), idx)] + weights,
        out_specs=outs,
        out_shape=(jax.ShapeDtypeStruct((rows, d), F32), jax.ShapeDtypeStruct((rows, d), BF16)),
        compiler_params=_params(2),
        name="mix_prompt",
    )(o, u, u, x, pool_w, pool_scale, w_o, g_post, g_pre)


def _mix_sample(o, u, pool_state, x, pool_w, pool_scale, w_o, g_post, g_pre, past):
    rows, d = x.shape
    c = u.shape[1]
    tm = rows
    idx = lambda i: (i, 0)
    weights, outs = _mix_specs(tm, d, c, pool_w, idx)
    st = pool_state.reshape(rows, -1)
    return pl.pallas_call(
        functools.partial(_mix_sample_kernel, past=past),
        grid=(rows // tm,),
        in_specs=[pl.BlockSpec((tm, ATTN_DIM), idx), pl.BlockSpec((tm, c), idx),
                  pl.BlockSpec((tm, st.shape[1]), idx), pl.BlockSpec((tm, d), idx)] + weights,
        out_specs=outs,
        out_shape=(jax.ShapeDtypeStruct((rows, d), F32), jax.ShapeDtypeStruct((rows, d), BF16)),
        compiler_params=_params(1),
        name="mix_sample",
    )(o, u, st, x, pool_w, pool_scale, w_o, g_post, g_pre)


def _mlp_kernel(h_ref, wu_ref, wd_ref, y1_ref, g_ref, *rest, shift):
    if shift:
        state_ref, fresh_ref, y_ref, rolled_ref, acc_ref = rest
        n_new = fresh_ref.shape[1]
        n_keep = state_ref.shape[1] - n_new
        rolled_ref[:, pl.ds(0, n_keep), :] = state_ref[:, pl.ds(n_new, n_keep), :]
        rolled_ref[:, pl.ds(n_keep, n_new), :] = fresh_ref[...]
    else:
        y_ref, acc_ref = rest
    j = pl.program_id(1)

    @pl.when(j == 0)
    def _():
        acc_ref[...] = jnp.zeros_like(acc_ref)

    a = jnp.maximum(_dot(h_ref[...], wu_ref[...]), 0.0)
    acc_ref[...] += _dot((a * a).astype(BF16), wd_ref[...])

    @pl.when(j == pl.num_programs(1) - 1)
    def _():
        y_ref[...] = y1_ref[...] + _rms(acc_ref[...], g_ref[...])


def _mlp(h2, y1, w_up, w_down, gain, tm, state=None, fresh=None):
    rows, d = y1.shape
    ff = w_up.shape[1]
    tf = min(MLP_FF, ff)
    n_i, n_j = rows // tm, ff // tf
    row = lambda i, j: (i, 0)
    shift = state is not None
    y_spec = pl.BlockSpec((tm, d), row)
    y_shape = jax.ShapeDtypeStruct((rows, d), F32)
    extra_in, out_specs, out_shape = [], y_spec, y_shape
    if shift:
        per_step = state.shape[0] // (n_i * n_j)
        assert per_step * n_i * n_j == state.shape[0]
        blk = lambda a: pl.BlockSpec((per_step,) + a.shape[1:], lambda i, j: (i * n_j + j, 0, 0))
        extra_in = [blk(state), blk(fresh)]
        out_specs = (y_spec, blk(state))
        out_shape = (y_shape, jax.ShapeDtypeStruct(state.shape, state.dtype))
    return pl.pallas_call(
        functools.partial(_mlp_kernel, shift=shift),
        grid=(n_i, n_j),
        in_specs=[
            pl.BlockSpec((tm, d), row),
            pl.BlockSpec((d, tf), lambda i, j: (0, j)),
            pl.BlockSpec((tf, d), lambda i, j: (j, 0)),
            pl.BlockSpec((tm, d), row),
            pl.BlockSpec((1, d), lambda i, j: (0, 0)),
        ] + extra_in,
        out_specs=out_specs,
        out_shape=out_shape,
        scratch_shapes=[pltpu.VMEM((tm, d), F32)],
        compiler_params=_params(2),
        name="mlp",
    )(h2, w_up, w_down, y1, gain, *((state, fresh) if shift else ()))


def _layer_weights(w_in, cmp_pos_k, cmp_w1_k, cmp_w2_k, cmp_pos_v, cmp_w1_v, cmp_w2_v,
                   pool_w, w_o, w_up, w_down, pool_dim):
    gate_lo = WKV_OFF + 2 * KV_DIM
    gate_hi = gate_lo + N_BRANCH * N_HEADS
    w_r = jnp.concatenate(
        [w_in[:, :gate_lo].astype(BF16), w_in[:, gate_hi:gate_hi + pool_dim].astype(BF16),
         jnp.pad(w_in[:, gate_lo:gate_hi].astype(BF16),
                 ((0, 0), (0, GATE_PAD - N_BRANCH * N_HEADS)))], axis=1)
    wcat = jnp.stack([jnp.concatenate([w1[:CMP_HALF], w1[CMP_HALF:]], axis=1)
                      for w1 in (cmp_w1_k, cmp_w1_v)]).astype(BF16)
    pos8 = jnp.stack([jnp.pad(p.reshape(1, CMP_IN), ((0, 7), (0, 0)))
                      for p in (cmp_pos_k, cmp_pos_v)]).astype(BF16)
    w2 = jnp.stack([cmp_w2_k, cmp_w2_v]).astype(BF16)
    return (w_r, wcat, pos8, w2, pool_w.astype(BF16), w_o.astype(BF16),
            w_up.astype(BF16), w_down.astype(BF16))


def kernel(x_prompt, x_sample, cache_kv, state_win_kv, state_pool, page_table, norm_mix_pre, w_in,
           cmp_pos_k, cmp_w1_k, cmp_w2_k, cmp_pos_v, cmp_w1_v, cmp_w2_v, pool_w, pool_scale, w_o,
           norm_mix_post, norm_mlp_pre, w_up, w_down, norm_mlp_post):
    batch, seq, d = x_prompt.shape
    nb, dec_seq, _ = x_sample.shape
    depth = w_in.shape[0]
    pool_dim = d - ATTN_DIM
    n_pages = page_table.shape[1]
    page = cache_kv.shape[2]
    past = n_pages * page
    wb = state_win_kv.shape[2]
    assert dec_seq == 1 and seq >= POOL_BUF and seq % CMP_STRIDE == 0
    assert w_in.shape[2] == ATTN_DIM + 6 * KV_DIM + N_BRANCH * N_HEADS + pool_dim

    tm_p = min(PROJ_ROWS, seq)
    tabs_p = _rope_tables(jnp.arange(seq, dtype=jnp.int32))
    tabs_s = _rope_tables(jnp.full((nb,), past, jnp.int32))
    nt_p = seq // tm_p

    y_p = x_prompt.reshape(batch * seq, d)
    y_s = x_sample.reshape(nb, d)
    kv_p, kv_s, win_p, win_s, pool_p, pool_s = [], [], [], [], [], []
    row_vec = lambda v: v.reshape(1, -1)
    for l in range(depth):
        w_r, wcat, pos8, w2, pw, wo, wu, wd = _layer_weights(
            w_in[l], cmp_pos_k[l], cmp_w1_k[l], cmp_w2_k[l], cmp_pos_v[l], cmp_w1_v[l],
            cmp_w2_v[l], pool_w[l], w_o[l], w_up[l], w_down[l], pool_dim)
        g_pre, g_post = row_vec(norm_mix_pre[l]), row_vec(norm_mix_post[l])
        g_mlp_pre, g_mlp_post = row_vec(norm_mlp_pre[l]), row_vec(norm_mlp_post[l])
        ps = row_vec(pool_scale[l])

        n_w = 2 * KV_HEADS
        win_rows = state_win_kv[l].reshape(nb, wb * n_w, HEAD_DIM)

        q_s, kv_s1, wkv_s, _, gates_s, u_s = _project(
            y_s, g_pre, w_r, tabs_s, lambda i: (i, 0), nb, pool_dim, False)

        q, kv, wkv, kva, gates, u, xc = _project(
            y_p, g_pre, w_r, tabs_p, lambda i: (i % nt_p, 0), tm_p, pool_dim, True)
        cmp_kv = _compress_prompt(xc, wcat, pos8, w2, batch, seq)
        o = _attend_prompt(q, cmp_kv, kva, gates, batch, seq)
        y1, h2 = _mix_prompt(o, u, y_p, pw, ps, wo, g_post, g_mlp_pre, batch, seq)
        y_p, win_rolled = _mlp(h2, y1, wu, wd, g_mlp_post, min(MLP_ROWS, batch * seq),
                               win_rows, wkv_s.reshape(nb, n_w, HEAD_DIM))
        kv_p.append(kv.reshape(batch, seq, CACHE_SLOTS, KV_HEADS, HEAD_DIM))
        wp = min(WINDOW, seq)
        win_p.append(wkv.reshape(batch, seq, 2, KV_HEADS, HEAD_DIM)[:, seq - wp:])
        pool_p.append(u.reshape(batch, seq, pool_dim)[:, seq - POOL_BUF:])

        o = _attend_sample(page_table, cache_kv[l], q_s, kv_s1, wkv_s, win_rows, gates_s,
                           wcat, pos8, w2)
        y1, h2 = _mix_sample(o.reshape(nb, ATTN_DIM), u_s, state_pool[l], y_s, pw, ps, wo,
                             g_post, g_mlp_pre, past)
        y_s = _mlp(h2, y1, wu, wd, g_mlp_post, nb)
        kv_s.append(kv_s1.reshape(nb, 1, CACHE_SLOTS, KV_HEADS, HEAD_DIM))
        win_s.append(win_rolled.reshape(nb, wb, 2, KV_HEADS, HEAD_DIM))
        pool_s.append(jnp.concatenate([state_pool[l], u_s[:, None]], axis=1)[:, 1:])

    return (y_p.reshape(batch, seq, d), y_s.reshape(nb, 1, d),
            jnp.stack(kv_p), jnp.stack(kv_s), jnp.stack(win_p), jnp.stack(win_s),
            jnp.stack(pool_p), jnp.stack(pool_s))
```
